```python
import math
import jax
import jax.numpy as jnp
from jax import lax
import numpy as np

D_MODEL = 1024
BATCH = 8
SEQ = 4096
DEPTH = 2

GRID_W = 64
CTX_LEN = 256
SHORT_CONV = 3
NORM_EPS = 1e-6
N_BRANCH = 3
MIX_WIDTH = 512

HY_WIDTH = MIX_WIDTH
HY_ORDER = 2
HY_FILTER_DIM = 64
HY_BANDS = 16
HY_EMB = 2 * HY_BANDS + 1
HY_DECAY_TARGET = 1e-2
HY_FAST_PCT = 0.3
HY_SLOW_PCT = 1.5

RW_HEAD = 64
RW_HEADS = MIX_WIDTH // RW_HEAD
RW_WIDTH = RW_HEADS * RW_HEAD
RW_DECAY_RANK = 64
RW_ICLR_RANK = 64
RW_GATE_RANK = 128
RW_LN_EPS = 64e-5

S5_GROUP = 16
S5_GROUPS = MIX_WIDTH // S5_GROUP
S5_WIDTH = S5_GROUPS * S5_GROUP
S5_STATE = 64

N_EXPERTS = 32
TOP_K = 4
D_EXPERT = 1024
SWIGLU_LIMIT = 7.0
SWIGLU_ALPHA = 1.702
MOE_BLOCK = 128

HY_COLS = (HY_ORDER + 1) * HY_WIDTH
RW_COLS = 3 * RW_WIDTH + RW_DECAY_RANK + RW_ICLR_RANK + RW_GATE_RANK
S5_COLS = S5_WIDTH
GATE_COLS = N_BRANCH * D_MODEL
IN_COLS = HY_COLS + RW_COLS + S5_COLS + GATE_COLS
RW_SPLITS = (RW_WIDTH, 2 * RW_WIDTH, 3 * RW_WIDTH, 3 * RW_WIDTH + RW_DECAY_RANK,
             3 * RW_WIDTH + RW_DECAY_RANK + RW_ICLR_RANK)

kernel_name = 'hybrid_hyena_rwkv7_s5_moe_dit'


def rmsnorm(x, g):
    xf = x.astype(jnp.float32)
    y = xf * lax.rsqrt(jnp.mean(xf * xf, axis=-1, keepdims=True) + NORM_EPS)
    return (y * g.astype(jnp.float32)).astype(x.dtype)


def modulate(x, g, shift, scale):
    return rmsnorm(x, g) * (1 + scale) + shift


def short_conv(z, w, b):
    n = z.shape[1]
    half = SHORT_CONV // 2
    zp = jnp.pad(z, ((0, 0), (half, half), (0, 0)))
    return sum(zp[:, j:j + n] * w[j] for j in range(SHORT_CONV)) + b


def grid_pos_embed(n_tokens):
    rows = n_tokens // GRID_W
    row_id, col_id = jnp.meshgrid(jnp.arange(rows), jnp.arange(GRID_W), indexing='ij')
    quarter = D_MODEL // 4
    omega = 1.0 / (10000.0 ** (jnp.arange(quarter, dtype=jnp.float32) / quarter))

    def enc(p):
        ang = p.reshape(-1)[:, None].astype(jnp.float32) * omega
        return jnp.concatenate([jnp.sin(ang), jnp.cos(ang)], axis=-1)

    return jnp.concatenate([enc(row_id), enc(col_id)], axis=-1)


def hyena_filters(n, w1, b1, f1, w2, b2, f2, w3, b3):
    f32 = jnp.float32
    t = jnp.linspace(0.0, 1.0, n, dtype=f32)[:, None]
    bands = jnp.linspace(1e-4, HY_BANDS - 1, HY_BANDS, dtype=f32)
    ang = (2 * math.pi * jnp.arange(n, dtype=f32) / n)[:, None] * bands
    feats = jnp.concatenate([t, jnp.cos(ang), -jnp.sin(ang)], axis=-1)
    hf = jnp.sin(f1.astype(f32) * (feats @ w1.astype(f32) + b1.astype(f32)))
    hf = jnp.sin(f2.astype(f32) * (hf @ w2.astype(f32) + b2.astype(f32)))
    hf = (hf @ w3.astype(f32) + b3.astype(f32)).reshape(n, HY_ORDER, 2, HY_WIDTH)
    deltas = jnp.abs(jnp.linspace(math.log(HY_DECAY_TARGET) / HY_SLOW_PCT,
                                  math.log(HY_DECAY_TARGET) / HY_FAST_PCT, HY_WIDTH, dtype=f32))
    hf = hf * jnp.exp(-t * deltas)[:, None, None, :]
    fwd = hf[:, :, 0]
    bwd = hf[1:, :, 1][::-1]
    return jnp.concatenate([fwd, jnp.zeros((1, HY_ORDER, HY_WIDTH), f32), bwd], axis=0)


def fft_long_conv(z, filt, bias):
    n = z.shape[1]
    zf = z.astype(jnp.float32)
    spec = jnp.fft.rfft(zf, n=2 * n, axis=1) * jnp.fft.rfft(filt, n=2 * n, axis=0)[None]
    y = jnp.fft.irfft(spec, n=2 * n, axis=1)[:, :n]
    return (y + zf * bias.astype(jnp.float32)).astype(z.dtype)


def hyena_branch(z, conv_w, conv_b, w1, b1, f1, w2, b2, f2, w3, b3, bias):
    u = short_conv(z, conv_w, conv_b)
    x1, x2, v = jnp.split(u, 3, axis=-1)
    filt = hyena_filters(z.shape[1], w1, b1, f1, w2, b2, f2, w3, b3)
    y = x1 * fft_long_conv(v, filt[:, 0], bias[0])
    return x2 * fft_long_conv(y, filt[:, 1], bias[1])


def _heads(t):
    return t.reshape(t.shape[:-1] + (RW_HEADS, RW_HEAD))


def rwkv_prep(z, conv_w, conv_b, w_up, w0, a_up, a0, g_up, k_k, k_a):
    u = short_conv(z, conv_w, conv_b).astype(jnp.float32)
    r, k, v, xw, xa, xg = jnp.split(u, RW_SPLITS, axis=-1)
    g = jax.nn.sigmoid(xg) @ g_up.astype(jnp.float32)
    kk = _heads(k * k_k)
    kk = kk * lax.rsqrt(jnp.maximum(jnp.sum(kk * kk, axis=-1, keepdims=True), 1e-24))
    w = -jax.nn.softplus(-(w0[:, None, None, :] + jnp.einsum('blr,drc->dblc', jnp.tanh(xw), w_up))) - 0.5
    decay = _heads(jnp.exp(-jnp.exp(w)))
    a = _heads(jax.nn.sigmoid(a0[:, None, None, :] + jnp.einsum('blr,drc->dblc', xa, a_up)))
    k_dir = _heads(k) * (1 + (a - 1) * _heads(k_a))
    return _heads(r), _heads(v), g, kk, decay, a, k_dir


def wkv_scan(r, decay, k, v, kk, a, s0, reverse, with_output):
    xs = tuple(jnp.moveaxis(t, 1, 0) for t in (r, decay, k, v, -kk, kk * a))

    def step(s, inp):
        r_t, w_t, k_t, v_t, a_t, b_t = inp
        sa = jnp.einsum('bhvk,bhk->bhv', s, a_t)
        s = s * w_t[:, :, None, :] + sa[..., None] * b_t[:, :, None, :] + v_t[..., None] * k_t[:, :, None, :]
        y = jnp.einsum('bhvk,bhk->bhv', s, r_t) if with_output else None
        return s, y

    s, ys = lax.scan(step, s0, xs, reverse=reverse)
    return s, (jnp.moveaxis(ys, 0, 1) if with_output else None)


def rwkv_readout(r, v, g, k_dir, y_dirs, r_k, ln_g, ln_b, dtype):
    bonus = jnp.sum(r * k_dir * r_k, axis=(0, -1))[..., None] * v
    y = y_dirs[0] + y_dirs[1] + bonus
    mu = jnp.mean(y, axis=-1, keepdims=True)
    var = jnp.mean(jnp.square(y - mu), axis=-1, keepdims=True)
    y = ((y - mu) * lax.rsqrt(var + RW_LN_EPS)).reshape(y.shape[:2] + (RW_WIDTH,))
    return ((y * ln_g + ln_b) * g).astype(dtype)


def rwkv_branch(z, zc, need_ctx, conv_w, conv_b, w_up, w0, a_up, a0, g_up, k_k, k_a, r_k, ln_g, ln_b):
    r, v, g, kk, decay, a, k_dir = rwkv_prep(z, conv_w, conv_b, w_up, w0, a_up, a0, g_up, k_k, k_a)
    rc, vc, gc, kkc, decayc, ac, k_dirc = rwkv_prep(zc, conv_w, conv_b, w_up, w0, a_up, a0, g_up, k_k, k_a)
    s0 = jnp.zeros((z.shape[0], RW_HEADS, RW_HEAD, RW_HEAD), jnp.float32)
    y_lat, y_ctx = [], []
    for d, rev in enumerate((False, True)):
        s_ctx, yc_d = wkv_scan(rc, decayc[d], k_dirc[d], vc, kkc, ac[d], s0, rev, need_ctx)
        _, yl_d = wkv_scan(r, decay[d], k_dir[d], v, kk, a[d], s_ctx, rev, True)
        y_lat.append(yl_d)
        y_ctx.append(yc_d)
    out = rwkv_readout(r, v, g, k_dir, y_lat, r_k, ln_g, ln_b, z.dtype)
    out_c = rwkv_readout(rc, vc, gc, k_dirc, y_ctx, r_k, ln_g, ln_b, zc.dtype) if need_ctx else None
    return out, out_c


def s5_discretise(a_re, a_im, log_dt, b_re, b_im):
    f32 = jnp.float32
    a_re, a_im, b_re, b_im = (t.astype(f32) for t in (a_re, a_im, b_re, b_im))
    dt = jnp.exp(log_dt.astype(f32))[:, None]
    mag = jnp.exp(a_re * dt)
    lb_re, lb_im = mag * jnp.cos(a_im * dt), mag * jnp.sin(a_im * dt)
    den = a_re * a_re + a_im * a_im
    nr = lb_re - 1.0
    cr = (nr * a_re + lb_im * a_im) / den
    ci = (lb_im * a_re - nr * a_im) / den
    bb_re = cr[..., None] * b_re - ci[..., None] * b_im
    bb_im = cr[..., None] * b_im + ci[..., None] * b_re
    return lb_re, lb_im, bb_re, bb_im


def _linear_recurrence_combine(e_i, e_j):
    ar_i, ai_i, br_i, bi_i = e_i
    ar_j, ai_j, br_j, bi_j = e_j
    return (ar_j * ar_i - ai_j * ai_i,
            ar_j * ai_i + ai_j * ar_i,
            ar_j * br_i - ai_j * bi_i + br_j,
            ar_j * bi_i + ai_j * br_i + bi_j)


def s5_scan(u, lb_re, lb_im, bb_re, bb_im, s0_re, s0_im, reverse):
    n = u.shape[1]
    bu_re = jnp.einsum('blgh,gph->blgp', u, bb_re)
    bu_im = jnp.einsum('blgh,gph->blgp', u, bb_im)
    edge = n - 1 if reverse else 0
    bu_re = bu_re.at[:, edge].add(lb_re * s0_re - lb_im * s0_im)
    bu_im = bu_im.at[:, edge].add(lb_re * s0_im + lb_im * s0_re)
    shape = (1, n) + lb_re.shape
    elems = (jnp.broadcast_to(lb_re, shape), jnp.broadcast_to(lb_im, shape), bu_re, bu_im)
    _, _, h_re, h_im = lax.associative_scan(_linear_recurrence_combine, elems, reverse=reverse, axis=1)
    return h_re, h_im


def s5_readout(h_re, h_im, u_g, c_re, c_im, d_skip, glu_w, glu_b, dtype):
    f32 = jnp.float32
    y = (jnp.einsum('blgp,ghp->blgh', h_re, c_re.astype(f32))
         - jnp.einsum('blgp,ghp->blgh', h_im, c_im.astype(f32))
         + u_g * d_skip.astype(f32).reshape(S5_GROUPS, S5_GROUP))
    y = jax.nn.gelu(y.reshape(y.shape[:2] + (S5_WIDTH,)))
    lin, gate = jnp.split(y @ glu_w.astype(f32) + glu_b.astype(f32), 2, axis=-1)
    return (lin * jax.nn.sigmoid(gate)).astype(dtype)


def s5_branch(u, uc, need_ctx, a_re, a_im, log_dt, b_re, b_im, c_re, c_im, d_skip, glu_w, glu_b):
    def groups(t):
        return t.astype(jnp.float32).reshape(t.shape[:2] + (S5_GROUPS, S5_GROUP))

    ul, ucg = groups(u), groups(uc)
    s0 = jnp.zeros((u.shape[0], S5_GROUPS, S5_STATE), jnp.float32)
    lat_re, lat_im, ctx_re, ctx_im = [], [], [], []
    for d, rev in enumerate((False, True)):
        lb_re, lb_im, bb_re, bb_im = s5_discretise(a_re[d], a_im[d], log_dt[d], b_re, b_im)
        hc_re, hc_im = s5_scan(ucg, lb_re, lb_im, bb_re, bb_im, s0, s0, rev)
        edge = 0 if rev else -1
        hl_re, hl_im = s5_scan(ul, lb_re, lb_im, bb_re, bb_im, hc_re[:, edge], hc_im[:, edge], rev)
        lat_re.append(hl_re)
        lat_im.append(hl_im)
        ctx_re.append(hc_re)
        ctx_im.append(hc_im)
    out = s5_readout(lat_re[0] + lat_re[1], lat_im[0] + lat_im[1], ul, c_re, c_im, d_skip, glu_w, glu_b, u.dtype)
    if not need_ctx:
        return out, None
    out_c = s5_readout(ctx_re[0] + ctx_re[1], ctx_im[0] + ctx_im[1], ucg, c_re, c_im, d_skip, glu_w, glu_b, uc.dtype)
    return out, out_c


def merge_branches(zg, y_hy, y_rw, y_s5, w_branch, w_out):
    g_hy, g_rw, g_s5 = jnp.split(jax.nn.sigmoid(zg), N_BRANCH, axis=-1)
    m = g_hy * (y_hy @ w_branch[0]) + g_rw * (y_rw @ w_branch[1]) + g_s5 * (y_s5 @ w_branch[2])
    return m @ w_out


def token_mixer(h, hc, need_ctx, w_in, hy_conv_w, hy_conv_b, hy_w1, hy_b1, hy_f1, hy_w2, hy_b2, hy_f2,
                hy_w3, hy_b3, hy_bias, rw_conv_w, rw_conv_b, rw_w_up, rw_w0, rw_a_up, rw_a0, rw_g_up,
                rw_k_k, rw_k_a, rw_r_k, rw_ln_g, rw_ln_b, s5_a_re, s5_a_im, s5_log_dt, s5_b_re, s5_b_im,
                s5_c_re, s5_c_im, s5_d, s5_glu_w, s5_glu_b, w_branch, w_out):
    bounds = (HY_COLS, HY_COLS + RW_COLS, HY_COLS + RW_COLS + S5_COLS)
    z_hy, z_rw, z_s5, z_gate = jnp.split(h @ w_in, bounds, axis=-1)
    zc_hy, zc_rw, zc_s5, zc_gate = jnp.split(hc @ w_in, bounds, axis=-1)
    hyena_args = (hy_conv_w, hy_conv_b, hy_w1, hy_b1, hy_f1, hy_w2, hy_b2, hy_f2, hy_w3, hy_b3, hy_bias)
    y_hy = hyena_branch(z_hy, *hyena_args)
    y_rw, yc_rw = rwkv_branch(z_rw, zc_rw, need_ctx, rw_conv_w, rw_conv_b, rw_w_up, rw_w0, rw_a_up, rw_a0,
                              rw_g_up, rw_k_k, rw_k_a, rw_r_k, rw_ln_g, rw_ln_b)
    y_s5, yc_s5 = s5_branch(z_s5, zc_s5, need_ctx, s5_a_re, s5_a_im, s5_log_dt, s5_b_re, s5_b_im,
                            s5_c_re, s5_c_im, s5_d, s5_glu_w, s5_glu_b)
    y = merge_branches(z_gate, y_hy, y_rw, y_s5, w_branch, w_out)
    if not need_ctx:
        return y, None
    yc = merge_branches(zc_gate, hyena_branch(zc_hy, *hyena_args), yc_rw, yc_s5, w_branch, w_out)
    return y, yc


def moe_ffn(h, router_w, router_b, w1, b1, w2, b2):
    bsz, n, dm = h.shape
    xt = h.reshape(-1, dm)
    n_tok = xt.shape[0]
    n_asg = n_tok * TOP_K
    logits = (xt @ router_w + router_b).astype(jnp.float32)
    top_val, top_idx = lax.top_k(logits, TOP_K)
    gate = jax.nn.softmax(top_val, axis=-1).astype(h.dtype)
    flat_e = top_idx.reshape(-1)
    flat_tok = jnp.repeat(jnp.arange(n_tok, dtype=jnp.int32), TOP_K)
    flat_w = gate.reshape(-1)
    order = jnp.argsort(flat_e)
    sorted_e = flat_e[order]
    counts = jnp.bincount(flat_e, length=N_EXPERTS)
    padded = (counts + MOE_BLOCK - 1) // MOE_BLOCK * MOE_BLOCK
    pad_end = jnp.cumsum(padded)
    pad_start = pad_end - padded
    grp_start = jnp.cumsum(counts) - counts
    dest = pad_start[sorted_e] + jnp.arange(n_asg, dtype=jnp.int32) - grp_start[sorted_e]
    n_blocks = -(-n_asg // MOE_BLOCK) + N_EXPERTS
    n_slots = n_blocks * MOE_BLOCK
    slot_tok = jnp.zeros((n_slots,), jnp.int32).at[dest].set(flat_tok[order])
    slot_w = jnp.zeros((n_slots,), h.dtype).at[dest].set(flat_w[order])
    blk_e = jnp.minimum(jnp.searchsorted(pad_end, jnp.arange(n_blocks) * MOE_BLOCK, side='right'), N_EXPERTS - 1)
    xs = xt[slot_tok].reshape(n_blocks, MOE_BLOCK, dm)

    def expert_block(args):
        xb, e = args
        gl, up = jnp.split(xb @ w1[e] + b1[e], 2, axis=-1)
        gl = jnp.minimum(gl, SWIGLU_LIMIT)
        up = jnp.clip(up, -SWIGLU_LIMIT, SWIGLU_LIMIT)
        act = (up + 1) * gl * jax.nn.sigmoid(SWIGLU_ALPHA * gl)
        return act @ w2[e] + b2[e]

    ys = lax.map(expert_block, (xs, blk_e)).reshape(n_slots, dm)
    out = jax.ops.segment_sum(ys * slot_w[:, None], slot_tok, num_segments=n_tok)
    return out.reshape(bsz, n, dm)


def setup_inputs(seed: int = 0) -> dict:
    key = jax.random.key(seed)
    ks = iter(jax.random.split(key, 64))
    f32 = jnp.float32

    def nrm(shape, scale):
        return jax.random.normal(next(ks), shape, f32) * scale

    def uni(shape, lo, hi):
        return jax.random.uniform(next(ks), shape, f32, minval=lo, maxval=hi)

    L, D = DEPTH, D_MODEL
    G, P, H = S5_GROUPS, S5_STATE, S5_GROUP
    return {
        'x': nrm((BATCH, SEQ, D), 1.0),
        'c': nrm((BATCH, D), 1.0),
        'ctx': nrm((BATCH, CTX_LEN, D), 1.0),
        'c_ctx': nrm((D,), 1.0),
        'ada_w': nrm((L, D, 6 * D), D ** -0.5),
        'ada_b': nrm((L, 6 * D), 0.02),
        'norm1_g': 1.0 + nrm((L, D), 0.02),
        'norm2_g': 1.0 + nrm((L, D), 0.02),
        'w_in': nrm((L, D, IN_COLS), D ** -0.5),
        'hy_conv_w': nrm((L, SHORT_CONV, HY_COLS), SHORT_CONV ** -0.5),
        'hy_conv_b': nrm((L, HY_COLS), 0.02),
        'hy_w1': nrm((L, HY_EMB, HY_FILTER_DIM), HY_EMB ** -0.5),
        'hy_b1': nrm((L, HY_FILTER_DIM), 0.02),
        'hy_f1': 1.0 + nrm((L, HY_FILTER_DIM), 0.02),
        'hy_w2': nrm((L, HY_FILTER_DIM, HY_FILTER_DIM), HY_FILTER_DIM ** -0.5),
        'hy_b2': nrm((L, HY_FILTER_DIM), 0.02),
        'hy_f2': 1.0 + nrm((L, HY_FILTER_DIM), 0.02),
        'hy_w3': nrm((L, HY_FILTER_DIM, HY_ORDER * 2 * HY_WIDTH), 0.005),
        'hy_b3': nrm((L, HY_ORDER * 2 * HY_WIDTH), 0.002),
        'hy_bias': nrm((L, HY_ORDER, HY_WIDTH), 0.1),
        'rw_conv_w': nrm((L, SHORT_CONV, RW_COLS), SHORT_CONV ** -0.5),
        'rw_conv_b': nrm((L, RW_COLS), 0.02),
        'rw_w_up': nrm((L, 2, RW_DECAY_RANK, RW_WIDTH), 0.5 * RW_DECAY_RANK ** -0.5),
        'rw_w0': uni((L, 2, RW_WIDTH), -6.0, 0.0),
        'rw_a_up': nrm((L, 2, RW_ICLR_RANK, RW_WIDTH), 0.5 * RW_ICLR_RANK ** -0.5),
        'rw_a0': nrm((L, 2, RW_WIDTH), 0.1),
        'rw_g_up': nrm((L, RW_GATE_RANK, RW_WIDTH), RW_GATE_RANK ** -0.5),
        'rw_k_k': 0.85 + nrm((L, RW_WIDTH), 0.02),
        'rw_k_a': 1.0 + nrm((L, RW_WIDTH), 0.02),
        'rw_r_k': nrm((L, RW_HEADS, RW_HEAD), 0.1),
        'rw_ln_g': 1.0 + nrm((L, RW_WIDTH), 0.02),
        'rw_ln_b': nrm((L, RW_WIDTH), 0.02),
        's5_a_re': -0.5 + nrm((L, 2, G, P), 0.01),
        's5_a_im': math.pi * jnp.arange(P, dtype=f32) + nrm((L, 2, G, P), 0.01),
        's5_log_dt': uni((L, 2, G), math.log(1e-3), math.log(1e-1)),
        's5_b_re': nrm((L, G, P, H), H ** -0.5),
        's5_b_im': nrm((L, G, P, H), H ** -0.5),
        's5_c_re': nrm((L, G, H, P), P ** -0.5),
        's5_c_im': nrm((L, G, H, P), P ** -0.5),
        's5_d': nrm((L, S5_WIDTH), 0.5),
        's5_glu_w': nrm((L, S5_WIDTH, 2 * S5_WIDTH), S5_WIDTH ** -0.5),
        's5_glu_b': nrm((L, 2 * S5_WIDTH), 0.02),
        'w_branch': nrm((L, N_BRANCH, MIX_WIDTH, D), MIX_WIDTH ** -0.5),
        'w_out': nrm((L, D, D), D ** -0.5),
        'router_w': nrm((L, D, N_EXPERTS), D ** -0.5),
        'router_b': nrm((L, N_EXPERTS), 0.01),
        'moe_w1': nrm((L, N_EXPERTS, D, 2 * D_EXPERT), D ** -0.5),
        'moe_b1': nrm((L, N_EXPERTS, 2 * D_EXPERT), 0.02),
        'moe_w2': nrm((L, N_EXPERTS, D_EXPERT, D), D_EXPERT ** -0.5),
        'moe_b2': nrm((L, N_EXPERTS, D), 0.02),
        'final_g': 1.0 + nrm((D,), 0.02),
    }


def reference(x, c, ctx, c_ctx, ada_w, ada_b, norm1_g, norm2_g, w_in, hy_conv_w, hy_conv_b, hy_w1, hy_b1,
              hy_f1, hy_w2, hy_b2, hy_f2, hy_w3, hy_b3, hy_bias, rw_conv_w, rw_conv_b, rw_w_up, rw_w0,
              rw_a_up, rw_a0, rw_g_up, rw_k_k, rw_k_a, rw_r_k, rw_ln_g, rw_ln_b, s5_a_re, s5_a_im,
              s5_log_dt, s5_b_re, s5_b_im, s5_c_re, s5_c_im, s5_d, s5_glu_w, s5_glu_b, w_branch, w_out,
              router_w, router_b, moe_w1, moe_b1, moe_w2, moe_b2, final_g):
    x = x + grid_pos_embed(x.shape[1]).astype(x.dtype)
    xc = ctx
    cond = jax.nn.silu(c)[:, None, :]
    cond_ctx = jax.nn.silu(c_ctx)
    for l in range(DEPTH):
        need_ctx = l < DEPTH - 1
        sh1, sc1, g1, sh2, sc2, g2 = jnp.split(cond @ ada_w[l] + ada_b[l], 6, axis=-1)
        csh1, csc1, cg1, csh2, csc2, cg2 = jnp.split(cond_ctx @ ada_w[l] + ada_b[l], 6, axis=-1)
        y, yc = token_mixer(
            modulate(x, norm1_g[l], sh1, sc1), modulate(xc, norm1_g[l], csh1, csc1), need_ctx,
            w_in[l], hy_conv_w[l], hy_conv_b[l], hy_w1[l], hy_b1[l], hy_f1[l], hy_w2[l], hy_b2[l], hy_f2[l],
            hy_w3[l], hy_b3[l], hy_bias[l], rw_conv_w[l], rw_conv_b[l], rw_w_up[l], rw_w0[l], rw_a_up[l],
            rw_a0[l], rw_g_up[l], rw_k_k[l], rw_k_a[l], rw_r_k[l], rw_ln_g[l], rw_ln_b[l], s5_a_re[l],
            s5_a_im[l], s5_log_dt[l], s5_b_re[l], s5_b_im[l], s5_c_re[l], s5_c_im[l], s5_d[l], s5_glu_w[l],
            s5_glu_b[l], w_branch[l], w_out[l])
        x = x + g1 * y
        x = x + g2 * moe_ffn(modulate(x, norm2_g[l], sh2, sc2), router_w[l], router_b[l],
                             moe_w1[l], moe_b1[l], moe_w2[l], moe_b2[l])
        if need_ctx:
            xc = xc + cg1 * yc
            xc = xc + cg2 * moe_ffn(modulate(xc, norm2_g[l], csh2, csc2), router_w[l], router_b[l],
                                    moe_w1[l], moe_b1[l], moe_w2[l], moe_b2[l])
    return rmsnorm(x, final_g)
```

```python
import functools
import math

import numpy as np
import jax
import jax.numpy as jnp
from jax import lax
from jax.experimental import pallas as pl
from jax.experimental.pallas import tpu as pltpu

F32 = jnp.float32
BF16 = jnp.bfloat16

D_MODEL = 1024
GRID_W = 64
SHORT_CONV = 3
NORM_EPS = 1e-6
N_BRANCH = 3
MIX_WIDTH = 512

HY_ORDER = 2
HY_BANDS = 16
HY_DECAY_TARGET = 1e-2
HY_FAST_PCT = 0.3
HY_SLOW_PCT = 1.5
HY_COLS = (HY_ORDER + 1) * MIX_WIDTH

RW_HEAD = 64
RW_HEADS = MIX_WIDTH // RW_HEAD
RW_DECAY_RANK = 64
RW_ICLR_RANK = 64
RW_GATE_RANK = 128
RW_LN_EPS = 64e-5
RW_COLS = 3 * MIX_WIDTH + RW_DECAY_RANK + RW_ICLR_RANK + RW_GATE_RANK

S5_GROUP = 16
S5_GROUPS = MIX_WIDTH // S5_GROUP
S5_STATE = 64
S5_HID = S5_GROUPS * S5_STATE

N_EXPERTS = 32
TOP_K = 4
D_EXPERT = 1024
SWIGLU_LIMIT = 7.0
SWIGLU_ALPHA = 1.702

LANES = 128
SUBLANES = 8
VMEM_LIMIT_BYTES = 56 * 1024 * 1024

ROW_TILE = 256
MOE_ROWS = 256
WKV_STEPS = 32
S5_STEPS = 64
S5_LANE_CHUNK = 512


def _params(*sem):
    return pltpu.CompilerParams(dimension_semantics=sem, vmem_limit_bytes=VMEM_LIMIT_BYTES)


def _dot(a, b):
    return jnp.dot(a, b, preferred_element_type=F32)


def _split3(x):
    hi = x.astype(BF16)
    r1 = x - hi.astype(F32)
    mid = r1.astype(BF16)
    lo = (r1 - mid.astype(F32)).astype(BF16)
    return hi, mid, lo


def _dot_hi(a, b):
    a0, a1, a2 = _split3(a)
    b0, b1, b2 = _split3(b)
    return (_dot(a0, b0) + (_dot(a0, b1) + _dot(a1, b0))
            + (_dot(a1, b1) + _dot(a0, b2) + _dot(a2, b0)))


def _seg_sum(x, ones_bd):
    hi, mid, lo = _split3(x)
    return _dot(hi, ones_bd) + _dot(mid, ones_bd) + _dot(lo, ones_bd)


def _sigmoid(x):
    return 1.0 / (1.0 + jnp.exp(-x))


def _rms_modulate(x, g, shift, scale):
    ms = jnp.mean(x * x, axis=-1, keepdims=True)
    y = x * lax.rsqrt(ms + NORM_EPS) * g
    return y * (1.0 + scale) + shift


def _k_ada(c_ref, w_ref, b_ref, o_ref):
    c = c_ref[...]
    s = c * _sigmoid(c)
    o_ref[...] = _dot_hi(s, w_ref[...]) + b_ref[...]


def ada_proj(cvec, w, b):
    rows, d = cvec.shape
    n = w.shape[1]
    tn = 1536
    return pl.pallas_call(
        _k_ada,
        grid=(n // tn,),
        in_specs=[pl.BlockSpec((rows, d), lambda j: (0, 0)),
                  pl.BlockSpec((d, tn), lambda j: (0, j)),
                  pl.BlockSpec((1, tn), lambda j: (0, j))],
        out_specs=pl.BlockSpec((rows, tn), lambda j: (0, j)),
        out_shape=jax.ShapeDtypeStruct((rows, n), F32),
        compiler_params=_params("arbitrary"),
        name="ada_proj",
    )(cvec, w, b.reshape(1, n))


def _k_add_pos(x_ref, p_ref, o_ref):
    o_ref[0] = x_ref[0] + p_ref[...]


def add_pos(x, pos):
    b, n, d = x.shape
    tm = min(ROW_TILE * 2, n)
    return pl.pallas_call(
        _k_add_pos,
        grid=(b, n // tm),
        in_specs=[pl.BlockSpec((1, tm, d), lambda i, j: (i, j, 0)),
                  pl.BlockSpec((tm, d), lambda i, j: (j, 0))],
        out_specs=pl.BlockSpec((1, tm, d), lambda i, j: (i, j, 0)),
        out_shape=jax.ShapeDtypeStruct(x.shape, F32),
        compiler_params=_params("arbitrary", "arbitrary"),
        name="add_pos",
    )(x, pos)


def _k_norm_mm(x_ref, sh_ref, sc_ref, g_ref, w_ref, o_ref, h_ref):
    @pl.when(pl.program_id(2) == 0)
    def _():
        h = _rms_modulate(x_ref[0], g_ref[...], sh_ref[0], sc_ref[0])
        h_ref[...] = h.astype(BF16)

    o_ref[0] = _dot(h_ref[...], w_ref[...])


def norm_mod_matmul(x, shift, scale, g, w):
    b, n, d = x.shape
    ncol = w.shape[1]
    tm = min(2 * ROW_TILE, n)
    tn = ncol if ncol <= 1792 else 1536
    return pl.pallas_call(
        _k_norm_mm,
        grid=(b, n // tm, ncol // tn),
        in_specs=[pl.BlockSpec((1, tm, d), lambda i, j, k: (i, j, 0)),
                  pl.BlockSpec((1, 1, d), lambda i, j, k: (i, 0, 0)),
                  pl.BlockSpec((1, 1, d), lambda i, j, k: (i, 0, 0)),
                  pl.BlockSpec((1, d), lambda i, j, k: (0, 0)),
                  pl.BlockSpec((d, tn), lambda i, j, k: (0, k))],
        out_specs=pl.BlockSpec((1, tm, tn), lambda i, j, k: (i, j, k)),
        out_shape=jax.ShapeDtypeStruct((b, n, ncol), F32),
        scratch_shapes=[pltpu.VMEM((tm, d), BF16)],
        compiler_params=_params("arbitrary", "arbitrary", "arbitrary"),
        name="norm_mod_matmul",
    )(x, shift, scale, g.reshape(1, d), w)


def _short_conv_val(z, w, b):
    n = z.shape[0]
    row = lax.broadcasted_iota(jnp.int32, z.shape, 0)
    zm = jnp.where(row == 0, 0.0, pltpu.roll(z, 1, 0))
    zp = jnp.where(row == n - 1, 0.0, pltpu.roll(z, n - 1, 0))
    return zm * w[0:1] + z * w[1:2] + zp * w[2:3] + b


def _k_sconv(z_ref, w_ref, b_ref, o_ref):
    o_ref[0] = _short_conv_val(z_ref[0], w_ref[...], b_ref[...])


def short_conv(z, w, b):
    bsz, n, c = z.shape
    tc = 256 if c % 256 == 0 else LANES
    return pl.pallas_call(
        _k_sconv,
        grid=(bsz, c // tc),
        in_specs=[pl.BlockSpec((1, n, tc), lambda i, j: (i, 0, j)),
                  pl.BlockSpec((SHORT_CONV, tc), lambda i, j: (0, j)),
                  pl.BlockSpec((1, tc), lambda i, j: (0, j))],
        out_specs=pl.BlockSpec((1, n, tc), lambda i, j: (i, 0, j)),
        out_shape=jax.ShapeDtypeStruct(z.shape, F32),
        compiler_params=_params("arbitrary", "arbitrary"),
        name="short_conv",
    )(z, w, b.reshape(1, c))


def _fft_split(n):
    total = 2 * n
    bits = total.bit_length() - 1
    assert 1 << bits == total
    n1 = 1 << ((bits + 1) // 2)
    n1 = max(n1, 2 * SUBLANES)
    return n1, total // n1


@functools.lru_cache(maxsize=None)
def _dft_mats(n):
    n1, n2 = _fft_split(n)
    total = n1 * n2
    k1 = np.arange(n1)[:, None]
    m1 = np.arange(n1)[None, :]
    m2 = np.arange(n2)[:, None, None]
    ang = -2.0 * np.pi * ((n2 * k1 * m1)[None] + m2 * k1[None]) / total
    fa = np.concatenate([np.cos(ang), np.sin(ang)], axis=1)
    k2 = np.arange(n2)[:, None]
    mm = np.arange(n2)[None, :]
    gang = -2.0 * np.pi * k2 * mm / n2
    gr, gi = np.cos(gang), np.sin(gang)
    fb = np.block([[gr, -gi], [gi, gr]])
    fc = np.block([[gr, gi], [-gi, gr]])
    m1d = np.arange(n1 // 2)[:, None]
    k1d = np.arange(n1)[None, :]
    dang = 2.0 * np.pi * ((n2 * m1d * k1d)[None] + m2 * k1d[None]) / total
    fd = np.concatenate([np.cos(dang), -np.sin(dang)], axis=2) / total
    return (fa.astype(np.float32), fb.astype(np.float32), fc.astype(np.float32), fd.astype(np.float32))


def _fft_stage_a(src_ref, y_ref, fa_ref, n1, n2, k_rows):
    def body(m2, c):
        xs = src_ref[pl.ds(m2, k_rows, stride=n2), :].astype(BF16)
        res = _dot(fa_ref[m2, :, 0:k_rows], xs)
        y_ref[pl.ds(m2, n1, stride=2 * n2), :] = res[0:n1]
        y_ref[pl.ds(n2 + m2, n1, stride=2 * n2), :] = res[n1:2 * n1]
        return c
    lax.fori_loop(0, n2, body, 0)


def _long_conv(buf_ref, y_ref, h_ref, fa_ref, fb_ref, fc_ref, fd_ref, n1, n2):
    _fft_stage_a(buf_ref, y_ref, fa_ref, n1, n2, n1 // 2)

    def body_k1(k1, c):
        rows = pl.ds(pl.multiple_of(k1 * 2 * n2, 2 * n2), 2 * n2)
        z = _dot(fb_ref[...], y_ref[rows, :].astype(BF16))
        zr, zi = z[0:n2], z[n2:2 * n2]
        h = h_ref[k1]
        hr, hi = h[0:n2], h[n2:2 * n2]
        p = jnp.concatenate([zr * hr - zi * hi, zr * hi + zi * hr], axis=0)
        y_ref[rows, :] = _dot(fc_ref[...], p.astype(BF16))
        return c
    lax.fori_loop(0, n1, body_k1, 0)

    def body_m2(m2, c):
        qr = y_ref[pl.ds(m2, n1, stride=2 * n2), :]
        qi = y_ref[pl.ds(n2 + m2, n1, stride=2 * n2), :]
        q = jnp.concatenate([qr, qi], axis=0).astype(BF16)
        buf_ref[pl.ds(m2, n1 // 2, stride=n2), :] = _dot(fd_ref[m2], q)
        return c
    lax.fori_loop(0, n2, body_m2, 0)


def _k_hyena(x1_ref, x2_ref, v_ref, w1_ref, w2_ref, wv_ref, b1_ref, b2_ref, bv_ref, h1_ref, h2_ref,
             fa_ref, fb_ref, fc_ref, fd_ref, o_ref, buf_ref, y_ref, *, n1, n2):
    conv = functools.partial(_long_conv, buf_ref, y_ref, fa_ref=fa_ref, fb_ref=fb_ref, fc_ref=fc_ref,
                             fd_ref=fd_ref, n1=n1, n2=n2)
    buf_ref[...] = _short_conv_val(v_ref[0], wv_ref[...], bv_ref[...])
    conv(h1_ref)
    buf_ref[...] = _short_conv_val(x1_ref[0], w1_ref[...], b1_ref[...]) * buf_ref[...]
    conv(h2_ref)
    o_ref[0] = _short_conv_val(x2_ref[0], w2_ref[...], b2_ref[...]) * buf_ref[...]


def _k_fspec(f_ref, fa_ref, fb_ref, o_ref, *, n1, n2):
    _fft_stage_a(f_ref, o_ref, fa_ref, n1, n2, n1)

    def body_k1(k1, c):
        rows = pl.ds(pl.multiple_of(k1 * 2 * n2, 2 * n2), 2 * n2)
        o_ref[rows, :] = _dot(fb_ref[...], o_ref[rows, :].astype(BF16))
        return c
    lax.fori_loop(0, n1, body_k1, 0)


def filter_spectrum(filt, n):
    n1, n2 = _fft_split(n)
    total, c = filt.shape
    fa, fb, _, _ = _dft_mats(n)
    out = pl.pallas_call(
        functools.partial(_k_fspec, n1=n1, n2=n2),
        grid=(c // LANES,),
        in_specs=[pl.BlockSpec((total, LANES), lambda j: (0, j)),
                  pl.BlockSpec((n2, 2 * n1, n1), lambda j: (0, 0, 0)),
                  pl.BlockSpec((2 * n2, 2 * n2), lambda j: (0, 0))],
        out_specs=pl.BlockSpec((total * 2, LANES), lambda j: (0, j)),
        out_shape=jax.ShapeDtypeStruct((2 * total, c), F32),
        compiler_params=_params("arbitrary"),
        name="filter_spectrum",
    )(filt, jnp.asarray(fa, BF16), jnp.asarray(fb, BF16))
    return out.reshape(n1, 2 * n2, c)


def _k_filter_mlp(feat_ref, win_ref, w1_ref, b1_ref, f1_ref, w2_ref, b2_ref, f2_ref, w3_ref, b3_ref, o_ref):
    h = jnp.sin(f1_ref[...] * (_dot_hi(feat_ref[...], w1_ref[...]) + b1_ref[...]))
    h = jnp.sin(f2_ref[...] * (_dot_hi(h, w2_ref[...]) + b2_ref[...]))
    o_ref[...] = (_dot_hi(h, w3_ref[...]) + b3_ref[...]) * win_ref[...]


@functools.lru_cache(maxsize=None)
def _filter_consts(n):
    t = np.linspace(0.0, 1.0, n, dtype=np.float32)[:, None]
    bands = np.linspace(1e-4, HY_BANDS - 1, HY_BANDS, dtype=np.float32)
    ang = (np.float32(2 * math.pi) * np.arange(n, dtype=np.float32) / np.float32(n))[:, None] * bands
    feats = np.concatenate([t, np.cos(ang), -np.sin(ang)], axis=-1).astype(np.float32)
    pad = (-feats.shape[1]) % SUBLANES
    feats = np.pad(feats, ((0, 0), (0, pad)))
    deltas = np.abs(np.linspace(math.log(HY_DECAY_TARGET) / HY_SLOW_PCT,
                                math.log(HY_DECAY_TARGET) / HY_FAST_PCT, MIX_WIDTH, dtype=np.float32))
    window = np.exp(-t * deltas).astype(np.float32)
    return feats, np.tile(window, (1, 2 * HY_ORDER))


def hyena_filter_table(n, w1, b1, f1, w2, b2, f2, w3, b3):
    feats, window = _filter_consts(n)
    kf = feats.shape[1]
    fd = w1.shape[1]
    ncol = w3.shape[1]
    w1p = jnp.pad(w1, ((0, kf - w1.shape[0]), (0, 0)))
    tm = min(ROW_TILE, n)
    full = lambda shape: pl.BlockSpec(shape, lambda i: (0,) * len(shape))
    return pl.pallas_call(
        _k_filter_mlp,
        grid=(n // tm,),
        in_specs=[pl.BlockSpec((tm, kf), lambda i: (i, 0)),
                  pl.BlockSpec((tm, ncol), lambda i: (i, 0)),
                  full((kf, fd)), full((1, fd)), full((1, fd)),
                  full((fd, fd)), full((1, fd)), full((1, fd)),
                  full((fd, ncol)), full((1, ncol))],
        out_specs=pl.BlockSpec((tm, ncol), lambda i: (i, 0)),
        out_shape=jax.ShapeDtypeStruct((n, ncol), F32),
        compiler_params=_params("arbitrary"),
        name="hyena_filter_mlp",
    )(jnp.asarray(feats), jnp.asarray(window), w1p, b1.reshape(1, fd), f1.reshape(1, fd),
      w2, b2.reshape(1, fd), f2.reshape(1, fd), w3, b3.reshape(1, ncol))


def hyena_spectra(n, w1, b1, f1, w2, b2, f2, w3, b3, bias):
    hf = hyena_filter_table(n, w1, b1, f1, w2, b2, f2, w3, b3).reshape(n, HY_ORDER, 2, MIX_WIDTH)
    fwd = hf[:, :, 0]
    bwd = hf[1:, :, 1][::-1]
    fwd = fwd.at[0].add(bias)
    filt = jnp.concatenate([fwd, jnp.zeros((1, HY_ORDER, MIX_WIDTH), F32), bwd], axis=0)
    spec = filter_spectrum(filt.reshape(2 * n, HY_ORDER * MIX_WIDTH), n)
    n1, n2 = _fft_split(n)
    return spec.reshape(n1, 2 * n2, HY_ORDER, MIX_WIDTH).transpose(2, 0, 1, 3)


def hyena_branch(z, conv_w, conv_b, spectra):
    bsz, n, _ = z.shape
    n1, n2 = _fft_split(n)
    fa, fb, fc, fd = (jnp.asarray(m, BF16) for m in _dft_mats(n))
    fa = fa[:, :, : n1 // 2]
    nt = MIX_WIDTH // LANES
    cb = conv_b.reshape(1, HY_COLS)
    zspec = lambda off: pl.BlockSpec((1, n, LANES), lambda j, i: (i, 0, off + j))
    wspec = lambda off: pl.BlockSpec((SHORT_CONV, LANES), lambda j, i: (0, off + j))
    bspec = lambda off: pl.BlockSpec((1, LANES), lambda j, i: (0, off + j))
    once = pl.Buffered(1)
    hspec = lambda o: pl.BlockSpec((None, n1, 2 * n2, LANES), lambda j, i: (o, 0, 0, j), pipeline_mode=once)
    full = lambda shape: pl.BlockSpec(shape, lambda j, i: (0,) * len(shape), pipeline_mode=once)
    return pl.pallas_call(
        functools.partial(_k_hyena, n1=n1, n2=n2),
        grid=(nt, bsz),
        in_specs=[zspec(0), zspec(nt), zspec(2 * nt), wspec(0), wspec(nt), wspec(2 * nt),
                  bspec(0), bspec(nt), bspec(2 * nt), hspec(0), hspec(1),
                  full(fa.shape), full(fb.shape), full(fc.shape), full(fd.shape)],
        out_specs=pl.BlockSpec((1, n, LANES), lambda j, i: (i, 0, j)),
        out_shape=jax.ShapeDtypeStruct((bsz, n, MIX_WIDTH), F32),
        scratch_shapes=[pltpu.VMEM((n, LANES), F32), pltpu.VMEM((4 * n, LANES), F32)],
        compiler_params=_params("arbitrary", "arbitrary"),
        name="hyena_long_conv",
    )(z, z, z, conv_w, conv_w, conv_w, cb, cb, cb, spectra, spectra, fa, fb, fc, fd)


def _k_rwprep(u_ref, gup_ref, wup_ref, w0_ref, aup_ref, a0_ref, kk_ref, ka_ref, rk_ref, e_ref,
              r_o, v_o, g_o, nkk_o, bonus_o, dec0_o, dec1_o, kd0_o, kd1_o, b0_o, b1_o):
    w = MIX_WIDTH
    u = u_ref[0]
    r, k, v = u[:, 0:w], u[:, w:2 * w], u[:, 2 * w:3 * w]
    xw = u[:, 3 * w:3 * w + RW_DECAY_RANK]
    xa = u[:, 3 * w + RW_DECAY_RANK:3 * w + RW_DECAY_RANK + RW_ICLR_RANK]
    xg = u[:, 3 * w + RW_DECAY_RANK + RW_ICLR_RANK:]
    ones_bd = e_ref[...]
    g_o[0] = _dot(_sigmoid(xg).astype(BF16), gup_ref[...])
    kk = k * kk_ref[...]
    kk = kk * lax.rsqrt(jnp.maximum(_seg_sum(kk * kk, ones_bd), 1e-24))
    txw = jnp.tanh(xw)
    kd_sum = None
    for d, (dec_o, kd_o, b_o) in enumerate(((dec0_o, kd0_o, b0_o), (dec1_o, kd1_o, b1_o))):
        x = -(w0_ref[d] + _dot_hi(txw, wup_ref[d]))
        softplus = jnp.maximum(x, 0.0) + jnp.log(1.0 + jnp.exp(-jnp.abs(x)))
        dec_o[0] = jnp.exp(-jnp.exp(-softplus - 0.5))
        a = _sigmoid(a0_ref[d] + _dot_hi(xa, aup_ref[d]))
        kd = k * (1.0 + (a - 1.0) * ka_ref[...])
        kd_o[0] = kd
        b_o[0] = kk * a
        kd_sum = kd if kd_sum is None else kd_sum + kd
    r_o[0] = r
    v_o[0] = v
    nkk_o[0] = -kk
    bonus_o[0] = _seg_sum(r * kd_sum * rk_ref[...], ones_bd) * v


def rwkv_prep(u, g_up, w_up, w0, a_up, a0, k_k, k_a, r_k, ones_bd):
    bsz, n, cols = u.shape
    w = MIX_WIDTH
    tm = min(ROW_TILE, n)
    full = lambda shape: pl.BlockSpec(shape, lambda i, j: (0,) * len(shape))
    tile = pl.BlockSpec((1, tm, w), lambda i, j: (i, j, 0))
    return pl.pallas_call(
        _k_rwprep,
        grid=(bsz, n // tm),
        in_specs=[pl.BlockSpec((1, tm, cols), lambda i, j: (i, j, 0)),
                  full((RW_GATE_RANK, w)), full((2, RW_DECAY_RANK, w)), full((2, 1, w)),
                  full((2, RW_ICLR_RANK, w)), full((2, 1, w)), full((1, w)), full((1, w)), full((1, w)),
                  full((w, w))],
        out_specs=[tile] * 11,
        out_shape=[jax.ShapeDtypeStruct((bsz, n, w), F32)] * 11,
        compiler_params=_params("arbitrary", "arbitrary"),
        name="rwkv_prep",
    )(u, g_up.astype(BF16), w_up, w0.reshape(2, 1, w), a_up, a0.reshape(2, 1, w), k_k.reshape(1, w),
      k_a.reshape(1, w), r_k.reshape(1, w), ones_bd)


def _k_wkv(r_ref, w_ref, k_ref, v_ref, a_ref, b_ref, y_ref, s_ref, *, steps):
    @pl.when(pl.program_id(0) == 0)
    def _():
        s_ref[...] = jnp.zeros_like(s_ref)

    def step(t, c):
        sa = jnp.zeros(s_ref.shape[1:], F32)
        for i in range(RW_HEAD):
            sa = sa + s_ref[i] * a_ref[t, pl.ds(i, 1), :]
        vv = v_ref[t]
        y = jnp.zeros(s_ref.shape[1:], F32)
        for i in range(RW_HEAD):
            s_new = (s_ref[i] * w_ref[t, pl.ds(i, 1), :] + sa * b_ref[t, pl.ds(i, 1), :]
                     + vv * k_ref[t, pl.ds(i, 1), :])
            s_ref[i] = s_new
            y = y + s_new * r_ref[t, pl.ds(i, 1), :]
        y_ref[t] = y
        return c
    lax.fori_loop(0, steps, step, 0)


def wkv_scan(r, w, k, v, a, b):
    t_total, _, lanes = r.shape
    steps = WKV_STEPS
    spec = pl.BlockSpec((steps, RW_HEAD, lanes), lambda i: (i, 0, 0))
    return pl.pallas_call(
        functools.partial(_k_wkv, steps=steps),
        grid=(t_total // steps,),
        in_specs=[spec] * 6,
        out_specs=spec,
        out_shape=jax.ShapeDtypeStruct(r.shape, F32),
        scratch_shapes=[pltpu.VMEM((RW_HEAD, RW_HEAD, lanes), F32)],
        compiler_params=_params("arbitrary"),
        name="wkv_scan",
    )(r, w, k, v, a, b)


def _scan_layout(ctx_f, lat_f, ctx_b, lat_b):
    bsz = lat_f.shape[0]
    seq_f = jnp.concatenate([ctx_f, lat_f], axis=1)
    seq_b = jnp.concatenate([ctx_b[:, ::-1], lat_b[:, ::-1]], axis=1)
    st = jnp.stack([seq_f, seq_b], axis=0)
    t_total = st.shape[2]
    st = st.reshape(2, bsz, t_total, RW_HEADS, RW_HEAD)
    return st.transpose(2, 4, 0, 1, 3).reshape(t_total, RW_HEAD, 2 * bsz * RW_HEADS)


def _scan_unlayout(y, bsz, n_ctx):
    t_total = y.shape[0]
    y = y.reshape(t_total, RW_HEAD, 2, bsz, RW_HEADS).transpose(2, 3, 0, 4, 1)
    y = y.reshape(2, bsz, t_total, MIX_WIDTH)
    y_ctx = y[0, :, :n_ctx] + y[1, :, :n_ctx][:, ::-1]
    y_lat = y[0, :, n_ctx:] + y[1, :, n_ctx:][:, ::-1]
    return y_ctx, y_lat


def rwkv_scan_branch(u_ctx, u_lat, p, ones_bd):
    args = (p["rw_g_up"], p["rw_w_up"], p["rw_w0"], p["rw_a_up"], p["rw_a0"], p["rw_k_k"], p["rw_k_a"],
            p["rw_r_k"].reshape(-1), ones_bd)
    pc = rwkv_prep(u_ctx, *args)
    pq = rwkv_prep(u_lat, *args)
    names = ("r", "v", "g", "nkk", "bonus", "dec0", "dec1", "kd0", "kd1", "b0", "b1")
    c = dict(zip(names, pc))
    q = dict(zip(names, pq))
    both = lambda f, b: _scan_layout(c[f], q[f], c[b], q[b])
    y = wkv_scan(both("r", "r"), both("dec0", "dec1"), both("kd0", "kd1"), both("v", "v"),
                 both("nkk", "nkk"), both("b0", "b1"))
    y_ctx, y_lat = _scan_unlayout(y, u_lat.shape[0], u_ctx.shape[1])
    return (y_ctx, c["bonus"], c["g"]), (y_lat, q["bonus"], q["g"])


def _k_s5(u_ref, bre_ref, bim_ref, lre_ref, lim_ref, cre_ref, cim_ref, y_ref, hre_ref, him_ref, st_ref,
          *, steps, reverse):
    @pl.when(pl.program_id(0) == 0)
    def _():
        st_ref[...] = jnp.zeros_like(st_ref)

    bsz = u_ref.shape[1]
    u = u_ref[...].reshape(steps * bsz, MIX_WIDTH).astype(BF16)
    hre_ref[...] = _dot(u, bre_ref[...])
    him_ref[...] = _dot(u, bim_ref[...])
    for ch in range(S5_HID // S5_LANE_CHUNK):
        cols = slice(ch * S5_LANE_CHUNK, (ch + 1) * S5_LANE_CHUNK)
        lre = lre_ref[:, cols]
        lim = lim_ref[:, cols]

        def step(i, carry):
            hr, hi = carry
            t = (steps - 1 - i) if reverse else i
            rows = pl.ds(pl.multiple_of(t * bsz, bsz), bsz)
            nr = lre * hr - lim * hi + hre_ref[rows, cols]
            ni = lre * hi + lim * hr + him_ref[rows, cols]
            hre_ref[rows, cols] = nr
            him_ref[rows, cols] = ni
            return nr, ni
        hr, hi = lax.fori_loop(0, steps, step, (st_ref[0, :, cols], st_ref[1, :, cols]))
        st_ref[0, :, cols] = hr
        st_ref[1, :, cols] = hi
    y = _dot(hre_ref[...].astype(BF16), cre_ref[...]) - _dot(him_ref[...].astype(BF16), cim_ref[...])
    y_ref[...] = y.reshape(steps, bsz, MIX_WIDTH)


def s5_scan(u_tb, bre, bim, lre, lim, cre, cim, n_ctx, reverse):
    t_total, bsz, w = u_tb.shape
    steps = S5_STEPS
    nc = n_ctx // steps
    nb = t_total // steps
    if reverse:
        idx = lambda i: (jnp.where(i < nc, nc - 1 - i, nb - 1 + nc - i), 0, 0)
    else:
        idx = lambda i: (i, 0, 0)
    full = lambda shape: pl.BlockSpec(shape, lambda i: (0,) * len(shape))
    return pl.pallas_call(
        functools.partial(_k_s5, steps=steps, reverse=reverse),
        grid=(nb,),
        in_specs=[pl.BlockSpec((steps, bsz, w), idx),
                  full((w, S5_HID)), full((w, S5_HID)), full((bsz, S5_HID)), full((bsz, S5_HID)),
                  full((S5_HID, w)), full((S5_HID, w))],
        out_specs=pl.BlockSpec((steps, bsz, w), idx),
        out_shape=jax.ShapeDtypeStruct(u_tb.shape, F32),
        scratch_shapes=[pltpu.VMEM((steps * bsz, S5_HID), F32), pltpu.VMEM((steps * bsz, S5_HID), F32),
                        pltpu.VMEM((2, bsz, S5_HID), F32)],
        compiler_params=_params("arbitrary"),
        name="s5_scan",
    )(u_tb, bre, bim, lre, lim, cre, cim)


def _k_s5disc(are_ref, aim_ref, ldt_ref, lre_ref, lim_ref, cr_ref, ci_ref):
    a_re, a_im = are_ref[...], aim_ref[...]
    dt = jnp.exp(ldt_ref[...])
    mag = jnp.exp(a_re * dt)
    lb_re, lb_im = mag * jnp.cos(a_im * dt), mag * jnp.sin(a_im * dt)
    den = a_re * a_re + a_im * a_im
    nr = lb_re - 1.0
    lre_ref[...] = lb_re
    lim_ref[...] = lb_im
    cr_ref[...] = (nr * a_re + lb_im * a_im) / den
    ci_ref[...] = (lb_im * a_re - nr * a_im) / den


def s5_operators(a_re, a_im, log_dt, b_re, b_im, c_re, c_im, bsz):
    g, p, h = S5_GROUPS, S5_STATE, S5_GROUP
    shp = jax.ShapeDtypeStruct((2 * g, p), F32)
    lb_re, lb_im, cr, ci = pl.pallas_call(
        _k_s5disc, out_shape=[shp] * 4, name="s5_discretise",
    )(a_re.reshape(2 * g, p), a_im.reshape(2 * g, p), jnp.broadcast_to(log_dt.reshape(2 * g, 1), (2 * g, p)))
    cr, ci = cr.reshape(2, g, p, 1), ci.reshape(2, g, p, 1)
    bb_re = cr * b_re[None] - ci * b_im[None]
    bb_im = cr * b_im[None] + ci * b_re[None]
    eye = jnp.eye(g, dtype=F32)
    bd_in = lambda m: jnp.einsum("dgph,gk->dghkp", m, eye).reshape(2, g * h, g * p).astype(BF16)
    bd_out = lambda m: jnp.einsum("ghp,gk->gpkh", m, eye).reshape(g * p, g * h).astype(BF16)
    lam = lambda m: jnp.broadcast_to(m.reshape(2, 1, g * p), (2, bsz, g * p))
    return bd_in(bb_re), bd_in(bb_im), lam(lb_re), lam(lb_im), bd_out(c_re), bd_out(c_im)


def s5_branch(z_ctx, z_lat, p):
    bsz, n_ctx, _ = z_ctx.shape
    u_tb = jnp.concatenate([z_ctx, z_lat], axis=1).transpose(1, 0, 2)
    bre, bim, lre, lim, cre, cim = s5_operators(p["s5_a_re"], p["s5_a_im"], p["s5_log_dt"], p["s5_b_re"],
                                                p["s5_b_im"], p["s5_c_re"], p["s5_c_im"], bsz)
    y = (s5_scan(u_tb, bre[0], bim[0], lre[0], lim[0], cre, cim, n_ctx, False)
         + s5_scan(u_tb, bre[1], bim[1], lre[1], lim[1], cre, cim, n_ctx, True))
    y = y.transpose(1, 0, 2)
    return y[:, :n_ctx], y[:, n_ctx:]


def _k_merge(yhy_ref, rwy_ref, rwb_ref, rwg_ref, s5y_ref, s5u_ref, zg_ref, x_ref, g1_ref, e_ref, lng_ref,
             lnb_ref, s5d_ref, gluw_ref, glub_ref, wb_ref, wo_ref, o_ref):
    w = MIX_WIDTH
    ones_bd = e_ref[...]
    y = rwy_ref[0] + rwb_ref[0]
    mu = _seg_sum(y, ones_bd) * (1.0 / RW_HEAD)
    yc = y - mu
    var = _seg_sum(yc * yc, ones_bd) * (1.0 / RW_HEAD)
    y_rw = (yc * lax.rsqrt(var + RW_LN_EPS) * lng_ref[...] + lnb_ref[...]) * rwg_ref[0]
    s = s5y_ref[0] + s5u_ref[0] * s5d_ref[...]
    s = 0.5 * s * (1.0 + jnp.tanh(math.sqrt(2.0 / math.pi) * (s + 0.044715 * (s * s * s))))
    lg = _dot(s.astype(BF16), gluw_ref[...]) + glub_ref[...]
    y_s5 = lg[:, 0:w] * _sigmoid(lg[:, w:2 * w])
    zg = zg_ref[0]
    d = D_MODEL
    m = (_sigmoid(zg[:, 0:d]) * _dot(yhy_ref[0].astype(BF16), wb_ref[0])
         + _sigmoid(zg[:, d:2 * d]) * _dot(y_rw.astype(BF16), wb_ref[1])
         + _sigmoid(zg[:, 2 * d:3 * d]) * _dot(y_s5.astype(BF16), wb_ref[2]))
    o_ref[0] = x_ref[0] + g1_ref[0] * _dot(m.astype(BF16), wo_ref[...])


def merge_residual(y_hy, rw, s5_y, s5_u, zg, x, g1, p, ones_bd):
    bsz, n, d = x.shape
    w = MIX_WIDTH
    tm = min(ROW_TILE, n)
    tile = lambda c: pl.BlockSpec((1, tm, c), lambda i, j: (i, j, 0))
    full = lambda shape: pl.BlockSpec(shape, lambda i, j: (0,) * len(shape))
    rw_y, rw_bonus, rw_g = rw
    return pl.pallas_call(
        _k_merge,
        grid=(bsz, n // tm),
        in_specs=[tile(w), tile(w), tile(w), tile(w), tile(w), tile(w), tile(N_BRANCH * d), tile(d),
                  pl.BlockSpec((1, 1, d), lambda i, j: (i, 0, 0)),
                  full((w, w)), full((1, w)), full((1, w)), full((1, w)), full((w, 2 * w)), full((1, 2 * w)),
                  full((N_BRANCH, w, d)), full((d, d))],
        out_specs=tile(d),
        out_shape=jax.ShapeDtypeStruct(x.shape, F32),
        compiler_params=_params("arbitrary", "arbitrary"),
        name="merge_residual",
    )(y_hy, rw_y, rw_bonus, rw_g, s5_y, s5_u, zg, x, g1, ones_bd, p["rw_ln_g"].reshape(1, w),
      p["rw_ln_b"].reshape(1, w), p["s5_d"].reshape(1, w), p["s5_glu_w"].astype(BF16),
      p["s5_glu_b"].reshape(1, 2 * w), p["w_branch"].astype(BF16), p["w_out"].astype(BF16))


def _k_router(x_ref, sh_ref, sc_ref, g_ref, rw_ref, rb_ref, h_o, idx_o, gate_o):
    h = _rms_modulate(x_ref[0], g_ref[...], sh_ref[0], sc_ref[0])
    h_o[0] = h.astype(BF16)
    logits = _dot_hi(h, rw_ref[...]) + rb_ref[...]
    lane = lax.broadcasted_iota(jnp.int32, logits.shape, 1)
    vals, idxs = [], []
    for _ in range(TOP_K):
        m = jnp.max(logits, axis=-1, keepdims=True)
        idx = jnp.min(jnp.where(logits == m, lane, LANES), axis=-1, keepdims=True)
        vals.append(m)
        idxs.append(idx)
        logits = jnp.where(lane == idx, -jnp.inf, logits)
    exps = [jnp.exp(v - vals[0]) for v in vals]
    inv = 1.0 / (exps[0] + exps[1] + exps[2] + exps[3])
    idx_out = jnp.zeros(logits.shape, jnp.int32)
    gate_out = jnp.zeros(logits.shape, F32)
    for j in range(TOP_K):
        idx_out = jnp.where(lane == j, idxs[j], idx_out)
        gate_out = jnp.where(lane == j, exps[j] * inv, gate_out)
    idx_o[0] = idx_out
    gate_o[0] = gate_out


def moe_route(x, shift, scale, g, router_w, router_b):
    bsz, n, d = x.shape
    tm = min(ROW_TILE, n)
    rw = jnp.pad(router_w, ((0, 0), (0, LANES - N_EXPERTS)))
    rb = jnp.pad(router_b, (0, LANES - N_EXPERTS), constant_values=-jnp.inf).reshape(1, LANES)
    tile = lambda c: pl.BlockSpec((1, tm, c), lambda i, j: (i, j, 0))
    vec = pl.BlockSpec((1, 1, d), lambda i, j: (i, 0, 0))
    full = lambda shape: pl.BlockSpec(shape, lambda i, j: (0,) * len(shape))
    return pl.pallas_call(
        _k_router,
        grid=(bsz, n // tm),
        in_specs=[tile(d), vec, vec, full((1, d)), full((d, LANES)), full((1, LANES))],
        out_specs=[tile(d), tile(LANES), tile(LANES)],
        out_shape=[jax.ShapeDtypeStruct((bsz, n, d), BF16), jax.ShapeDtypeStruct((bsz, n, LANES), jnp.int32),
                   jax.ShapeDtypeStruct((bsz, n, LANES), F32)],
        compiler_params=_params("arbitrary", "arbitrary"),
        name="moe_route",
    )(x, shift, scale, g.reshape(1, d), rw, rb)


def _k_expert(blk_e_ref, xs_ref, w1_ref, b1_ref, w2_ref, b2_ref, sw_ref, o_ref):
    del blk_e_ref
    hid = _dot(xs_ref[...], w1_ref[0]) + b1_ref[0]
    gl = jnp.minimum(hid[:, 0:D_EXPERT], SWIGLU_LIMIT)
    up = jnp.clip(hid[:, D_EXPERT:2 * D_EXPERT], -SWIGLU_LIMIT, SWIGLU_LIMIT)
    act = (up + 1.0) * gl * _sigmoid(SWIGLU_ALPHA * gl)
    o_ref[...] = (_dot(act.astype(BF16), w2_ref[0]) + b2_ref[0]) * sw_ref[...]


def expert_ffn(xs, blk_e, slot_w, w1, b1, w2, b2):
    n_slots, d = xs.shape
    n_blocks = n_slots // MOE_ROWS
    grid_spec = pltpu.PrefetchScalarGridSpec(
        num_scalar_prefetch=1,
        grid=(n_blocks,),
        in_specs=[pl.BlockSpec((MOE_ROWS, d), lambda i, e: (i, 0)),
                  pl.BlockSpec((1, d, 2 * D_EXPERT), lambda i, e: (e[i], 0, 0)),
                  pl.BlockSpec((1, 1, 2 * D_EXPERT), lambda i, e: (e[i], 0, 0)),
                  pl.BlockSpec((1, D_EXPERT, d), lambda i, e: (e[i], 0, 0)),
                  pl.BlockSpec((1, 1, d), lambda i, e: (e[i], 0, 0)),
                  pl.BlockSpec((MOE_ROWS, 1), lambda i, e: (i, 0))],
        out_specs=pl.BlockSpec((MOE_ROWS, d), lambda i, e: (i, 0)),
    )
    return pl.pallas_call(
        _k_expert,
        grid_spec=grid_spec,
        out_shape=jax.ShapeDtypeStruct((n_slots, d), F32),
        compiler_params=_params("arbitrary"),
        name="expert_ffn",
    )(blk_e, xs, w1, b1.reshape(N_EXPERTS, 1, -1), w2, b2.reshape(N_EXPERTS, 1, -1), slot_w.reshape(-1, 1))


def moe_ffn(x, shift, scale, g, router_w, router_b, w1, b1, w2, b2):
    bsz, n, d = x.shape
    h, idx, gate = moe_route(x, shift, scale, g, router_w, router_b)
    n_tok = bsz * n
    n_asg = n_tok * TOP_K
    flat_e = idx[..., :TOP_K].reshape(-1)
    flat_w = gate[..., :TOP_K].reshape(-1)
    flat_tok = jnp.repeat(jnp.arange(n_tok, dtype=jnp.int32), TOP_K)
    order = jnp.argsort(flat_e)
    sorted_e = flat_e[order]
    counts = jnp.bincount(flat_e, length=N_EXPERTS)
    padded = (counts + MOE_ROWS - 1) // MOE_ROWS * MOE_ROWS
    pad_end = jnp.cumsum(padded)
    pad_start = pad_end - padded
    grp_start = jnp.cumsum(counts) - counts
    dest = (pad_start[sorted_e] + jnp.arange(n_asg, dtype=jnp.int32) - grp_start[sorted_e]).astype(jnp.int32)
    n_blocks = -(-n_asg // MOE_ROWS) + N_EXPERTS
    n_slots = n_blocks * MOE_ROWS
    slot_tok = jnp.zeros((n_slots,), jnp.int32).at[dest].set(flat_tok[order])
    slot_w = jnp.zeros((n_slots,), F32).at[dest].set(flat_w[order])
    slot_of = jnp.zeros((n_asg,), jnp.int32).at[order].set(dest)
    blk_e = jnp.minimum(jnp.searchsorted(pad_end, jnp.arange(n_blocks) * MOE_ROWS, side="right"),
                        N_EXPERTS - 1).astype(jnp.int32)
    xs = h.reshape(n_tok, d)[slot_tok]
    ys = expert_ffn(xs, blk_e, slot_w, w1, b1, w2, b2)
    out = ys[slot_of].reshape(n_tok, TOP_K, d).sum(axis=1)
    return out.reshape(bsz, n, d)


def _k_rmsnorm(x_ref, g_ref, o_ref):
    x = x_ref[0]
    ms = jnp.mean(x * x, axis=-1, keepdims=True)
    o_ref[0] = x * lax.rsqrt(ms + NORM_EPS) * g_ref[...]


def final_rmsnorm(x, g):
    b, n, d = x.shape
    tm = min(2 * ROW_TILE, n)
    return pl.pallas_call(
        _k_rmsnorm,
        grid=(b, n // tm),
        in_specs=[pl.BlockSpec((1, tm, d), lambda i, j: (i, j, 0)), pl.BlockSpec((1, d), lambda i, j: (0, 0))],
        out_specs=pl.BlockSpec((1, tm, d), lambda i, j: (i, j, 0)),
        out_shape=jax.ShapeDtypeStruct(x.shape, F32),
        compiler_params=_params("arbitrary", "arbitrary"),
        name="final_rmsnorm",
    )(x, g.reshape(1, d))


@functools.lru_cache(maxsize=None)
def _grid_pos_embed(n_tokens):
    rows = n_tokens // GRID_W
    row_id, col_id = np.meshgrid(np.arange(rows), np.arange(GRID_W), indexing="ij")
    quarter = D_MODEL // 4
    omega = (1.0 / (10000.0 ** (np.arange(quarter, dtype=np.float32) / quarter))).astype(np.float32)

    def enc(pos):
        ang = pos.reshape(-1)[:, None].astype(np.float32) * omega
        return np.concatenate([np.sin(ang), np.cos(ang)], axis=-1)

    return np.concatenate([enc(row_id), enc(col_id)], axis=-1).astype(np.float32)


_LAYER_KEYS = ("ada_w", "ada_b", "norm1_g", "norm2_g", "w_in", "hy_conv_w", "hy_conv_b", "hy_w1", "hy_b1",
               "hy_f1", "hy_w2", "hy_b2", "hy_f2", "hy_w3", "hy_b3", "hy_bias", "rw_conv_w", "rw_conv_b",
               "rw_w_up", "rw_w0", "rw_a_up", "rw_a0", "rw_g_up", "rw_k_k", "rw_k_a", "rw_r_k", "rw_ln_g",
               "rw_ln_b", "s5_a_re", "s5_a_im", "s5_log_dt", "s5_b_re", "s5_b_im", "s5_c_re", "s5_c_im",
               "s5_d", "s5_glu_w", "s5_glu_b", "w_branch", "w_out", "router_w", "router_b", "moe_w1",
               "moe_b1", "moe_w2", "moe_b2")


def _token_mixer(x, xc, mod, mod_c, p, need_ctx, ones_bd):
    sh1, sc1, g1 = mod
    csh1, csc1, cg1 = mod_c
    w_in = p["w_in"].astype(BF16)
    c0, c1, c2 = HY_COLS, HY_COLS + RW_COLS, HY_COLS + RW_COLS + MIX_WIDTH
    proj = lambda t, s, c, lo, hi: norm_mod_matmul(t, s, c, p["norm1_g"], w_in[:, lo:hi])
    hy_args = (p["hy_w1"], p["hy_b1"], p["hy_f1"], p["hy_w2"], p["hy_b2"], p["hy_f2"], p["hy_w3"], p["hy_b3"],
               p["hy_bias"])

    z_hy = proj(x, sh1, sc1, 0, c0)
    y_hy = hyena_branch(z_hy, p["hy_conv_w"], p["hy_conv_b"], hyena_spectra(x.shape[1], *hy_args))
    u_rw = short_conv(proj(x, sh1, sc1, c0, c1), p["rw_conv_w"], p["rw_conv_b"])
    uc_rw = short_conv(proj(xc, csh1, csc1, c0, c1), p["rw_conv_w"], p["rw_conv_b"])
    rw_c, rw_l = rwkv_scan_branch(uc_rw, u_rw, p, ones_bd)
    z_s5 = proj(x, sh1, sc1, c1, c2)
    zc_s5 = proj(xc, csh1, csc1, c1, c2)
    s5_c, s5_l = s5_branch(zc_s5, z_s5, p)
    zg = proj(x, sh1, sc1, c2, c2 + N_BRANCH * D_MODEL)
    x_new = merge_residual(y_hy, rw_l, s5_l, z_s5, zg, x, g1, p, ones_bd)
    if not need_ctx:
        return x_new, None
    zc_hy = proj(xc, csh1, csc1, 0, c0)
    yc_hy = hyena_branch(zc_hy, p["hy_conv_w"], p["hy_conv_b"], hyena_spectra(xc.shape[1], *hy_args))
    zcg = proj(xc, csh1, csc1, c2, c2 + N_BRANCH * D_MODEL)
    xc_new = merge_residual(yc_hy, rw_c, s5_c, zc_s5, zcg, xc, cg1, p, ones_bd)
    return x_new, xc_new


def kernel(x, c, ctx, c_ctx, ada_w, ada_b, norm1_g, norm2_g, w_in, hy_conv_w, hy_conv_b, hy_w1, hy_b1, hy_f1, hy_w2, hy_b2, hy_f2, hy_w3, hy_b3, hy_bias, rw_conv_w, rw_conv_b, rw_w_up, rw_w0, rw_a_up, rw_a0, rw_g_up, rw_k_k, rw_k_a, rw_r_k, rw_ln_g, rw_ln_b, s5_a_re, s5_a_im, s5_log_dt, s5_b_re, s5_b_im, s5_c_re, s5_c_im, s5_d, s5_glu_w, s5_glu_b, w_branch, w_out, router_w, router_b, moe_w1, moe_b1, moe_w2, moe_b2, final_g):
    stacked = dict(zip(_LAYER_KEYS, (ada_w, ada_b, norm1_g, norm2_g, w_in, hy_conv_w, hy_conv_b, hy_w1, hy_b1,
                                     hy_f1, hy_w2, hy_b2, hy_f2, hy_w3, hy_b3, hy_bias, rw_conv_w, rw_conv_b,
                                     rw_w_up, rw_w0, rw_a_up, rw_a0, rw_g_up, rw_k_k, rw_k_a, rw_r_k, rw_ln_g,
                                     rw_ln_b, s5_a_re, s5_a_im, s5_log_dt, s5_b_re, s5_b_im, s5_c_re, s5_c_im,
                                     s5_d, s5_glu_w, s5_glu_b, w_branch, w_out, router_w, router_b, moe_w1,
                                     moe_b1, moe_w2, moe_b2)))
    bsz, n, d = x.shape
    depth = ada_w.shape[0]
    x = add_pos(x, jnp.asarray(_grid_pos_embed(n)))
    xc = ctx
    ones_bd = jnp.asarray(np.kron(np.eye(RW_HEADS), np.ones((RW_HEAD, RW_HEAD))), BF16)
    cvec = jnp.zeros((2 * SUBLANES, d), F32).at[:bsz].set(c).at[bsz].set(c_ctx)
    for l in range(depth):
        p = {k: v[l] for k, v in stacked.items()}
        need_ctx = l < depth - 1
        ada = ada_proj(cvec, p["ada_w"], p["ada_b"])
        lat = [t.reshape(bsz, 1, d) for t in jnp.split(ada[:bsz], 6, axis=-1)]
        cx = [jnp.broadcast_to(t.reshape(1, 1, d), (bsz, 1, d)) for t in jnp.split(ada[bsz], 6, axis=-1)]
        x, xc_new = _token_mixer(x, xc, lat[0:3], cx[0:3], p, need_ctx, ones_bd)
        w1, w2 = p["moe_w1"].astype(BF16), p["moe_w2"].astype(BF16)
        moe = functools.partial(moe_ffn, g=p["norm2_g"], router_w=p["router_w"], router_b=p["router_b"],
                                w1=w1, b1=p["moe_b1"], w2=w2, b2=p["moe_b2"])
        x = x + lat[5] * moe(x, lat[3], lat[4])
        if need_ctx:
            xc = xc_new + cx[5] * moe(xc_new, cx[3], cx[4])
    return final_rmsnorm(x, final_g)
```

```python
import functools
import math

import numpy as np
import jax
import jax.numpy as jnp
from jax import lax
from jax.experimental import pallas as pl
from jax.experimental.pallas import tpu as pltpu

F32 = jnp.float32
BF16 = jnp.bfloat16

D_MODEL = 1024
GRID_W = 64
SHORT_CONV = 3
NORM_EPS = 1e-6
N_BRANCH = 3
MIX_WIDTH = 512

HY_ORDER = 2
HY_BANDS = 16
HY_DECAY_TARGET = 1e-2
HY_FAST_PCT = 0.3
HY_SLOW_PCT = 1.5
HY_COLS = (HY_ORDER + 1) * MIX_WIDTH

RW_HEAD = 64
RW_HEADS = MIX_WIDTH // RW_HEAD
RW_DECAY_RANK = 64
RW_ICLR_RANK = 64
RW_GATE_RANK = 128
RW_LN_EPS = 64e-5
RW_COLS = 3 * MIX_WIDTH + RW_DECAY_RANK + RW_ICLR_RANK + RW_GATE_RANK

S5_GROUP = 16
S5_GROUPS = MIX_WIDTH // S5_GROUP
S5_STATE = 64
S5_HID = S5_GROUPS * S5_STATE

N_EXPERTS = 32
TOP_K = 4
D_EXPERT = 1024
SWIGLU_LIMIT = 7.0
SWIGLU_ALPHA = 1.702

LANES = 128
SUBLANES = 8
VMEM_LIMIT_BYTES = 56 * 1024 * 1024

ROW_TILE = 256
MOE_ROWS = 256
WKV_STEPS = 32
S5_STEPS = 64
S5_LANE_CHUNK = 512


def _params(*sem):
    return pltpu.CompilerParams(dimension_semantics=sem, vmem_limit_bytes=VMEM_LIMIT_BYTES)


def _dot(a, b):
    return jnp.dot(a, b, preferred_element_type=F32)


def _split3(x):
    hi = x.astype(BF16)
    r1 = x - hi.astype(F32)
    mid = r1.astype(BF16)
    lo = (r1 - mid.astype(F32)).astype(BF16)
    return hi, mid, lo


def _dot_hi(a, b):
    a0, a1, a2 = _split3(a)
    b0, b1, b2 = _split3(b)
    return (_dot(a0, b0) + (_dot(a0, b1) + _dot(a1, b0))
            + (_dot(a1, b1) + _dot(a0, b2) + _dot(a2, b0)))


def _seg_sum(x, ones_bd):
    hi, mid, lo = _split3(x)
    return _dot(hi, ones_bd) + _dot(mid, ones_bd) + _dot(lo, ones_bd)


def _sigmoid(x):
    return 1.0 / (1.0 + jnp.exp(-x))


def _rms_modulate(x, g, shift, scale):
    ms = jnp.mean(x * x, axis=-1, keepdims=True)
    y = x * lax.rsqrt(ms + NORM_EPS) * g
    return y * (1.0 + scale) + shift


def _k_ada(c_ref, w_ref, b_ref, o_ref):
    c = c_ref[...]
    s = c * _sigmoid(c)
    o_ref[...] = _dot_hi(s, w_ref[...]) + b_ref[...]


def ada_proj(cvec, w, b):
    rows, d = cvec.shape
    n = w.shape[1]
    tn = 1536
    return pl.pallas_call(
        _k_ada,
        grid=(n // tn,),
        in_specs=[pl.BlockSpec((rows, d), lambda j: (0, 0)),
                  pl.BlockSpec((d, tn), lambda j: (0, j)),
                  pl.BlockSpec((1, tn), lambda j: (0, j))],
        out_specs=pl.BlockSpec((rows, tn), lambda j: (0, j)),
        out_shape=jax.ShapeDtypeStruct((rows, n), F32),
        compiler_params=_params("arbitrary"),
        name="ada_proj",
    )(cvec, w, b.reshape(1, n))


def _k_add_pos(x_ref, p_ref, o_ref):
    o_ref[0] = x_ref[0] + p_ref[...]


def add_pos(x, pos):
    b, n, d = x.shape
    tm = min(ROW_TILE * 2, n)
    return pl.pallas_call(
        _k_add_pos,
        grid=(b, n // tm),
        in_specs=[pl.BlockSpec((1, tm, d), lambda i, j: (i, j, 0)),
                  pl.BlockSpec((tm, d), lambda i, j: (j, 0))],
        out_specs=pl.BlockSpec((1, tm, d), lambda i, j: (i, j, 0)),
        out_shape=jax.ShapeDtypeStruct(x.shape, F32),
        compiler_params=_params("arbitrary", "arbitrary"),
        name="add_pos",
    )(x, pos)


def _k_norm_mm(x_ref, sh_ref, sc_ref, g_ref, w_ref, o_ref, h_ref):
    @pl.when(pl.program_id(2) == 0)
    def _():
        h = _rms_modulate(x_ref[0], g_ref[...], sh_ref[0], sc_ref[0])
        h_ref[...] = h.astype(BF16)

    o_ref[0] = _dot(h_ref[...], w_ref[...])


def norm_mod_matmul(x, shift, scale, g, w):
    b, n, d = x.shape
    ncol = w.shape[1]
    tm = min(2 * ROW_TILE, n)
    tn = ncol if ncol <= 1792 else 1536
    return pl.pallas_call(
        _k_norm_mm,
        grid=(b, n // tm, ncol // tn),
        in_specs=[pl.BlockSpec((1, tm, d), lambda i, j, k: (i, j, 0)),
                  pl.BlockSpec((1, 1, d), lambda i, j, k: (i, 0, 0)),
                  pl.BlockSpec((1, 1, d), lambda i, j, k: (i, 0, 0)),
                  pl.BlockSpec((1, d), lambda i, j, k: (0, 0)),
                  pl.BlockSpec((d, tn), lambda i, j, k: (0, k))],
        out_specs=pl.BlockSpec((1, tm, tn), lambda i, j, k: (i, j, k)),
        out_shape=jax.ShapeDtypeStruct((b, n, ncol), F32),
        scratch_shapes=[pltpu.VMEM((tm, d), BF16)],
        compiler_params=_params("arbitrary", "arbitrary", "arbitrary"),
        name="norm_mod_matmul",
    )(x, shift, scale, g.reshape(1, d), w)


def _short_conv_val(z, w, b):
    n = z.shape[0]
    row = lax.broadcasted_iota(jnp.int32, z.shape, 0)
    zm = jnp.where(row == 0, 0.0, pltpu.roll(z, 1, 0))
    zp = jnp.where(row == n - 1, 0.0, pltpu.roll(z, n - 1, 0))
    return zm * w[0:1] + z * w[1:2] + zp * w[2:3] + b


def _k_sconv(z_ref, w_ref, b_ref, o_ref):
    o_ref[0] = _short_conv_val(z_ref[0], w_ref[...], b_ref[...])


def short_conv(z, w, b):
    bsz, n, c = z.shape
    tc = 256 if c % 256 == 0 else LANES
    return pl.pallas_call(
        _k_sconv,
        grid=(bsz, c // tc),
        in_specs=[pl.BlockSpec((1, n, tc), lambda i, j: (i, 0, j)),
                  pl.BlockSpec((SHORT_CONV, tc), lambda i, j: (0, j)),
                  pl.BlockSpec((1, tc), lambda i, j: (0, j))],
        out_specs=pl.BlockSpec((1, n, tc), lambda i, j: (i, 0, j)),
        out_shape=jax.ShapeDtypeStruct(z.shape, F32),
        compiler_params=_params("arbitrary", "arbitrary"),
        name="short_conv",
    )(z, w, b.reshape(1, c))


def _fft_split(n):
    total = 2 * n
    bits = total.bit_length() - 1
    assert 1 << bits == total
    n1 = 1 << ((bits + 1) // 2)
    n1 = max(n1, 2 * SUBLANES)
    return n1, total // n1


@functools.lru_cache(maxsize=None)
def _dft_mats(n):
    n1, n2 = _fft_split(n)
    total = n1 * n2
    k1 = np.arange(n1)[:, None]
    m1 = np.arange(n1)[None, :]
    m2 = np.arange(n2)[:, None, None]
    ang = -2.0 * np.pi * ((n2 * k1 * m1)[None] + m2 * k1[None]) / total
    fa = np.concatenate([np.cos(ang), np.sin(ang)], axis=1)
    k2 = np.arange(n2)[:, None]
    mm = np.arange(n2)[None, :]
    gang = -2.0 * np.pi * k2 * mm / n2
    gr, gi = np.cos(gang), np.sin(gang)
    fb = np.block([[gr, -gi], [gi, gr]])
    fc = np.block([[gr, gi], [-gi, gr]])
    m1d = np.arange(n1 // 2)[:, None]
    k1d = np.arange(n1)[None, :]
    dang = 2.0 * np.pi * ((n2 * m1d * k1d)[None] + m2 * k1d[None]) / total
    fd = np.concatenate([np.cos(dang), -np.sin(dang)], axis=2) / total
    return (fa.astype(np.float32), fb.astype(np.float32), fc.astype(np.float32), fd.astype(np.float32))


def _time_pitch(n2):
    return n2 + SUBLANES


def _spec_pitch(n2):
    return 2 * n2 + SUBLANES


def _fft_stage_a(src_ref, y_ref, fa_ref, n1, n2, k_rows):
    tp, sp = _time_pitch(n2), _spec_pitch(n2)

    def body(m2, c):
        xs = src_ref[pl.ds(m2, k_rows, stride=tp), :].astype(BF16)
        res = _dot(fa_ref[m2, :, 0:k_rows], xs)
        y_ref[pl.ds(m2, n1, stride=sp), :] = res[0:n1]
        y_ref[pl.ds(n2 + m2, n1, stride=sp), :] = res[n1:2 * n1]
        return c
    lax.fori_loop(0, n2, body, 0, unroll=2)


def _long_conv(buf_ref, y_ref, h_ref, fa_ref, fb_ref, fc_ref, fd_ref, n1, n2):
    tp, sp = _time_pitch(n2), _spec_pitch(n2)
    _fft_stage_a(buf_ref, y_ref, fa_ref, n1, n2, n1 // 2)

    def body_k1(k1, c):
        rows = pl.ds(pl.multiple_of(k1 * sp, SUBLANES), 2 * n2)
        z = _dot(fb_ref[...], y_ref[rows, :].astype(BF16))
        zr, zi = z[0:n2], z[n2:2 * n2]
        h = h_ref[k1]
        hr, hi = h[0:n2], h[n2:2 * n2]
        p = jnp.concatenate([zr * hr - zi * hi, zr * hi + zi * hr], axis=0)
        y_ref[rows, :] = _dot(fc_ref[...], p.astype(BF16))
        return c
    lax.fori_loop(0, n1, body_k1, 0, unroll=4)

    def body_m2(m2, c):
        qr = y_ref[pl.ds(m2, n1, stride=sp), :]
        qi = y_ref[pl.ds(n2 + m2, n1, stride=sp), :]
        q = jnp.concatenate([qr, qi], axis=0).astype(BF16)
        buf_ref[pl.ds(m2, n1 // 2, stride=tp), :] = _dot(fd_ref[m2], q)
        return c
    lax.fori_loop(0, n2, body_m2, 0, unroll=2)


def _store_time_blocks(buf_ref, val, n2):
    tp = _time_pitch(n2)
    for m1 in range(val.shape[0] // n2):
        buf_ref[m1 * tp:m1 * tp + n2, :] = val[m1 * n2:(m1 + 1) * n2]


def _load_time_blocks(buf_ref, n, n2):
    tp = _time_pitch(n2)
    return jnp.concatenate([buf_ref[m1 * tp:m1 * tp + n2, :] for m1 in range(n // n2)], axis=0)


def _k_hyena(x1_ref, x2_ref, v_ref, w1_ref, w2_ref, wv_ref, b1_ref, b2_ref, bv_ref, h1_ref, h2_ref,
             fa_ref, fb_ref, fc_ref, fd_ref, o_ref, buf_ref, y_ref, *, n1, n2):
    n = x1_ref.shape[1]
    conv = functools.partial(_long_conv, buf_ref, y_ref, fa_ref=fa_ref, fb_ref=fb_ref, fc_ref=fc_ref,
                             fd_ref=fd_ref, n1=n1, n2=n2)
    _store_time_blocks(buf_ref, _short_conv_val(v_ref[0], wv_ref[...], bv_ref[...]), n2)
    conv(h1_ref)
    gated = _short_conv_val(x1_ref[0], w1_ref[...], b1_ref[...]) * _load_time_blocks(buf_ref, n, n2)
    _store_time_blocks(buf_ref, gated, n2)
    conv(h2_ref)
    o_ref[0] = _short_conv_val(x2_ref[0], w2_ref[...], b2_ref[...]) * _load_time_blocks(buf_ref, n, n2)


def _k_fspec(f_ref, fa_ref, fb_ref, o_ref, y_ref, *, n1, n2):
    sp = _spec_pitch(n2)
    _fft_stage_a(f_ref, y_ref, fa_ref, n1, n2, n1)

    def body_k1(k1, c):
        src = pl.ds(pl.multiple_of(k1 * sp, SUBLANES), 2 * n2)
        dst = pl.ds(pl.multiple_of(k1 * 2 * n2, 2 * n2), 2 * n2)
        o_ref[dst, :] = _dot(fb_ref[...], y_ref[src, :].astype(BF16))
        return c
    lax.fori_loop(0, n1, body_k1, 0, unroll=2)


def filter_spectrum(filt, n):
    n1, n2 = _fft_split(n)
    total, c = filt.shape
    tp, sp = _time_pitch(n2), _spec_pitch(n2)
    fa, fb, _, _ = _dft_mats(n)
    blocks = jnp.pad(filt.reshape(n1, n2, c), ((0, 0), (0, tp - n2), (0, 0))).reshape(n1 * tp, c)
    out = pl.pallas_call(
        functools.partial(_k_fspec, n1=n1, n2=n2),
        grid=(c // LANES,),
        in_specs=[pl.BlockSpec((n1 * tp, LANES), lambda j: (0, j)),
                  pl.BlockSpec((n2, 2 * n1, n1), lambda j: (0, 0, 0)),
                  pl.BlockSpec((2 * n2, 2 * n2), lambda j: (0, 0))],
        out_specs=pl.BlockSpec((total * 2, LANES), lambda j: (0, j)),
        out_shape=jax.ShapeDtypeStruct((2 * total, c), F32),
        scratch_shapes=[pltpu.VMEM((n1 * sp, LANES), F32)],
        compiler_params=_params("arbitrary"),
        name="filter_spectrum",
    )(blocks, jnp.asarray(fa, BF16), jnp.asarray(fb, BF16))
    return out.reshape(n1, 2 * n2, c)


def _k_filter_mlp(feat_ref, win_ref, w1_ref, b1_ref, f1_ref, w2_ref, b2_ref, f2_ref, w3_ref, b3_ref, o_ref):
    h = jnp.sin(f1_ref[...] * (_dot_hi(feat_ref[...], w1_ref[...]) + b1_ref[...]))
    h = jnp.sin(f2_ref[...] * (_dot_hi(h, w2_ref[...]) + b2_ref[...]))
    o_ref[...] = (_dot_hi(h, w3_ref[...]) + b3_ref[...]) * win_ref[...]


@functools.lru_cache(maxsize=None)
def _filter_consts(n):
    t = np.linspace(0.0, 1.0, n, dtype=np.float32)[:, None]
    bands = np.linspace(1e-4, HY_BANDS - 1, HY_BANDS, dtype=np.float32)
    ang = (np.float32(2 * math.pi) * np.arange(n, dtype=np.float32) / np.float32(n))[:, None] * bands
    feats = np.concatenate([t, np.cos(ang), -np.sin(ang)], axis=-1).astype(np.float32)
    pad = (-feats.shape[1]) % SUBLANES
    feats = np.pad(feats, ((0, 0), (0, pad)))
    deltas = np.abs(np.linspace(math.log(HY_DECAY_TARGET) / HY_SLOW_PCT,
                                math.log(HY_DECAY_TARGET) / HY_FAST_PCT, MIX_WIDTH, dtype=np.float32))
    window = np.exp(-t * deltas).astype(np.float32)
    return feats, np.tile(window, (1, 2 * HY_ORDER))


def hyena_filter_table(n, w1, b1, f1, w2, b2, f2, w3, b3):
    feats, window = _filter_consts(n)
    kf = feats.shape[1]
    fd = w1.shape[1]
    ncol = w3.shape[1]
    w1p = jnp.pad(w1, ((0, kf - w1.shape[0]), (0, 0)))
    tm = min(ROW_TILE, n)
    full = lambda shape: pl.BlockSpec(shape, lambda i: (0,) * len(shape))
    return pl.pallas_call(
        _k_filter_mlp,
        grid=(n // tm,),
        in_specs=[pl.BlockSpec((tm, kf), lambda i: (i, 0)),
                  pl.BlockSpec((tm, ncol), lambda i: (i, 0)),
                  full((kf, fd)), full((1, fd)), full((1, fd)),
                  full((fd, fd)), full((1, fd)), full((1, fd)),
                  full((fd, ncol)), full((1, ncol))],
        out_specs=pl.BlockSpec((tm, ncol), lambda i: (i, 0)),
        out_shape=jax.ShapeDtypeStruct((n, ncol), F32),
        compiler_params=_params("arbitrary"),
        name="hyena_filter_mlp",
    )(jnp.asarray(feats), jnp.asarray(window), w1p, b1.reshape(1, fd), f1.reshape(1, fd),
      w2, b2.reshape(1, fd), f2.reshape(1, fd), w3, b3.reshape(1, ncol))


def hyena_spectra(n, w1, b1, f1, w2, b2, f2, w3, b3, bias):
    hf = hyena_filter_table(n, w1, b1, f1, w2, b2, f2, w3, b3).reshape(n, HY_ORDER, 2, MIX_WIDTH)
    fwd = hf[:, :, 0]
    bwd = hf[1:, :, 1][::-1]
    fwd = fwd.at[0].add(bias)
    filt = jnp.concatenate([fwd, jnp.zeros((1, HY_ORDER, MIX_WIDTH), F32), bwd], axis=0)
    spec = filter_spectrum(filt.reshape(2 * n, HY_ORDER * MIX_WIDTH), n)
    n1, n2 = _fft_split(n)
    return spec.reshape(n1, 2 * n2, HY_ORDER, MIX_WIDTH).transpose(2, 0, 1, 3)


def hyena_branch(z, conv_w, conv_b, spectra):
    bsz, n, _ = z.shape
    n1, n2 = _fft_split(n)
    fa, fb, fc, fd = (jnp.asarray(m, BF16) for m in _dft_mats(n))
    fa = fa[:, :, : n1 // 2]
    nt = MIX_WIDTH // LANES
    cb = conv_b.reshape(1, HY_COLS)
    zspec = lambda off: pl.BlockSpec((1, n, LANES), lambda j, i: (i, 0, off + j))
    wspec = lambda off: pl.BlockSpec((SHORT_CONV, LANES), lambda j, i: (0, off + j))
    bspec = lambda off: pl.BlockSpec((1, LANES), lambda j, i: (0, off + j))
    once = pl.Buffered(1)
    hspec = lambda o: pl.BlockSpec((None, n1, 2 * n2, LANES), lambda j, i: (o, 0, 0, j), pipeline_mode=once)
    full = lambda shape: pl.BlockSpec(shape, lambda j, i: (0,) * len(shape), pipeline_mode=once)
    return pl.pallas_call(
        functools.partial(_k_hyena, n1=n1, n2=n2),
        grid=(nt, bsz),
        in_specs=[zspec(0), zspec(nt), zspec(2 * nt), wspec(0), wspec(nt), wspec(2 * nt),
                  bspec(0), bspec(nt), bspec(2 * nt), hspec(0), hspec(1),
                  full(fa.shape), full(fb.shape), full(fc.shape), full(fd.shape)],
        out_specs=pl.BlockSpec((1, n, LANES), lambda j, i: (i, 0, j)),
        out_shape=jax.ShapeDtypeStruct((bsz, n, MIX_WIDTH), F32),
        scratch_shapes=[pltpu.VMEM((n1 // 2 * _time_pitch(n2), LANES), F32),
                        pltpu.VMEM((n1 * _spec_pitch(n2), LANES), F32)],
        compiler_params=_params("arbitrary", "arbitrary"),
        name="hyena_long_conv",
    )(z, z, z, conv_w, conv_w, conv_w, cb, cb, cb, spectra, spectra, fa, fb, fc, fd)


def _k_rwprep(u_ref, gup_ref, wup_ref, w0_ref, aup_ref, a0_ref, kk_ref, ka_ref, rk_ref, e_ref,
              r_o, v_o, g_o, nkk_o, bonus_o, dec0_o, dec1_o, kd0_o, kd1_o, b0_o, b1_o):
    w = MIX_WIDTH
    u = u_ref[0]
    r, k, v = u[:, 0:w], u[:, w:2 * w], u[:, 2 * w:3 * w]
    xw = u[:, 3 * w:3 * w + RW_DECAY_RANK]
    xa = u[:, 3 * w + RW_DECAY_RANK:3 * w + RW_DECAY_RANK + RW_ICLR_RANK]
    xg = u[:, 3 * w + RW_DECAY_RANK + RW_ICLR_RANK:]
    ones_bd = e_ref[...]
    g_o[0] = _dot(_sigmoid(xg).astype(BF16), gup_ref[...])
    kk = k * kk_ref[...]
    kk = kk * lax.rsqrt(jnp.maximum(_seg_sum(kk * kk, ones_bd), 1e-24))
    txw = jnp.tanh(xw)
    kd_sum = None
    for d, (dec_o, kd_o, b_o) in enumerate(((dec0_o, kd0_o, b0_o), (dec1_o, kd1_o, b1_o))):
        x = -(w0_ref[d] + _dot_hi(txw, wup_ref[d]))
        softplus = jnp.maximum(x, 0.0) + jnp.log(1.0 + jnp.exp(-jnp.abs(x)))
        dec_o[0] = jnp.exp(-jnp.exp(-softplus - 0.5))
        a = _sigmoid(a0_ref[d] + _dot_hi(xa, aup_ref[d]))
        kd = k * (1.0 + (a - 1.0) * ka_ref[...])
        kd_o[0] = kd
        b_o[0] = kk * a
        kd_sum = kd if kd_sum is None else kd_sum + kd
    r_o[0] = r
    v_o[0] = v
    nkk_o[0] = -kk
    bonus_o[0] = _seg_sum(r * kd_sum * rk_ref[...], ones_bd) * v


def rwkv_prep(u, g_up, w_up, w0, a_up, a0, k_k, k_a, r_k, ones_bd):
    bsz, n, cols = u.shape
    w = MIX_WIDTH
    tm = min(ROW_TILE, n)
    full = lambda shape: pl.BlockSpec(shape, lambda i, j: (0,) * len(shape))
    tile = pl.BlockSpec((1, tm, w), lambda i, j: (i, j, 0))
    return pl.pallas_call(
        _k_rwprep,
        grid=(bsz, n // tm),
        in_specs=[pl.BlockSpec((1, tm, cols), lambda i, j: (i, j, 0)),
                  full((RW_GATE_RANK, w)), full((2, RW_DECAY_RANK, w)), full((2, 1, w)),
                  full((2, RW_ICLR_RANK, w)), full((2, 1, w)), full((1, w)), full((1, w)), full((1, w)),
                  full((w, w))],
        out_specs=[tile] * 11,
        out_shape=[jax.ShapeDtypeStruct((bsz, n, w), F32)] * 11,
        compiler_params=_params("arbitrary", "arbitrary"),
        name="rwkv_prep",
    )(u, g_up.astype(BF16), w_up, w0.reshape(2, 1, w), a_up, a0.reshape(2, 1, w), k_k.reshape(1, w),
      k_a.reshape(1, w), r_k.reshape(1, w), ones_bd)


def _k_wkv(r_ref, w_ref, k_ref, v_ref, a_ref, b_ref, y_ref, s_ref, *, steps):
    @pl.when(pl.program_id(0) == 0)
    def _():
        s_ref[...] = jnp.zeros_like(s_ref)

    def step(t, c):
        sa = jnp.zeros(s_ref.shape[1:], F32)
        for i in range(RW_HEAD):
            sa = sa + s_ref[i] * a_ref[t, pl.ds(i, 1), :]
        vv = v_ref[t]
        y = jnp.zeros(s_ref.shape[1:], F32)
        for i in range(RW_HEAD):
            s_new = (s_ref[i] * w_ref[t, pl.ds(i, 1), :] + sa * b_ref[t, pl.ds(i, 1), :]
                     + vv * k_ref[t, pl.ds(i, 1), :])
            s_ref[i] = s_new
            y = y + s_new * r_ref[t, pl.ds(i, 1), :]
        y_ref[t] = y
        return c
    lax.fori_loop(0, steps, step, 0)


def wkv_scan(r, w, k, v, a, b):
    t_total, _, lanes = r.shape
    steps = WKV_STEPS
    spec = pl.BlockSpec((steps, RW_HEAD, lanes), lambda i: (i, 0, 0))
    return pl.pallas_call(
        functools.partial(_k_wkv, steps=steps),
        grid=(t_total // steps,),
        in_specs=[spec] * 6,
        out_specs=spec,
        out_shape=jax.ShapeDtypeStruct(r.shape, F32),
        scratch_shapes=[pltpu.VMEM((RW_HEAD, RW_HEAD, lanes), F32)],
        compiler_params=_params("arbitrary"),
        name="wkv_scan",
    )(r, w, k, v, a, b)


def _scan_layout(ctx_f, lat_f, ctx_b, lat_b):
    bsz = lat_f.shape[0]
    seq_f = jnp.concatenate([ctx_f, lat_f], axis=1)
    seq_b = jnp.concatenate([ctx_b[:, ::-1], lat_b[:, ::-1]], axis=1)
    st = jnp.stack([seq_f, seq_b], axis=0)
    t_total = st.shape[2]
    st = st.reshape(2, bsz, t_total, RW_HEADS, RW_HEAD)
    return st.transpose(2, 4, 0, 1, 3).reshape(t_total, RW_HEAD, 2 * bsz * RW_HEADS)


def _scan_unlayout(y, bsz, n_ctx):
    t_total = y.shape[0]
    y = y.reshape(t_total, RW_HEAD, 2, bsz, RW_HEADS).transpose(2, 3, 0, 4, 1)
    y = y.reshape(2, bsz, t_total, MIX_WIDTH)
    y_ctx = y[0, :, :n_ctx] + y[1, :, :n_ctx][:, ::-1]
    y_lat = y[0, :, n_ctx:] + y[1, :, n_ctx:][:, ::-1]
    return y_ctx, y_lat


def rwkv_scan_branch(u_ctx, u_lat, p, ones_bd):
    args = (p["rw_g_up"], p["rw_w_up"], p["rw_w0"], p["rw_a_up"], p["rw_a0"], p["rw_k_k"], p["rw_k_a"],
            p["rw_r_k"].reshape(-1), ones_bd)
    pc = rwkv_prep(u_ctx, *args)
    pq = rwkv_prep(u_lat, *args)
    names = ("r", "v", "g", "nkk", "bonus", "dec0", "dec1", "kd0", "kd1", "b0", "b1")
    c = dict(zip(names, pc))
    q = dict(zip(names, pq))
    both = lambda f, b: _scan_layout(c[f], q[f], c[b], q[b])
    y = wkv_scan(both("r", "r"), both("dec0", "dec1"), both("kd0", "kd1"), both("v", "v"),
                 both("nkk", "nkk"), both("b0", "b1"))
    y_ctx, y_lat = _scan_unlayout(y, u_lat.shape[0], u_ctx.shape[1])
    return (y_ctx, c["bonus"], c["g"]), (y_lat, q["bonus"], q["g"])


def _k_s5(u_ref, bre_ref, bim_ref, lre_ref, lim_ref, cre_ref, cim_ref, y_ref, hre_ref, him_ref, st_ref,
          *, steps, reverse):
    @pl.when(pl.program_id(0) == 0)
    def _():
        st_ref[...] = jnp.zeros_like(st_ref)

    bsz = u_ref.shape[1]
    u = u_ref[...].reshape(steps * bsz, MIX_WIDTH).astype(BF16)
    hre_ref[...] = _dot(u, bre_ref[...])
    him_ref[...] = _dot(u, bim_ref[...])
    for ch in range(S5_HID // S5_LANE_CHUNK):
        cols = slice(ch * S5_LANE_CHUNK, (ch + 1) * S5_LANE_CHUNK)
        lre = lre_ref[:, cols]
        lim = lim_ref[:, cols]

        def step(i, carry):
            hr, hi = carry
            t = (steps - 1 - i) if reverse else i
            rows = pl.ds(pl.multiple_of(t * bsz, bsz), bsz)
            nr = lre * hr - lim * hi + hre_ref[rows, cols]
            ni = lre * hi + lim * hr + him_ref[rows, cols]
            hre_ref[rows, cols] = nr
            him_ref[rows, cols] = ni
            return nr, ni
        hr, hi = lax.fori_loop(0, steps, step, (st_ref[0, :, cols], st_ref[1, :, cols]))
        st_ref[0, :, cols] = hr
        st_ref[1, :, cols] = hi
    y = _dot(hre_ref[...].astype(BF16), cre_ref[...]) - _dot(him_ref[...].astype(BF16), cim_ref[...])
    y_ref[...] = y.reshape(steps, bsz, MIX_WIDTH)


def s5_scan(u_tb, bre, bim, lre, lim, cre, cim, n_ctx, reverse):
    t_total, bsz, w = u_tb.shape
    steps = S5_STEPS
    nc = n_ctx // steps
    nb = t_total // steps
    if reverse:
        idx = lambda i: (jnp.where(i < nc, nc - 1 - i, nb - 1 + nc - i), 0, 0)
    else:
        idx = lambda i: (i, 0, 0)
    full = lambda shape: pl.BlockSpec(shape, lambda i: (0,) * len(shape))
    return pl.pallas_call(
        functools.partial(_k_s5, steps=steps, reverse=reverse),
        grid=(nb,),
        in_specs=[pl.BlockSpec((steps, bsz, w), idx),
                  full((w, S5_HID)), full((w, S5_HID)), full((bsz, S5_HID)), full((bsz, S5_HID)),
                  full((S5_HID, w)), full((S5_HID, w))],
        out_specs=pl.BlockSpec((steps, bsz, w), idx),
        out_shape=jax.ShapeDtypeStruct(u_tb.shape, F32),
        scratch_shapes=[pltpu.VMEM((steps * bsz, S5_HID), F32), pltpu.VMEM((steps * bsz, S5_HID), F32),
                        pltpu.VMEM((2, bsz, S5_HID), F32)],
        compiler_params=_params("arbitrary"),
        name="s5_scan",
    )(u_tb, bre, bim, lre, lim, cre, cim)


def _k_s5disc(are_ref, aim_ref, ldt_ref, lre_ref, lim_ref, cr_ref, ci_ref):
    a_re, a_im = are_ref[...], aim_ref[...]
    dt = jnp.exp(ldt_ref[...])
    mag = jnp.exp(a_re * dt)
    lb_re, lb_im = mag * jnp.cos(a_im * dt), mag * jnp.sin(a_im * dt)
    den = a_re * a_re + a_im * a_im
    nr = lb_re - 1.0
    lre_ref[...] = lb_re
    lim_ref[...] = lb_im
    cr_ref[...] = (nr * a_re + lb_im * a_im) / den
    ci_ref[...] = (lb_im * a_re - nr * a_im) / den


def s5_operators(a_re, a_im, log_dt, b_re, b_im, c_re, c_im, bsz):
    g, p, h = S5_GROUPS, S5_STATE, S5_GROUP
    shp = jax.ShapeDtypeStruct((2 * g, p), F32)
    lb_re, lb_im, cr, ci = pl.pallas_call(
        _k_s5disc, out_shape=[shp] * 4, name="s5_discretise",
    )(a_re.reshape(2 * g, p), a_im.reshape(2 * g, p), jnp.broadcast_to(log_dt.reshape(2 * g, 1), (2 * g, p)))
    cr, ci = cr.reshape(2, g, p, 1), ci.reshape(2, g, p, 1)
    bb_re = cr * b_re[None] - ci * b_im[None]
    bb_im = cr * b_im[None] + ci * b_re[None]
    eye = jnp.eye(g, dtype=F32)
    bd_in = lambda m: jnp.einsum("dgph,gk->dghkp", m, eye).reshape(2, g * h, g * p).astype(BF16)
    bd_out = lambda m: jnp.einsum("ghp,gk->gpkh", m, eye).reshape(g * p, g * h).astype(BF16)
    lam = lambda m: jnp.broadcast_to(m.reshape(2, 1, g * p), (2, bsz, g * p))
    return bd_in(bb_re), bd_in(bb_im), lam(lb_re), lam(lb_im), bd_out(c_re), bd_out(c_im)


def s5_branch(z_ctx, z_lat, p):
    bsz, n_ctx, _ = z_ctx.shape
    u_tb = jnp.concatenate([z_ctx, z_lat], axis=1).transpose(1, 0, 2)
    bre, bim, lre, lim, cre, cim = s5_operators(p["s5_a_re"], p["s5_a_im"], p["s5_log_dt"], p["s5_b_re"],
                                                p["s5_b_im"], p["s5_c_re"], p["s5_c_im"], bsz)
    y = (s5_scan(u_tb, bre[0], bim[0], lre[0], lim[0], cre, cim, n_ctx, False)
         + s5_scan(u_tb, bre[1], bim[1], lre[1], lim[1], cre, cim, n_ctx, True))
    y = y.transpose(1, 0, 2)
    return y[:, :n_ctx], y[:, n_ctx:]


def _k_merge(yhy_ref, rwy_ref, rwb_ref, rwg_ref, s5y_ref, s5u_ref, zg_ref, x_ref, g1_ref, e_ref, lng_ref,
             lnb_ref, s5d_ref, gluw_ref, glub_ref, wb_ref, wo_ref, o_ref):
    w = MIX_WIDTH
    ones_bd = e_ref[...]
    y = rwy_ref[0] + rwb_ref[0]
    mu = _seg_sum(y, ones_bd) * (1.0 / RW_HEAD)
    yc = y - mu
    var = _seg_sum(yc * yc, ones_bd) * (1.0 / RW_HEAD)
    y_rw = (yc * lax.rsqrt(var + RW_LN_EPS) * lng_ref[...] + lnb_ref[...]) * rwg_ref[0]
    s = s5y_ref[0] + s5u_ref[0] * s5d_ref[...]
    s = 0.5 * s * (1.0 + jnp.tanh(math.sqrt(2.0 / math.pi) * (s + 0.044715 * (s * s * s))))
    lg = _dot(s.astype(BF16), gluw_ref[...]) + glub_ref[...]
    y_s5 = lg[:, 0:w] * _sigmoid(lg[:, w:2 * w])
    zg = zg_ref[0]
    d = D_MODEL
    m = (_sigmoid(zg[:, 0:d]) * _dot(yhy_ref[0].astype(BF16), wb_ref[0])
         + _sigmoid(zg[:, d:2 * d]) * _dot(y_rw.astype(BF16), wb_ref[1])
         + _sigmoid(zg[:, 2 * d:3 * d]) * _dot(y_s5.astype(BF16), wb_ref[2]))
    o_ref[0] = x_ref[0] + g1_ref[0] * _dot(m.astype(BF16), wo_ref[...])


def merge_residual(y_hy, rw, s5_y, s5_u, zg, x, g1, p, ones_bd):
    bsz, n, d = x.shape
    w = MIX_WIDTH
    tm = min(ROW_TILE, n)
    tile = lambda c: pl.BlockSpec((1, tm, c), lambda i, j: (i, j, 0))
    full = lambda shape: pl.BlockSpec(shape, lambda i, j: (0,) * len(shape))
    rw_y, rw_bonus, rw_g = rw
    return pl.pallas_call(
        _k_merge,
        grid=(bsz, n // tm),
        in_specs=[tile(w), tile(w), tile(w), tile(w), tile(w), tile(w), tile(N_BRANCH * d), tile(d),
                  pl.BlockSpec((1, 1, d), lambda i, j: (i, 0, 0)),
                  full((w, w)), full((1, w)), full((1, w)), full((1, w)), full((w, 2 * w)), full((1, 2 * w)),
                  full((N_BRANCH, w, d)), full((d, d))],
        out_specs=tile(d),
        out_shape=jax.ShapeDtypeStruct(x.shape, F32),
        compiler_params=_params("arbitrary", "arbitrary"),
        name="merge_residual",
    )(y_hy, rw_y, rw_bonus, rw_g, s5_y, s5_u, zg, x, g1, ones_bd, p["rw_ln_g"].reshape(1, w),
      p["rw_ln_b"].reshape(1, w), p["s5_d"].reshape(1, w), p["s5_glu_w"].astype(BF16),
      p["s5_glu_b"].reshape(1, 2 * w), p["w_branch"].astype(BF16), p["w_out"].astype(BF16))


def _k_router(x_ref, sh_ref, sc_ref, g_ref, rw_ref, rb_ref, h_o, idx_o, gate_o):
    h = _rms_modulate(x_ref[0], g_ref[...], sh_ref[0], sc_ref[0])
    h_o[0] = h.astype(BF16)
    logits = _dot_hi(h, rw_ref[...]) + rb_ref[...]
    lane = lax.broadcasted_iota(jnp.int32, logits.shape, 1)
    vals, idxs = [], []
    for _ in range(TOP_K):
        m = jnp.max(logits, axis=-1, keepdims=True)
        idx = jnp.min(jnp.where(logits == m, lane, LANES), axis=-1, keepdims=True)
        vals.append(m)
        idxs.append(idx)
        logits = jnp.where(lane == idx, -jnp.inf, logits)
    exps = [jnp.exp(v - vals[0]) for v in vals]
    inv = 1.0 / (exps[0] + exps[1] + exps[2] + exps[3])
    idx_out = jnp.zeros(logits.shape, jnp.int32)
    gate_out = jnp.zeros(logits.shape, F32)
    for j in range(TOP_K):
        idx_out = jnp.where(lane == j, idxs[j], idx_out)
        gate_out = jnp.where(lane == j, exps[j] * inv, gate_out)
    idx_o[0] = idx_out
    gate_o[0] = gate_out


def moe_route(x, shift, scale, g, router_w, router_b):
    bsz, n, d = x.shape
    tm = min(ROW_TILE, n)
    rw = jnp.pad(router_w, ((0, 0), (0, LANES - N_EXPERTS)))
    rb = jnp.pad(router_b, (0, LANES - N_EXPERTS), constant_values=-jnp.inf).reshape(1, LANES)
    tile = lambda c: pl.BlockSpec((1, tm, c), lambda i, j: (i, j, 0))
    vec = pl.BlockSpec((1, 1, d), lambda i, j: (i, 0, 0))
    full = lambda shape: pl.BlockSpec(shape, lambda i, j: (0,) * len(shape))
    return pl.pallas_call(
        _k_router,
        grid=(bsz, n // tm),
        in_specs=[tile(d), vec, vec, full((1, d)), full((d, LANES)), full((1, LANES))],
        out_specs=[tile(d), tile(LANES), tile(LANES)],
        out_shape=[jax.ShapeDtypeStruct((bsz, n, d), BF16), jax.ShapeDtypeStruct((bsz, n, LANES), jnp.int32),
                   jax.ShapeDtypeStruct((bsz, n, LANES), F32)],
        compiler_params=_params("arbitrary", "arbitrary"),
        name="moe_route",
    )(x, shift, scale, g.reshape(1, d), rw, rb)


def _k_expert(blk_e_ref, xs_ref, w1_ref, b1_ref, w2_ref, b2_ref, sw_ref, o_ref):
    del blk_e_ref
    hid = _dot(xs_ref[...], w1_ref[0]) + b1_ref[0]
    gl = jnp.minimum(hid[:, 0:D_EXPERT], SWIGLU_LIMIT)
    up = jnp.clip(hid[:, D_EXPERT:2 * D_EXPERT], -SWIGLU_LIMIT, SWIGLU_LIMIT)
    act = (up + 1.0) * gl * _sigmoid(SWIGLU_ALPHA * gl)
    o_ref[...] = (_dot(act.astype(BF16), w2_ref[0]) + b2_ref[0]) * sw_ref[...]


def expert_ffn(xs, blk_e, slot_w, w1, b1, w2, b2):
    n_slots, d = xs.shape
    n_blocks = n_slots // MOE_ROWS
    grid_spec = pltpu.PrefetchScalarGridSpec(
        num_scalar_prefetch=1,
        grid=(n_blocks,),
        in_specs=[pl.BlockSpec((MOE_ROWS, d), lambda i, e: (i, 0)),
                  pl.BlockSpec((1, d, 2 * D_EXPERT), lambda i, e: (e[i], 0, 0)),
                  pl.BlockSpec((1, 1, 2 * D_EXPERT), lambda i, e: (e[i], 0, 0)),
                  pl.BlockSpec((1, D_EXPERT, d), lambda i, e: (e[i], 0, 0)),
                  pl.BlockSpec((1, 1, d), lambda i, e: (e[i], 0, 0)),
                  pl.BlockSpec((MOE_ROWS, 1), lambda i, e: (i, 0))],
        out_specs=pl.BlockSpec((MOE_ROWS, d), lambda i, e: (i, 0)),
    )
    return pl.pallas_call(
        _k_expert,
        grid_spec=grid_spec,
        out_shape=jax.ShapeDtypeStruct((n_slots, d), F32),
        compiler_params=_params("arbitrary"),
        name="expert_ffn",
    )(blk_e, xs, w1, b1.reshape(N_EXPERTS, 1, -1), w2, b2.reshape(N_EXPERTS, 1, -1), slot_w.reshape(-1, 1))


def moe_ffn(x, shift, scale, g, router_w, router_b, w1, b1, w2, b2):
    bsz, n, d = x.shape
    h, idx, gate = moe_route(x, shift, scale, g, router_w, router_b)
    n_tok = bsz * n
    n_asg = n_tok * TOP_K
    flat_e = idx[..., :TOP_K].reshape(-1)
    flat_w = gate[..., :TOP_K].reshape(-1)
    order = jnp.argsort(flat_e).astype(jnp.int32)
    rank_of = jnp.argsort(order).astype(jnp.int32)
    experts = jnp.arange(N_EXPERTS, dtype=jnp.int32)
    grp_end = jnp.searchsorted(flat_e[order], experts, side="right").astype(jnp.int32)
    grp_start = jnp.concatenate([jnp.zeros((1,), jnp.int32), grp_end[:-1]])
    counts = grp_end - grp_start
    padded = (counts + MOE_ROWS - 1) // MOE_ROWS * MOE_ROWS
    pad_end = jnp.cumsum(padded)
    pad_start = pad_end - padded
    n_blocks = -(-n_asg // MOE_ROWS) + N_EXPERTS
    n_slots = n_blocks * MOE_ROWS
    blk_e = jnp.minimum(jnp.searchsorted(pad_end, jnp.arange(n_blocks) * MOE_ROWS, side="right"),
                        N_EXPERTS - 1).astype(jnp.int32)
    slot_e = jnp.repeat(blk_e, MOE_ROWS)
    slot_rank = jnp.arange(n_slots, dtype=jnp.int32) - pad_start[slot_e]
    slot_used = slot_rank < counts[slot_e]
    slot_asg = order[jnp.where(slot_used, grp_start[slot_e] + slot_rank, 0)]
    slot_tok = jnp.where(slot_used, slot_asg // TOP_K, 0)
    slot_w = jnp.where(slot_used, flat_w[slot_asg], 0.0)
    slot_of = pad_start[flat_e] + rank_of - grp_start[flat_e]
    xs = h.reshape(n_tok, d)[slot_tok]
    ys = expert_ffn(xs, blk_e, slot_w, w1, b1, w2, b2)
    out = ys[slot_of].reshape(n_tok, TOP_K, d).sum(axis=1)
    return out.reshape(bsz, n, d)


def _k_rmsnorm(x_ref, g_ref, o_ref):
    x = x_ref[0]
    ms = jnp.mean(x * x, axis=-1, keepdims=True)
    o_ref[0] = x * lax.rsqrt(ms + NORM_EPS) * g_ref[...]


def final_rmsnorm(x, g):
    b, n, d = x.shape
    tm = min(2 * ROW_TILE, n)
    return pl.pallas_call(
        _k_rmsnorm,
        grid=(b, n // tm),
        in_specs=[pl.BlockSpec((1, tm, d), lambda i, j: (i, j, 0)), pl.BlockSpec((1, d), lambda i, j: (0, 0))],
        out_specs=pl.BlockSpec((1, tm, d), lambda i, j: (i, j, 0)),
        out_shape=jax.ShapeDtypeStruct(x.shape, F32),
        compiler_params=_params("arbitrary", "arbitrary"),
        name="final_rmsnorm",
    )(x, g.reshape(1, d))


@functools.lru_cache(maxsize=None)
def _grid_pos_embed(n_tokens):
    rows = n_tokens // GRID_W
    row_id, col_id = np.meshgrid(np.arange(rows), np.arange(GRID_W), indexing="ij")
    quarter = D_MODEL // 4
    omega = (1.0 / (10000.0 ** (np.arange(quarter, dtype=np.float32) / quarter))).astype(np.float32)

    def enc(pos):
        ang = pos.reshape(-1)[:, None].astype(np.float32) * omega
        return np.concatenate([np.sin(ang), np.cos(ang)], axis=-1)

    return np.concatenate([enc(row_id), enc(col_id)], axis=-1).astype(np.float32)


_LAYER_KEYS = ("ada_w", "ada_b", "norm1_g", "norm2_g", "w_in", "hy_conv_w", "hy_conv_b", "hy_w1", "hy_b1",
               "hy_f1", "hy_w2", "hy_b2", "hy_f2", "hy_w3", "hy_b3", "hy_bias", "rw_conv_w", "rw_conv_b",
               "rw_w_up", "rw_w0", "rw_a_up", "rw_a0", "rw_g_up", "rw_k_k", "rw_k_a", "rw_r_k", "rw_ln_g",
               "rw_ln_b", "s5_a_re", "s5_a_im", "s5_log_dt", "s5_b_re", "s5_b_im", "s5_c_re", "s5_c_im",
               "s5_d", "s5_glu_w", "s5_glu_b", "w_branch", "w_out", "router_w", "router_b", "moe_w1",
               "moe_b1", "moe_w2", "moe_b2")


def _token_mixer(x, xc, mod, mod_c, p, need_ctx, ones_bd):
    sh1, sc1, g1 = mod
    csh1, csc1, cg1 = mod_c
    w_in = p["w_in"].astype(BF16)
    c0, c1, c2 = HY_COLS, HY_COLS + RW_COLS, HY_COLS + RW_COLS + MIX_WIDTH
    proj = lambda t, s, c, lo, hi: norm_mod_matmul(t, s, c, p["norm1_g"], w_in[:, lo:hi])
    hy_args = (p["hy_w1"], p["hy_b1"], p["hy_f1"], p["hy_w2"], p["hy_b2"], p["hy_f2"], p["hy_w3"], p["hy_b3"],
               p["hy_bias"])

    z_hy = proj(x, sh1, sc1, 0, c0)
    y_hy = hyena_branch(z_hy, p["hy_conv_w"], p["hy_conv_b"], hyena_spectra(x.shape[1], *hy_args))
    u_rw = short_conv(proj(x, sh1, sc1, c0, c1), p["rw_conv_w"], p["rw_conv_b"])
    uc_rw = short_conv(proj(xc, csh1, csc1, c0, c1), p["rw_conv_w"], p["rw_conv_b"])
    rw_c, rw_l = rwkv_scan_branch(uc_rw, u_rw, p, ones_bd)
    z_s5 = proj(x, sh1, sc1, c1, c2)
    zc_s5 = proj(xc, csh1, csc1, c1, c2)
    s5_c, s5_l = s5_branch(zc_s5, z_s5, p)
    zg = proj(x, sh1, sc1, c2, c2 + N_BRANCH * D_MODEL)
    x_new = merge_residual(y_hy, rw_l, s5_l, z_s5, zg, x, g1, p, ones_bd)
    if not need_ctx:
        return x_new, None
    zc_hy = proj(xc, csh1, csc1, 0, c0)
    yc_hy = hyena_branch(zc_hy, p["hy_conv_w"], p["hy_conv_b"], hyena_spectra(xc.shape[1], *hy_args))
    zcg = proj(xc, csh1, csc1, c2, c2 + N_BRANCH * D_MODEL)
    xc_new = merge_residual(yc_hy, rw_c, s5_c, zc_s5, zcg, xc, cg1, p, ones_bd)
    return x_new, xc_new


def kernel(x, c, ctx, c_ctx, ada_w, ada_b, norm1_g, norm2_g, w_in, hy_conv_w, hy_conv_b, hy_w1, hy_b1, hy_f1, hy_w2, hy_b2, hy_f2, hy_w3, hy_b3, hy_bias, rw_conv_w, rw_conv_b, rw_w_up, rw_w0, rw_a_up, rw_a0, rw_g_up, rw_k_k, rw_k_a, rw_r_k, rw_ln_g, rw_ln_b, s5_a_re, s5_a_im, s5_log_dt, s5_b_re, s5_b_im, s5_c_re, s5_c_im, s5_d, s5_glu_w, s5_glu_b, w_branch, w_out, router_w, router_b, moe_w1, moe_b1, moe_w2, moe_b2, final_g):
    stacked = dict(zip(_LAYER_KEYS, (ada_w, ada_b, norm1_g, norm2_g, w_in, hy_conv_w, hy_conv_b, hy_w1, hy_b1,
                                     hy_f1, hy_w2, hy_b2, hy_f2, hy_w3, hy_b3, hy_bias, rw_conv_w, rw_conv_b,
                                     rw_w_up, rw_w0, rw_a_up, rw_a0, rw_g_up, rw_k_k, rw_k_a, rw_r_k, rw_ln_g,
                                     rw_ln_b, s5_a_re, s5_a_im, s5_log_dt, s5_b_re, s5_b_im, s5_c_re, s5_c_im,
                                     s5_d, s5_glu_w, s5_glu_b, w_branch, w_out, router_w, router_b, moe_w1,
                                     moe_b1, moe_w2, moe_b2)))
    bsz, n, d = x.shape
    depth = ada_w.shape[0]
    x = add_pos(x, jnp.asarray(_grid_pos_embed(n)))
    xc = ctx
    ones_bd = jnp.asarray(np.kron(np.eye(RW_HEADS), np.ones((RW_HEAD, RW_HEAD))), BF16)
    cvec = jnp.zeros((2 * SUBLANES, d), F32).at[:bsz].set(c).at[bsz].set(c_ctx)
    for l in range(depth):
        p = {k: v[l] for k, v in stacked.items()}
        need_ctx = l < depth - 1
        ada = ada_proj(cvec, p["ada_w"], p["ada_b"])
        lat = [t.reshape(bsz, 1, d) for t in jnp.split(ada[:bsz], 6, axis=-1)]
        cx = [jnp.broadcast_to(t.reshape(1, 1, d), (bsz, 1, d)) for t in jnp.split(ada[bsz], 6, axis=-1)]
        x, xc_new = _token_mixer(x, xc, lat[0:3], cx[0:3], p, need_ctx, ones_bd)
        w1, w2 = p["moe_w1"].astype(BF16), p["moe_w2"].astype(BF16)
        moe = functools.partial(moe_ffn, g=p["norm2_g"], router_w=p["router_w"], router_b=p["router_b"],
                                w1=w1, b1=p["moe_b1"], w2=w2, b2=p["moe_b2"])
        x = x + lat[5] * moe(x, lat[3], lat[4])
        if need_ctx:
            xc = xc_new + cx[5] * moe(xc_new, cx[3], cx[4])
    return final_rmsnorm(x, final_g)
```

```python
import functools
import math

import numpy as np
import jax
import jax.numpy as jnp
from jax import lax
from jax.experimental import pallas as pl
from jax.experimental.pallas import tpu as pltpu

F32 = jnp.float32
BF16 = jnp.bfloat16

D_MODEL = 1024
GRID_W = 64
SHORT_CONV = 3
NORM_EPS = 1e-6
N_BRANCH = 3
MIX_WIDTH = 512

HY_ORDER = 2
HY_BANDS = 16
HY_DECAY_TARGET = 1e-2
HY_FAST_PCT = 0.3
HY_SLOW_PCT = 1.5
HY_COLS = (HY_ORDER + 1) * MIX_WIDTH

RW_HEAD = 64
RW_HEADS = MIX_WIDTH // RW_HEAD
RW_DECAY_RANK = 64
RW_ICLR_RANK = 64
RW_GATE_RANK = 128
RW_LN_EPS = 64e-5
RW_COLS = 3 * MIX_WIDTH + RW_DECAY_RANK + RW_ICLR_RANK + RW_GATE_RANK

S5_GROUP = 16
S5_GROUPS = MIX_WIDTH // S5_GROUP
S5_STATE = 64
S5_HID = S5_GROUPS * S5_STATE

N_EXPERTS = 32
TOP_K = 4
D_EXPERT = 1024
SWIGLU_LIMIT = 7.0
SWIGLU_ALPHA = 1.702

LANES = 128
SUBLANES = 8
VMEM_LIMIT_BYTES = 56 * 1024 * 1024

ROW_TILE = 256
MOE_ROWS = 256
WKV_STEPS = 16
S5_STEPS = 64
S5_LANE_CHUNK = 512


def _params(*sem):
    return pltpu.CompilerParams(dimension_semantics=sem, vmem_limit_bytes=VMEM_LIMIT_BYTES)


def _dot(a, b):
    return jnp.dot(a, b, preferred_element_type=F32)


def _split3(x):
    hi = x.astype(BF16)
    r1 = x - hi.astype(F32)
    mid = r1.astype(BF16)
    lo = (r1 - mid.astype(F32)).astype(BF16)
    return hi, mid, lo


def _dot_hi(a, b):
    a0, a1, a2 = _split3(a)
    b0, b1, b2 = _split3(b)
    return (_dot(a0, b0) + (_dot(a0, b1) + _dot(a1, b0))
            + (_dot(a1, b1) + _dot(a0, b2) + _dot(a2, b0)))


def _seg_sum(x, ones_bd):
    hi, mid, lo = _split3(x)
    return _dot(hi, ones_bd) + _dot(mid, ones_bd) + _dot(lo, ones_bd)


def _sigmoid(x):
    return 1.0 / (1.0 + jnp.exp(-x))


def _rms_modulate(x, g, shift, scale):
    ms = jnp.mean(x * x, axis=-1, keepdims=True)
    y = x * lax.rsqrt(ms + NORM_EPS) * g
    return y * (1.0 + scale) + shift


def _k_ada(c_ref, w_ref, b_ref, o_ref):
    c = c_ref[...]
    s = c * _sigmoid(c)
    o_ref[...] = _dot_hi(s, w_ref[...]) + b_ref[...]


def ada_proj(cvec, w, b):
    rows, d = cvec.shape
    n = w.shape[1]
    tn = 1536
    return pl.pallas_call(
        _k_ada,
        grid=(n // tn,),
        in_specs=[pl.BlockSpec((rows, d), lambda j: (0, 0)),
                  pl.BlockSpec((d, tn), lambda j: (0, j)),
                  pl.BlockSpec((1, tn), lambda j: (0, j))],
        out_specs=pl.BlockSpec((rows, tn), lambda j: (0, j)),
        out_shape=jax.ShapeDtypeStruct((rows, n), F32),
        compiler_params=_params("arbitrary"),
        name="ada_proj",
    )(cvec, w, b.reshape(1, n))


def _k_add_pos(x_ref, p_ref, o_ref):
    o_ref[0] = x_ref[0] + p_ref[...]


def add_pos(x, pos):
    b, n, d = x.shape
    tm = min(ROW_TILE * 2, n)
    return pl.pallas_call(
        _k_add_pos,
        grid=(b, n // tm),
        in_specs=[pl.BlockSpec((1, tm, d), lambda i, j: (i, j, 0)),
                  pl.BlockSpec((tm, d), lambda i, j: (j, 0))],
        out_specs=pl.BlockSpec((1, tm, d), lambda i, j: (i, j, 0)),
        out_shape=jax.ShapeDtypeStruct(x.shape, F32),
        compiler_params=_params("arbitrary", "arbitrary"),
        name="add_pos",
    )(x, pos)


def _k_norm_mm(x_ref, sh_ref, sc_ref, g_ref, w_ref, o_ref, h_ref):
    @pl.when(pl.program_id(2) == 0)
    def _():
        h = _rms_modulate(x_ref[0], g_ref[...], sh_ref[0], sc_ref[0])
        h_ref[...] = h.astype(BF16)

    o_ref[0] = _dot(h_ref[...], w_ref[...])


def norm_mod_matmul(x, shift, scale, g, w):
    b, n, d = x.shape
    ncol = w.shape[1]
    tm = min(2 * ROW_TILE, n)
    tn = ncol if ncol <= 1792 else 1536
    return pl.pallas_call(
        _k_norm_mm,
        grid=(b, n // tm, ncol // tn),
        in_specs=[pl.BlockSpec((1, tm, d), lambda i, j, k: (i, j, 0)),
                  pl.BlockSpec((1, 1, d), lambda i, j, k: (i, 0, 0)),
                  pl.BlockSpec((1, 1, d), lambda i, j, k: (i, 0, 0)),
                  pl.BlockSpec((1, d), lambda i, j, k: (0, 0)),
                  pl.BlockSpec((d, tn), lambda i, j, k: (0, k))],
        out_specs=pl.BlockSpec((1, tm, tn), lambda i, j, k: (i, j, k)),
        out_shape=jax.ShapeDtypeStruct((b, n, ncol), F32),
        scratch_shapes=[pltpu.VMEM((tm, d), BF16)],
        compiler_params=_params("arbitrary", "arbitrary", "arbitrary"),
        name="norm_mod_matmul",
    )(x, shift, scale, g.reshape(1, d), w)


def _short_conv_val(z, w, b):
    n = z.shape[0]
    row = lax.broadcasted_iota(jnp.int32, z.shape, 0)
    zm = jnp.where(row == 0, 0.0, pltpu.roll(z, 1, 0))
    zp = jnp.where(row == n - 1, 0.0, pltpu.roll(z, n - 1, 0))
    return zm * w[0:1] + z * w[1:2] + zp * w[2:3] + b


def _k_sconv(z_ref, w_ref, b_ref, o_ref):
    o_ref[0] = _short_conv_val(z_ref[0], w_ref[...], b_ref[...])


def short_conv(z, w, b):
    bsz, n, c = z.shape
    tc = 256 if c % 256 == 0 else LANES
    return pl.pallas_call(
        _k_sconv,
        grid=(bsz, c // tc),
        in_specs=[pl.BlockSpec((1, n, tc), lambda i, j: (i, 0, j)),
                  pl.BlockSpec((SHORT_CONV, tc), lambda i, j: (0, j)),
                  pl.BlockSpec((1, tc), lambda i, j: (0, j))],
        out_specs=pl.BlockSpec((1, n, tc), lambda i, j: (i, 0, j)),
        out_shape=jax.ShapeDtypeStruct(z.shape, F32),
        compiler_params=_params("arbitrary", "arbitrary"),
        name="short_conv",
    )(z, w, b.reshape(1, c))


def _fft_split(n):
    total = 2 * n
    bits = total.bit_length() - 1
    assert 1 << bits == total
    n1 = 1 << ((bits + 1) // 2)
    n1 = max(n1, 2 * SUBLANES)
    return n1, total // n1


@functools.lru_cache(maxsize=None)
def _dft_mats(n):
    n1, n2 = _fft_split(n)
    total = n1 * n2
    k1 = np.arange(n1)[:, None]
    m1 = np.arange(n1)[None, :]
    m2 = np.arange(n2)[:, None, None]
    ang = -2.0 * np.pi * ((n2 * k1 * m1)[None] + m2 * k1[None]) / total
    fa = np.concatenate([np.cos(ang), np.sin(ang)], axis=1)
    k2 = np.arange(n2)[:, None]
    mm = np.arange(n2)[None, :]
    gang = -2.0 * np.pi * k2 * mm / n2
    gr, gi = np.cos(gang), np.sin(gang)
    fb = np.block([[gr, -gi], [gi, gr]])
    fc = np.block([[gr, gi], [-gi, gr]])
    m1d = np.arange(n1 // 2)[:, None]
    k1d = np.arange(n1)[None, :]
    dang = 2.0 * np.pi * ((n2 * m1d * k1d)[None] + m2 * k1d[None]) / total
    fd = np.concatenate([np.cos(dang), -np.sin(dang)], axis=2) / total
    return (fa.astype(np.float32), fb.astype(np.float32), fc.astype(np.float32), fd.astype(np.float32))


def _time_pitch(n2):
    return n2 + SUBLANES


def _spec_pitch(n2):
    return 2 * n2 + SUBLANES


def _fft_stage_a(src_ref, y_ref, fa_ref, n1, n2, k_rows):
    tp, sp = _time_pitch(n2), _spec_pitch(n2)

    def body(m2, c):
        xs = src_ref[pl.ds(m2, k_rows, stride=tp), :].astype(BF16)
        res = _dot(fa_ref[m2, :, 0:k_rows], xs)
        y_ref[pl.ds(m2, n1, stride=sp), :] = res[0:n1]
        y_ref[pl.ds(n2 + m2, n1, stride=sp), :] = res[n1:2 * n1]
        return c
    lax.fori_loop(0, n2, body, 0, unroll=2)


def _long_conv(buf_ref, y_ref, h_ref, fa_ref, fb_ref, fc_ref, fd_ref, n1, n2):
    tp, sp = _time_pitch(n2), _spec_pitch(n2)
    _fft_stage_a(buf_ref, y_ref, fa_ref, n1, n2, n1 // 2)

    def body_k1(k1, c):
        rows = pl.ds(pl.multiple_of(k1 * sp, SUBLANES), 2 * n2)
        z = _dot(fb_ref[...], y_ref[rows, :].astype(BF16))
        zr, zi = z[0:n2], z[n2:2 * n2]
        h = h_ref[k1]
        hr, hi = h[0:n2], h[n2:2 * n2]
        p = jnp.concatenate([zr * hr - zi * hi, zr * hi + zi * hr], axis=0)
        y_ref[rows, :] = _dot(fc_ref[...], p.astype(BF16))
        return c
    lax.fori_loop(0, n1, body_k1, 0, unroll=4)

    def body_m2(m2, c):
        qr = y_ref[pl.ds(m2, n1, stride=sp), :]
        qi = y_ref[pl.ds(n2 + m2, n1, stride=sp), :]
        q = jnp.concatenate([qr, qi], axis=0).astype(BF16)
        buf_ref[pl.ds(m2, n1 // 2, stride=tp), :] = _dot(fd_ref[m2], q)
        return c
    lax.fori_loop(0, n2, body_m2, 0, unroll=2)


def _store_time_blocks(buf_ref, val, n2):
    tp = _time_pitch(n2)
    for m1 in range(val.shape[0] // n2):
        buf_ref[m1 * tp:m1 * tp + n2, :] = val[m1 * n2:(m1 + 1) * n2]


def _load_time_blocks(buf_ref, n, n2):
    tp = _time_pitch(n2)
    return jnp.concatenate([buf_ref[m1 * tp:m1 * tp + n2, :] for m1 in range(n // n2)], axis=0)


def _k_hyena(x1_ref, x2_ref, v_ref, w1_ref, w2_ref, wv_ref, b1_ref, b2_ref, bv_ref, h1_ref, h2_ref,
             fa_ref, fb_ref, fc_ref, fd_ref, o_ref, buf_ref, y_ref, *, n1, n2):
    n = x1_ref.shape[1]
    conv = functools.partial(_long_conv, buf_ref, y_ref, fa_ref=fa_ref, fb_ref=fb_ref, fc_ref=fc_ref,
                             fd_ref=fd_ref, n1=n1, n2=n2)
    _store_time_blocks(buf_ref, _short_conv_val(v_ref[0], wv_ref[...], bv_ref[...]), n2)
    conv(h1_ref)
    gated = _short_conv_val(x1_ref[0], w1_ref[...], b1_ref[...]) * _load_time_blocks(buf_ref, n, n2)
    _store_time_blocks(buf_ref, gated, n2)
    conv(h2_ref)
    o_ref[0] = _short_conv_val(x2_ref[0], w2_ref[...], b2_ref[...]) * _load_time_blocks(buf_ref, n, n2)


def _k_fspec(f_ref, fa_ref, fb_ref, o_ref, y_ref, *, n1, n2):
    sp = _spec_pitch(n2)
    _fft_stage_a(f_ref, y_ref, fa_ref, n1, n2, n1)

    def body_k1(k1, c):
        src = pl.ds(pl.multiple_of(k1 * sp, SUBLANES), 2 * n2)
        dst = pl.ds(pl.multiple_of(k1 * 2 * n2, 2 * n2), 2 * n2)
        o_ref[dst, :] = _dot(fb_ref[...], y_ref[src, :].astype(BF16))
        return c
    lax.fori_loop(0, n1, body_k1, 0, unroll=2)


def filter_spectrum(filt, n):
    n1, n2 = _fft_split(n)
    total, c = filt.shape
    tp, sp = _time_pitch(n2), _spec_pitch(n2)
    fa, fb, _, _ = _dft_mats(n)
    blocks = jnp.pad(filt.reshape(n1, n2, c), ((0, 0), (0, tp - n2), (0, 0))).reshape(n1 * tp, c)
    out = pl.pallas_call(
        functools.partial(_k_fspec, n1=n1, n2=n2),
        grid=(c // LANES,),
        in_specs=[pl.BlockSpec((n1 * tp, LANES), lambda j: (0, j)),
                  pl.BlockSpec((n2, 2 * n1, n1), lambda j: (0, 0, 0)),
                  pl.BlockSpec((2 * n2, 2 * n2), lambda j: (0, 0))],
        out_specs=pl.BlockSpec((total * 2, LANES), lambda j: (0, j)),
        out_shape=jax.ShapeDtypeStruct((2 * total, c), F32),
        scratch_shapes=[pltpu.VMEM((n1 * sp, LANES), F32)],
        compiler_params=_params("arbitrary"),
        name="filter_spectrum",
    )(blocks, jnp.asarray(fa, BF16), jnp.asarray(fb, BF16))
    return out.reshape(n1, 2 * n2, c)


def _k_filter_mlp(feat_ref, win_ref, w1_ref, b1_ref, f1_ref, w2_ref, b2_ref, f2_ref, w3_ref, b3_ref, o_ref):
    h = jnp.sin(f1_ref[...] * (_dot_hi(feat_ref[...], w1_ref[...]) + b1_ref[...]))
    h = jnp.sin(f2_ref[...] * (_dot_hi(h, w2_ref[...]) + b2_ref[...]))
    o_ref[...] = (_dot_hi(h, w3_ref[...]) + b3_ref[...]) * win_ref[...]


@functools.lru_cache(maxsize=None)
def _filter_consts(n):
    t = np.linspace(0.0, 1.0, n, dtype=np.float32)[:, None]
    bands = np.linspace(1e-4, HY_BANDS - 1, HY_BANDS, dtype=np.float32)
    ang = (np.float32(2 * math.pi) * np.arange(n, dtype=np.float32) / np.float32(n))[:, None] * bands
    feats = np.concatenate([t, np.cos(ang), -np.sin(ang)], axis=-1).astype(np.float32)
    pad = (-feats.shape[1]) % SUBLANES
    feats = np.pad(feats, ((0, 0), (0, pad)))
    deltas = np.abs(np.linspace(math.log(HY_DECAY_TARGET) / HY_SLOW_PCT,
                                math.log(HY_DECAY_TARGET) / HY_FAST_PCT, MIX_WIDTH, dtype=np.float32))
    window = np.exp(-t * deltas).astype(np.float32)
    return feats, np.tile(window, (1, 2 * HY_ORDER))


def hyena_filter_table(n, w1, b1, f1, w2, b2, f2, w3, b3):
    feats, window = _filter_consts(n)
    kf = feats.shape[1]
    fd = w1.shape[1]
    ncol = w3.shape[1]
    w1p = jnp.pad(w1, ((0, kf - w1.shape[0]), (0, 0)))
    tm = min(ROW_TILE, n)
    full = lambda shape: pl.BlockSpec(shape, lambda i: (0,) * len(shape))
    return pl.pallas_call(
        _k_filter_mlp,
        grid=(n // tm,),
        in_specs=[pl.BlockSpec((tm, kf), lambda i: (i, 0)),
                  pl.BlockSpec((tm, ncol), lambda i: (i, 0)),
                  full((kf, fd)), full((1, fd)), full((1, fd)),
                  full((fd, fd)), full((1, fd)), full((1, fd)),
                  full((fd, ncol)), full((1, ncol))],
        out_specs=pl.BlockSpec((tm, ncol), lambda i: (i, 0)),
        out_shape=jax.ShapeDtypeStruct((n, ncol), F32),
        compiler_params=_params("arbitrary"),
        name="hyena_filter_mlp",
    )(jnp.asarray(feats), jnp.asarray(window), w1p, b1.reshape(1, fd), f1.reshape(1, fd),
      w2, b2.reshape(1, fd), f2.reshape(1, fd), w3, b3.reshape(1, ncol))


def hyena_spectra(n, w1, b1, f1, w2, b2, f2, w3, b3, bias):
    hf = hyena_filter_table(n, w1, b1, f1, w2, b2, f2, w3, b3).reshape(n, HY_ORDER, 2, MIX_WIDTH)
    fwd = hf[:, :, 0]
    bwd = hf[1:, :, 1][::-1]
    fwd = fwd.at[0].add(bias)
    filt = jnp.concatenate([fwd, jnp.zeros((1, HY_ORDER, MIX_WIDTH), F32), bwd], axis=0)
    spec = filter_spectrum(filt.reshape(2 * n, HY_ORDER * MIX_WIDTH), n)
    n1, n2 = _fft_split(n)
    return spec.reshape(n1, 2 * n2, HY_ORDER, MIX_WIDTH).transpose(2, 0, 1, 3)


def hyena_branch(z, conv_w, conv_b, spectra):
    bsz, n, _ = z.shape
    n1, n2 = _fft_split(n)
    fa, fb, fc, fd = (jnp.asarray(m, BF16) for m in _dft_mats(n))
    fa = fa[:, :, : n1 // 2]
    nt = MIX_WIDTH // LANES
    cb = conv_b.reshape(1, HY_COLS)
    zspec = lambda off: pl.BlockSpec((1, n, LANES), lambda j, i: (i, 0, off + j))
    wspec = lambda off: pl.BlockSpec((SHORT_CONV, LANES), lambda j, i: (0, off + j))
    bspec = lambda off: pl.BlockSpec((1, LANES), lambda j, i: (0, off + j))
    once = pl.Buffered(1)
    hspec = lambda o: pl.BlockSpec((None, n1, 2 * n2, LANES), lambda j, i: (o, 0, 0, j), pipeline_mode=once)
    full = lambda shape: pl.BlockSpec(shape, lambda j, i: (0,) * len(shape), pipeline_mode=once)
    return pl.pallas_call(
        functools.partial(_k_hyena, n1=n1, n2=n2),
        grid=(nt, bsz),
        in_specs=[zspec(0), zspec(nt), zspec(2 * nt), wspec(0), wspec(nt), wspec(2 * nt),
                  bspec(0), bspec(nt), bspec(2 * nt), hspec(0), hspec(1),
                  full(fa.shape), full(fb.shape), full(fc.shape), full(fd.shape)],
        out_specs=pl.BlockSpec((1, n, LANES), lambda j, i: (i, 0, j)),
        out_shape=jax.ShapeDtypeStruct((bsz, n, MIX_WIDTH), F32),
        scratch_shapes=[pltpu.VMEM((n1 // 2 * _time_pitch(n2), LANES), F32),
                        pltpu.VMEM((n1 * _spec_pitch(n2), LANES), F32)],
        compiler_params=_params("arbitrary", "arbitrary"),
        name="hyena_long_conv",
    )(z, z, z, conv_w, conv_w, conv_w, cb, cb, cb, spectra, spectra, fa, fb, fc, fd)


def _store_head_pairs(o_ref, left, right):
    rows = left.shape[0]
    lane = lax.broadcasted_iota(jnp.int32, (rows, LANES), 1)
    low = lane < RW_HEAD
    for q in range(RW_HEADS // 2):
        lcol = left[:, q * LANES:(q + 1) * LANES]
        rcol = right[:, q * LANES:(q + 1) * LANES]
        heads = (jnp.where(low, lcol, pltpu.roll(rcol, RW_HEAD, 1)),
                 jnp.where(low, pltpu.roll(lcol, RW_HEAD, 1), rcol))
        for h, piece in zip((2 * q, 2 * q + 1), heads):
            for c in range(rows // WKV_STEPS):
                o_ref[c, h * WKV_STEPS:(h + 1) * WKV_STEPS, :] = piece[c * WKV_STEPS:(c + 1) * WKV_STEPS]


def _load_head_sum(y0_ref, y1_ref, h):
    sl = slice(h * WKV_STEPS, (h + 1) * WKV_STEPS)
    return jnp.concatenate([y0_ref[c, sl, :] + y1_ref[c, sl, :] for c in range(y0_ref.shape[0])], axis=0)


def _k_rwprep(uc_ref, ul_ref, gup_ref, wup_ref, w0_ref, aup_ref, a0_ref, kk_ref, ka_ref, rk_ref, e_ref,
              g_o, bonus_o, rv_o, nb0_o, nb1_o, wk0_o, wk1_o, *, ctx_tiles):
    w = MIX_WIDTH
    u = jnp.where(pl.program_id(1) < ctx_tiles, uc_ref[0], ul_ref[0])
    r, k, v = u[:, 0:w], u[:, w:2 * w], u[:, 2 * w:3 * w]
    xw = u[:, 3 * w:3 * w + RW_DECAY_RANK]
    xa = u[:, 3 * w + RW_DECAY_RANK:3 * w + RW_DECAY_RANK + RW_ICLR_RANK]
    xg = u[:, 3 * w + RW_DECAY_RANK + RW_ICLR_RANK:]
    ones_bd = e_ref[...]
    g_o[0] = _dot(_sigmoid(xg).astype(BF16), gup_ref[...])
    kk = k * kk_ref[...]
    kk = kk * lax.rsqrt(jnp.maximum(_seg_sum(kk * kk, ones_bd), 1e-24))
    txw = jnp.tanh(xw)
    kd_sum = None
    for d, (nb_o, wk_o) in enumerate(((nb0_o, wk0_o), (nb1_o, wk1_o))):
        x = -(w0_ref[d] + _dot_hi(txw, wup_ref[d]))
        softplus = jnp.maximum(x, 0.0) + jnp.log(1.0 + jnp.exp(-jnp.abs(x)))
        decay = jnp.exp(-jnp.exp(-softplus - 0.5))
        a = _sigmoid(a0_ref[d] + _dot_hi(xa, aup_ref[d]))
        kd = k * (1.0 + (a - 1.0) * ka_ref[...])
        _store_head_pairs(nb_o, -kk, kk * a)
        _store_head_pairs(wk_o, decay, kd)
        kd_sum = kd if kd_sum is None else kd_sum + kd
    _store_head_pairs(rv_o, r, v)
    bonus_o[0] = _seg_sum(r * kd_sum * rk_ref[...], ones_bd) * v


def rwkv_prep(u_ctx, u_lat, g_up, w_up, w0, a_up, a0, k_k, k_a, r_k, ones_bd):
    bsz, n_ctx, cols = u_ctx.shape
    n_lat = u_lat.shape[1]
    t_total = n_ctx + n_lat
    w = MIX_WIDTH
    tm = min(ROW_TILE, n_ctx)
    ctx_tiles = n_ctx // tm
    full = lambda shape: pl.BlockSpec(shape, lambda i, j: (0,) * len(shape))
    tile = pl.BlockSpec((1, tm, w), lambda i, j: (i, j, 0))
    pair = pl.BlockSpec((tm // WKV_STEPS, RW_HEADS * WKV_STEPS, LANES), lambda i, j: (j, i, 0))
    pair_shape = jax.ShapeDtypeStruct((t_total // WKV_STEPS, bsz * RW_HEADS * WKV_STEPS, LANES), F32)
    return pl.pallas_call(
        functools.partial(_k_rwprep, ctx_tiles=ctx_tiles),
        grid=(bsz, t_total // tm),
        in_specs=[pl.BlockSpec((1, tm, cols), lambda i, j: (i, jnp.minimum(j, ctx_tiles - 1), 0)),
                  pl.BlockSpec((1, tm, cols), lambda i, j: (i, jnp.maximum(j - ctx_tiles, 0), 0)),
                  full((RW_GATE_RANK, w)), full((2, RW_DECAY_RANK, w)), full((2, 1, w)),
                  full((2, RW_ICLR_RANK, w)), full((2, 1, w)), full((1, w)), full((1, w)), full((1, w)),
                  full((w, w))],
        out_specs=[tile, tile] + [pair] * 5,
        out_shape=[jax.ShapeDtypeStruct((bsz, t_total, w), F32)] * 2 + [pair_shape] * 5,
        compiler_params=_params("arbitrary", "arbitrary"),
        name="rwkv_prep",
    )(u_ctx, u_lat, g_up.astype(BF16), w_up, w0.reshape(2, 1, w), a_up, a0.reshape(2, 1, w),
      k_k.reshape(1, w), k_a.reshape(1, w), r_k.reshape(1, w), ones_bd)


def _k_wkv(rv0_ref, rv1_ref, nb0_ref, nb1_ref, wk0_ref, wk1_ref, y0_ref, y1_ref, s_ref, op_ref, *, steps):
    @pl.when(pl.program_id(0) == 0)
    def _():
        s_ref[...] = jnp.zeros_like(s_ref)

    chains = rv0_ref.shape[0] // steps
    rv0, rv1, nb0, nb1, wk0, wk1, y0, y1 = (rv0_ref, rv1_ref, nb0_ref, nb1_ref, wk0_ref, wk1_ref, y0_ref, y1_ref)

    def step(t, c):
        fwd = pl.ds(t, chains, stride=steps)
        bwd = pl.ds(steps - 1 - t, chains, stride=steps)
        for j, (ref0, ref1) in enumerate(((rv0, rv1), (nb0, nb1), (wk0, wk1))):
            both = jnp.concatenate([ref0[fwd, :], ref1[bwd, :]], axis=0)
            op_ref[2 * j:2 * j + 2] = both.T.reshape(2, RW_HEAD, 2 * chains)
        r_ref, v_ref, a_ref, b_ref, w_ref, k_ref = (op_ref.at[j] for j in range(6))
        sa = jnp.zeros(s_ref.shape[1:], F32)
        for i in range(RW_HEAD):
            sa = sa + s_ref[i] * a_ref[pl.ds(i, 1), :]
        vv = v_ref[...]
        y = jnp.zeros(s_ref.shape[1:], F32)
        for i in range(RW_HEAD):
            s_new = s_ref[i] * w_ref[pl.ds(i, 1), :] + sa * b_ref[pl.ds(i, 1), :] + vv * k_ref[pl.ds(i, 1), :]
            s_ref[i] = s_new
            y = y + s_new * r_ref[pl.ds(i, 1), :]
        yt = jnp.concatenate([y, y], axis=0).T
        y0[fwd, :] = yt[0:chains]
        y1[bwd, :] = yt[chains:2 * chains]
        return c
    lax.fori_loop(0, steps, step, 0)


def wkv_scan(rv, nb0, nb1, wk0, wk1, n_ctx):
    nb, rows, _ = rv.shape
    steps = WKV_STEPS
    chains = rows // steps
    nc = n_ctx // steps
    fwd = pl.BlockSpec((None, rows, LANES), lambda i: (i, 0, 0))
    bwd = pl.BlockSpec((None, rows, LANES), lambda i: (jnp.where(i < nc, nc - 1 - i, nb - 1 + nc - i), 0, 0))
    out = jax.ShapeDtypeStruct(rv.shape, F32)
    return pl.pallas_call(
        functools.partial(_k_wkv, steps=steps),
        grid=(nb,),
        in_specs=[fwd, bwd, fwd, bwd, fwd, bwd],
        out_specs=[fwd, bwd],
        out_shape=[out, out],
        scratch_shapes=[pltpu.VMEM((RW_HEAD, RW_HEAD, 2 * chains), F32),
                        pltpu.VMEM((6, RW_HEAD, 2 * chains), F32)],
        compiler_params=_params("arbitrary"),
        name="wkv_scan",
    )(rv, rv, nb0, nb1, wk0, wk1)


def rwkv_scan_branch(u_ctx, u_lat, p, ones_bd):
    g, bonus, rv, nb0, nb1, wk0, wk1 = rwkv_prep(
        u_ctx, u_lat, p["rw_g_up"], p["rw_w_up"], p["rw_w0"], p["rw_a_up"], p["rw_a0"], p["rw_k_k"],
        p["rw_k_a"], p["rw_r_k"].reshape(-1), ones_bd)
    y0, y1 = wkv_scan(rv, nb0, nb1, wk0, wk1, u_ctx.shape[1])
    return y0, y1, bonus, g


def _s5_block(i, n_ctx_blocks, n_blocks, reverse):
    if not reverse:
        return i
    return jnp.where(i < n_ctx_blocks, n_ctx_blocks - 1 - i, n_blocks - 1 + n_ctx_blocks - i)


def _k_s5(uc_ref, ul_ref, bre_ref, bim_ref, lre_ref, lim_ref, cre_ref, cim_ref, y_ref, hre_ref, him_ref,
          st_ref, tb_ref, *, steps, reverse, n_ctx_blocks, n_blocks):
    @pl.when(pl.program_id(0) == 0)
    def _():
        st_ref[...] = jnp.zeros_like(st_ref)

    bsz = uc_ref.shape[0]
    nq = MIX_WIDTH // LANES
    in_ctx = _s5_block(pl.program_id(0), n_ctx_blocks, n_blocks, reverse) < n_ctx_blocks
    for b in range(bsz):
        ub = jnp.where(in_ctx, uc_ref[b], ul_ref[b])
        for q in range(nq):
            tb_ref[q, pl.ds(b, steps, stride=bsz), :] = ub[:, q * LANES:(q + 1) * LANES]
    u = jnp.concatenate([tb_ref[q] for q in range(nq)], axis=1).astype(BF16)
    hre_ref[...] = _dot(u, bre_ref[...])
    him_ref[...] = _dot(u, bim_ref[...])
    for ch in range(S5_HID // S5_LANE_CHUNK):
        cols = slice(ch * S5_LANE_CHUNK, (ch + 1) * S5_LANE_CHUNK)
        lre = lre_ref[:, cols]
        lim = lim_ref[:, cols]

        def step(i, carry):
            hr, hi = carry
            t = (steps - 1 - i) if reverse else i
            rows = pl.ds(pl.multiple_of(t * bsz, bsz), bsz)
            nr = lre * hr - lim * hi + hre_ref[rows, cols]
            ni = lre * hi + lim * hr + him_ref[rows, cols]
            hre_ref[rows, cols] = nr
            him_ref[rows, cols] = ni
            return nr, ni
        hr, hi = lax.fori_loop(0, steps, step, (st_ref[0, :, cols], st_ref[1, :, cols]))
        st_ref[0, :, cols] = hr
        st_ref[1, :, cols] = hi
    y = _dot(hre_ref[...].astype(BF16), cre_ref[...]) - _dot(him_ref[...].astype(BF16), cim_ref[...])
    for q in range(nq):
        tb_ref[q] = y[:, q * LANES:(q + 1) * LANES]
    for b in range(bsz):
        for q in range(nq):
            y_ref[b, :, q * LANES:(q + 1) * LANES] = tb_ref[q, pl.ds(b, steps, stride=bsz), :]


def s5_scan(z_ctx, z_lat, bre, bim, lre, lim, cre, cim, reverse):
    bsz, n_ctx, w = z_ctx.shape
    t_total = n_ctx + z_lat.shape[1]
    steps = S5_STEPS
    nc = n_ctx // steps
    nb = t_total // steps
    blk = functools.partial(_s5_block, n_ctx_blocks=nc, n_blocks=nb, reverse=reverse)
    full = lambda shape: pl.BlockSpec(shape, lambda i: (0,) * len(shape))
    return pl.pallas_call(
        functools.partial(_k_s5, steps=steps, reverse=reverse, n_ctx_blocks=nc, n_blocks=nb),
        grid=(nb,),
        in_specs=[pl.BlockSpec((bsz, steps, w), lambda i: (0, jnp.minimum(blk(i), nc - 1), 0)),
                  pl.BlockSpec((bsz, steps, w), lambda i: (0, jnp.maximum(blk(i) - nc, 0), 0)),
                  full((w, S5_HID)), full((w, S5_HID)), full((bsz, S5_HID)), full((bsz, S5_HID)),
                  full((S5_HID, w)), full((S5_HID, w))],
        out_specs=pl.BlockSpec((bsz, steps, w), lambda i: (0, blk(i), 0)),
        out_shape=jax.ShapeDtypeStruct((bsz, t_total, w), F32),
        scratch_shapes=[pltpu.VMEM((steps * bsz, S5_HID), F32), pltpu.VMEM((steps * bsz, S5_HID), F32),
                        pltpu.VMEM((2, bsz, S5_HID), F32), pltpu.VMEM((w // LANES, steps * bsz, LANES), F32)],
        compiler_params=_params("arbitrary"),
        name="s5_scan",
    )(z_ctx, z_lat, bre, bim, lre, lim, cre, cim)


def _k_s5disc(are_ref, aim_ref, ldt_ref, lre_ref, lim_ref, cr_ref, ci_ref):
    a_re, a_im = are_ref[...], aim_ref[...]
    dt = jnp.exp(ldt_ref[...])
    mag = jnp.exp(a_re * dt)
    lb_re, lb_im = mag * jnp.cos(a_im * dt), mag * jnp.sin(a_im * dt)
    den = a_re * a_re + a_im * a_im
    nr = lb_re - 1.0
    lre_ref[...] = lb_re
    lim_ref[...] = lb_im
    cr_ref[...] = (nr * a_re + lb_im * a_im) / den
    ci_ref[...] = (lb_im * a_re - nr * a_im) / den


def s5_operators(a_re, a_im, log_dt, b_re, b_im, c_re, c_im, bsz):
    g, p, h = S5_GROUPS, S5_STATE, S5_GROUP
    shp = jax.ShapeDtypeStruct((2 * g, p), F32)
    lb_re, lb_im, cr, ci = pl.pallas_call(
        _k_s5disc, out_shape=[shp] * 4, name="s5_discretise",
    )(a_re.reshape(2 * g, p), a_im.reshape(2 * g, p), jnp.broadcast_to(log_dt.reshape(2 * g, 1), (2 * g, p)))
    cr, ci = cr.reshape(2, g, p, 1), ci.reshape(2, g, p, 1)
    bb_re = cr * b_re[None] - ci * b_im[None]
    bb_im = cr * b_im[None] + ci * b_re[None]
    eye = jnp.eye(g, dtype=F32)
    bd_in = lambda m: jnp.einsum("dgph,gk->dghkp", m, eye).reshape(2, g * h, g * p).astype(BF16)
    bd_out = lambda m: jnp.einsum("ghp,gk->gpkh", m, eye).reshape(g * p, g * h).astype(BF16)
    lam = lambda m: jnp.broadcast_to(m.reshape(2, 1, g * p), (2, bsz, g * p))
    return bd_in(bb_re), bd_in(bb_im), lam(lb_re), lam(lb_im), bd_out(c_re), bd_out(c_im)


def s5_branch(z_ctx, z_lat, p):
    bsz = z_ctx.shape[0]
    bre, bim, lre, lim, cre, cim = s5_operators(p["s5_a_re"], p["s5_a_im"], p["s5_log_dt"], p["s5_b_re"],
                                                p["s5_b_im"], p["s5_c_re"], p["s5_c_im"], bsz)
    return (s5_scan(z_ctx, z_lat, bre[0], bim[0], lre[0], lim[0], cre, cim, False),
            s5_scan(z_ctx, z_lat, bre[1], bim[1], lre[1], lim[1], cre, cim, True))


def _k_merge(yhy_ref, rwy0_ref, rwy1_ref, rwb_ref, rwg_ref, s5y0_ref, s5y1_ref, s5u_ref, zg_ref, x_ref, g1_ref,
             e_ref, lng_ref, lnb_ref, s5d_ref, gluw_ref, glub_ref, wb_ref, wo_ref, o_ref):
    w = MIX_WIDTH
    ones_bd = e_ref[...]
    lane = lax.broadcasted_iota(jnp.int32, (x_ref.shape[1], LANES), 1)
    cols = []
    for q in range(RW_HEADS // 2):
        even = _load_head_sum(rwy0_ref, rwy1_ref, 2 * q)
        odd = _load_head_sum(rwy0_ref, rwy1_ref, 2 * q + 1)
        cols.append(jnp.where(lane < RW_HEAD, even, pltpu.roll(odd, RW_HEAD, 1)))
    y = jnp.concatenate(cols, axis=1) + rwb_ref[0]
    mu = _seg_sum(y, ones_bd) * (1.0 / RW_HEAD)
    yc = y - mu
    var = _seg_sum(yc * yc, ones_bd) * (1.0 / RW_HEAD)
    y_rw = (yc * lax.rsqrt(var + RW_LN_EPS) * lng_ref[...] + lnb_ref[...]) * rwg_ref[0]
    s = s5y0_ref[0] + s5y1_ref[0] + s5u_ref[0] * s5d_ref[...]
    s = 0.5 * s * (1.0 + jnp.tanh(math.sqrt(2.0 / math.pi) * (s + 0.044715 * (s * s * s))))
    lg = _dot(s.astype(BF16), gluw_ref[...]) + glub_ref[...]
    y_s5 = lg[:, 0:w] * _sigmoid(lg[:, w:2 * w])
    zg = zg_ref[0]
    d = D_MODEL
    m = (_sigmoid(zg[:, 0:d]) * _dot(yhy_ref[0].astype(BF16), wb_ref[0])
         + _sigmoid(zg[:, d:2 * d]) * _dot(y_rw.astype(BF16), wb_ref[1])
         + _sigmoid(zg[:, 2 * d:3 * d]) * _dot(y_s5.astype(BF16), wb_ref[2]))
    o_ref[0] = x_ref[0] + g1_ref[0] * _dot(m.astype(BF16), wo_ref[...])


def merge_residual(y_hy, rw, s5_y, t_off, s5_u, zg, x, g1, p, ones_bd):
    bsz, n, d = x.shape
    w = MIX_WIDTH
    tm = min(ROW_TILE, n)
    assert t_off % tm == 0
    off = t_off // tm
    tile = lambda c: pl.BlockSpec((1, tm, c), lambda i, j: (i, j, 0))
    seq_tile = pl.BlockSpec((1, tm, w), lambda i, j: (i, off + j, 0))
    chain_tile = pl.BlockSpec((tm // WKV_STEPS, RW_HEADS * WKV_STEPS, LANES), lambda i, j: (off + j, i, 0))
    full = lambda shape: pl.BlockSpec(shape, lambda i, j: (0,) * len(shape))
    rw_y0, rw_y1, rw_bonus, rw_g = rw
    return pl.pallas_call(
        _k_merge,
        grid=(bsz, n // tm),
        in_specs=[tile(w), chain_tile, chain_tile, seq_tile, seq_tile, seq_tile, seq_tile, tile(w),
                  tile(N_BRANCH * d), tile(d), pl.BlockSpec((1, 1, d), lambda i, j: (i, 0, 0)),
                  full((w, w)), full((1, w)), full((1, w)), full((1, w)), full((w, 2 * w)), full((1, 2 * w)),
                  full((N_BRANCH, w, d)), full((d, d))],
        out_specs=tile(d),
        out_shape=jax.ShapeDtypeStruct(x.shape, F32),
        compiler_params=_params("arbitrary", "arbitrary"),
        name="merge_residual",
    )(y_hy, rw_y0, rw_y1, rw_bonus, rw_g, s5_y[0], s5_y[1], s5_u, zg, x, g1, ones_bd, p["rw_ln_g"].reshape(1, w),
      p["rw_ln_b"].reshape(1, w), p["s5_d"].reshape(1, w), p["s5_glu_w"].astype(BF16),
      p["s5_glu_b"].reshape(1, 2 * w), p["w_branch"].astype(BF16), p["w_out"].astype(BF16))


def _k_router(x_ref, sh_ref, sc_ref, g_ref, rw_ref, rb_ref, h_o, idx_o, gate_o):
    h = _rms_modulate(x_ref[0], g_ref[...], sh_ref[0], sc_ref[0])
    h_o[0] = h.astype(BF16)
    logits = _dot_hi(h, rw_ref[...]) + rb_ref[...]
    lane = lax.broadcasted_iota(jnp.int32, logits.shape, 1)
    vals, idxs = [], []
    for _ in range(TOP_K):
        m = jnp.max(logits, axis=-1, keepdims=True)
        idx = jnp.min(jnp.where(logits == m, lane, LANES), axis=-1, keepdims=True)
        vals.append(m)
        idxs.append(idx)
        logits = jnp.where(lane == idx, -jnp.inf, logits)
    exps = [jnp.exp(v - vals[0]) for v in vals]
    inv = 1.0 / (exps[0] + exps[1] + exps[2] + exps[3])
    idx_out = jnp.zeros(logits.shape, jnp.int32)
    gate_out = jnp.zeros(logits.shape, F32)
    for j in range(TOP_K):
        idx_out = jnp.where(lane == j, idxs[j], idx_out)
        gate_out = jnp.where(lane == j, exps[j] * inv, gate_out)
    idx_o[0] = idx_out
    gate_o[0] = gate_out


def moe_route(x, shift, scale, g, router_w, router_b):
    bsz, n, d = x.shape
    tm = min(ROW_TILE, n)
    rw = jnp.pad(router_w, ((0, 0), (0, LANES - N_EXPERTS)))
    rb = jnp.pad(router_b, (0, LANES - N_EXPERTS), constant_values=-jnp.inf).reshape(1, LANES)
    tile = lambda c: pl.BlockSpec((1, tm, c), lambda i, j: (i, j, 0))
    vec = pl.BlockSpec((1, 1, d), lambda i, j: (i, 0, 0))
    full = lambda shape: pl.BlockSpec(shape, lambda i, j: (0,) * len(shape))
    return pl.pallas_call(
        _k_router,
        grid=(bsz, n // tm),
        in_specs=[tile(d), vec, vec, full((1, d)), full((d, LANES)), full((1, LANES))],
        out_specs=[tile(d), tile(LANES), tile(LANES)],
        out_shape=[jax.ShapeDtypeStruct((bsz, n, d), BF16), jax.ShapeDtypeStruct((bsz, n, LANES), jnp.int32),
                   jax.ShapeDtypeStruct((bsz, n, LANES), F32)],
        compiler_params=_params("arbitrary", "arbitrary"),
        name="moe_route",
    )(x, shift, scale, g.reshape(1, d), rw, rb)


def _k_expert(blk_e_ref, xs_ref, w1_ref, b1_ref, w2_ref, b2_ref, sw_ref, o_ref):
    del blk_e_ref
    hid = _dot(xs_ref[...], w1_ref[0]) + b1_ref[0]
    gl = jnp.minimum(hid[:, 0:D_EXPERT], SWIGLU_LIMIT)
    up = jnp.clip(hid[:, D_EXPERT:2 * D_EXPERT], -SWIGLU_LIMIT, SWIGLU_LIMIT)
    act = (up + 1.0) * gl * _sigmoid(SWIGLU_ALPHA * gl)
    o_ref[...] = (_dot(act.astype(BF16), w2_ref[0]) + b2_ref[0]) * sw_ref[...]


def expert_ffn(xs, blk_e, slot_w, w1, b1, w2, b2):
    n_slots, d = xs.shape
    n_blocks = n_slots // MOE_ROWS
    grid_spec = pltpu.PrefetchScalarGridSpec(
        num_scalar_prefetch=1,
        grid=(n_blocks,),
        in_specs=[pl.BlockSpec((MOE_ROWS, d), lambda i, e: (i, 0)),
                  pl.BlockSpec((1, d, 2 * D_EXPERT), lambda i, e: (e[i], 0, 0)),
                  pl.BlockSpec((1, 1, 2 * D_EXPERT), lambda i, e: (e[i], 0, 0)),
                  pl.BlockSpec((1, D_EXPERT, d), lambda i, e: (e[i], 0, 0)),
                  pl.BlockSpec((1, 1, d), lambda i, e: (e[i], 0, 0)),
                  pl.BlockSpec((MOE_ROWS, 1), lambda i, e: (i, 0))],
        out_specs=pl.BlockSpec((MOE_ROWS, d), lambda i, e: (i, 0)),
    )
    return pl.pallas_call(
        _k_expert,
        grid_spec=grid_spec,
        out_shape=jax.ShapeDtypeStruct((n_slots, d), F32),
        compiler_params=_params("arbitrary"),
        name="expert_ffn",
    )(blk_e, xs, w1, b1.reshape(N_EXPERTS, 1, -1), w2, b2.reshape(N_EXPERTS, 1, -1), slot_w.reshape(-1, 1))


def moe_ffn(x, shift, scale, g, router_w, router_b, w1, b1, w2, b2):
    bsz, n, d = x.shape
    h, idx, gate = moe_route(x, shift, scale, g, router_w, router_b)
    n_tok = bsz * n
    n_asg = n_tok * TOP_K
    flat_e = idx[..., :TOP_K].reshape(-1)
    flat_w = gate[..., :TOP_K].reshape(-1)
    order = jnp.argsort(flat_e).astype(jnp.int32)
    rank_of = jnp.argsort(order).astype(jnp.int32)
    experts = jnp.arange(N_EXPERTS, dtype=jnp.int32)
    grp_end = jnp.searchsorted(flat_e[order], experts, side="right").astype(jnp.int32)
    grp_start = jnp.concatenate([jnp.zeros((1,), jnp.int32), grp_end[:-1]])
    counts = grp_end - grp_start
    padded = (counts + MOE_ROWS - 1) // MOE_ROWS * MOE_ROWS
    pad_end = jnp.cumsum(padded)
    pad_start = pad_end - padded
    n_blocks = -(-n_asg // MOE_ROWS) + N_EXPERTS
    n_slots = n_blocks * MOE_ROWS
    blk_e = jnp.minimum(jnp.searchsorted(pad_end, jnp.arange(n_blocks) * MOE_ROWS, side="right"),
                        N_EXPERTS - 1).astype(jnp.int32)
    slot_e = jnp.repeat(blk_e, MOE_ROWS)
    slot_rank = jnp.arange(n_slots, dtype=jnp.int32) - pad_start[slot_e]
    slot_used = slot_rank < counts[slot_e]
    slot_asg = order[jnp.where(slot_used, grp_start[slot_e] + slot_rank, 0)]
    slot_tok = jnp.where(slot_used, slot_asg // TOP_K, 0)
    slot_w = jnp.where(slot_used, flat_w[slot_asg], 0.0)
    slot_of = pad_start[flat_e] + rank_of - grp_start[flat_e]
    xs = h.reshape(n_tok, d)[slot_tok]
    ys = expert_ffn(xs, blk_e, slot_w, w1, b1, w2, b2)
    out = ys[slot_of].reshape(n_tok, TOP_K, d).sum(axis=1)
    return out.reshape(bsz, n, d)


def _k_rmsnorm(x_ref, g_ref, o_ref):
    x = x_ref[0]
    ms = jnp.mean(x * x, axis=-1, keepdims=True)
    o_ref[0] = x * lax.rsqrt(ms + NORM_EPS) * g_ref[...]


def final_rmsnorm(x, g):
    b, n, d = x.shape
    tm = min(2 * ROW_TILE, n)
    return pl.pallas_call(
        _k_rmsnorm,
        grid=(b, n // tm),
        in_specs=[pl.BlockSpec((1, tm, d), lambda i, j: (i, j, 0)), pl.BlockSpec((1, d), lambda i, j: (0, 0))],
        out_specs=pl.BlockSpec((1, tm, d), lambda i, j: (i, j, 0)),
        out_shape=jax.ShapeDtypeStruct(x.shape, F32),
        compiler_params=_params("arbitrary", "arbitrary"),
        name="final_rmsnorm",
    )(x, g.reshape(1, d))


@functools.lru_cache(maxsize=None)
def _grid_pos_embed(n_tokens):
    rows = n_tokens // GRID_W
    row_id, col_id = np.meshgrid(np.arange(rows), np.arange(GRID_W), indexing="ij")
    quarter = D_MODEL // 4
    omega = (1.0 / (10000.0 ** (np.arange(quarter, dtype=np.float32) / quarter))).astype(np.float32)

    def enc(pos):
        ang = pos.reshape(-1)[:, None].astype(np.float32) * omega
        return np.concatenate([np.sin(ang), np.cos(ang)], axis=-1)

    return np.concatenate([enc(row_id), enc(col_id)], axis=-1).astype(np.float32)


_LAYER_KEYS = ("ada_w", "ada_b", "norm1_g", "norm2_g", "w_in", "hy_conv_w", "hy_conv_b", "hy_w1", "hy_b1",
               "hy_f1", "hy_w2", "hy_b2", "hy_f2", "hy_w3", "hy_b3", "hy_bias", "rw_conv_w", "rw_conv_b",
               "rw_w_up", "rw_w0", "rw_a_up", "rw_a0", "rw_g_up", "rw_k_k", "rw_k_a", "rw_r_k", "rw_ln_g",
               "rw_ln_b", "s5_a_re", "s5_a_im", "s5_log_dt", "s5_b_re", "s5_b_im", "s5_c_re", "s5_c_im",
               "s5_d", "s5_glu_w", "s5_glu_b", "w_branch", "w_out", "router_w", "router_b", "moe_w1",
               "moe_b1", "moe_w2", "moe_b2")


def _token_mixer(x, xc, mod, mod_c, p, need_ctx, ones_bd):
    sh1, sc1, g1 = mod
    csh1, csc1, cg1 = mod_c
    w_in = p["w_in"].astype(BF16)
    c0, c1, c2 = HY_COLS, HY_COLS + RW_COLS, HY_COLS + RW_COLS + MIX_WIDTH
    proj = lambda t, s, c, lo, hi: norm_mod_matmul(t, s, c, p["norm1_g"], w_in[:, lo:hi])
    hy_args = (p["hy_w1"], p["hy_b1"], p["hy_f1"], p["hy_w2"], p["hy_b2"], p["hy_f2"], p["hy_w3"], p["hy_b3"],
               p["hy_bias"])

    z_hy = proj(x, sh1, sc1, 0, c0)
    y_hy = hyena_branch(z_hy, p["hy_conv_w"], p["hy_conv_b"], hyena_spectra(x.shape[1], *hy_args))
    u_rw = short_conv(proj(x, sh1, sc1, c0, c1), p["rw_conv_w"], p["rw_conv_b"])
    uc_rw = short_conv(proj(xc, csh1, csc1, c0, c1), p["rw_conv_w"], p["rw_conv_b"])
    rw = rwkv_scan_branch(uc_rw, u_rw, p, ones_bd)
    n_ctx = xc.shape[1]
    z_s5 = proj(x, sh1, sc1, c1, c2)
    zc_s5 = proj(xc, csh1, csc1, c1, c2)
    s5_y = s5_branch(zc_s5, z_s5, p)
    zg = proj(x, sh1, sc1, c2, c2 + N_BRANCH * D_MODEL)
    x_new = merge_residual(y_hy, rw, s5_y, n_ctx, z_s5, zg, x, g1, p, ones_bd)
    if not need_ctx:
        return x_new, None
    zc_hy = proj(xc, csh1, csc1, 0, c0)
    yc_hy = hyena_branch(zc_hy, p["hy_conv_w"], p["hy_conv_b"], hyena_spectra(xc.shape[1], *hy_args))
    zcg = proj(xc, csh1, csc1, c2, c2 + N_BRANCH * D_MODEL)
    xc_new = merge_residual(yc_hy, rw, s5_y, 0, zc_s5, zcg, xc, cg1, p, ones_bd)
    return x_new, xc_new


def kernel(x, c, ctx, c_ctx, ada_w, ada_b, norm1_g, norm2_g, w_in, hy_conv_w, hy_conv_b, hy_w1, hy_b1, hy_f1, hy_w2, hy_b2, hy_f2, hy_w3, hy_b3, hy_bias, rw_conv_w, rw_conv_b, rw_w_up, rw_w0, rw_a_up, rw_a0, rw_g_up, rw_k_k, rw_k_a, rw_r_k, rw_ln_g, rw_ln_b, s5_a_re, s5_a_im, s5_log_dt, s5_b_re, s5_b_im, s5_c_re, s5_c_im, s5_d, s5_glu_w, s5_glu_b, w_branch, w_out, router_w, router_b, moe_w1, moe_b1, moe_w2, moe_b2, final_g):
    stacked = dict(zip(_LAYER_KEYS, (ada_w, ada_b, norm1_g, norm2_g, w_in, hy_conv_w, hy_conv_b, hy_w1, hy_b1,
                                     hy_f1, hy_w2, hy_b2, hy_f2, hy_w3, hy_b3, hy_bias, rw_conv_w, rw_conv_b,
                                     rw_w_up, rw_w0, rw_a_up, rw_a0, rw_g_up, rw_k_k, rw_k_a, rw_r_k, rw_ln_g,
                                     rw_ln_b, s5_a_re, s5_a_im, s5_log_dt, s5_b_re, s5_b_im, s5_c_re, s5_c_im,
                                     s5_d, s5_glu_w, s5_glu_b, w_branch, w_out, router_w, router_b, moe_w1,
                                     moe_b1, moe_w2, moe_b2)))
    bsz, n, d = x.shape
    depth = ada_w.shape[0]
    x = add_pos(x, jnp.asarray(_grid_pos_embed(n)))
    xc = ctx
    ones_bd = jnp.asarray(np.kron(np.eye(RW_HEADS), np.ones((RW_HEAD, RW_HEAD))), BF16)
    cvec = jnp.zeros((2 * SUBLANES, d), F32).at[:bsz].set(c).at[bsz].set(c_ctx)
    for l in range(depth):
        p = {k: v[l] for k, v in stacked.items()}
        need_ctx = l < depth - 1
        ada = ada_proj(cvec, p["ada_w"], p["ada_b"])
        lat = [t.reshape(bsz, 1, d) for t in jnp.split(ada[:bsz], 6, axis=-1)]
        cx = [jnp.broadcast_to(t.reshape(1, 1, d), (bsz, 1, d)) for t in jnp.split(ada[bsz], 6, axis=-1)]
        x, xc_new = _token_mixer(x, xc, lat[0:3], cx[0:3], p, need_ctx, ones_bd)
        w1, w2 = p["moe_w1"].astype(BF16), p["moe_w2"].astype(BF16)
        moe = functools.partial(moe_ffn, g=p["norm2_g"], router_w=p["router_w"], router_b=p["router_b"],
                                w1=w1, b1=p["moe_b1"], w2=w2, b2=p["moe_b2"])
        x = x + lat[5] * moe(x, lat[3], lat[4])
        if need_ctx:
            xc = xc_new + cx[5] * moe(xc_new, cx[3], cx[4])
    return final_rmsnorm(x, final_g)
```

```python
import functools
import math

import numpy as np
import jax
import jax.numpy as jnp
from jax import lax
from jax.experimental import pallas as pl
from jax.experimental.pallas import tpu as pltpu

F32 = jnp.float32
BF16 = jnp.bfloat16

D_MODEL = 1024
GRID_W = 64
SHORT_CONV = 3
NORM_EPS = 1e-6
N_BRANCH = 3
MIX_WIDTH = 512

HY_ORDER = 2
HY_BANDS = 16
HY_DECAY_TARGET = 1e-2
HY_FAST_PCT = 0.3
HY_SLOW_PCT = 1.5
HY_COLS = (HY_ORDER + 1) * MIX_WIDTH

RW_HEAD = 64
RW_HEADS = MIX_WIDTH // RW_HEAD
RW_DECAY_RANK = 64
RW_ICLR_RANK = 64
RW_GATE_RANK = 128
RW_LN_EPS = 64e-5
RW_COLS = 3 * MIX_WIDTH + RW_DECAY_RANK + RW_ICLR_RANK + RW_GATE_RANK

S5_GROUP = 16
S5_GROUPS = MIX_WIDTH // S5_GROUP
S5_STATE = 64
S5_HID = S5_GROUPS * S5_STATE

N_EXPERTS = 32
TOP_K = 4
D_EXPERT = 1024
SWIGLU_LIMIT = 7.0
SWIGLU_ALPHA = 1.702

LANES = 128
SUBLANES = 8
VMEM_LIMIT_BYTES = 56 * 1024 * 1024

ROW_TILE = 256
MOE_ROWS = 256
WKV_STEPS = 16
WKV_K_UNROLL = 4
S5_STEPS = 64
S5_LANE_CHUNK = 512


def _params(*sem):
    return pltpu.CompilerParams(dimension_semantics=sem, vmem_limit_bytes=VMEM_LIMIT_BYTES)


def _dot(a, b):
    return jnp.dot(a, b, preferred_element_type=F32)


def _split3(x):
    hi = x.astype(BF16)
    r1 = x - hi.astype(F32)
    mid = r1.astype(BF16)
    lo = (r1 - mid.astype(F32)).astype(BF16)
    return hi, mid, lo


def _dot_hi(a, b):
    a0, a1, a2 = _split3(a)
    b0, b1, b2 = _split3(b)
    return (_dot(a0, b0) + (_dot(a0, b1) + _dot(a1, b0))
            + (_dot(a1, b1) + _dot(a0, b2) + _dot(a2, b0)))


def _seg_sum(x, ones_bd):
    hi, mid, lo = _split3(x)
    return _dot(hi, ones_bd) + _dot(mid, ones_bd) + _dot(lo, ones_bd)


def _sigmoid(x):
    return 1.0 / (1.0 + jnp.exp(-x))


def _rms_modulate(x, g, shift, scale):
    ms = jnp.mean(x * x, axis=-1, keepdims=True)
    y = x * lax.rsqrt(ms + NORM_EPS) * g
    return y * (1.0 + scale) + shift


def _k_ada(c_ref, w_ref, b_ref, o_ref):
    c = c_ref[...]
    s = c * _sigmoid(c)
    o_ref[...] = _dot_hi(s, w_ref[...]) + b_ref[...]


def ada_proj(cvec, w, b):
    rows, d = cvec.shape
    n = w.shape[1]
    tn = 1536
    return pl.pallas_call(
        _k_ada,
        grid=(n // tn,),
        in_specs=[pl.BlockSpec((rows, d), lambda j: (0, 0)),
                  pl.BlockSpec((d, tn), lambda j: (0, j)),
                  pl.BlockSpec((1, tn), lambda j: (0, j))],
        out_specs=pl.BlockSpec((rows, tn), lambda j: (0, j)),
        out_shape=jax.ShapeDtypeStruct((rows, n), F32),
        compiler_params=_params("arbitrary"),
        name="ada_proj",
    )(cvec, w, b.reshape(1, n))


def _k_add_pos(x_ref, p_ref, o_ref):
    o_ref[0] = x_ref[0] + p_ref[...]


def add_pos(x, pos):
    b, n, d = x.shape
    tm = min(ROW_TILE * 2, n)
    return pl.pallas_call(
        _k_add_pos,
        grid=(b, n // tm),
        in_specs=[pl.BlockSpec((1, tm, d), lambda i, j: (i, j, 0)),
                  pl.BlockSpec((tm, d), lambda i, j: (j, 0))],
        out_specs=pl.BlockSpec((1, tm, d), lambda i, j: (i, j, 0)),
        out_shape=jax.ShapeDtypeStruct(x.shape, F32),
        compiler_params=_params("arbitrary", "arbitrary"),
        name="add_pos",
    )(x, pos)


def _k_norm_mm(x_ref, sh_ref, sc_ref, g_ref, w_ref, o_ref, h_ref):
    @pl.when(pl.program_id(2) == 0)
    def _():
        h = _rms_modulate(x_ref[0], g_ref[...], sh_ref[0], sc_ref[0])
        h_ref[...] = h.astype(BF16)

    o_ref[0] = _dot(h_ref[...], w_ref[...])


def norm_mod_matmul(x, shift, scale, g, w):
    b, n, d = x.shape
    ncol = w.shape[1]
    tm = min(2 * ROW_TILE, n)
    tn = ncol if ncol <= 1792 else 1536
    return pl.pallas_call(
        _k_norm_mm,
        grid=(b, n // tm, ncol // tn),
        in_specs=[pl.BlockSpec((1, tm, d), lambda i, j, k: (i, j, 0)),
                  pl.BlockSpec((1, 1, d), lambda i, j, k: (i, 0, 0)),
                  pl.BlockSpec((1, 1, d), lambda i, j, k: (i, 0, 0)),
                  pl.BlockSpec((1, d), lambda i, j, k: (0, 0)),
                  pl.BlockSpec((d, tn), lambda i, j, k: (0, k))],
        out_specs=pl.BlockSpec((1, tm, tn), lambda i, j, k: (i, j, k)),
        out_shape=jax.ShapeDtypeStruct((b, n, ncol), F32),
        scratch_shapes=[pltpu.VMEM((tm, d), BF16)],
        compiler_params=_params("arbitrary", "arbitrary", "arbitrary"),
        name="norm_mod_matmul",
    )(x, shift, scale, g.reshape(1, d), w)


def _short_conv_val(z, w, b):
    n = z.shape[0]
    row = lax.broadcasted_iota(jnp.int32, z.shape, 0)
    zm = jnp.where(row == 0, 0.0, pltpu.roll(z, 1, 0))
    zp = jnp.where(row == n - 1, 0.0, pltpu.roll(z, n - 1, 0))
    return zm * w[0:1] + z * w[1:2] + zp * w[2:3] + b


def _k_sconv(z_ref, w_ref, b_ref, o_ref):
    o_ref[0] = _short_conv_val(z_ref[0], w_ref[...], b_ref[...])


def short_conv(z, w, b):
    bsz, n, c = z.shape
    tc = 256 if c % 256 == 0 else LANES
    return pl.pallas_call(
        _k_sconv,
        grid=(bsz, c // tc),
        in_specs=[pl.BlockSpec((1, n, tc), lambda i, j: (i, 0, j)),
                  pl.BlockSpec((SHORT_CONV, tc), lambda i, j: (0, j)),
                  pl.BlockSpec((1, tc), lambda i, j: (0, j))],
        out_specs=pl.BlockSpec((1, n, tc), lambda i, j: (i, 0, j)),
        out_shape=jax.ShapeDtypeStruct(z.shape, F32),
        compiler_params=_params("arbitrary", "arbitrary"),
        name="short_conv",
    )(z, w, b.reshape(1, c))


def _fft_split(n):
    total = 2 * n
    bits = total.bit_length() - 1
    assert 1 << bits == total
    n1 = 1 << ((bits + 1) // 2)
    n1 = max(n1, 2 * SUBLANES)
    return n1, total // n1


@functools.lru_cache(maxsize=None)
def _dft_mats(n):
    n1, n2 = _fft_split(n)
    total = n1 * n2
    k1 = np.arange(n1)[:, None]
    m1 = np.arange(n1)[None, :]
    m2 = np.arange(n2)[:, None, None]
    ang = -2.0 * np.pi * ((n2 * k1 * m1)[None] + m2 * k1[None]) / total
    fa = np.concatenate([np.cos(ang), np.sin(ang)], axis=1)
    k2 = np.arange(n2)[:, None]
    mm = np.arange(n2)[None, :]
    gang = -2.0 * np.pi * k2 * mm / n2
    gr, gi = np.cos(gang), np.sin(gang)
    fb = np.block([[gr, -gi], [gi, gr]])
    fc = np.block([[gr, gi], [-gi, gr]])
    m1d = np.arange(n1 // 2)[:, None]
    k1d = np.arange(n1)[None, :]
    dang = 2.0 * np.pi * ((n2 * m1d * k1d)[None] + m2 * k1d[None]) / total
    fd = np.concatenate([np.cos(dang), -np.sin(dang)], axis=2) / total
    return (fa.astype(np.float32), fb.astype(np.float32), fc.astype(np.float32), fd.astype(np.float32))


def _time_pitch(n2):
    return n2 + SUBLANES


def _spec_pitch(n2):
    return 2 * n2 + SUBLANES


def _fft_stage_a(src_ref, y_ref, fa_ref, n1, n2, k_rows):
    tp, sp = _time_pitch(n2), _spec_pitch(n2)

    def body(m2, c):
        xs = src_ref[pl.ds(m2, k_rows, stride=tp), :].astype(BF16)
        res = _dot(fa_ref[m2, :, 0:k_rows], xs)
        y_ref[pl.ds(m2, n1, stride=sp), :] = res[0:n1]
        y_ref[pl.ds(n2 + m2, n1, stride=sp), :] = res[n1:2 * n1]
        return c
    lax.fori_loop(0, n2, body, 0, unroll=2)


def _long_conv(buf_ref, y_ref, h_ref, fa_ref, fb_ref, fc_ref, fd_ref, n1, n2):
    tp, sp = _time_pitch(n2), _spec_pitch(n2)
    _fft_stage_a(buf_ref, y_ref, fa_ref, n1, n2, n1 // 2)

    def body_k1(k1, c):
        rows = pl.ds(pl.multiple_of(k1 * sp, SUBLANES), 2 * n2)
        z = _dot(fb_ref[...], y_ref[rows, :].astype(BF16))
        zr, zi = z[0:n2], z[n2:2 * n2]
        h = h_ref[k1]
        hr, hi = h[0:n2], h[n2:2 * n2]
        p = jnp.concatenate([zr * hr - zi * hi, zr * hi + zi * hr], axis=0)
        y_ref[rows, :] = _dot(fc_ref[...], p.astype(BF16))
        return c
    lax.fori_loop(0, n1, body_k1, 0, unroll=4)

    def body_m2(m2, c):
        qr = y_ref[pl.ds(m2, n1, stride=sp), :]
        qi = y_ref[pl.ds(n2 + m2, n1, stride=sp), :]
        q = jnp.concatenate([qr, qi], axis=0).astype(BF16)
        buf_ref[pl.ds(m2, n1 // 2, stride=tp), :] = _dot(fd_ref[m2], q)
        return c
    lax.fori_loop(0, n2, body_m2, 0, unroll=2)


def _store_time_blocks(buf_ref, val, n2):
    tp = _time_pitch(n2)
    for m1 in range(val.shape[0] // n2):
        buf_ref[m1 * tp:m1 * tp + n2, :] = val[m1 * n2:(m1 + 1) * n2]


def _load_time_blocks(buf_ref, n, n2):
    tp = _time_pitch(n2)
    return jnp.concatenate([buf_ref[m1 * tp:m1 * tp + n2, :] for m1 in range(n // n2)], axis=0)


def _k_hyena(x1_ref, x2_ref, v_ref, w1_ref, w2_ref, wv_ref, b1_ref, b2_ref, bv_ref, h1_ref, h2_ref,
             fa_ref, fb_ref, fc_ref, fd_ref, o_ref, buf_ref, y_ref, *, n1, n2):
    n = x1_ref.shape[1]
    conv = functools.partial(_long_conv, buf_ref, y_ref, fa_ref=fa_ref, fb_ref=fb_ref, fc_ref=fc_ref,
                             fd_ref=fd_ref, n1=n1, n2=n2)
    _store_time_blocks(buf_ref, _short_conv_val(v_ref[0], wv_ref[...], bv_ref[...]), n2)
    conv(h1_ref)
    gated = _short_conv_val(x1_ref[0], w1_ref[...], b1_ref[...]) * _load_time_blocks(buf_ref, n, n2)
    _store_time_blocks(buf_ref, gated, n2)
    conv(h2_ref)
    o_ref[0] = _short_conv_val(x2_ref[0], w2_ref[...], b2_ref[...]) * _load_time_blocks(buf_ref, n, n2)


def _k_fspec(f_ref, fa_ref, fb_ref, o_ref, y_ref, *, n1, n2):
    sp = _spec_pitch(n2)
    _fft_stage_a(f_ref, y_ref, fa_ref, n1, n2, n1)

    def body_k1(k1, c):
        src = pl.ds(pl.multiple_of(k1 * sp, SUBLANES), 2 * n2)
        dst = pl.ds(pl.multiple_of(k1 * 2 * n2, 2 * n2), 2 * n2)
        o_ref[dst, :] = _dot(fb_ref[...], y_ref[src, :].astype(BF16))
        return c
    lax.fori_loop(0, n1, body_k1, 0, unroll=2)


def filter_spectrum(filt, n):
    n1, n2 = _fft_split(n)
    total, c = filt.shape
    tp, sp = _time_pitch(n2), _spec_pitch(n2)
    fa, fb, _, _ = _dft_mats(n)
    blocks = jnp.pad(filt.reshape(n1, n2, c), ((0, 0), (0, tp - n2), (0, 0))).reshape(n1 * tp, c)
    out = pl.pallas_call(
        functools.partial(_k_fspec, n1=n1, n2=n2),
        grid=(c // LANES,),
        in_specs=[pl.BlockSpec((n1 * tp, LANES), lambda j: (0, j)),
                  pl.BlockSpec((n2, 2 * n1, n1), lambda j: (0, 0, 0)),
                  pl.BlockSpec((2 * n2, 2 * n2), lambda j: (0, 0))],
        out_specs=pl.BlockSpec((total * 2, LANES), lambda j: (0, j)),
        out_shape=jax.ShapeDtypeStruct((2 * total, c), F32),
        scratch_shapes=[pltpu.VMEM((n1 * sp, LANES), F32)],
        compiler_params=_params("arbitrary"),
        name="filter_spectrum",
    )(blocks, jnp.asarray(fa, BF16), jnp.asarray(fb, BF16))
    return out.reshape(n1, 2 * n2, c)


def _k_filter_mlp(feat_ref, win_ref, w1_ref, b1_ref, f1_ref, w2_ref, b2_ref, f2_ref, w3_ref, b3_ref, o_ref):
    h = jnp.sin(f1_ref[...] * (_dot_hi(feat_ref[...], w1_ref[...]) + b1_ref[...]))
    h = jnp.sin(f2_ref[...] * (_dot_hi(h, w2_ref[...]) + b2_ref[...]))
    o_ref[...] = (_dot_hi(h, w3_ref[...]) + b3_ref[...]) * win_ref[...]


@functools.lru_cache(maxsize=None)
def _filter_consts(n):
    t = np.linspace(0.0, 1.0, n, dtype=np.float32)[:, None]
    bands = np.linspace(1e-4, HY_BANDS - 1, HY_BANDS, dtype=np.float32)
    ang = (np.float32(2 * math.pi) * np.arange(n, dtype=np.float32) / np.float32(n))[:, None] * bands
    feats = np.concatenate([t, np.cos(ang), -np.sin(ang)], axis=-1).astype(np.float32)
    pad = (-feats.shape[1]) % SUBLANES
    feats = np.pad(feats, ((0, 0), (0, pad)))
    deltas = np.abs(np.linspace(math.log(HY_DECAY_TARGET) / HY_SLOW_PCT,
                                math.log(HY_DECAY_TARGET) / HY_FAST_PCT, MIX_WIDTH, dtype=np.float32))
    window = np.exp(-t * deltas).astype(np.float32)
    return feats, np.tile(window, (1, 2 * HY_ORDER))


def hyena_filter_table(n, w1, b1, f1, w2, b2, f2, w3, b3):
    feats, window = _filter_consts(n)
    kf = feats.shape[1]
    fd = w1.shape[1]
    ncol = w3.shape[1]
    w1p = jnp.pad(w1, ((0, kf - w1.shape[0]), (0, 0)))
    tm = min(ROW_TILE, n)
    full = lambda shape: pl.BlockSpec(shape, lambda i: (0,) * len(shape))
    return pl.pallas_call(
        _k_filter_mlp,
        grid=(n // tm,),
        in_specs=[pl.BlockSpec((tm, kf), lambda i: (i, 0)),
                  pl.BlockSpec((tm, ncol), lambda i: (i, 0)),
                  full((kf, fd)), full((1, fd)), full((1, fd)),
                  full((fd, fd)), full((1, fd)), full((1, fd)),
                  full((fd, ncol)), full((1, ncol))],
        out_specs=pl.BlockSpec((tm, ncol), lambda i: (i, 0)),
        out_shape=jax.ShapeDtypeStruct((n, ncol), F32),
        compiler_params=_params("arbitrary"),
        name="hyena_filter_mlp",
    )(jnp.asarray(feats), jnp.asarray(window), w1p, b1.reshape(1, fd), f1.reshape(1, fd),
      w2, b2.reshape(1, fd), f2.reshape(1, fd), w3, b3.reshape(1, ncol))


def hyena_spectra(n, w1, b1, f1, w2, b2, f2, w3, b3, bias):
    hf = hyena_filter_table(n, w1, b1, f1, w2, b2, f2, w3, b3).reshape(n, HY_ORDER, 2, MIX_WIDTH)
    fwd = hf[:, :, 0]
    bwd = hf[1:, :, 1][::-1]
    fwd = fwd.at[0].add(bias)
    filt = jnp.concatenate([fwd, jnp.zeros((1, HY_ORDER, MIX_WIDTH), F32), bwd], axis=0)
    spec = filter_spectrum(filt.reshape(2 * n, HY_ORDER * MIX_WIDTH), n)
    n1, n2 = _fft_split(n)
    return spec.reshape(n1, 2 * n2, HY_ORDER, MIX_WIDTH).transpose(2, 0, 1, 3)


def hyena_branch(z, conv_w, conv_b, spectra):
    bsz, n, _ = z.shape
    n1, n2 = _fft_split(n)
    fa, fb, fc, fd = (jnp.asarray(m, BF16) for m in _dft_mats(n))
    fa = fa[:, :, : n1 // 2]
    nt = MIX_WIDTH // LANES
    cb = conv_b.reshape(1, HY_COLS)
    zspec = lambda off: pl.BlockSpec((1, n, LANES), lambda j, i: (i, 0, off + j))
    wspec = lambda off: pl.BlockSpec((SHORT_CONV, LANES), lambda j, i: (0, off + j))
    bspec = lambda off: pl.BlockSpec((1, LANES), lambda j, i: (0, off + j))
    once = pl.Buffered(1)
    hspec = lambda o: pl.BlockSpec((None, n1, 2 * n2, LANES), lambda j, i: (o, 0, 0, j), pipeline_mode=once)
    full = lambda shape: pl.BlockSpec(shape, lambda j, i: (0,) * len(shape), pipeline_mode=once)
    return pl.pallas_call(
        functools.partial(_k_hyena, n1=n1, n2=n2),
        grid=(nt, bsz),
        in_specs=[zspec(0), zspec(nt), zspec(2 * nt), wspec(0), wspec(nt), wspec(2 * nt),
                  bspec(0), bspec(nt), bspec(2 * nt), hspec(0), hspec(1),
                  full(fa.shape), full(fb.shape), full(fc.shape), full(fd.shape)],
        out_specs=pl.BlockSpec((1, n, LANES), lambda j, i: (i, 0, j)),
        out_shape=jax.ShapeDtypeStruct((bsz, n, MIX_WIDTH), F32),
        scratch_shapes=[pltpu.VMEM((n1 // 2 * _time_pitch(n2), LANES), F32),
                        pltpu.VMEM((n1 * _spec_pitch(n2), LANES), F32)],
        compiler_params=_params("arbitrary", "arbitrary"),
        name="hyena_long_conv",
    )(z, z, z, conv_w, conv_w, conv_w, cb, cb, cb, spectra, spectra, fa, fb, fc, fd)


def _store_head_pairs(o_ref, left, right):
    rows = left.shape[0]
    lane = lax.broadcasted_iota(jnp.int32, (rows, LANES), 1)
    low = lane < RW_HEAD
    for q in range(RW_HEADS // 2):
        lcol = left[:, q * LANES:(q + 1) * LANES]
        rcol = right[:, q * LANES:(q + 1) * LANES]
        heads = (jnp.where(low, lcol, pltpu.roll(rcol, RW_HEAD, 1)),
                 jnp.where(low, pltpu.roll(lcol, RW_HEAD, 1), rcol))
        for h, piece in zip((2 * q, 2 * q + 1), heads):
            for c in range(rows // WKV_STEPS):
                o_ref[c, h * WKV_STEPS:(h + 1) * WKV_STEPS, :] = piece[c * WKV_STEPS:(c + 1) * WKV_STEPS]


def _load_head_sum(y0_ref, y1_ref, h):
    sl = slice(h * WKV_STEPS, (h + 1) * WKV_STEPS)
    return jnp.concatenate([y0_ref[c, sl, :] + y1_ref[c, sl, :] for c in range(y0_ref.shape[0])], axis=0)


def _k_rwprep(uc_ref, ul_ref, gup_ref, wup_ref, w0_ref, aup_ref, a0_ref, kk_ref, ka_ref, rk_ref, e_ref,
              g_o, bonus_o, rv_o, nb0_o, nb1_o, wk0_o, wk1_o, *, ctx_tiles):
    w = MIX_WIDTH
    u = jnp.where(pl.program_id(1) < ctx_tiles, uc_ref[0], ul_ref[0])
    r, k, v = u[:, 0:w], u[:, w:2 * w], u[:, 2 * w:3 * w]
    xw = u[:, 3 * w:3 * w + RW_DECAY_RANK]
    xa = u[:, 3 * w + RW_DECAY_RANK:3 * w + RW_DECAY_RANK + RW_ICLR_RANK]
    xg = u[:, 3 * w + RW_DECAY_RANK + RW_ICLR_RANK:]
    ones_bd = e_ref[...]
    g_o[0] = _dot(_sigmoid(xg).astype(BF16), gup_ref[...])
    kk = k * kk_ref[...]
    kk = kk * lax.rsqrt(jnp.maximum(_seg_sum(kk * kk, ones_bd), 1e-24))
    txw = jnp.tanh(xw)
    kd_sum = None
    for d, (nb_o, wk_o) in enumerate(((nb0_o, wk0_o), (nb1_o, wk1_o))):
        x = -(w0_ref[d] + _dot_hi(txw, wup_ref[d]))
        softplus = jnp.maximum(x, 0.0) + jnp.log(1.0 + jnp.exp(-jnp.abs(x)))
        decay = jnp.exp(-jnp.exp(-softplus - 0.5))
        a = _sigmoid(a0_ref[d] + _dot_hi(xa, aup_ref[d]))
        kd = k * (1.0 + (a - 1.0) * ka_ref[...])
        _store_head_pairs(nb_o, -kk, kk * a)
        _store_head_pairs(wk_o, decay, kd)
        kd_sum = kd if kd_sum is None else kd_sum + kd
    _store_head_pairs(rv_o, r, v)
    bonus_o[0] = _seg_sum(r * kd_sum * rk_ref[...], ones_bd) * v


def rwkv_prep(u_ctx, u_lat, g_up, w_up, w0, a_up, a0, k_k, k_a, r_k, ones_bd):
    bsz, n_ctx, cols = u_ctx.shape
    n_lat = u_lat.shape[1]
    t_total = n_ctx + n_lat
    w = MIX_WIDTH
    tm = min(ROW_TILE, n_ctx)
    ctx_tiles = n_ctx // tm
    full = lambda shape: pl.BlockSpec(shape, lambda i, j: (0,) * len(shape))
    tile = pl.BlockSpec((1, tm, w), lambda i, j: (i, j, 0))
    pair = pl.BlockSpec((tm // WKV_STEPS, RW_HEADS * WKV_STEPS, LANES), lambda i, j: (j, i, 0))
    pair_shape = jax.ShapeDtypeStruct((t_total // WKV_STEPS, bsz * RW_HEADS * WKV_STEPS, LANES), F32)
    return pl.pallas_call(
        functools.partial(_k_rwprep, ctx_tiles=ctx_tiles),
        grid=(bsz, t_total // tm),
        in_specs=[pl.BlockSpec((1, tm, cols), lambda i, j: (i, jnp.minimum(j, ctx_tiles - 1), 0)),
                  pl.BlockSpec((1, tm, cols), lambda i, j: (i, jnp.maximum(j - ctx_tiles, 0), 0)),
                  full((RW_GATE_RANK, w)), full((2, RW_DECAY_RANK, w)), full((2, 1, w)),
                  full((2, RW_ICLR_RANK, w)), full((2, 1, w)), full((1, w)), full((1, w)), full((1, w)),
                  full((w, w))],
        out_specs=[tile, tile] + [pair] * 5,
        out_shape=[jax.ShapeDtypeStruct((bsz, t_total, w), F32)] * 2 + [pair_shape] * 5,
        compiler_params=_params("arbitrary", "arbitrary"),
        name="rwkv_prep",
    )(u_ctx, u_lat, g_up.astype(BF16), w_up, w0.reshape(2, 1, w), a_up, a0.reshape(2, 1, w),
      k_k.reshape(1, w), k_a.reshape(1, w), r_k.reshape(1, w), ones_bd)


def _k_wkv(rv0_ref, rv1_ref, nb0_ref, nb1_ref, wk0_ref, wk1_ref, y0_ref, y1_ref, s_ref, op_ref, yb_ref, *,
           steps):
    @pl.when(pl.program_id(0) == 0)
    def _():
        s_ref[...] = jnp.zeros_like(s_ref)

    chains = rv0_ref.shape[0] // steps
    r_op, v_op, a_op, b_op, w_op, k_op = range(6)

    def rows(t):
        return pl.ds(t, chains, stride=steps), pl.ds(steps - 1 - t, chains, stride=steps)

    for t in range(steps):
        fwd, bwd = rows(t)
        for j, (ref0, ref1) in enumerate(((rv0_ref, rv1_ref), (nb0_ref, nb1_ref), (wk0_ref, wk1_ref))):
            both = jnp.concatenate([ref0[fwd, :], ref1[bwd, :]], axis=0)
            op_ref[t, 2 * j:2 * j + 2] = both.T.reshape(2, RW_HEAD, 2 * chains)

    def advance(t, sa):
        nxt = jnp.minimum(t + 1, steps - 1)
        row = lambda op, i, slot=t: op_ref[slot, op, pl.ds(i, 1), :]
        vv = op_ref[t, v_op]

        def k_rows(ib, acc):
            y, sa_next = acc
            for u in range(WKV_K_UNROLL):
                i = ib * WKV_K_UNROLL + u
                s_new = s_ref[i] * row(w_op, i) + sa * row(b_op, i) + vv * row(k_op, i)
                s_ref[i] = s_new
                y = y + s_new * row(r_op, i)
                sa_next = sa_next + s_new * row(a_op, i, nxt)
            return y, sa_next
        zero = jnp.zeros(s_ref.shape[1:], F32)
        y, sa_next = lax.fori_loop(0, RW_HEAD // WKV_K_UNROLL, k_rows, (zero, zero))
        yb_ref[t] = y
        return sa_next

    sa = jnp.zeros(s_ref.shape[1:], F32)
    for i in range(RW_HEAD):
        sa = sa + s_ref[i] * op_ref[0, a_op, pl.ds(i, 1), :]
    lax.fori_loop(0, steps, advance, sa)

    for t in range(steps):
        y = yb_ref[t]
        yt = jnp.concatenate([y, y], axis=0).T
        fwd, bwd = rows(t)
        y0_ref[fwd, :] = yt[0:chains]
        y1_ref[bwd, :] = yt[chains:2 * chains]


def wkv_scan(rv, nb0, nb1, wk0, wk1, n_ctx):
    nb, rows, _ = rv.shape
    steps = WKV_STEPS
    chains = rows // steps
    nc = n_ctx // steps
    fwd = pl.BlockSpec((None, rows, LANES), lambda i: (i, 0, 0))
    bwd = pl.BlockSpec((None, rows, LANES), lambda i: (jnp.where(i < nc, nc - 1 - i, nb - 1 + nc - i), 0, 0))
    out = jax.ShapeDtypeStruct(rv.shape, F32)
    return pl.pallas_call(
        functools.partial(_k_wkv, steps=steps),
        grid=(nb,),
        in_specs=[fwd, bwd, fwd, bwd, fwd, bwd],
        out_specs=[fwd, bwd],
        out_shape=[out, out],
        scratch_shapes=[pltpu.VMEM((RW_HEAD, RW_HEAD, 2 * chains), F32),
                        pltpu.VMEM((steps, 6, RW_HEAD, 2 * chains), F32),
                        pltpu.VMEM((steps, RW_HEAD, 2 * chains), F32)],
        compiler_params=_params("arbitrary"),
        name="wkv_scan",
    )(rv, rv, nb0, nb1, wk0, wk1)


def rwkv_scan_branch(u_ctx, u_lat, p, ones_bd):
    g, bonus, rv, nb0, nb1, wk0, wk1 = rwkv_prep(
        u_ctx, u_lat, p["rw_g_up"], p["rw_w_up"], p["rw_w0"], p["rw_a_up"], p["rw_a0"], p["rw_k_k"],
        p["rw_k_a"], p["rw_r_k"].reshape(-1), ones_bd)
    y0, y1 = wkv_scan(rv, nb0, nb1, wk0, wk1, u_ctx.shape[1])
    return y0, y1, bonus, g


def _s5_block(i, n_ctx_blocks, n_blocks, reverse):
    if not reverse:
        return i
    return jnp.where(i < n_ctx_blocks, n_ctx_blocks - 1 - i, n_blocks - 1 + n_ctx_blocks - i)


def _k_s5(uc_ref, ul_ref, bre_ref, bim_ref, lre_ref, lim_ref, cre_ref, cim_ref, y_ref, hre_ref, him_ref,
          st_ref, tb_ref, *, steps, reverse, n_ctx_blocks, n_blocks):
    @pl.when(pl.program_id(0) == 0)
    def _():
        st_ref[...] = jnp.zeros_like(st_ref)

    bsz = uc_ref.shape[0]
    nq = MIX_WIDTH // LANES
    in_ctx = _s5_block(pl.program_id(0), n_ctx_blocks, n_blocks, reverse) < n_ctx_blocks
    for b in range(bsz):
        ub = jnp.where(in_ctx, uc_ref[b], ul_ref[b])
        for q in range(nq):
            tb_ref[q, pl.ds(b, steps, stride=bsz), :] = ub[:, q * LANES:(q + 1) * LANES]
    u = jnp.concatenate([tb_ref[q] for q in range(nq)], axis=1).astype(BF16)
    hre_ref[...] = _dot(u, bre_ref[...])
    him_ref[...] = _dot(u, bim_ref[...])
    for ch in range(S5_HID // S5_LANE_CHUNK):
        cols = slice(ch * S5_LANE_CHUNK, (ch + 1) * S5_LANE_CHUNK)
        lre = lre_ref[:, cols]
        lim = lim_ref[:, cols]

        def step(i, carry):
            hr, hi = carry
            t = (steps - 1 - i) if reverse else i
            rows = pl.ds(pl.multiple_of(t * bsz, bsz), bsz)
            nr = lre * hr - lim * hi + hre_ref[rows, cols]
            ni = lre * hi + lim * hr + him_ref[rows, cols]
            hre_ref[rows, cols] = nr
            him_ref[rows, cols] = ni
            return nr, ni
        hr, hi = lax.fori_loop(0, steps, step, (st_ref[0, :, cols], st_ref[1, :, cols]))
        st_ref[0, :, cols] = hr
        st_ref[1, :, cols] = hi
    y = _dot(hre_ref[...].astype(BF16), cre_ref[...]) - _dot(him_ref[...].astype(BF16), cim_ref[...])
    for q in range(nq):
        tb_ref[q] = y[:, q * LANES:(q + 1) * LANES]
    for b in range(bsz):
        for q in range(nq):
            y_ref[b, :, q * LANES:(q + 1) * LANES] = tb_ref[q, pl.ds(b, steps, stride=bsz), :]


def s5_scan(z_ctx, z_lat, bre, bim, lre, lim, cre, cim, reverse):
    bsz, n_ctx, w = z_ctx.shape
    t_total = n_ctx + z_lat.shape[1]
    steps = S5_STEPS
    nc = n_ctx // steps
    nb = t_total // steps
    blk = functools.partial(_s5_block, n_ctx_blocks=nc, n_blocks=nb, reverse=reverse)
    full = lambda shape: pl.BlockSpec(shape, lambda i: (0,) * len(shape))
    return pl.pallas_call(
        functools.partial(_k_s5, steps=steps, reverse=reverse, n_ctx_blocks=nc, n_blocks=nb),
        grid=(nb,),
        in_specs=[pl.BlockSpec((bsz, steps, w), lambda i: (0, jnp.minimum(blk(i), nc - 1), 0)),
                  pl.BlockSpec((bsz, steps, w), lambda i: (0, jnp.maximum(blk(i) - nc, 0), 0)),
                  full((w, S5_HID)), full((w, S5_HID)), full((bsz, S5_HID)), full((bsz, S5_HID)),
                  full((S5_HID, w)), full((S5_HID, w))],
        out_specs=pl.BlockSpec((bsz, steps, w), lambda i: (0, blk(i), 0)),
        out_shape=jax.ShapeDtypeStruct((bsz, t_total, w), F32),
        scratch_shapes=[pltpu.VMEM((steps * bsz, S5_HID), F32), pltpu.VMEM((steps * bsz, S5_HID), F32),
                        pltpu.VMEM((2, bsz, S5_HID), F32), pltpu.VMEM((w // LANES, steps * bsz, LANES), F32)],
        compiler_params=_params("arbitrary"),
        name="s5_scan",
    )(z_ctx, z_lat, bre, bim, lre, lim, cre, cim)


def _k_s5disc(are_ref, aim_ref, ldt_ref, lre_ref, lim_ref, cr_ref, ci_ref):
    a_re, a_im = are_ref[...], aim_ref[...]
    dt = jnp.exp(ldt_ref[...])
    mag = jnp.exp(a_re * dt)
    lb_re, lb_im = mag * jnp.cos(a_im * dt), mag * jnp.sin(a_im * dt)
    den = a_re * a_re + a_im * a_im
    nr = lb_re - 1.0
    lre_ref[...] = lb_re
    lim_ref[...] = lb_im
    cr_ref[...] = (nr * a_re + lb_im * a_im) / den
    ci_ref[...] = (lb_im * a_re - nr * a_im) / den


def s5_operators(a_re, a_im, log_dt, b_re, b_im, c_re, c_im, bsz):
    g, p, h = S5_GROUPS, S5_STATE, S5_GROUP
    shp = jax.ShapeDtypeStruct((2 * g, p), F32)
    lb_re, lb_im, cr, ci = pl.pallas_call(
        _k_s5disc, out_shape=[shp] * 4, name="s5_discretise",
    )(a_re.reshape(2 * g, p), a_im.reshape(2 * g, p), jnp.broadcast_to(log_dt.reshape(2 * g, 1), (2 * g, p)))
    cr, ci = cr.reshape(2, g, p, 1), ci.reshape(2, g, p, 1)
    bb_re = cr * b_re[None] - ci * b_im[None]
    bb_im = cr * b_im[None] + ci * b_re[None]
    eye = jnp.eye(g, dtype=F32)
    bd_in = lambda m: jnp.einsum("dgph,gk->dghkp", m, eye).reshape(2, g * h, g * p).astype(BF16)
    bd_out = lambda m: jnp.einsum("ghp,gk->gpkh", m, eye).reshape(g * p, g * h).astype(BF16)
    lam = lambda m: jnp.broadcast_to(m.reshape(2, 1, g * p), (2, bsz, g * p))
    return bd_in(bb_re), bd_in(bb_im), lam(lb_re), lam(lb_im), bd_out(c_re), bd_out(c_im)


def s5_branch(z_ctx, z_lat, p):
    bsz = z_ctx.shape[0]
    bre, bim, lre, lim, cre, cim = s5_operators(p["s5_a_re"], p["s5_a_im"], p["s5_log_dt"], p["s5_b_re"],
                                                p["s5_b_im"], p["s5_c_re"], p["s5_c_im"], bsz)
    return (s5_scan(z_ctx, z_lat, bre[0], bim[0], lre[0], lim[0], cre, cim, False),
            s5_scan(z_ctx, z_lat, bre[1], bim[1], lre[1], lim[1], cre, cim, True))


def _k_merge(yhy_ref, rwy0_ref, rwy1_ref, rwb_ref, rwg_ref, s5y0_ref, s5y1_ref, s5u_ref, zg_ref, x_ref, g1_ref,
             e_ref, lng_ref, lnb_ref, s5d_ref, gluw_ref, glub_ref, wb_ref, wo_ref, o_ref):
    w = MIX_WIDTH
    ones_bd = e_ref[...]
    lane = lax.broadcasted_iota(jnp.int32, (x_ref.shape[1], LANES), 1)
    cols = []
    for q in range(RW_HEADS // 2):
        even = _load_head_sum(rwy0_ref, rwy1_ref, 2 * q)
        odd = _load_head_sum(rwy0_ref, rwy1_ref, 2 * q + 1)
        cols.append(jnp.where(lane < RW_HEAD, even, pltpu.roll(odd, RW_HEAD, 1)))
    y = jnp.concatenate(cols, axis=1) + rwb_ref[0]
    mu = _seg_sum(y, ones_bd) * (1.0 / RW_HEAD)
    yc = y - mu
    var = _seg_sum(yc * yc, ones_bd) * (1.0 / RW_HEAD)
    y_rw = (yc * lax.rsqrt(var + RW_LN_EPS) * lng_ref[...] + lnb_ref[...]) * rwg_ref[0]
    s = s5y0_ref[0] + s5y1_ref[0] + s5u_ref[0] * s5d_ref[...]
    s = 0.5 * s * (1.0 + jnp.tanh(math.sqrt(2.0 / math.pi) * (s + 0.044715 * (s * s * s))))
    lg = _dot(s.astype(BF16), gluw_ref[...]) + glub_ref[...]
    y_s5 = lg[:, 0:w] * _sigmoid(lg[:, w:2 * w])
    zg = zg_ref[0]
    d = D_MODEL
    m = (_sigmoid(zg[:, 0:d]) * _dot(yhy_ref[0].astype(BF16), wb_ref[0])
         + _sigmoid(zg[:, d:2 * d]) * _dot(y_rw.astype(BF16), wb_ref[1])
         + _sigmoid(zg[:, 2 * d:3 * d]) * _dot(y_s5.astype(BF16), wb_ref[2]))
    o_ref[0] = x_ref[0] + g1_ref[0] * _dot(m.astype(BF16), wo_ref[...])


def merge_residual(y_hy, rw, s5_y, t_off, s5_u, zg, x, g1, p, ones_bd):
    bsz, n, d = x.shape
    w = MIX_WIDTH
    tm = min(ROW_TILE, n)
    assert t_off % tm == 0
    off = t_off // tm
    tile = lambda c: pl.BlockSpec((1, tm, c), lambda i, j: (i, j, 0))
    seq_tile = pl.BlockSpec((1, tm, w), lambda i, j: (i, off + j, 0))
    chain_tile = pl.BlockSpec((tm // WKV_STEPS, RW_HEADS * WKV_STEPS, LANES), lambda i, j: (off + j, i, 0))
    full = lambda shape: pl.BlockSpec(shape, lambda i, j: (0,) * len(shape))
    rw_y0, rw_y1, rw_bonus, rw_g = rw
    return pl.pallas_call(
        _k_merge,
        grid=(bsz, n // tm),
        in_specs=[tile(w), chain_tile, chain_tile, seq_tile, seq_tile, seq_tile, seq_tile, tile(w),
                  tile(N_BRANCH * d), tile(d), pl.BlockSpec((1, 1, d), lambda i, j: (i, 0, 0)),
                  full((w, w)), full((1, w)), full((1, w)), full((1, w)), full((w, 2 * w)), full((1, 2 * w)),
                  full((N_BRANCH, w, d)), full((d, d))],
        out_specs=tile(d),
        out_shape=jax.ShapeDtypeStruct(x.shape, F32),
        compiler_params=_params("arbitrary", "arbitrary"),
        name="merge_residual",
    )(y_hy, rw_y0, rw_y1, rw_bonus, rw_g, s5_y[0], s5_y[1], s5_u, zg, x, g1, ones_bd, p["rw_ln_g"].reshape(1, w),
      p["rw_ln_b"].reshape(1, w), p["s5_d"].reshape(1, w), p["s5_glu_w"].astype(BF16),
      p["s5_glu_b"].reshape(1, 2 * w), p["w_branch"].astype(BF16), p["w_out"].astype(BF16))


def _k_router(x_ref, sh_ref, sc_ref, g_ref, rw_ref, rb_ref, h_o, idx_o, gate_o, cnt_o):
    @pl.when((pl.program_id(0) == 0) & (pl.program_id(1) == 0))
    def _():
        cnt_o[...] = jnp.zeros_like(cnt_o)

    h = _rms_modulate(x_ref[0], g_ref[...], sh_ref[0], sc_ref[0])
    h_o[0] = h.astype(BF16)
    logits = _dot_hi(h, rw_ref[...]) + rb_ref[...]
    lane = lax.broadcasted_iota(jnp.int32, logits.shape, 1)
    vals, idxs = [], []
    for _ in range(TOP_K):
        m = jnp.max(logits, axis=-1, keepdims=True)
        idx = jnp.min(jnp.where(logits == m, lane, LANES), axis=-1, keepdims=True)
        vals.append(m)
        idxs.append(idx)
        logits = jnp.where(lane == idx, -jnp.inf, logits)
    exps = [jnp.exp(v - vals[0]) for v in vals]
    inv = 1.0 / (exps[0] + exps[1] + exps[2] + exps[3])
    idx_out = jnp.zeros(logits.shape, jnp.int32)
    gate_out = jnp.zeros(logits.shape, F32)
    picked = jnp.zeros(logits.shape, F32)
    for j in range(TOP_K):
        idx_out = jnp.where(lane == j, idxs[j], idx_out)
        gate_out = jnp.where(lane == j, exps[j] * inv, gate_out)
        picked = picked + jnp.where(lane == idxs[j], 1.0, 0.0)
    idx_o[0] = idx_out
    gate_o[0] = gate_out
    cnt_o[...] = cnt_o[...] + jnp.sum(picked, axis=0, keepdims=True)


def moe_route(x, shift, scale, g, router_w, router_b):
    bsz, n, d = x.shape
    tm = min(ROW_TILE, n)
    rw = jnp.pad(router_w, ((0, 0), (0, LANES - N_EXPERTS)))
    rb = jnp.pad(router_b, (0, LANES - N_EXPERTS), constant_values=-jnp.inf).reshape(1, LANES)
    tile = lambda c: pl.BlockSpec((1, tm, c), lambda i, j: (i, j, 0))
    vec = pl.BlockSpec((1, 1, d), lambda i, j: (i, 0, 0))
    full = lambda shape: pl.BlockSpec(shape, lambda i, j: (0,) * len(shape))
    return pl.pallas_call(
        _k_router,
        grid=(bsz, n // tm),
        in_specs=[tile(d), vec, vec, full((1, d)), full((d, LANES)), full((1, LANES))],
        out_specs=[tile(d), tile(LANES), tile(LANES), full((SUBLANES, LANES))],
        out_shape=[jax.ShapeDtypeStruct((bsz, n, d), BF16), jax.ShapeDtypeStruct((bsz, n, LANES), jnp.int32),
                   jax.ShapeDtypeStruct((bsz, n, LANES), F32), jax.ShapeDtypeStruct((SUBLANES, LANES), F32)],
        compiler_params=_params("arbitrary", "arbitrary"),
        name="moe_route",
    )(x, shift, scale, g.reshape(1, d), rw, rb)


def _k_expert(blk_e_ref, xs_ref, w1_ref, b1_ref, w2_ref, b2_ref, sw_ref, o_ref, w1b_ref, w2b_ref):
    i = pl.program_id(0)

    @pl.when((i == 0) | (blk_e_ref[i] != blk_e_ref[jnp.maximum(i - 1, 0)]))
    def _():
        w1b_ref[...] = w1_ref[0].astype(BF16)
        w2b_ref[...] = w2_ref[0].astype(BF16)

    hid = _dot(xs_ref[...], w1b_ref[...]) + b1_ref[0]
    gl = jnp.minimum(hid[:, 0:D_EXPERT], SWIGLU_LIMIT)
    up = jnp.clip(hid[:, D_EXPERT:2 * D_EXPERT], -SWIGLU_LIMIT, SWIGLU_LIMIT)
    act = (up + 1.0) * gl * _sigmoid(SWIGLU_ALPHA * gl)
    o_ref[...] = ((_dot(act.astype(BF16), w2b_ref[...]) + b2_ref[0]) * sw_ref[...]).astype(o_ref.dtype)


def expert_ffn(xs, blk_e, slot_w, w1, b1, w2, b2):
    n_slots, d = xs.shape
    n_blocks = n_slots // MOE_ROWS
    grid_spec = pltpu.PrefetchScalarGridSpec(
        num_scalar_prefetch=1,
        grid=(n_blocks,),
        in_specs=[pl.BlockSpec((MOE_ROWS, d), lambda i, e: (i, 0)),
                  pl.BlockSpec((1, d, 2 * D_EXPERT), lambda i, e: (e[i], 0, 0)),
                  pl.BlockSpec((1, 1, 2 * D_EXPERT), lambda i, e: (e[i], 0, 0)),
                  pl.BlockSpec((1, D_EXPERT, d), lambda i, e: (e[i], 0, 0)),
                  pl.BlockSpec((1, 1, d), lambda i, e: (e[i], 0, 0)),
                  pl.BlockSpec((MOE_ROWS, 1), lambda i, e: (i, 0))],
        out_specs=pl.BlockSpec((MOE_ROWS, d), lambda i, e: (i, 0)),
        scratch_shapes=[pltpu.VMEM((d, 2 * D_EXPERT), BF16), pltpu.VMEM((D_EXPERT, d), BF16)],
    )
    return pl.pallas_call(
        _k_expert,
        grid_spec=grid_spec,
        out_shape=jax.ShapeDtypeStruct((n_slots, d), BF16),
        compiler_params=_params("arbitrary"),
        name="expert_ffn",
    )(blk_e, xs, w1, b1.reshape(N_EXPERTS, 1, -1), w2, b2.reshape(N_EXPERTS, 1, -1), slot_w.reshape(-1, 1))


def moe_ffn(x, shift, scale, g, router_w, router_b, w1, b1, w2, b2):
    bsz, n, d = x.shape
    h, idx, gate, cnt = moe_route(x, shift, scale, g, router_w, router_b)
    n_tok = bsz * n
    n_asg = n_tok * TOP_K
    flat_e = idx[..., :TOP_K].reshape(-1)
    flat_w = gate[..., :TOP_K].reshape(-1)
    order = jnp.argsort(flat_e).astype(jnp.int32)
    rank_of = jnp.argsort(order).astype(jnp.int32)
    counts = cnt[0, :N_EXPERTS].astype(jnp.int32)
    grp_start = jnp.cumsum(counts) - counts
    padded = (counts + MOE_ROWS - 1) // MOE_ROWS * MOE_ROWS
    pad_end = jnp.cumsum(padded)
    pad_start = pad_end - padded
    n_blocks = -(-n_asg // MOE_ROWS) + N_EXPERTS
    n_slots = n_blocks * MOE_ROWS
    blk_first = jnp.arange(n_blocks, dtype=jnp.int32) * MOE_ROWS
    blk_e = jnp.minimum(jnp.sum(pad_end[None, :] <= blk_first[:, None], axis=1), N_EXPERTS - 1).astype(jnp.int32)
    slot_e = jnp.repeat(blk_e, MOE_ROWS)
    slot_rank = jnp.arange(n_slots, dtype=jnp.int32) - pad_start[slot_e]
    slot_used = slot_rank < counts[slot_e]
    slot_asg = order[jnp.where(slot_used, grp_start[slot_e] + slot_rank, 0)]
    slot_tok = jnp.where(slot_used, slot_asg // TOP_K, 0)
    slot_w = jnp.where(slot_used, flat_w[slot_asg], 0.0)
    slot_of = pad_start[flat_e] + rank_of - grp_start[flat_e]
    xs = h.reshape(n_tok, d)[slot_tok]
    ys = expert_ffn(xs, blk_e, slot_w, w1, b1, w2, b2)
    slot_of = slot_of.reshape(n_tok, TOP_K)
    out = sum(ys[slot_of[:, j]].astype(F32) for j in range(TOP_K))
    return out.reshape(bsz, n, d)


def _k_rmsnorm(x_ref, g_ref, o_ref):
    x = x_ref[0]
    ms = jnp.mean(x * x, axis=-1, keepdims=True)
    o_ref[0] = x * lax.rsqrt(ms + NORM_EPS) * g_ref[...]


def final_rmsnorm(x, g):
    b, n, d = x.shape
    tm = min(2 * ROW_TILE, n)
    return pl.pallas_call(
        _k_rmsnorm,
        grid=(b, n // tm),
        in_specs=[pl.BlockSpec((1, tm, d), lambda i, j: (i, j, 0)), pl.BlockSpec((1, d), lambda i, j: (0, 0))],
        out_specs=pl.BlockSpec((1, tm, d), lambda i, j: (i, j, 0)),
        out_shape=jax.ShapeDtypeStruct(x.shape, F32),
        compiler_params=_params("arbitrary", "arbitrary"),
        name="final_rmsnorm",
    )(x, g.reshape(1, d))


@functools.lru_cache(maxsize=None)
def _grid_pos_embed(n_tokens):
    rows = n_tokens // GRID_W
    row_id, col_id = np.meshgrid(np.arange(rows), np.arange(GRID_W), indexing="ij")
    quarter = D_MODEL // 4
    omega = (1.0 / (10000.0 ** (np.arange(quarter, dtype=np.float32) / quarter))).astype(np.float32)

    def enc(pos):
        ang = pos.reshape(-1)[:, None].astype(np.float32) * omega
        return np.concatenate([np.sin(ang), np.cos(ang)], axis=-1)

    return np.concatenate([enc(row_id), enc(col_id)], axis=-1).astype(np.float32)


_LAYER_KEYS = ("ada_w", "ada_b", "norm1_g", "norm2_g", "w_in", "hy_conv_w", "hy_conv_b", "hy_w1", "hy_b1",
               "hy_f1", "hy_w2", "hy_b2", "hy_f2", "hy_w3", "hy_b3", "hy_bias", "rw_conv_w", "rw_conv_b",
               "rw_w_up", "rw_w0", "rw_a_up", "rw_a0", "rw_g_up", "rw_k_k", "rw_k_a", "rw_r_k", "rw_ln_g",
               "rw_ln_b", "s5_a_re", "s5_a_im", "s5_log_dt", "s5_b_re", "s5_b_im", "s5_c_re", "s5_c_im",
               "s5_d", "s5_glu_w", "s5_glu_b", "w_branch", "w_out", "router_w", "router_b", "moe_w1",
               "moe_b1", "moe_w2", "moe_b2")


def _token_mixer(x, xc, mod, mod_c, p, need_ctx, ones_bd):
    sh1, sc1, g1 = mod
    csh1, csc1, cg1 = mod_c
    w_in = p["w_in"].astype(BF16)
    c0, c1, c2 = HY_COLS, HY_COLS + RW_COLS, HY_COLS + RW_COLS + MIX_WIDTH
    proj = lambda t, s, c, lo, hi: norm_mod_matmul(t, s, c, p["norm1_g"], w_in[:, lo:hi])
    hy_args = (p["hy_w1"], p["hy_b1"], p["hy_f1"], p["hy_w2"], p["hy_b2"], p["hy_f2"], p["hy_w3"], p["hy_b3"],
               p["hy_bias"])

    z_hy = proj(x, sh1, sc1, 0, c0)
    y_hy = hyena_branch(z_hy, p["hy_conv_w"], p["hy_conv_b"], hyena_spectra(x.shape[1], *hy_args))
    u_rw = short_conv(proj(x, sh1, sc1, c0, c1), p["rw_conv_w"], p["rw_conv_b"])
    uc_rw = short_conv(proj(xc, csh1, csc1, c0, c1), p["rw_conv_w"], p["rw_conv_b"])
    rw = rwkv_scan_branch(uc_rw, u_rw, p, ones_bd)
    n_ctx = xc.shape[1]
    z_s5 = proj(x, sh1, sc1, c1, c2)
    zc_s5 = proj(xc, csh1, csc1, c1, c2)
    s5_y = s5_branch(zc_s5, z_s5, p)
    zg = proj(x, sh1, sc1, c2, c2 + N_BRANCH * D_MODEL)
    x_new = merge_residual(y_hy, rw, s5_y, n_ctx, z_s5, zg, x, g1, p, ones_bd)
    if not need_ctx:
        return x_new, None
    zc_hy = proj(xc, csh1, csc1, 0, c0)
    yc_hy = hyena_branch(zc_hy, p["hy_conv_w"], p["hy_conv_b"], hyena_spectra(xc.shape[1], *hy_args))
    zcg = proj(xc, csh1, csc1, c2, c2 + N_BRANCH * D_MODEL)
    xc_new = merge_residual(yc_hy, rw, s5_y, 0, zc_s5, zcg, xc, cg1, p, ones_bd)
    return x_new, xc_new


def kernel(x, c, ctx, c_ctx, ada_w, ada_b, norm1_g, norm2_g, w_in, hy_conv_w, hy_conv_b, hy_w1, hy_b1, hy_f1, hy_w2, hy_b2, hy_f2, hy_w3, hy_b3, hy_bias, rw_conv_w, rw_conv_b, rw_w_up, rw_w0, rw_a_up, rw_a0, rw_g_up, rw_k_k, rw_k_a, rw_r_k, rw_ln_g, rw_ln_b, s5_a_re, s5_a_im, s5_log_dt, s5_b_re, s5_b_im, s5_c_re, s5_c_im, s5_d, s5_glu_w, s5_glu_b, w_branch, w_out, router_w, router_b, moe_w1, moe_b1, moe_w2, moe_b2, final_g):
    stacked = dict(zip(_LAYER_KEYS, (ada_w, ada_b, norm1_g, norm2_g, w_in, hy_conv_w, hy_conv_b, hy_w1, hy_b1,
                                     hy_f1, hy_w2, hy_b2, hy_f2, hy_w3, hy_b3, hy_bias, rw_conv_w, rw_conv_b,
                                     rw_w_up, rw_w0, rw_a_up, rw_a0, rw_g_up, rw_k_k, rw_k_a, rw_r_k, rw_ln_g,
                                     rw_ln_b, s5_a_re, s5_a_im, s5_log_dt, s5_b_re, s5_b_im, s5_c_re, s5_c_im,
                                     s5_d, s5_glu_w, s5_glu_b, w_branch, w_out, router_w, router_b, moe_w1,
                                     moe_b1, moe_w2, moe_b2)))
    bsz, n, d = x.shape
    depth = ada_w.shape[0]
    x = add_pos(x, jnp.asarray(_grid_pos_embed(n)))
    xc = ctx
    ones_bd = jnp.asarray(np.kron(np.eye(RW_HEADS), np.ones((RW_HEAD, RW_HEAD))), BF16)
    cvec = jnp.zeros((2 * SUBLANES, d), F32).at[:bsz].set(c).at[bsz].set(c_ctx)
    for l in range(depth):
        p = {k: v[l] for k, v in stacked.items()}
        need_ctx = l < depth - 1
        ada = ada_proj(cvec, p["ada_w"], p["ada_b"])
        lat = [t.reshape(bsz, 1, d) for t in jnp.split(ada[:bsz], 6, axis=-1)]
        cx = [jnp.broadcast_to(t.reshape(1, 1, d), (bsz, 1, d)) for t in jnp.split(ada[bsz], 6, axis=-1)]
        x, xc_new = _token_mixer(x, xc, lat[0:3], cx[0:3], p, need_ctx, ones_bd)
        moe = functools.partial(moe_ffn, g=p["norm2_g"], router_w=p["router_w"], router_b=p["router_b"],
                                w1=p["moe_w1"], b1=p["moe_b1"], w2=p["moe_w2"], b2=p["moe_b2"])
        x = x + lat[5] * moe(x, lat[3], lat[4])
        if need_ctx:
            xc = xc_new + cx[5] * moe(xc_new, cx[3], cx[4])
    return final_rmsnorm(x, final_g)
```

```python
import functools
import math

import numpy as np
import jax
import jax.numpy as jnp
from jax import lax
from jax.experimental import pallas as pl
from jax.experimental.pallas import tpu as pltpu

F32 = jnp.float32
BF16 = jnp.bfloat16

D_MODEL = 1024
GRID_W = 64
SHORT_CONV = 3
NORM_EPS = 1e-6
N_BRANCH = 3
MIX_WIDTH = 512

HY_ORDER = 2
HY_BANDS = 16
HY_DECAY_TARGET = 1e-2
HY_FAST_PCT = 0.3
HY_SLOW_PCT = 1.5
HY_COLS = (HY_ORDER + 1) * MIX_WIDTH

RW_HEAD = 64
RW_HEADS = MIX_WIDTH // RW_HEAD
RW_DECAY_RANK = 64
RW_ICLR_RANK = 64
RW_GATE_RANK = 128
RW_LN_EPS = 64e-5
RW_COLS = 3 * MIX_WIDTH + RW_DECAY_RANK + RW_ICLR_RANK + RW_GATE_RANK

S5_GROUP = 16
S5_GROUPS = MIX_WIDTH // S5_GROUP
S5_STATE = 64
S5_HID = S5_GROUPS * S5_STATE

N_EXPERTS = 32
TOP_K = 4
D_EXPERT = 1024
SWIGLU_LIMIT = 7.0
SWIGLU_ALPHA = 1.702

LANES = 128
SUBLANES = 8
VMEM_LIMIT_BYTES = 56 * 1024 * 1024

ROW_TILE = 256
MOE_ROWS = 256
WKV_STEPS = 16
WKV_K_UNROLL = 4
S5_STEPS = 64
S5_LANE_CHUNK = 512


def _params(*sem):
    return pltpu.CompilerParams(dimension_semantics=sem, vmem_limit_bytes=VMEM_LIMIT_BYTES)


def _dot(a, b):
    return jnp.dot(a, b, preferred_element_type=F32)


def _split3(x):
    hi = x.astype(BF16)
    r1 = x - hi.astype(F32)
    mid = r1.astype(BF16)
    lo = (r1 - mid.astype(F32)).astype(BF16)
    return hi, mid, lo


def _dot_hi(a, b):
    a0, a1, a2 = _split3(a)
    b0, b1, b2 = _split3(b)
    return (_dot(a0, b0) + (_dot(a0, b1) + _dot(a1, b0))
            + (_dot(a1, b1) + _dot(a0, b2) + _dot(a2, b0)))


def _seg_sum(x, ones_bd):
    hi, mid, lo = _split3(x)
    return _dot(hi, ones_bd) + _dot(mid, ones_bd) + _dot(lo, ones_bd)


def _sigmoid(x):
    return 1.0 / (1.0 + jnp.exp(-x))


def _rms_modulate(x, g, shift, scale):
    ms = jnp.mean(x * x, axis=-1, keepdims=True)
    y = x * lax.rsqrt(ms + NORM_EPS) * g
    return y * (1.0 + scale) + shift


def _k_ada(c_ref, w_ref, b_ref, o_ref):
    c = c_ref[...]
    s = c * _sigmoid(c)
    o_ref[...] = _dot_hi(s, w_ref[...]) + b_ref[...]


def ada_proj(cvec, w, b):
    rows, d = cvec.shape
    n = w.shape[1]
    tn = 1536
    return pl.pallas_call(
        _k_ada,
        grid=(n // tn,),
        in_specs=[pl.BlockSpec((rows, d), lambda j: (0, 0)),
                  pl.BlockSpec((d, tn), lambda j: (0, j)),
                  pl.BlockSpec((1, tn), lambda j: (0, j))],
        out_specs=pl.BlockSpec((rows, tn), lambda j: (0, j)),
        out_shape=jax.ShapeDtypeStruct((rows, n), F32),
        compiler_params=_params("arbitrary"),
        name="ada_proj",
    )(cvec, w, b.reshape(1, n))


def _k_add_pos(x_ref, p_ref, o_ref):
    o_ref[0] = x_ref[0] + p_ref[...]


def add_pos(x, pos):
    b, n, d = x.shape
    tm = min(ROW_TILE * 2, n)
    return pl.pallas_call(
        _k_add_pos,
        grid=(b, n // tm),
        in_specs=[pl.BlockSpec((1, tm, d), lambda i, j: (i, j, 0)),
                  pl.BlockSpec((tm, d), lambda i, j: (j, 0))],
        out_specs=pl.BlockSpec((1, tm, d), lambda i, j: (i, j, 0)),
        out_shape=jax.ShapeDtypeStruct(x.shape, F32),
        compiler_params=_params("arbitrary", "arbitrary"),
        name="add_pos",
    )(x, pos)


def _k_norm_mm(x_ref, sh_ref, sc_ref, g_ref, w_ref, o_ref, h_ref):
    @pl.when(pl.program_id(2) == 0)
    def _():
        h = _rms_modulate(x_ref[0], g_ref[...], sh_ref[0], sc_ref[0])
        h_ref[...] = h.astype(BF16)

    o_ref[0] = _dot(h_ref[...], w_ref[...])


def norm_mod_matmul(x, shift, scale, g, w):
    b, n, d = x.shape
    ncol = w.shape[1]
    tm = min(2 * ROW_TILE, n)
    tn = ncol if ncol <= 1792 else 1536
    return pl.pallas_call(
        _k_norm_mm,
        grid=(b, n // tm, ncol // tn),
        in_specs=[pl.BlockSpec((1, tm, d), lambda i, j, k: (i, j, 0)),
                  pl.BlockSpec((1, 1, d), lambda i, j, k: (i, 0, 0)),
                  pl.BlockSpec((1, 1, d), lambda i, j, k: (i, 0, 0)),
                  pl.BlockSpec((1, d), lambda i, j, k: (0, 0)),
                  pl.BlockSpec((d, tn), lambda i, j, k: (0, k))],
        out_specs=pl.BlockSpec((1, tm, tn), lambda i, j, k: (i, j, k)),
        out_shape=jax.ShapeDtypeStruct((b, n, ncol), F32),
        scratch_shapes=[pltpu.VMEM((tm, d), BF16)],
        compiler_params=_params("arbitrary", "arbitrary", "arbitrary"),
        name="norm_mod_matmul",
    )(x, shift, scale, g.reshape(1, d), w)


def _short_conv_val(z, w, b):
    n = z.shape[0]
    row = lax.broadcasted_iota(jnp.int32, z.shape, 0)
    zm = jnp.where(row == 0, 0.0, pltpu.roll(z, 1, 0))
    zp = jnp.where(row == n - 1, 0.0, pltpu.roll(z, n - 1, 0))
    return zm * w[0:1] + z * w[1:2] + zp * w[2:3] + b


def _k_sconv(z_ref, w_ref, b_ref, o_ref):
    o_ref[0] = _short_conv_val(z_ref[0], w_ref[...], b_ref[...])


def short_conv(z, w, b):
    bsz, n, c = z.shape
    tc = 256 if c % 256 == 0 else LANES
    return pl.pallas_call(
        _k_sconv,
        grid=(bsz, c // tc),
        in_specs=[pl.BlockSpec((1, n, tc), lambda i, j: (i, 0, j)),
                  pl.BlockSpec((SHORT_CONV, tc), lambda i, j: (0, j)),
                  pl.BlockSpec((1, tc), lambda i, j: (0, j))],
        out_specs=pl.BlockSpec((1, n, tc), lambda i, j: (i, 0, j)),
        out_shape=jax.ShapeDtypeStruct(z.shape, F32),
        compiler_params=_params("arbitrary", "arbitrary"),
        name="short_conv",
    )(z, w, b.reshape(1, c))


def _fft_split(n):
    total = 2 * n
    bits = total.bit_length() - 1
    assert 1 << bits == total
    n1 = 1 << ((bits + 1) // 2)
    n1 = max(n1, 2 * SUBLANES)
    return n1, total // n1


@functools.lru_cache(maxsize=None)
def _dft_mats(n):
    n1, n2 = _fft_split(n)
    total = n1 * n2
    k1 = np.arange(n1)[:, None]
    m1 = np.arange(n1)[None, :]
    m2 = np.arange(n2)[:, None, None]
    ang = -2.0 * np.pi * ((n2 * k1 * m1)[None] + m2 * k1[None]) / total
    fa = np.concatenate([np.cos(ang), np.sin(ang)], axis=1)
    k2 = np.arange(n2)[:, None]
    mm = np.arange(n2)[None, :]
    gang = -2.0 * np.pi * k2 * mm / n2
    gr, gi = np.cos(gang), np.sin(gang)
    fb = np.block([[gr, -gi], [gi, gr]])
    fc = np.block([[gr, gi], [-gi, gr]])
    m1d = np.arange(n1 // 2)[:, None]
    k1d = np.arange(n1)[None, :]
    dang = 2.0 * np.pi * ((n2 * m1d * k1d)[None] + m2 * k1d[None]) / total
    fd = np.concatenate([np.cos(dang), -np.sin(dang)], axis=2) / total
    return (fa.astype(np.float32), fb.astype(np.float32), fc.astype(np.float32), fd.astype(np.float32))


def _time_pitch(n2):
    return n2 + SUBLANES


def _spec_pitch(n2):
    return 2 * n2 + SUBLANES


def _fft_stage_a(src_ref, y_ref, fa_ref, n1, n2, k_rows):
    tp, sp = _time_pitch(n2), _spec_pitch(n2)

    def body(m2, c):
        xs = src_ref[pl.ds(m2, k_rows, stride=tp), :].astype(BF16)
        res = _dot(fa_ref[m2, :, 0:k_rows], xs)
        y_ref[pl.ds(m2, n1, stride=sp), :] = res[0:n1]
        y_ref[pl.ds(n2 + m2, n1, stride=sp), :] = res[n1:2 * n1]
        return c
    lax.fori_loop(0, n2, body, 0, unroll=4)


def _long_conv(buf_ref, y_ref, h_ref, fa_ref, fb_ref, fc_ref, fd_ref, n1, n2):
    tp, sp = _time_pitch(n2), _spec_pitch(n2)
    _fft_stage_a(buf_ref, y_ref, fa_ref, n1, n2, n1 // 2)

    def body_k1(k1, c):
        rows = pl.ds(pl.multiple_of(k1 * sp, SUBLANES), 2 * n2)
        z = _dot(fb_ref[...], y_ref[rows, :].astype(BF16))
        zr, zi = z[0:n2], z[n2:2 * n2]
        h = h_ref[k1]
        hr, hi = h[0:n2], h[n2:2 * n2]
        p = jnp.concatenate([zr * hr - zi * hi, zr * hi + zi * hr], axis=0)
        y_ref[rows, :] = _dot(fc_ref[...], p.astype(BF16))
        return c
    lax.fori_loop(0, n1, body_k1, 0, unroll=8)

    def body_m2(m2, c):
        qr = y_ref[pl.ds(m2, n1, stride=sp), :]
        qi = y_ref[pl.ds(n2 + m2, n1, stride=sp), :]
        q = jnp.concatenate([qr, qi], axis=0).astype(BF16)
        buf_ref[pl.ds(m2, n1 // 2, stride=tp), :] = _dot(fd_ref[m2], q)
        return c
    lax.fori_loop(0, n2, body_m2, 0, unroll=4)


def _store_time_blocks(buf_ref, val, n2):
    tp = _time_pitch(n2)
    for m1 in range(val.shape[0] // n2):
        buf_ref[m1 * tp:m1 * tp + n2, :] = val[m1 * n2:(m1 + 1) * n2]


def _load_time_blocks(buf_ref, n, n2):
    tp = _time_pitch(n2)
    return jnp.concatenate([buf_ref[m1 * tp:m1 * tp + n2, :] for m1 in range(n // n2)], axis=0)


def _k_hyena(x1_ref, x2_ref, v_ref, w1_ref, w2_ref, wv_ref, b1_ref, b2_ref, bv_ref, h1_ref, h2_ref,
             fa_ref, fb_ref, fc_ref, fd_ref, o_ref, buf_ref, y_ref, *, n1, n2):
    n = x1_ref.shape[1]
    conv = functools.partial(_long_conv, buf_ref, y_ref, fa_ref=fa_ref, fb_ref=fb_ref, fc_ref=fc_ref,
                             fd_ref=fd_ref, n1=n1, n2=n2)
    _store_time_blocks(buf_ref, _short_conv_val(v_ref[0], wv_ref[...], bv_ref[...]), n2)
    conv(h1_ref)
    gated = _short_conv_val(x1_ref[0], w1_ref[...], b1_ref[...]) * _load_time_blocks(buf_ref, n, n2)
    _store_time_blocks(buf_ref, gated, n2)
    conv(h2_ref)
    o_ref[0] = _short_conv_val(x2_ref[0], w2_ref[...], b2_ref[...]) * _load_time_blocks(buf_ref, n, n2)


def _k_fspec(f_ref, fa_ref, fb_ref, o_ref, y_ref, *, n1, n2):
    sp = _spec_pitch(n2)
    _fft_stage_a(f_ref, y_ref, fa_ref, n1, n2, n1)

    def body_k1(k1, c):
        src = pl.ds(pl.multiple_of(k1 * sp, SUBLANES), 2 * n2)
        dst = pl.ds(pl.multiple_of(k1 * 2 * n2, 2 * n2), 2 * n2)
        o_ref[dst, :] = _dot(fb_ref[...], y_ref[src, :].astype(BF16))
        return c
    lax.fori_loop(0, n1, body_k1, 0, unroll=2)


def filter_spectrum(filt, n):
    n1, n2 = _fft_split(n)
    total, c = filt.shape
    tp, sp = _time_pitch(n2), _spec_pitch(n2)
    fa, fb, _, _ = _dft_mats(n)
    blocks = jnp.pad(filt.reshape(n1, n2, c), ((0, 0), (0, tp - n2), (0, 0))).reshape(n1 * tp, c)
    out = pl.pallas_call(
        functools.partial(_k_fspec, n1=n1, n2=n2),
        grid=(c // LANES,),
        in_specs=[pl.BlockSpec((n1 * tp, LANES), lambda j: (0, j)),
                  pl.BlockSpec((n2, 2 * n1, n1), lambda j: (0, 0, 0)),
                  pl.BlockSpec((2 * n2, 2 * n2), lambda j: (0, 0))],
        out_specs=pl.BlockSpec((total * 2, LANES), lambda j: (0, j)),
        out_shape=jax.ShapeDtypeStruct((2 * total, c), F32),
        scratch_shapes=[pltpu.VMEM((n1 * sp, LANES), F32)],
        compiler_params=_params("arbitrary"),
        name="filter_spectrum",
    )(blocks, jnp.asarray(fa, BF16), jnp.asarray(fb, BF16))
    return out.reshape(n1, 2 * n2, c)


def _k_filter_mlp(feat_ref, win_ref, w1_ref, b1_ref, f1_ref, w2_ref, b2_ref, f2_ref, w3_ref, b3_ref, o_ref):
    h = jnp.sin(f1_ref[...] * (_dot_hi(feat_ref[...], w1_ref[...]) + b1_ref[...]))
    h = jnp.sin(f2_ref[...] * (_dot_hi(h, w2_ref[...]) + b2_ref[...]))
    o_ref[...] = (_dot_hi(h, w3_ref[...]) + b3_ref[...]) * win_ref[...]


@functools.lru_cache(maxsize=None)
def _filter_consts(n):
    t = np.linspace(0.0, 1.0, n, dtype=np.float32)[:, None]
    bands = np.linspace(1e-4, HY_BANDS - 1, HY_BANDS, dtype=np.float32)
    ang = (np.float32(2 * math.pi) * np.arange(n, dtype=np.float32) / np.float32(n))[:, None] * bands
    feats = np.concatenate([t, np.cos(ang), -np.sin(ang)], axis=-1).astype(np.float32)
    pad = (-feats.shape[1]) % SUBLANES
    feats = np.pad(feats, ((0, 0), (0, pad)))
    deltas = np.abs(np.linspace(math.log(HY_DECAY_TARGET) / HY_SLOW_PCT,
                                math.log(HY_DECAY_TARGET) / HY_FAST_PCT, MIX_WIDTH, dtype=np.float32))
    window = np.exp(-t * deltas).astype(np.float32)
    return feats, np.tile(window, (1, 2 * HY_ORDER))


def hyena_filter_table(n, w1, b1, f1, w2, b2, f2, w3, b3):
    feats, window = _filter_consts(n)
    kf = feats.shape[1]
    fd = w1.shape[1]
    ncol = w3.shape[1]
    w1p = jnp.pad(w1, ((0, kf - w1.shape[0]), (0, 0)))
    tm = min(ROW_TILE, n)
    full = lambda shape: pl.BlockSpec(shape, lambda i: (0,) * len(shape))
    return pl.pallas_call(
        _k_filter_mlp,
        grid=(n // tm,),
        in_specs=[pl.BlockSpec((tm, kf), lambda i: (i, 0)),
                  pl.BlockSpec((tm, ncol), lambda i: (i, 0)),
                  full((kf, fd)), full((1, fd)), full((1, fd)),
                  full((fd, fd)), full((1, fd)), full((1, fd)),
                  full((fd, ncol)), full((1, ncol))],
        out_specs=pl.BlockSpec((tm, ncol), lambda i: (i, 0)),
        out_shape=jax.ShapeDtypeStruct((n, ncol), F32),
        compiler_params=_params("arbitrary"),
        name="hyena_filter_mlp",
    )(jnp.asarray(feats), jnp.asarray(window), w1p, b1.reshape(1, fd), f1.reshape(1, fd),
      w2, b2.reshape(1, fd), f2.reshape(1, fd), w3, b3.reshape(1, ncol))


def hyena_spectra(n, w1, b1, f1, w2, b2, f2, w3, b3, bias):
    hf = hyena_filter_table(n, w1, b1, f1, w2, b2, f2, w3, b3).reshape(n, HY_ORDER, 2, MIX_WIDTH)
    fwd = hf[:, :, 0]
    bwd = hf[1:, :, 1][::-1]
    fwd = fwd.at[0].add(bias)
    filt = jnp.concatenate([fwd, jnp.zeros((1, HY_ORDER, MIX_WIDTH), F32), bwd], axis=0)
    spec = filter_spectrum(filt.reshape(2 * n, HY_ORDER * MIX_WIDTH), n)
    n1, n2 = _fft_split(n)
    return spec.reshape(n1, 2 * n2, HY_ORDER, MIX_WIDTH).transpose(2, 0, 1, 3)


def hyena_branch(z, conv_w, conv_b, spectra):
    bsz, n, _ = z.shape
    n1, n2 = _fft_split(n)
    fa, fb, fc, fd = (jnp.asarray(m, BF16) for m in _dft_mats(n))
    fa = fa[:, :, : n1 // 2]
    nt = MIX_WIDTH // LANES
    cb = conv_b.reshape(1, HY_COLS)
    zspec = lambda off: pl.BlockSpec((1, n, LANES), lambda j, i: (i, 0, off + j))
    wspec = lambda off: pl.BlockSpec((SHORT_CONV, LANES), lambda j, i: (0, off + j))
    bspec = lambda off: pl.BlockSpec((1, LANES), lambda j, i: (0, off + j))
    once = pl.Buffered(1)
    hspec = lambda o: pl.BlockSpec((None, n1, 2 * n2, LANES), lambda j, i: (o, 0, 0, j), pipeline_mode=once)
    full = lambda shape: pl.BlockSpec(shape, lambda j, i: (0,) * len(shape), pipeline_mode=once)
    return pl.pallas_call(
        functools.partial(_k_hyena, n1=n1, n2=n2),
        grid=(nt, bsz),
        in_specs=[zspec(0), zspec(nt), zspec(2 * nt), wspec(0), wspec(nt), wspec(2 * nt),
                  bspec(0), bspec(nt), bspec(2 * nt), hspec(0), hspec(1),
                  full(fa.shape), full(fb.shape), full(fc.shape), full(fd.shape)],
        out_specs=pl.BlockSpec((1, n, LANES), lambda j, i: (i, 0, j)),
        out_shape=jax.ShapeDtypeStruct((bsz, n, MIX_WIDTH), F32),
        scratch_shapes=[pltpu.VMEM((n1 // 2 * _time_pitch(n2), LANES), F32),
                        pltpu.VMEM((n1 * _spec_pitch(n2), LANES), F32)],
        compiler_params=_params("arbitrary", "arbitrary"),
        name="hyena_long_conv",
    )(z, z, z, conv_w, conv_w, conv_w, cb, cb, cb, spectra, spectra, fa, fb, fc, fd)


def _store_head_pairs(o_ref, left, right):
    rows = left.shape[0]
    lane = lax.broadcasted_iota(jnp.int32, (rows, LANES), 1)
    low = lane < RW_HEAD
    for q in range(RW_HEADS // 2):
        lcol = left[:, q * LANES:(q + 1) * LANES]
        rcol = right[:, q * LANES:(q + 1) * LANES]
        heads = (jnp.where(low, lcol, pltpu.roll(rcol, RW_HEAD, 1)),
                 jnp.where(low, pltpu.roll(lcol, RW_HEAD, 1), rcol))
        for h, piece in zip((2 * q, 2 * q + 1), heads):
            for c in range(rows // WKV_STEPS):
                o_ref[c, h * WKV_STEPS:(h + 1) * WKV_STEPS, :] = piece[c * WKV_STEPS:(c + 1) * WKV_STEPS]


def _load_head_sum(y0_ref, y1_ref, h):
    sl = slice(h * WKV_STEPS, (h + 1) * WKV_STEPS)
    return jnp.concatenate([y0_ref[c, sl, :] + y1_ref[c, sl, :] for c in range(y0_ref.shape[0])], axis=0)


def _k_rwprep(uc_ref, ul_ref, gup_ref, wup_ref, w0_ref, aup_ref, a0_ref, kk_ref, ka_ref, rk_ref, e_ref,
              g_o, bonus_o, rv_o, nb0_o, nb1_o, wk0_o, wk1_o, *, ctx_tiles):
    w = MIX_WIDTH
    u = jnp.where(pl.program_id(1) < ctx_tiles, uc_ref[0], ul_ref[0])
    r, k, v = u[:, 0:w], u[:, w:2 * w], u[:, 2 * w:3 * w]
    xw = u[:, 3 * w:3 * w + RW_DECAY_RANK]
    xa = u[:, 3 * w + RW_DECAY_RANK:3 * w + RW_DECAY_RANK + RW_ICLR_RANK]
    xg = u[:, 3 * w + RW_DECAY_RANK + RW_ICLR_RANK:]
    ones_bd = e_ref[...]
    g_o[0] = _dot(_sigmoid(xg).astype(BF16), gup_ref[...])
    kk = k * kk_ref[...]
    kk = kk * lax.rsqrt(jnp.maximum(_seg_sum(kk * kk, ones_bd), 1e-24))
    txw = jnp.tanh(xw)
    kd_sum = None
    for d, (nb_o, wk_o) in enumerate(((nb0_o, wk0_o), (nb1_o, wk1_o))):
        x = -(w0_ref[d] + _dot_hi(txw, wup_ref[d]))
        softplus = jnp.maximum(x, 0.0) + jnp.log(1.0 + jnp.exp(-jnp.abs(x)))
        decay = jnp.exp(-jnp.exp(-softplus - 0.5))
        a = _sigmoid(a0_ref[d] + _dot_hi(xa, aup_ref[d]))
        kd = k * (1.0 + (a - 1.0) * ka_ref[...])
        _store_head_pairs(nb_o, -kk, kk * a)
        _store_head_pairs(wk_o, decay, kd)
        kd_sum = kd if kd_sum is None else kd_sum + kd
    _store_head_pairs(rv_o, r, v)
    bonus_o[0] = _seg_sum(r * kd_sum * rk_ref[...], ones_bd) * v


def rwkv_prep(u_ctx, u_lat, g_up, w_up, w0, a_up, a0, k_k, k_a, r_k, ones_bd):
    bsz, n_ctx, cols = u_ctx.shape
    n_lat = u_lat.shape[1]
    t_total = n_ctx + n_lat
    w = MIX_WIDTH
    tm = min(ROW_TILE, n_ctx)
    ctx_tiles = n_ctx // tm
    full = lambda shape: pl.BlockSpec(shape, lambda i, j: (0,) * len(shape))
    tile = pl.BlockSpec((1, tm, w), lambda i, j: (i, j, 0))
    pair = pl.BlockSpec((tm // WKV_STEPS, RW_HEADS * WKV_STEPS, LANES), lambda i, j: (j, i, 0))
    pair_shape = jax.ShapeDtypeStruct((t_total // WKV_STEPS, bsz * RW_HEADS * WKV_STEPS, LANES), F32)
    return pl.pallas_call(
        functools.partial(_k_rwprep, ctx_tiles=ctx_tiles),
        grid=(bsz, t_total // tm),
        in_specs=[pl.BlockSpec((1, tm, cols), lambda i, j: (i, jnp.minimum(j, ctx_tiles - 1), 0)),
                  pl.BlockSpec((1, tm, cols), lambda i, j: (i, jnp.maximum(j - ctx_tiles, 0), 0)),
                  full((RW_GATE_RANK, w)), full((2, RW_DECAY_RANK, w)), full((2, 1, w)),
                  full((2, RW_ICLR_RANK, w)), full((2, 1, w)), full((1, w)), full((1, w)), full((1, w)),
                  full((w, w))],
        out_specs=[tile, tile] + [pair] * 5,
        out_shape=[jax.ShapeDtypeStruct((bsz, t_total, w), F32)] * 2 + [pair_shape] * 5,
        compiler_params=_params("arbitrary", "arbitrary"),
        name="rwkv_prep",
    )(u_ctx, u_lat, g_up.astype(BF16), w_up, w0.reshape(2, 1, w), a_up, a0.reshape(2, 1, w),
      k_k.reshape(1, w), k_a.reshape(1, w), r_k.reshape(1, w), ones_bd)


def _k_wkv(rv0_ref, rv1_ref, nb0_ref, nb1_ref, wk0_ref, wk1_ref, y0_ref, y1_ref, s_ref, op_ref, yb_ref, *,
           steps):
    @pl.when(pl.program_id(0) == 0)
    def _():
        s_ref[...] = jnp.zeros_like(s_ref)

    chains = rv0_ref.shape[0] // steps
    r_op, v_op, a_op, b_op, w_op, k_op = range(6)

    def rows(t):
        return pl.ds(t, chains, stride=steps), pl.ds(steps - 1 - t, chains, stride=steps)

    for t in range(steps):
        fwd, bwd = rows(t)
        for j, (ref0, ref1) in enumerate(((rv0_ref, rv1_ref), (nb0_ref, nb1_ref), (wk0_ref, wk1_ref))):
            both = jnp.concatenate([ref0[fwd, :], ref1[bwd, :]], axis=0)
            op_ref[t, 2 * j:2 * j + 2] = both.T.reshape(2, RW_HEAD, 2 * chains)

    def advance(t, sa):
        nxt = jnp.minimum(t + 1, steps - 1)
        row = lambda op, i, slot=t: op_ref[slot, op, pl.ds(i, 1), :]
        vv = op_ref[t, v_op]

        def k_rows(ib, acc):
            y, sa_next = acc
            for u in range(WKV_K_UNROLL):
                i = ib * WKV_K_UNROLL + u
                s_new = s_ref[i] * row(w_op, i) + sa * row(b_op, i) + vv * row(k_op, i)
                s_ref[i] = s_new
                y = y + s_new * row(r_op, i)
                sa_next = sa_next + s_new * row(a_op, i, nxt)
            return y, sa_next
        zero = jnp.zeros(s_ref.shape[1:], F32)
        y, sa_next = lax.fori_loop(0, RW_HEAD // WKV_K_UNROLL, k_rows, (zero, zero))
        yb_ref[t] = y
        return sa_next

    sa = jnp.zeros(s_ref.shape[1:], F32)
    for i in range(RW_HEAD):
        sa = sa + s_ref[i] * op_ref[0, a_op, pl.ds(i, 1), :]
    lax.fori_loop(0, steps, advance, sa)

    for t in range(steps):
        y = yb_ref[t]
        yt = jnp.concatenate([y, y], axis=0).T
        fwd, bwd = rows(t)
        y0_ref[fwd, :] = yt[0:chains]
        y1_ref[bwd, :] = yt[chains:2 * chains]


def wkv_scan(rv, nb0, nb1, wk0, wk1, n_ctx):
    nb, rows, _ = rv.shape
    steps = WKV_STEPS
    chains = rows // steps
    nc = n_ctx // steps
    fwd = pl.BlockSpec((None, rows, LANES), lambda i: (i, 0, 0))
    bwd = pl.BlockSpec((None, rows, LANES), lambda i: (jnp.where(i < nc, nc - 1 - i, nb - 1 + nc - i), 0, 0))
    out = jax.ShapeDtypeStruct(rv.shape, F32)
    return pl.pallas_call(
        functools.partial(_k_wkv, steps=steps),
        grid=(nb,),
        in_specs=[fwd, bwd, fwd, bwd, fwd, bwd],
        out_specs=[fwd, bwd],
        out_shape=[out, out],
        scratch_shapes=[pltpu.VMEM((RW_HEAD, RW_HEAD, 2 * chains), F32),
                        pltpu.VMEM((steps, 6, RW_HEAD, 2 * chains), F32),
                        pltpu.VMEM((steps, RW_HEAD, 2 * chains), F32)],
        compiler_params=_params("arbitrary"),
        name="wkv_scan",
    )(rv, rv, nb0, nb1, wk0, wk1)


def rwkv_scan_branch(u_ctx, u_lat, p, ones_bd):
    g, bonus, rv, nb0, nb1, wk0, wk1 = rwkv_prep(
        u_ctx, u_lat, p["rw_g_up"], p["rw_w_up"], p["rw_w0"], p["rw_a_up"], p["rw_a0"], p["rw_k_k"],
        p["rw_k_a"], p["rw_r_k"].reshape(-1), ones_bd)
    y0, y1 = wkv_scan(rv, nb0, nb1, wk0, wk1, u_ctx.shape[1])
    return y0, y1, bonus, g


def _s5_block(i, n_ctx_blocks, n_blocks, reverse):
    if not reverse:
        return i
    return jnp.where(i < n_ctx_blocks, n_ctx_blocks - 1 - i, n_blocks - 1 + n_ctx_blocks - i)


def _k_s5(uc_ref, ul_ref, bre_ref, bim_ref, lre_ref, lim_ref, cre_ref, cim_ref, y_ref, hre_ref, him_ref,
          st_ref, tb_ref, *, steps, reverse, n_ctx_blocks, n_blocks):
    @pl.when(pl.program_id(0) == 0)
    def _():
        st_ref[...] = jnp.zeros_like(st_ref)

    bsz = uc_ref.shape[0]
    nq = MIX_WIDTH // LANES
    in_ctx = _s5_block(pl.program_id(0), n_ctx_blocks, n_blocks, reverse) < n_ctx_blocks
    for b in range(bsz):
        ub = jnp.where(in_ctx, uc_ref[b], ul_ref[b])
        for q in range(nq):
            tb_ref[q, pl.ds(b, steps, stride=bsz), :] = ub[:, q * LANES:(q + 1) * LANES]
    u = jnp.concatenate([tb_ref[q] for q in range(nq)], axis=1).astype(BF16)
    hre_ref[...] = _dot(u, bre_ref[...])
    him_ref[...] = _dot(u, bim_ref[...])
    for ch in range(S5_HID // S5_LANE_CHUNK):
        cols = slice(ch * S5_LANE_CHUNK, (ch + 1) * S5_LANE_CHUNK)
        lre = lre_ref[:, cols]
        lim = lim_ref[:, cols]

        def step(i, carry):
            hr, hi = carry
            t = (steps - 1 - i) if reverse else i
            rows = pl.ds(pl.multiple_of(t * bsz, bsz), bsz)
            nr = lre * hr - lim * hi + hre_ref[rows, cols]
            ni = lre * hi + lim * hr + him_ref[rows, cols]
            hre_ref[rows, cols] = nr
            him_ref[rows, cols] = ni
            return nr, ni
        hr, hi = lax.fori_loop(0, steps, step, (st_ref[0, :, cols], st_ref[1, :, cols]))
        st_ref[0, :, cols] = hr
        st_ref[1, :, cols] = hi
    y = _dot(hre_ref[...].astype(BF16), cre_ref[...]) - _dot(him_ref[...].astype(BF16), cim_ref[...])
    for q in range(nq):
        tb_ref[q] = y[:, q * LANES:(q + 1) * LANES]
    for b in range(bsz):
        for q in range(nq):
            y_ref[b, :, q * LANES:(q + 1) * LANES] = tb_ref[q, pl.ds(b, steps, stride=bsz), :]


def s5_scan(z_ctx, z_lat, bre, bim, lre, lim, cre, cim, reverse):
    bsz, n_ctx, w = z_ctx.shape
    t_total = n_ctx + z_lat.shape[1]
    steps = S5_STEPS
    nc = n_ctx // steps
    nb = t_total // steps
    blk = functools.partial(_s5_block, n_ctx_blocks=nc, n_blocks=nb, reverse=reverse)
    full = lambda shape: pl.BlockSpec(shape, lambda i: (0,) * len(shape))
    return pl.pallas_call(
        functools.partial(_k_s5, steps=steps, reverse=reverse, n_ctx_blocks=nc, n_blocks=nb),
        grid=(nb,),
        in_specs=[pl.BlockSpec((bsz, steps, w), lambda i: (0, jnp.minimum(blk(i), nc - 1), 0)),
                  pl.BlockSpec((bsz, steps, w), lambda i: (0, jnp.maximum(blk(i) - nc, 0), 0)),
                  full((w, S5_HID)), full((w, S5_HID)), full((bsz, S5_HID)), full((bsz, S5_HID)),
                  full((S5_HID, w)), full((S5_HID, w))],
        out_specs=pl.BlockSpec((bsz, steps, w), lambda i: (0, blk(i), 0)),
        out_shape=jax.ShapeDtypeStruct((bsz, t_total, w), F32),
        scratch_shapes=[pltpu.VMEM((steps * bsz, S5_HID), F32), pltpu.VMEM((steps * bsz, S5_HID), F32),
                        pltpu.VMEM((2, bsz, S5_HID), F32), pltpu.VMEM((w // LANES, steps * bsz, LANES), F32)],
        compiler_params=_params("arbitrary"),
        name="s5_scan",
    )(z_ctx, z_lat, bre, bim, lre, lim, cre, cim)


def _k_s5disc(are_ref, aim_ref, ldt_ref, lre_ref, lim_ref, cr_ref, ci_ref):
    a_re, a_im = are_ref[...], aim_ref[...]
    dt = jnp.exp(ldt_ref[...])
    mag = jnp.exp(a_re * dt)
    lb_re, lb_im = mag * jnp.cos(a_im * dt), mag * jnp.sin(a_im * dt)
    den = a_re * a_re + a_im * a_im
    nr = lb_re - 1.0
    lre_ref[...] = lb_re
    lim_ref[...] = lb_im
    cr_ref[...] = (nr * a_re + lb_im * a_im) / den
    ci_ref[...] = (lb_im * a_re - nr * a_im) / den


def s5_operators(a_re, a_im, log_dt, b_re, b_im, c_re, c_im, bsz):
    g, p, h = S5_GROUPS, S5_STATE, S5_GROUP
    shp = jax.ShapeDtypeStruct((2 * g, p), F32)
    lb_re, lb_im, cr, ci = pl.pallas_call(
        _k_s5disc, out_shape=[shp] * 4, name="s5_discretise",
    )(a_re.reshape(2 * g, p), a_im.reshape(2 * g, p), jnp.broadcast_to(log_dt.reshape(2 * g, 1), (2 * g, p)))
    cr, ci = cr.reshape(2, g, p, 1), ci.reshape(2, g, p, 1)
    bb_re = cr * b_re[None] - ci * b_im[None]
    bb_im = cr * b_im[None] + ci * b_re[None]
    eye = jnp.eye(g, dtype=F32)
    bd_in = lambda m: jnp.einsum("dgph,gk->dghkp", m, eye).reshape(2, g * h, g * p).astype(BF16)
    bd_out = lambda m: jnp.einsum("ghp,gk->gpkh", m, eye).reshape(g * p, g * h).astype(BF16)
    lam = lambda m: jnp.broadcast_to(m.reshape(2, 1, g * p), (2, bsz, g * p))
    return bd_in(bb_re), bd_in(bb_im), lam(lb_re), lam(lb_im), bd_out(c_re), bd_out(c_im)


def s5_branch(z_ctx, z_lat, p):
    bsz = z_ctx.shape[0]
    bre, bim, lre, lim, cre, cim = s5_operators(p["s5_a_re"], p["s5_a_im"], p["s5_log_dt"], p["s5_b_re"],
                                                p["s5_b_im"], p["s5_c_re"], p["s5_c_im"], bsz)
    return (s5_scan(z_ctx, z_lat, bre[0], bim[0], lre[0], lim[0], cre, cim, False),
            s5_scan(z_ctx, z_lat, bre[1], bim[1], lre[1], lim[1], cre, cim, True))


def _k_merge(yhy_ref, rwy0_ref, rwy1_ref, rwb_ref, rwg_ref, s5y0_ref, s5y1_ref, s5u_ref, zg_ref, x_ref, g1_ref,
             e_ref, lng_ref, lnb_ref, s5d_ref, gluw_ref, glub_ref, wb_ref, wo_ref, o_ref):
    w = MIX_WIDTH
    ones_bd = e_ref[...]
    lane = lax.broadcasted_iota(jnp.int32, (x_ref.shape[1], LANES), 1)
    cols = []
    for q in range(RW_HEADS // 2):
        even = _load_head_sum(rwy0_ref, rwy1_ref, 2 * q)
        odd = _load_head_sum(rwy0_ref, rwy1_ref, 2 * q + 1)
        cols.append(jnp.where(lane < RW_HEAD, even, pltpu.roll(odd, RW_HEAD, 1)))
    y = jnp.concatenate(cols, axis=1) + rwb_ref[0]
    mu = _seg_sum(y, ones_bd) * (1.0 / RW_HEAD)
    yc = y - mu
    var = _seg_sum(yc * yc, ones_bd) * (1.0 / RW_HEAD)
    y_rw = (yc * lax.rsqrt(var + RW_LN_EPS) * lng_ref[...] + lnb_ref[...]) * rwg_ref[0]
    s = s5y0_ref[0] + s5y1_ref[0] + s5u_ref[0] * s5d_ref[...]
    s = 0.5 * s * (1.0 + jnp.tanh(math.sqrt(2.0 / math.pi) * (s + 0.044715 * (s * s * s))))
    lg = _dot(s.astype(BF16), gluw_ref[...]) + glub_ref[...]
    y_s5 = lg[:, 0:w] * _sigmoid(lg[:, w:2 * w])
    zg = zg_ref[0]
    d = D_MODEL
    m = (_sigmoid(zg[:, 0:d]) * _dot(yhy_ref[0].astype(BF16), wb_ref[0])
         + _sigmoid(zg[:, d:2 * d]) * _dot(y_rw.astype(BF16), wb_ref[1])
         + _sigmoid(zg[:, 2 * d:3 * d]) * _dot(y_s5.astype(BF16), wb_ref[2]))
    o_ref[0] = x_ref[0] + g1_ref[0] * _dot(m.astype(BF16), wo_ref[...])


def merge_residual(y_hy, rw, s5_y, t_off, s5_u, zg, x, g1, p, ones_bd):
    bsz, n, d = x.shape
    w = MIX_WIDTH
    tm = min(ROW_TILE, n)
    assert t_off % tm == 0
    off = t_off // tm
    tile = lambda c: pl.BlockSpec((1, tm, c), lambda i, j: (i, j, 0))
    seq_tile = pl.BlockSpec((1, tm, w), lambda i, j: (i, off + j, 0))
    chain_tile = pl.BlockSpec((tm // WKV_STEPS, RW_HEADS * WKV_STEPS, LANES), lambda i, j: (off + j, i, 0))
    full = lambda shape: pl.BlockSpec(shape, lambda i, j: (0,) * len(shape))
    rw_y0, rw_y1, rw_bonus, rw_g = rw
    return pl.pallas_call(
        _k_merge,
        grid=(bsz, n // tm),
        in_specs=[tile(w), chain_tile, chain_tile, seq_tile, seq_tile, seq_tile, seq_tile, tile(w),
                  tile(N_BRANCH * d), tile(d), pl.BlockSpec((1, 1, d), lambda i, j: (i, 0, 0)),
                  full((w, w)), full((1, w)), full((1, w)), full((1, w)), full((w, 2 * w)), full((1, 2 * w)),
                  full((N_BRANCH, w, d)), full((d, d))],
        out_specs=tile(d),
        out_shape=jax.ShapeDtypeStruct(x.shape, F32),
        compiler_params=_params("arbitrary", "arbitrary"),
        name="merge_residual",
    )(y_hy, rw_y0, rw_y1, rw_bonus, rw_g, s5_y[0], s5_y[1], s5_u, zg, x, g1, ones_bd, p["rw_ln_g"].reshape(1, w),
      p["rw_ln_b"].reshape(1, w), p["s5_d"].reshape(1, w), p["s5_glu_w"].astype(BF16),
      p["s5_glu_b"].reshape(1, 2 * w), p["w_branch"].astype(BF16), p["w_out"].astype(BF16))


def _k_router(x_ref, sh_ref, sc_ref, g_ref, rw_ref, rb_ref, h_o, idx_o, gate_o, cnt_o):
    @pl.when((pl.program_id(0) == 0) & (pl.program_id(1) == 0))
    def _():
        cnt_o[...] = jnp.zeros_like(cnt_o)

    h = _rms_modulate(x_ref[0], g_ref[...], sh_ref[0], sc_ref[0])
    h_o[0] = h.astype(BF16)
    logits = _dot_hi(h, rw_ref[...]) + rb_ref[...]
    lane = lax.broadcasted_iota(jnp.int32, logits.shape, 1)
    vals, idxs = [], []
    for _ in range(TOP_K):
        m = jnp.max(logits, axis=-1, keepdims=True)
        idx = jnp.min(jnp.where(logits == m, lane, LANES), axis=-1, keepdims=True)
        vals.append(m)
        idxs.append(idx)
        logits = jnp.where(lane == idx, -jnp.inf, logits)
    exps = [jnp.exp(v - vals[0]) for v in vals]
    inv = 1.0 / (exps[0] + exps[1] + exps[2] + exps[3])
    idx_out = jnp.zeros(logits.shape, jnp.int32)
    gate_out = jnp.zeros(logits.shape, F32)
    picked = jnp.zeros(logits.shape, F32)
    for j in range(TOP_K):
        idx_out = jnp.where(lane == j, idxs[j], idx_out)
        gate_out = jnp.where(lane == j, exps[j] * inv, gate_out)
        picked = picked + jnp.where(lane == idxs[j], 1.0, 0.0)
    idx_o[0] = idx_out
    gate_o[0] = gate_out
    cnt_o[...] = cnt_o[...] + jnp.sum(picked, axis=0, keepdims=True)


def moe_route(x, shift, scale, g, router_w, router_b):
    bsz, n, d = x.shape
    tm = min(ROW_TILE, n)
    rw = jnp.pad(router_w, ((0, 0), (0, LANES - N_EXPERTS)))
    rb = jnp.pad(router_b, (0, LANES - N_EXPERTS), constant_values=-jnp.inf).reshape(1, LANES)
    tile = lambda c: pl.BlockSpec((1, tm, c), lambda i, j: (i, j, 0))
    vec = pl.BlockSpec((1, 1, d), lambda i, j: (i, 0, 0))
    full = lambda shape: pl.BlockSpec(shape, lambda i, j: (0,) * len(shape))
    return pl.pallas_call(
        _k_router,
        grid=(bsz, n // tm),
        in_specs=[tile(d), vec, vec, full((1, d)), full((d, LANES)), full((1, LANES))],
        out_specs=[tile(d), tile(LANES), tile(LANES), full((SUBLANES, LANES))],
        out_shape=[jax.ShapeDtypeStruct((bsz, n, d), BF16), jax.ShapeDtypeStruct((bsz, n, LANES), jnp.int32),
                   jax.ShapeDtypeStruct((bsz, n, LANES), F32), jax.ShapeDtypeStruct((SUBLANES, LANES), F32)],
        compiler_params=_params("arbitrary", "arbitrary"),
        name="moe_route",
    )(x, shift, scale, g.reshape(1, d), rw, rb)


def _k_expert(blk_e_ref, xs_ref, w1_ref, b1_ref, w2_ref, b2_ref, sw_ref, o_ref, w1b_ref, w2b_ref):
    i = pl.program_id(0)

    @pl.when((i == 0) | (blk_e_ref[i] != blk_e_ref[jnp.maximum(i - 1, 0)]))
    def _():
        w1b_ref[...] = w1_ref[0].astype(BF16)
        w2b_ref[...] = w2_ref[0].astype(BF16)

    hid = _dot(xs_ref[...], w1b_ref[...]) + b1_ref[0]
    gl = jnp.minimum(hid[:, 0:D_EXPERT], SWIGLU_LIMIT)
    up = jnp.clip(hid[:, D_EXPERT:2 * D_EXPERT], -SWIGLU_LIMIT, SWIGLU_LIMIT)
    act = (up + 1.0) * gl * _sigmoid(SWIGLU_ALPHA * gl)
    o_ref[...] = ((_dot(act.astype(BF16), w2b_ref[...]) + b2_ref[0]) * sw_ref[...]).astype(o_ref.dtype)


def expert_ffn(xs, blk_e, slot_w, w1, b1, w2, b2):
    n_slots, d = xs.shape
    n_blocks = n_slots // MOE_ROWS
    grid_spec = pltpu.PrefetchScalarGridSpec(
        num_scalar_prefetch=1,
        grid=(n_blocks,),
        in_specs=[pl.BlockSpec((MOE_ROWS, d), lambda i, e: (i, 0)),
                  pl.BlockSpec((1, d, 2 * D_EXPERT), lambda i, e: (e[i], 0, 0)),
                  pl.BlockSpec((1, 1, 2 * D_EXPERT), lambda i, e: (e[i], 0, 0)),
                  pl.BlockSpec((1, D_EXPERT, d), lambda i, e: (e[i], 0, 0)),
                  pl.BlockSpec((1, 1, d), lambda i, e: (e[i], 0, 0)),
                  pl.BlockSpec((MOE_ROWS, 1), lambda i, e: (i, 0))],
        out_specs=pl.BlockSpec((MOE_ROWS, d), lambda i, e: (i, 0)),
        scratch_shapes=[pltpu.VMEM((d, 2 * D_EXPERT), BF16), pltpu.VMEM((D_EXPERT, d), BF16)],
    )
    return pl.pallas_call(
        _k_expert,
        grid_spec=grid_spec,
        out_shape=jax.ShapeDtypeStruct((n_slots, d), BF16),
        compiler_params=_params("arbitrary"),
        name="expert_ffn",
    )(blk_e, xs, w1, b1.reshape(N_EXPERTS, 1, -1), w2, b2.reshape(N_EXPERTS, 1, -1), slot_w.reshape(-1, 1))


def _k_offset_add(tab_ref, e_ref, v_ref, o_ref):
    e = e_ref[...]
    acc = v_ref[...]
    for j in range(N_EXPERTS):
        acc = acc + jnp.where(e == j, tab_ref[j], 0)
    o_ref[...] = acc


def expert_offset_add(flat_e, values, table):
    n = flat_e.shape[0]
    rows = n // LANES
    return pl.pallas_call(
        _k_offset_add,
        in_specs=[pl.BlockSpec(memory_space=pltpu.SMEM), pl.BlockSpec(memory_space=pltpu.VMEM),
                  pl.BlockSpec(memory_space=pltpu.VMEM)],
        out_specs=pl.BlockSpec(memory_space=pltpu.VMEM),
        out_shape=jax.ShapeDtypeStruct((rows, LANES), jnp.int32),
        name="expert_offset_add",
    )(table.astype(jnp.int32), flat_e.reshape(rows, LANES), values.reshape(rows, LANES)).reshape(n)


def moe_ffn(x, shift, scale, g, router_w, router_b, w1, b1, w2, b2):
    bsz, n, d = x.shape
    h, idx, gate, cnt = moe_route(x, shift, scale, g, router_w, router_b)
    n_tok = bsz * n
    n_asg = n_tok * TOP_K
    flat_e = idx[..., :TOP_K].reshape(-1)
    flat_w = gate[..., :TOP_K].reshape(-1)
    order = jnp.argsort(flat_e).astype(jnp.int32)
    rank_of = jnp.argsort(order).astype(jnp.int32)
    counts = cnt[0, :N_EXPERTS].astype(jnp.int32)
    grp_start = jnp.cumsum(counts) - counts
    padded = (counts + MOE_ROWS - 1) // MOE_ROWS * MOE_ROWS
    pad_end = jnp.cumsum(padded)
    pad_start = pad_end - padded
    n_blocks = -(-n_asg // MOE_ROWS) + N_EXPERTS
    n_slots = n_blocks * MOE_ROWS
    blk_first = jnp.arange(n_blocks, dtype=jnp.int32) * MOE_ROWS
    blk_e = jnp.minimum(jnp.sum(pad_end[None, :] <= blk_first[:, None], axis=1), N_EXPERTS - 1).astype(jnp.int32)
    slot_rank = (blk_first - pad_start[blk_e])[:, None] + jnp.arange(MOE_ROWS, dtype=jnp.int32)[None, :]
    slot_used = (slot_rank < counts[blk_e][:, None]).reshape(-1)
    slot_src = jnp.where(slot_used, (grp_start[blk_e][:, None] + slot_rank).reshape(-1), 0)
    slot_asg = order[slot_src]
    slot_tok = jnp.where(slot_used, slot_asg // TOP_K, 0)
    slot_w = jnp.where(slot_used, flat_w[slot_asg], 0.0)
    slot_of = expert_offset_add(flat_e, rank_of, pad_start - grp_start)
    xs = h.reshape(n_tok, d)[slot_tok]
    ys = expert_ffn(xs, blk_e, slot_w, w1, b1, w2, b2)
    slot_of = slot_of.reshape(n_tok, TOP_K)
    out = sum(ys[slot_of[:, j]].astype(F32) for j in range(TOP_K))
    return out.reshape(bsz, n, d)


def _k_rmsnorm(x_ref, g_ref, o_ref):
    x = x_ref[0]
    ms = jnp.mean(x * x, axis=-1, keepdims=True)
    o_ref[0] = x * lax.rsqrt(ms + NORM_EPS) * g_ref[...]


def final_rmsnorm(x, g):
    b, n, d = x.shape
    tm = min(2 * ROW_TILE, n)
    return pl.pallas_call(
        _k_rmsnorm,
        grid=(b, n // tm),
        in_specs=[pl.BlockSpec((1, tm, d), lambda i, j: (i, j, 0)), pl.BlockSpec((1, d), lambda i, j: (0, 0))],
        out_specs=pl.BlockSpec((1, tm, d), lambda i, j: (i, j, 0)),
        out_shape=jax.ShapeDtypeStruct(x.shape, F32),
        compiler_params=_params("arbitrary", "arbitrary"),
        name="final_rmsnorm",
    )(x, g.reshape(1, d))


@functools.lru_cache(maxsize=None)
def _grid_pos_embed(n_tokens):
    rows = n_tokens // GRID_W
    row_id, col_id = np.meshgrid(np.arange(rows), np.arange(GRID_W), indexing="ij")
    quarter = D_MODEL // 4
    omega = (1.0 / (10000.0 ** (np.arange(quarter, dtype=np.float32) / quarter))).astype(np.float32)

    def enc(pos):
        ang = pos.reshape(-1)[:, None].astype(np.float32) * omega
        return np.concatenate([np.sin(ang), np.cos(ang)], axis=-1)

    return np.concatenate([enc(row_id), enc(col_id)], axis=-1).astype(np.float32)


_LAYER_KEYS = ("ada_w", "ada_b", "norm1_g", "norm2_g", "w_in", "hy_conv_w", "hy_conv_b", "hy_w1", "hy_b1",
               "hy_f1", "hy_w2", "hy_b2", "hy_f2", "hy_w3", "hy_b3", "hy_bias", "rw_conv_w", "rw_conv_b",
               "rw_w_up", "rw_w0", "rw_a_up", "rw_a0", "rw_g_up", "rw_k_k", "rw_k_a", "rw_r_k", "rw_ln_g",
               "rw_ln_b", "s5_a_re", "s5_a_im", "s5_log_dt", "s5_b_re", "s5_b_im", "s5_c_re", "s5_c_im",
               "s5_d", "s5_glu_w", "s5_glu_b", "w_branch", "w_out", "router_w", "router_b", "moe_w1",
               "moe_b1", "moe_w2", "moe_b2")


def _token_mixer(x, xc, mod, mod_c, p, need_ctx, ones_bd):
    sh1, sc1, g1 = mod
    csh1, csc1, cg1 = mod_c
    w_in = p["w_in"].astype(BF16)
    c0, c1, c2 = HY_COLS, HY_COLS + RW_COLS, HY_COLS + RW_COLS + MIX_WIDTH
    proj = lambda t, s, c, lo, hi: norm_mod_matmul(t, s, c, p["norm1_g"], w_in[:, lo:hi])
    hy_args = (p["hy_w1"], p["hy_b1"], p["hy_f1"], p["hy_w2"], p["hy_b2"], p["hy_f2"], p["hy_w3"], p["hy_b3"],
               p["hy_bias"])

    z_hy = proj(x, sh1, sc1, 0, c0)
    y_hy = hyena_branch(z_hy, p["hy_conv_w"], p["hy_conv_b"], hyena_spectra(x.shape[1], *hy_args))
    u_rw = short_conv(proj(x, sh1, sc1, c0, c1), p["rw_conv_w"], p["rw_conv_b"])
    uc_rw = short_conv(proj(xc, csh1, csc1, c0, c1), p["rw_conv_w"], p["rw_conv_b"])
    rw = rwkv_scan_branch(uc_rw, u_rw, p, ones_bd)
    n_ctx = xc.shape[1]
    z_s5 = proj(x, sh1, sc1, c1, c2)
    zc_s5 = proj(xc, csh1, csc1, c1, c2)
    s5_y = s5_branch(zc_s5, z_s5, p)
    zg = proj(x, sh1, sc1, c2, c2 + N_BRANCH * D_MODEL)
    x_new = merge_residual(y_hy, rw, s5_y, n_ctx, z_s5, zg, x, g1, p, ones_bd)
    if not need_ctx:
        return x_new, None
    zc_hy = proj(xc, csh1, csc1, 0, c0)
    yc_hy = hyena_branch(zc_hy, p["hy_conv_w"], p["hy_conv_b"], hyena_spectra(xc.shape[1], *hy_args))
    zcg = proj(xc, csh1, csc1, c2, c2 + N_BRANCH * D_MODEL)
    xc_new = merge_residual(yc_hy, rw, s5_y, 0, zc_s5, zcg, xc, cg1, p, ones_bd)
    return x_new, xc_new


def kernel(x, c, ctx, c_ctx, ada_w, ada_b, norm1_g, norm2_g, w_in, hy_conv_w, hy_conv_b, hy_w1, hy_b1, hy_f1, hy_w2, hy_b2, hy_f2, hy_w3, hy_b3, hy_bias, rw_conv_w, rw_conv_b, rw_w_up, rw_w0, rw_a_up, rw_a0, rw_g_up, rw_k_k, rw_k_a, rw_r_k, rw_ln_g, rw_ln_b, s5_a_re, s5_a_im, s5_log_dt, s5_b_re, s5_b_im, s5_c_re, s5_c_im, s5_d, s5_glu_w, s5_glu_b, w_branch, w_out, router_w, router_b, moe_w1, moe_b1, moe_w2, moe_b2, final_g):
    stacked = dict(zip(_LAYER_KEYS, (ada_w, ada_b, norm1_g, norm2_g, w_in, hy_conv_w, hy_conv_b, hy_w1, hy_b1,
                                     hy_f1, hy_w2, hy_b2, hy_f2, hy_w3, hy_b3, hy_bias, rw_conv_w, rw_conv_b,
                                     rw_w_up, rw_w0, rw_a_up, rw_a0, rw_g_up, rw_k_k, rw_k_a, rw_r_k, rw_ln_g,
                                     rw_ln_b, s5_a_re, s5_a_im, s5_log_dt, s5_b_re, s5_b_im, s5_c_re, s5_c_im,
                                     s5_d, s5_glu_w, s5_glu_b, w_branch, w_out, router_w, router_b, moe_w1,
                                     moe_b1, moe_w2, moe_b2)))
    bsz, n, d = x.shape
    depth = ada_w.shape[0]
    x = add_pos(x, jnp.asarray(_grid_pos_embed(n)))
    xc = ctx
    ones_bd = jnp.asarray(np.kron(np.eye(RW_HEADS), np.ones((RW_HEAD, RW_HEAD))), BF16)
    cvec = jnp.zeros((2 * SUBLANES, d), F32).at[:bsz].set(c).at[bsz].set(c_ctx)
    for l in range(depth):
        p = {k: v[l] for k, v in stacked.items()}
        need_ctx = l < depth - 1
        ada = ada_proj(cvec, p["ada_w"], p["ada_b"])
        lat = [t.reshape(bsz, 1, d) for t in jnp.split(ada[:bsz], 6, axis=-1)]
        cx = [jnp.broadcast_to(t.reshape(1, 1, d), (bsz, 1, d)) for t in jnp.split(ada[bsz], 6, axis=-1)]
        x, xc_new = _token_mixer(x, xc, lat[0:3], cx[0:3], p, need_ctx, ones_bd)
        moe = functools.partial(moe_ffn, g=p["norm2_g"], router_w=p["router_w"], router_b=p["router_b"],
                                w1=p["moe_w1"], b1=p["moe_b1"], w2=p["moe_w2"], b2=p["moe_b2"])
        x = x + lat[5] * moe(x, lat[3], lat[4])
        if need_ctx:
            xc = xc_new + cx[5] * moe(xc_new, cx[3], cx[4])
    return final_rmsnorm(x, final_g)
```

```python
import functools
import math

import numpy as np
import jax
import jax.numpy as jnp
from jax import lax
from jax.experimental import pallas as pl
from jax.experimental.pallas import tpu as pltpu

F32 = jnp.float32
BF16 = jnp.bfloat16

D_MODEL = 1024
GRID_W = 64
SHORT_CONV = 3
NORM_EPS = 1e-6
N_BRANCH = 3
MIX_WIDTH = 512

HY_ORDER = 2
HY_BANDS = 16
HY_DECAY_TARGET = 1e-2
HY_FAST_PCT = 0.3
HY_SLOW_PCT = 1.5
HY_COLS = (HY_ORDER + 1) * MIX_WIDTH

RW_HEAD = 64
RW_HEADS = MIX_WIDTH // RW_HEAD
RW_DECAY_RANK = 64
RW_ICLR_RANK = 64
RW_GATE_RANK = 128
RW_LN_EPS = 64e-5
RW_COLS = 3 * MIX_WIDTH + RW_DECAY_RANK + RW_ICLR_RANK + RW_GATE_RANK

S5_GROUP = 16
S5_GROUPS = MIX_WIDTH // S5_GROUP
S5_STATE = 64
S5_HID = S5_GROUPS * S5_STATE

N_EXPERTS = 32
TOP_K = 4
D_EXPERT = 1024
SWIGLU_LIMIT = 7.0
SWIGLU_ALPHA = 1.702

LANES = 128
SUBLANES = 8
VMEM_LIMIT_BYTES = 56 * 1024 * 1024

ROW_TILE = 256
MOE_ROWS = 256
WKV_STEPS = 16
WKV_K_UNROLL = 4
S5_STEPS = 64
S5_LANE_CHUNK = 512
S5_DIAG_PARTS = 2


def _params(*sem):
    return pltpu.CompilerParams(dimension_semantics=sem, vmem_limit_bytes=VMEM_LIMIT_BYTES)


def _dot(a, b):
    return jnp.dot(a, b, preferred_element_type=F32)


def _split3(x):
    hi = x.astype(BF16)
    r1 = x - hi.astype(F32)
    mid = r1.astype(BF16)
    lo = (r1 - mid.astype(F32)).astype(BF16)
    return hi, mid, lo


def _dot_hi(a, b):
    a0, a1, a2 = _split3(a)
    b0, b1, b2 = _split3(b)
    return (_dot(a0, b0) + (_dot(a0, b1) + _dot(a1, b0))
            + (_dot(a1, b1) + _dot(a0, b2) + _dot(a2, b0)))


def _seg_sum(x, ones_bd):
    hi, mid, lo = _split3(x)
    return _dot(hi, ones_bd) + _dot(mid, ones_bd) + _dot(lo, ones_bd)


def _sigmoid(x):
    return 1.0 / (1.0 + jnp.exp(-x))


def _rms_modulate(x, g, shift, scale):
    ms = jnp.mean(x * x, axis=-1, keepdims=True)
    y = x * lax.rsqrt(ms + NORM_EPS) * g
    return y * (1.0 + scale) + shift


def _k_ada(c_ref, w_ref, b_ref, o_ref):
    c = c_ref[...]
    s = c * _sigmoid(c)
    o_ref[...] = _dot_hi(s, w_ref[...]) + b_ref[...]


def ada_proj(cvec, w, b):
    rows, d = cvec.shape
    n = w.shape[1]
    tn = 1536
    return pl.pallas_call(
        _k_ada,
        grid=(n // tn,),
        in_specs=[pl.BlockSpec((rows, d), lambda j: (0, 0)),
                  pl.BlockSpec((d, tn), lambda j: (0, j)),
                  pl.BlockSpec((1, tn), lambda j: (0, j))],
        out_specs=pl.BlockSpec((rows, tn), lambda j: (0, j)),
        out_shape=jax.ShapeDtypeStruct((rows, n), F32),
        compiler_params=_params("arbitrary"),
        name="ada_proj",
    )(cvec, w, b.reshape(1, n))


def _k_add_pos(x_ref, p_ref, o_ref):
    o_ref[0] = x_ref[0] + p_ref[...]


def add_pos(x, pos):
    b, n, d = x.shape
    tm = min(ROW_TILE * 2, n)
    return pl.pallas_call(
        _k_add_pos,
        grid=(b, n // tm),
        in_specs=[pl.BlockSpec((1, tm, d), lambda i, j: (i, j, 0)),
                  pl.BlockSpec((tm, d), lambda i, j: (j, 0))],
        out_specs=pl.BlockSpec((1, tm, d), lambda i, j: (i, j, 0)),
        out_shape=jax.ShapeDtypeStruct(x.shape, F32),
        compiler_params=_params("arbitrary", "arbitrary"),
        name="add_pos",
    )(x, pos)


def _k_norm_mm(x_ref, sh_ref, sc_ref, g_ref, w_ref, o_ref, h_ref):
    @pl.when(pl.program_id(2) == 0)
    def _():
        h = _rms_modulate(x_ref[0], g_ref[...], sh_ref[0], sc_ref[0])
        h_ref[...] = h.astype(BF16)

    o_ref[0] = _dot(h_ref[...], w_ref[...])


def norm_mod_matmul(x, shift, scale, g, w):
    b, n, d = x.shape
    ncol = w.shape[1]
    tm = min(2 * ROW_TILE, n)
    tn = ncol if ncol <= 1792 else 1536
    return pl.pallas_call(
        _k_norm_mm,
        grid=(b, n // tm, ncol // tn),
        in_specs=[pl.BlockSpec((1, tm, d), lambda i, j, k: (i, j, 0)),
                  pl.BlockSpec((1, 1, d), lambda i, j, k: (i, 0, 0)),
                  pl.BlockSpec((1, 1, d), lambda i, j, k: (i, 0, 0)),
                  pl.BlockSpec((1, d), lambda i, j, k: (0, 0)),
                  pl.BlockSpec((d, tn), lambda i, j, k: (0, k))],
        out_specs=pl.BlockSpec((1, tm, tn), lambda i, j, k: (i, j, k)),
        out_shape=jax.ShapeDtypeStruct((b, n, ncol), F32),
        scratch_shapes=[pltpu.VMEM((tm, d), BF16)],
        compiler_params=_params("arbitrary", "arbitrary", "arbitrary"),
        name="norm_mod_matmul",
    )(x, shift, scale, g.reshape(1, d), w)


def _short_conv_val(z, w, b):
    n = z.shape[0]
    row = lax.broadcasted_iota(jnp.int32, z.shape, 0)
    zm = jnp.where(row == 0, 0.0, pltpu.roll(z, 1, 0))
    zp = jnp.where(row == n - 1, 0.0, pltpu.roll(z, n - 1, 0))
    return zm * w[0:1] + z * w[1:2] + zp * w[2:3] + b


def _k_sconv(z_ref, w_ref, b_ref, o_ref):
    o_ref[0] = _short_conv_val(z_ref[0], w_ref[...], b_ref[...])


def short_conv(z, w, b):
    bsz, n, c = z.shape
    tc = 256 if c % 256 == 0 else LANES
    return pl.pallas_call(
        _k_sconv,
        grid=(bsz, c // tc),
        in_specs=[pl.BlockSpec((1, n, tc), lambda i, j: (i, 0, j)),
                  pl.BlockSpec((SHORT_CONV, tc), lambda i, j: (0, j)),
                  pl.BlockSpec((1, tc), lambda i, j: (0, j))],
        out_specs=pl.BlockSpec((1, n, tc), lambda i, j: (i, 0, j)),
        out_shape=jax.ShapeDtypeStruct(z.shape, F32),
        compiler_params=_params("arbitrary", "arbitrary"),
        name="short_conv",
    )(z, w, b.reshape(1, c))


def _fft_split(n):
    total = 2 * n
    bits = total.bit_length() - 1
    assert 1 << bits == total
    n1 = 1 << ((bits + 1) // 2)
    n1 = max(n1, 2 * SUBLANES)
    return n1, total // n1


@functools.lru_cache(maxsize=None)
def _dft_mats(n):
    n1, n2 = _fft_split(n)
    total = n1 * n2
    k1 = np.arange(n1)[:, None]
    m1 = np.arange(n1)[None, :]
    m2 = np.arange(n2)[:, None, None]
    ang = -2.0 * np.pi * ((n2 * k1 * m1)[None] + m2 * k1[None]) / total
    fa = np.concatenate([np.cos(ang), np.sin(ang)], axis=1)
    k2 = np.arange(n2)[:, None]
    mm = np.arange(n2)[None, :]
    gang = -2.0 * np.pi * k2 * mm / n2
    gr, gi = np.cos(gang), np.sin(gang)
    fb = np.block([[gr, -gi], [gi, gr]])
    fc = np.block([[gr, gi], [-gi, gr]])
    m1d = np.arange(n1 // 2)[:, None]
    k1d = np.arange(n1)[None, :]
    dang = 2.0 * np.pi * ((n2 * m1d * k1d)[None] + m2 * k1d[None]) / total
    fd = np.concatenate([np.cos(dang), -np.sin(dang)], axis=2) / total
    return (fa.astype(np.float32), fb.astype(np.float32), fc.astype(np.float32), fd.astype(np.float32))


def _time_pitch(n2):
    return n2 + SUBLANES


def _spec_pitch(n2):
    return 2 * n2 + SUBLANES


def _fft_stage_a(src_ref, y_ref, fa_ref, n1, n2, k_rows):
    tp, sp = _time_pitch(n2), _spec_pitch(n2)

    def body(m2, c):
        xs = src_ref[pl.ds(m2, k_rows, stride=tp), :].astype(BF16)
        res = _dot(fa_ref[m2, :, 0:k_rows], xs)
        y_ref[pl.ds(m2, n1, stride=sp), :] = res[0:n1]
        y_ref[pl.ds(n2 + m2, n1, stride=sp), :] = res[n1:2 * n1]
        return c
    lax.fori_loop(0, n2, body, 0, unroll=4)


def _long_conv(buf_ref, y_ref, h_ref, fa_ref, fb_ref, fc_ref, fd_ref, n1, n2):
    tp, sp = _time_pitch(n2), _spec_pitch(n2)
    _fft_stage_a(buf_ref, y_ref, fa_ref, n1, n2, n1 // 2)

    def body_k1(k1, c):
        rows = pl.ds(pl.multiple_of(k1 * sp, SUBLANES), 2 * n2)
        z = _dot(fb_ref[...], y_ref[rows, :].astype(BF16))
        zr, zi = z[0:n2], z[n2:2 * n2]
        h = h_ref[k1]
        hr, hi = h[0:n2], h[n2:2 * n2]
        p = jnp.concatenate([zr * hr - zi * hi, zr * hi + zi * hr], axis=0)
        y_ref[rows, :] = _dot(fc_ref[...], p.astype(BF16))
        return c
    lax.fori_loop(0, n1, body_k1, 0, unroll=8)

    def body_m2(m2, c):
        qr = y_ref[pl.ds(m2, n1, stride=sp), :]
        qi = y_ref[pl.ds(n2 + m2, n1, stride=sp), :]
        q = jnp.concatenate([qr, qi], axis=0).astype(BF16)
        buf_ref[pl.ds(m2, n1 // 2, stride=tp), :] = _dot(fd_ref[m2], q)
        return c
    lax.fori_loop(0, n2, body_m2, 0, unroll=4)


def _store_time_blocks(buf_ref, val, n2):
    tp = _time_pitch(n2)
    for m1 in range(val.shape[0] // n2):
        buf_ref[m1 * tp:m1 * tp + n2, :] = val[m1 * n2:(m1 + 1) * n2]


def _load_time_blocks(buf_ref, n, n2):
    tp = _time_pitch(n2)
    return jnp.concatenate([buf_ref[m1 * tp:m1 * tp + n2, :] for m1 in range(n // n2)], axis=0)


def _k_hyena(x1_ref, x2_ref, v_ref, w1_ref, w2_ref, wv_ref, b1_ref, b2_ref, bv_ref, h1_ref, h2_ref,
             fa_ref, fb_ref, fc_ref, fd_ref, o_ref, buf_ref, y_ref, *, n1, n2):
    n = x1_ref.shape[1]
    conv = functools.partial(_long_conv, buf_ref, y_ref, fa_ref=fa_ref, fb_ref=fb_ref, fc_ref=fc_ref,
                             fd_ref=fd_ref, n1=n1, n2=n2)
    _store_time_blocks(buf_ref, _short_conv_val(v_ref[0], wv_ref[...], bv_ref[...]), n2)
    conv(h1_ref)
    gated = _short_conv_val(x1_ref[0], w1_ref[...], b1_ref[...]) * _load_time_blocks(buf_ref, n, n2)
    _store_time_blocks(buf_ref, gated, n2)
    conv(h2_ref)
    o_ref[0] = _short_conv_val(x2_ref[0], w2_ref[...], b2_ref[...]) * _load_time_blocks(buf_ref, n, n2)


def _k_fspec(f_ref, fa_ref, fb_ref, o_ref, y_ref, *, n1, n2):
    sp = _spec_pitch(n2)
    _fft_stage_a(f_ref, y_ref, fa_ref, n1, n2, n1)

    def body_k1(k1, c):
        src = pl.ds(pl.multiple_of(k1 * sp, SUBLANES), 2 * n2)
        dst = pl.ds(pl.multiple_of(k1 * 2 * n2, 2 * n2), 2 * n2)
        o_ref[dst, :] = _dot(fb_ref[...], y_ref[src, :].astype(BF16))
        return c
    lax.fori_loop(0, n1, body_k1, 0, unroll=2)


def filter_spectrum(filt, n):
    n1, n2 = _fft_split(n)
    total, c = filt.shape
    tp, sp = _time_pitch(n2), _spec_pitch(n2)
    fa, fb, _, _ = _dft_mats(n)
    blocks = jnp.pad(filt.reshape(n1, n2, c), ((0, 0), (0, tp - n2), (0, 0))).reshape(n1 * tp, c)
    out = pl.pallas_call(
        functools.partial(_k_fspec, n1=n1, n2=n2),
        grid=(c // LANES,),
        in_specs=[pl.BlockSpec((n1 * tp, LANES), lambda j: (0, j)),
                  pl.BlockSpec((n2, 2 * n1, n1), lambda j: (0, 0, 0)),
                  pl.BlockSpec((2 * n2, 2 * n2), lambda j: (0, 0))],
        out_specs=pl.BlockSpec((total * 2, LANES), lambda j: (0, j)),
        out_shape=jax.ShapeDtypeStruct((2 * total, c), F32),
        scratch_shapes=[pltpu.VMEM((n1 * sp, LANES), F32)],
        compiler_params=_params("arbitrary"),
        name="filter_spectrum",
    )(blocks, jnp.asarray(fa, BF16), jnp.asarray(fb, BF16))
    return out.reshape(n1, 2 * n2, c)


def _k_filter_mlp(feat_ref, win_ref, w1_ref, b1_ref, f1_ref, w2_ref, b2_ref, f2_ref, w3_ref, b3_ref, o_ref):
    h = jnp.sin(f1_ref[...] * (_dot_hi(feat_ref[...], w1_ref[...]) + b1_ref[...]))
    h = jnp.sin(f2_ref[...] * (_dot_hi(h, w2_ref[...]) + b2_ref[...]))
    o_ref[...] = (_dot_hi(h, w3_ref[...]) + b3_ref[...]) * win_ref[...]


@functools.lru_cache(maxsize=None)
def _filter_consts(n):
    t = np.linspace(0.0, 1.0, n, dtype=np.float32)[:, None]
    bands = np.linspace(1e-4, HY_BANDS - 1, HY_BANDS, dtype=np.float32)
    ang = (np.float32(2 * math.pi) * np.arange(n, dtype=np.float32) / np.float32(n))[:, None] * bands
    feats = np.concatenate([t, np.cos(ang), -np.sin(ang)], axis=-1).astype(np.float32)
    pad = (-feats.shape[1]) % SUBLANES
    feats = np.pad(feats, ((0, 0), (0, pad)))
    deltas = np.abs(np.linspace(math.log(HY_DECAY_TARGET) / HY_SLOW_PCT,
                                math.log(HY_DECAY_TARGET) / HY_FAST_PCT, MIX_WIDTH, dtype=np.float32))
    window = np.exp(-t * deltas).astype(np.float32)
    return feats, np.tile(window, (1, 2 * HY_ORDER))


def hyena_filter_table(n, w1, b1, f1, w2, b2, f2, w3, b3):
    feats, window = _filter_consts(n)
    kf = feats.shape[1]
    fd = w1.shape[1]
    ncol = w3.shape[1]
    w1p = jnp.pad(w1, ((0, kf - w1.shape[0]), (0, 0)))
    tm = min(ROW_TILE, n)
    full = lambda shape: pl.BlockSpec(shape, lambda i: (0,) * len(shape))
    return pl.pallas_call(
        _k_filter_mlp,
        grid=(n // tm,),
        in_specs=[pl.BlockSpec((tm, kf), lambda i: (i, 0)),
                  pl.BlockSpec((tm, ncol), lambda i: (i, 0)),
                  full((kf, fd)), full((1, fd)), full((1, fd)),
                  full((fd, fd)), full((1, fd)), full((1, fd)),
                  full((fd, ncol)), full((1, ncol))],
        out_specs=pl.BlockSpec((tm, ncol), lambda i: (i, 0)),
        out_shape=jax.ShapeDtypeStruct((n, ncol), F32),
        compiler_params=_params("arbitrary"),
        name="hyena_filter_mlp",
    )(jnp.asarray(feats), jnp.asarray(window), w1p, b1.reshape(1, fd), f1.reshape(1, fd),
      w2, b2.reshape(1, fd), f2.reshape(1, fd), w3, b3.reshape(1, ncol))


def hyena_spectra(n, w1, b1, f1, w2, b2, f2, w3, b3, bias):
    hf = hyena_filter_table(n, w1, b1, f1, w2, b2, f2, w3, b3).reshape(n, HY_ORDER, 2, MIX_WIDTH)
    fwd = hf[:, :, 0]
    bwd = hf[1:, :, 1][::-1]
    fwd = fwd.at[0].add(bias)
    filt = jnp.concatenate([fwd, jnp.zeros((1, HY_ORDER, MIX_WIDTH), F32), bwd], axis=0)
    spec = filter_spectrum(filt.reshape(2 * n, HY_ORDER * MIX_WIDTH), n)
    n1, n2 = _fft_split(n)
    return spec.reshape(n1, 2 * n2, HY_ORDER, MIX_WIDTH).transpose(2, 0, 1, 3)


def hyena_branch(z, conv_w, conv_b, spectra):
    bsz, n, _ = z.shape
    n1, n2 = _fft_split(n)
    fa, fb, fc, fd = (jnp.asarray(m, BF16) for m in _dft_mats(n))
    fa = fa[:, :, : n1 // 2]
    nt = MIX_WIDTH // LANES
    cb = conv_b.reshape(1, HY_COLS)
    zspec = lambda off: pl.BlockSpec((1, n, LANES), lambda j, i: (i, 0, off + j))
    wspec = lambda off: pl.BlockSpec((SHORT_CONV, LANES), lambda j, i: (0, off + j))
    bspec = lambda off: pl.BlockSpec((1, LANES), lambda j, i: (0, off + j))
    once = pl.Buffered(1)
    hspec = lambda o: pl.BlockSpec((None, n1, 2 * n2, LANES), lambda j, i: (o, 0, 0, j), pipeline_mode=once)
    full = lambda shape: pl.BlockSpec(shape, lambda j, i: (0,) * len(shape), pipeline_mode=once)
    return pl.pallas_call(
        functools.partial(_k_hyena, n1=n1, n2=n2),
        grid=(nt, bsz),
        in_specs=[zspec(0), zspec(nt), zspec(2 * nt), wspec(0), wspec(nt), wspec(2 * nt),
                  bspec(0), bspec(nt), bspec(2 * nt), hspec(0), hspec(1),
                  full(fa.shape), full(fb.shape), full(fc.shape), full(fd.shape)],
        out_specs=pl.BlockSpec((1, n, LANES), lambda j, i: (i, 0, j)),
        out_shape=jax.ShapeDtypeStruct((bsz, n, MIX_WIDTH), F32),
        scratch_shapes=[pltpu.VMEM((n1 // 2 * _time_pitch(n2), LANES), F32),
                        pltpu.VMEM((n1 * _spec_pitch(n2), LANES), F32)],
        compiler_params=_params("arbitrary", "arbitrary"),
        name="hyena_long_conv",
    )(z, z, z, conv_w, conv_w, conv_w, cb, cb, cb, spectra, spectra, fa, fb, fc, fd)


def _store_head_pairs(o_ref, left, right):
    rows = left.shape[0]
    lane = lax.broadcasted_iota(jnp.int32, (rows, LANES), 1)
    low = lane < RW_HEAD
    for q in range(RW_HEADS // 2):
        lcol = left[:, q * LANES:(q + 1) * LANES]
        rcol = right[:, q * LANES:(q + 1) * LANES]
        heads = (jnp.where(low, lcol, pltpu.roll(rcol, RW_HEAD, 1)),
                 jnp.where(low, pltpu.roll(lcol, RW_HEAD, 1), rcol))
        for h, piece in zip((2 * q, 2 * q + 1), heads):
            for c in range(rows // WKV_STEPS):
                o_ref[c, h * WKV_STEPS:(h + 1) * WKV_STEPS, :] = piece[c * WKV_STEPS:(c + 1) * WKV_STEPS]


def _load_head_sum(y0_ref, y1_ref, h):
    sl = slice(h * WKV_STEPS, (h + 1) * WKV_STEPS)
    return jnp.concatenate([y0_ref[c, sl, :] + y1_ref[c, sl, :] for c in range(y0_ref.shape[0])], axis=0)


def _k_rwprep(uc_ref, ul_ref, gup_ref, wup_ref, w0_ref, aup_ref, a0_ref, kk_ref, ka_ref, rk_ref, e_ref,
              g_o, bonus_o, rv_o, nb0_o, nb1_o, wk0_o, wk1_o, *, ctx_tiles):
    w = MIX_WIDTH
    u = jnp.where(pl.program_id(1) < ctx_tiles, uc_ref[0], ul_ref[0])
    r, k, v = u[:, 0:w], u[:, w:2 * w], u[:, 2 * w:3 * w]
    xw = u[:, 3 * w:3 * w + RW_DECAY_RANK]
    xa = u[:, 3 * w + RW_DECAY_RANK:3 * w + RW_DECAY_RANK + RW_ICLR_RANK]
    xg = u[:, 3 * w + RW_DECAY_RANK + RW_ICLR_RANK:]
    ones_bd = e_ref[...]
    g_o[0] = _dot(_sigmoid(xg).astype(BF16), gup_ref[...])
    kk = k * kk_ref[...]
    kk = kk * lax.rsqrt(jnp.maximum(_seg_sum(kk * kk, ones_bd), 1e-24))
    txw = jnp.tanh(xw)
    kd_sum = None
    for d, (nb_o, wk_o) in enumerate(((nb0_o, wk0_o), (nb1_o, wk1_o))):
        x = -(w0_ref[d] + _dot_hi(txw, wup_ref[d]))
        softplus = jnp.maximum(x, 0.0) + jnp.log(1.0 + jnp.exp(-jnp.abs(x)))
        decay = jnp.exp(-jnp.exp(-softplus - 0.5))
        a = _sigmoid(a0_ref[d] + _dot_hi(xa, aup_ref[d]))
        kd = k * (1.0 + (a - 1.0) * ka_ref[...])
        _store_head_pairs(nb_o, -kk, kk * a)
        _store_head_pairs(wk_o, decay, kd)
        kd_sum = kd if kd_sum is None else kd_sum + kd
    _store_head_pairs(rv_o, r, v)
    bonus_o[0] = _seg_sum(r * kd_sum * rk_ref[...], ones_bd) * v


def rwkv_prep(u_ctx, u_lat, g_up, w_up, w0, a_up, a0, k_k, k_a, r_k, ones_bd):
    bsz, n_ctx, cols = u_ctx.shape
    n_lat = u_lat.shape[1]
    t_total = n_ctx + n_lat
    w = MIX_WIDTH
    tm = min(ROW_TILE, n_ctx)
    ctx_tiles = n_ctx // tm
    full = lambda shape: pl.BlockSpec(shape, lambda i, j: (0,) * len(shape))
    tile = pl.BlockSpec((1, tm, w), lambda i, j: (i, j, 0))
    pair = pl.BlockSpec((tm // WKV_STEPS, RW_HEADS * WKV_STEPS, LANES), lambda i, j: (j, i, 0))
    pair_shape = jax.ShapeDtypeStruct((t_total // WKV_STEPS, bsz * RW_HEADS * WKV_STEPS, LANES), F32)
    return pl.pallas_call(
        functools.partial(_k_rwprep, ctx_tiles=ctx_tiles),
        grid=(bsz, t_total // tm),
        in_specs=[pl.BlockSpec((1, tm, cols), lambda i, j: (i, jnp.minimum(j, ctx_tiles - 1), 0)),
                  pl.BlockSpec((1, tm, cols), lambda i, j: (i, jnp.maximum(j - ctx_tiles, 0), 0)),
                  full((RW_GATE_RANK, w)), full((2, RW_DECAY_RANK, w)), full((2, 1, w)),
                  full((2, RW_ICLR_RANK, w)), full((2, 1, w)), full((1, w)), full((1, w)), full((1, w)),
                  full((w, w))],
        out_specs=[tile, tile] + [pair] * 5,
        out_shape=[jax.ShapeDtypeStruct((bsz, t_total, w), F32)] * 2 + [pair_shape] * 5,
        compiler_params=_params("arbitrary", "arbitrary"),
        name="rwkv_prep",
    )(u_ctx, u_lat, g_up.astype(BF16), w_up, w0.reshape(2, 1, w), a_up, a0.reshape(2, 1, w),
      k_k.reshape(1, w), k_a.reshape(1, w), r_k.reshape(1, w), ones_bd)


def _k_wkv(rv0_ref, rv1_ref, nb0_ref, nb1_ref, wk0_ref, wk1_ref, y0_ref, y1_ref, s_ref, op_ref, yb_ref, *,
           steps):
    @pl.when(pl.program_id(0) == 0)
    def _():
        s_ref[...] = jnp.zeros_like(s_ref)

    chains = rv0_ref.shape[0] // steps
    r_op, v_op, a_op, b_op, w_op, k_op = range(6)

    def rows(t):
        return pl.ds(t, chains, stride=steps), pl.ds(steps - 1 - t, chains, stride=steps)

    for t in range(steps):
        fwd, bwd = rows(t)
        for j, (ref0, ref1) in enumerate(((rv0_ref, rv1_ref), (nb0_ref, nb1_ref), (wk0_ref, wk1_ref))):
            both = jnp.concatenate([ref0[fwd, :], ref1[bwd, :]], axis=0)
            op_ref[t, 2 * j:2 * j + 2] = both.T.reshape(2, RW_HEAD, 2 * chains)

    def advance(t, sa):
        nxt = jnp.minimum(t + 1, steps - 1)
        row = lambda op, i, slot=t: op_ref[slot, op, pl.ds(i, 1), :]
        vv = op_ref[t, v_op]

        def k_rows(ib, acc):
            y, sa_next = acc
            for u in range(WKV_K_UNROLL):
                i = ib * WKV_K_UNROLL + u
                s_new = s_ref[i] * row(w_op, i) + sa * row(b_op, i) + vv * row(k_op, i)
                s_ref[i] = s_new
                y = y + s_new * row(r_op, i)
                sa_next = sa_next + s_new * row(a_op, i, nxt)
            return y, sa_next
        zero = jnp.zeros(s_ref.shape[1:], F32)
        y, sa_next = lax.fori_loop(0, RW_HEAD // WKV_K_UNROLL, k_rows, (zero, zero))
        yb_ref[t] = y
        return sa_next

    sa = jnp.zeros(s_ref.shape[1:], F32)
    for i in range(RW_HEAD):
        sa = sa + s_ref[i] * op_ref[0, a_op, pl.ds(i, 1), :]
    lax.fori_loop(0, steps, advance, sa)

    for t in range(steps):
        y = yb_ref[t]
        yt = jnp.concatenate([y, y], axis=0).T
        fwd, bwd = rows(t)
        y0_ref[fwd, :] = yt[0:chains]
        y1_ref[bwd, :] = yt[chains:2 * chains]


def wkv_scan(rv, nb0, nb1, wk0, wk1, n_ctx):
    nb, rows, _ = rv.shape
    steps = WKV_STEPS
    chains = rows // steps
    nc = n_ctx // steps
    fwd = pl.BlockSpec((None, rows, LANES), lambda i: (i, 0, 0))
    bwd = pl.BlockSpec((None, rows, LANES), lambda i: (jnp.where(i < nc, nc - 1 - i, nb - 1 + nc - i), 0, 0))
    out = jax.ShapeDtypeStruct(rv.shape, F32)
    return pl.pallas_call(
        functools.partial(_k_wkv, steps=steps),
        grid=(nb,),
        in_specs=[fwd, bwd, fwd, bwd, fwd, bwd],
        out_specs=[fwd, bwd],
        out_shape=[out, out],
        scratch_shapes=[pltpu.VMEM((RW_HEAD, RW_HEAD, 2 * chains), F32),
                        pltpu.VMEM((steps, 6, RW_HEAD, 2 * chains), F32),
                        pltpu.VMEM((steps, RW_HEAD, 2 * chains), F32)],
        compiler_params=_params("arbitrary"),
        name="wkv_scan",
    )(rv, rv, nb0, nb1, wk0, wk1)


def rwkv_scan_branch(u_ctx, u_lat, p, ones_bd):
    g, bonus, rv, nb0, nb1, wk0, wk1 = rwkv_prep(
        u_ctx, u_lat, p["rw_g_up"], p["rw_w_up"], p["rw_w0"], p["rw_a_up"], p["rw_a0"], p["rw_k_k"],
        p["rw_k_a"], p["rw_r_k"].reshape(-1), ones_bd)
    y0, y1 = wkv_scan(rv, nb0, nb1, wk0, wk1, u_ctx.shape[1])
    return y0, y1, bonus, g


def _s5_block(i, n_ctx_blocks, n_blocks, reverse):
    if not reverse:
        return i
    return jnp.where(i < n_ctx_blocks, n_ctx_blocks - 1 - i, n_blocks - 1 + n_ctx_blocks - i)


def _k_s5(uc_ref, ul_ref, bre_ref, bim_ref, lre_ref, lim_ref, cre_ref, cim_ref, y_ref, hre_ref, him_ref,
          st_ref, tb_ref, *, steps, reverse, n_ctx_blocks, n_blocks):
    @pl.when(pl.program_id(0) == 0)
    def _():
        st_ref[...] = jnp.zeros_like(st_ref)

    bsz = uc_ref.shape[0]
    nq = MIX_WIDTH // LANES
    in_ctx = _s5_block(pl.program_id(0), n_ctx_blocks, n_blocks, reverse) < n_ctx_blocks
    for b in range(bsz):
        ub = jnp.where(in_ctx, uc_ref[b], ul_ref[b])
        for q in range(nq):
            tb_ref[q, pl.ds(b, steps, stride=bsz), :] = ub[:, q * LANES:(q + 1) * LANES]
    u = jnp.concatenate([tb_ref[q] for q in range(nq)], axis=1).astype(BF16)
    wc, hc = MIX_WIDTH // S5_DIAG_PARTS, S5_HID // S5_DIAG_PARTS
    for part in range(S5_DIAG_PARTS):
        ws, hs = slice(part * wc, (part + 1) * wc), slice(part * hc, (part + 1) * hc)
        hre_ref[:, hs] = _dot(u[:, ws], bre_ref[ws, hs])
        him_ref[:, hs] = _dot(u[:, ws], bim_ref[ws, hs])
    for ch in range(S5_HID // S5_LANE_CHUNK):
        cols = slice(ch * S5_LANE_CHUNK, (ch + 1) * S5_LANE_CHUNK)
        lre = lre_ref[:, cols]
        lim = lim_ref[:, cols]

        def step(i, carry):
            hr, hi = carry
            t = (steps - 1 - i) if reverse else i
            rows = pl.ds(pl.multiple_of(t * bsz, bsz), bsz)
            nr = lre * hr - lim * hi + hre_ref[rows, cols]
            ni = lre * hi + lim * hr + him_ref[rows, cols]
            hre_ref[rows, cols] = nr
            him_ref[rows, cols] = ni
            return nr, ni
        hr, hi = lax.fori_loop(0, steps, step, (st_ref[0, :, cols], st_ref[1, :, cols]))
        st_ref[0, :, cols] = hr
        st_ref[1, :, cols] = hi
    lanes_per_part = wc // LANES
    for part in range(S5_DIAG_PARTS):
        ws, hs = slice(part * wc, (part + 1) * wc), slice(part * hc, (part + 1) * hc)
        y = (_dot(hre_ref[:, hs].astype(BF16), cre_ref[hs, ws])
             - _dot(him_ref[:, hs].astype(BF16), cim_ref[hs, ws]))
        for q in range(lanes_per_part):
            tb_ref[part * lanes_per_part + q] = y[:, q * LANES:(q + 1) * LANES]
    for b in range(bsz):
        for q in range(nq):
            y_ref[b, :, q * LANES:(q + 1) * LANES] = tb_ref[q, pl.ds(b, steps, stride=bsz), :]


def s5_scan(z_ctx, z_lat, bre, bim, lre, lim, cre, cim, reverse):
    bsz, n_ctx, w = z_ctx.shape
    t_total = n_ctx + z_lat.shape[1]
    steps = S5_STEPS
    nc = n_ctx // steps
    nb = t_total // steps
    blk = functools.partial(_s5_block, n_ctx_blocks=nc, n_blocks=nb, reverse=reverse)
    full = lambda shape: pl.BlockSpec(shape, lambda i: (0,) * len(shape))
    return pl.pallas_call(
        functools.partial(_k_s5, steps=steps, reverse=reverse, n_ctx_blocks=nc, n_blocks=nb),
        grid=(nb,),
        in_specs=[pl.BlockSpec((bsz, steps, w), lambda i: (0, jnp.minimum(blk(i), nc - 1), 0)),
                  pl.BlockSpec((bsz, steps, w), lambda i: (0, jnp.maximum(blk(i) - nc, 0), 0)),
                  full((w, S5_HID)), full((w, S5_HID)), full((bsz, S5_HID)), full((bsz, S5_HID)),
                  full((S5_HID, w)), full((S5_HID, w))],
        out_specs=pl.BlockSpec((bsz, steps, w), lambda i: (0, blk(i), 0)),
        out_shape=jax.ShapeDtypeStruct((bsz, t_total, w), F32),
        scratch_shapes=[pltpu.VMEM((steps * bsz, S5_HID), F32), pltpu.VMEM((steps * bsz, S5_HID), F32),
                        pltpu.VMEM((2, bsz, S5_HID), F32), pltpu.VMEM((w // LANES, steps * bsz, LANES), F32)],
        compiler_params=_params("arbitrary"),
        name="s5_scan",
    )(z_ctx, z_lat, bre, bim, lre, lim, cre, cim)


def _k_s5disc(are_ref, aim_ref, ldt_ref, lre_ref, lim_ref, cr_ref, ci_ref):
    a_re, a_im = are_ref[...], aim_ref[...]
    dt = jnp.exp(ldt_ref[...])
    mag = jnp.exp(a_re * dt)
    lb_re, lb_im = mag * jnp.cos(a_im * dt), mag * jnp.sin(a_im * dt)
    den = a_re * a_re + a_im * a_im
    nr = lb_re - 1.0
    lre_ref[...] = lb_re
    lim_ref[...] = lb_im
    cr_ref[...] = (nr * a_re + lb_im * a_im) / den
    ci_ref[...] = (lb_im * a_re - nr * a_im) / den


def s5_operators(a_re, a_im, log_dt, b_re, b_im, c_re, c_im, bsz):
    g, p, h = S5_GROUPS, S5_STATE, S5_GROUP
    shp = jax.ShapeDtypeStruct((2 * g, p), F32)
    lb_re, lb_im, cr, ci = pl.pallas_call(
        _k_s5disc, out_shape=[shp] * 4, name="s5_discretise",
    )(a_re.reshape(2 * g, p), a_im.reshape(2 * g, p), jnp.broadcast_to(log_dt.reshape(2 * g, 1), (2 * g, p)))
    cr, ci = cr.reshape(2, g, p, 1), ci.reshape(2, g, p, 1)
    bb_re = cr * b_re[None] - ci * b_im[None]
    bb_im = cr * b_im[None] + ci * b_re[None]
    eye = jnp.eye(g, dtype=F32)
    bd_in = lambda m: jnp.einsum("dgph,gk->dghkp", m, eye).reshape(2, g * h, g * p).astype(BF16)
    bd_out = lambda m: jnp.einsum("ghp,gk->gpkh", m, eye).reshape(g * p, g * h).astype(BF16)
    lam = lambda m: jnp.broadcast_to(m.reshape(2, 1, g * p), (2, bsz, g * p))
    return bd_in(bb_re), bd_in(bb_im), lam(lb_re), lam(lb_im), bd_out(c_re), bd_out(c_im)


def s5_branch(z_ctx, z_lat, p):
    bsz = z_ctx.shape[0]
    bre, bim, lre, lim, cre, cim = s5_operators(p["s5_a_re"], p["s5_a_im"], p["s5_log_dt"], p["s5_b_re"],
                                                p["s5_b_im"], p["s5_c_re"], p["s5_c_im"], bsz)
    return (s5_scan(z_ctx, z_lat, bre[0], bim[0], lre[0], lim[0], cre, cim, False),
            s5_scan(z_ctx, z_lat, bre[1], bim[1], lre[1], lim[1], cre, cim, True))


def _k_merge(yhy_ref, rwy0_ref, rwy1_ref, rwb_ref, rwg_ref, s5y0_ref, s5y1_ref, s5u_ref, zg_ref, x_ref, g1_ref,
             e_ref, lng_ref, lnb_ref, s5d_ref, gluw_ref, glub_ref, wb_ref, wo_ref, o_ref):
    w = MIX_WIDTH
    ones_bd = e_ref[...]
    lane = lax.broadcasted_iota(jnp.int32, (x_ref.shape[1], LANES), 1)
    cols = []
    for q in range(RW_HEADS // 2):
        even = _load_head_sum(rwy0_ref, rwy1_ref, 2 * q)
        odd = _load_head_sum(rwy0_ref, rwy1_ref, 2 * q + 1)
        cols.append(jnp.where(lane < RW_HEAD, even, pltpu.roll(odd, RW_HEAD, 1)))
    y = jnp.concatenate(cols, axis=1) + rwb_ref[0]
    mu = _seg_sum(y, ones_bd) * (1.0 / RW_HEAD)
    yc = y - mu
    var = _seg_sum(yc * yc, ones_bd) * (1.0 / RW_HEAD)
    y_rw = (yc * lax.rsqrt(var + RW_LN_EPS) * lng_ref[...] + lnb_ref[...]) * rwg_ref[0]
    s = s5y0_ref[0] + s5y1_ref[0] + s5u_ref[0] * s5d_ref[...]
    s = 0.5 * s * (1.0 + jnp.tanh(math.sqrt(2.0 / math.pi) * (s + 0.044715 * (s * s * s))))
    lg = _dot(s.astype(BF16), gluw_ref[...]) + glub_ref[...]
    y_s5 = lg[:, 0:w] * _sigmoid(lg[:, w:2 * w])
    zg = zg_ref[0]
    d = D_MODEL
    m = (_sigmoid(zg[:, 0:d]) * _dot(yhy_ref[0].astype(BF16), wb_ref[0])
         + _sigmoid(zg[:, d:2 * d]) * _dot(y_rw.astype(BF16), wb_ref[1])
         + _sigmoid(zg[:, 2 * d:3 * d]) * _dot(y_s5.astype(BF16), wb_ref[2]))
    o_ref[0] = x_ref[0] + g1_ref[0] * _dot(m.astype(BF16), wo_ref[...])


def merge_residual(y_hy, rw, s5_y, t_off, s5_u, zg, x, g1, p, ones_bd):
    bsz, n, d = x.shape
    w = MIX_WIDTH
    tm = min(ROW_TILE, n)
    assert t_off % tm == 0
    off = t_off // tm
    tile = lambda c: pl.BlockSpec((1, tm, c), lambda i, j: (i, j, 0))
    seq_tile = pl.BlockSpec((1, tm, w), lambda i, j: (i, off + j, 0))
    chain_tile = pl.BlockSpec((tm // WKV_STEPS, RW_HEADS * WKV_STEPS, LANES), lambda i, j: (off + j, i, 0))
    full = lambda shape: pl.BlockSpec(shape, lambda i, j: (0,) * len(shape))
    rw_y0, rw_y1, rw_bonus, rw_g = rw
    return pl.pallas_call(
        _k_merge,
        grid=(bsz, n // tm),
        in_specs=[tile(w), chain_tile, chain_tile, seq_tile, seq_tile, seq_tile, seq_tile, tile(w),
                  tile(N_BRANCH * d), tile(d), pl.BlockSpec((1, 1, d), lambda i, j: (i, 0, 0)),
                  full((w, w)), full((1, w)), full((1, w)), full((1, w)), full((w, 2 * w)), full((1, 2 * w)),
                  full((N_BRANCH, w, d)), full((d, d))],
        out_specs=tile(d),
        out_shape=jax.ShapeDtypeStruct(x.shape, F32),
        compiler_params=_params("arbitrary", "arbitrary"),
        name="merge_residual",
    )(y_hy, rw_y0, rw_y1, rw_bonus, rw_g, s5_y[0], s5_y[1], s5_u, zg, x, g1, ones_bd, p["rw_ln_g"].reshape(1, w),
      p["rw_ln_b"].reshape(1, w), p["s5_d"].reshape(1, w), p["s5_glu_w"].astype(BF16),
      p["s5_glu_b"].reshape(1, 2 * w), p["w_branch"].astype(BF16), p["w_out"].astype(BF16))


def _k_router(x_ref, sh_ref, sc_ref, g_ref, rw_ref, rb_ref, h_o, idx_o, gate_o, cnt_o):
    @pl.when((pl.program_id(0) == 0) & (pl.program_id(1) == 0))
    def _():
        cnt_o[...] = jnp.zeros_like(cnt_o)

    h = _rms_modulate(x_ref[0], g_ref[...], sh_ref[0], sc_ref[0])
    h_o[0] = h.astype(BF16)
    logits = _dot_hi(h, rw_ref[...]) + rb_ref[...]
    lane = lax.broadcasted_iota(jnp.int32, logits.shape, 1)
    vals, idxs = [], []
    for _ in range(TOP_K):
        m = jnp.max(logits, axis=-1, keepdims=True)
        idx = jnp.min(jnp.where(logits == m, lane, LANES), axis=-1, keepdims=True)
        vals.append(m)
        idxs.append(idx)
        logits = jnp.where(lane == idx, -jnp.inf, logits)
    exps = [jnp.exp(v - vals[0]) for v in vals]
    inv = 1.0 / (exps[0] + exps[1] + exps[2] + exps[3])
    idx_out = jnp.zeros(logits.shape, jnp.int32)
    gate_out = jnp.zeros(logits.shape, F32)
    picked = jnp.zeros(logits.shape, F32)
    for j in range(TOP_K):
        idx_out = jnp.where(lane == j, idxs[j], idx_out)
        gate_out = jnp.where(lane == j, exps[j] * inv, gate_out)
        picked = picked + jnp.where(lane == idxs[j], 1.0, 0.0)
    idx_o[0] = idx_out
    gate_o[0] = gate_out
    cnt_o[...] = cnt_o[...] + jnp.sum(picked, axis=0, keepdims=True)


def moe_route(x, shift, scale, g, router_w, router_b):
    bsz, n, d = x.shape
    tm = min(ROW_TILE, n)
    rw = jnp.pad(router_w, ((0, 0), (0, LANES - N_EXPERTS)))
    rb = jnp.pad(router_b, (0, LANES - N_EXPERTS), constant_values=-jnp.inf).reshape(1, LANES)
    tile = lambda c: pl.BlockSpec((1, tm, c), lambda i, j: (i, j, 0))
    vec = pl.BlockSpec((1, 1, d), lambda i, j: (i, 0, 0))
    full = lambda shape: pl.BlockSpec(shape, lambda i, j: (0,) * len(shape))
    return pl.pallas_call(
        _k_router,
        grid=(bsz, n // tm),
        in_specs=[tile(d), vec, vec, full((1, d)), full((d, LANES)), full((1, LANES))],
        out_specs=[tile(d), tile(LANES), tile(LANES), full((SUBLANES, LANES))],
        out_shape=[jax.ShapeDtypeStruct((bsz, n, d), BF16), jax.ShapeDtypeStruct((bsz, n, LANES), jnp.int32),
                   jax.ShapeDtypeStruct((bsz, n, LANES), F32), jax.ShapeDtypeStruct((SUBLANES, LANES), F32)],
        compiler_params=_params("arbitrary", "arbitrary"),
        name="moe_route",
    )(x, shift, scale, g.reshape(1, d), rw, rb)


def _k_expert(blk_e_ref, xs_ref, w1_ref, b1_ref, w2_ref, b2_ref, sw_ref, o_ref, w1b_ref, w2b_ref):
    i = pl.program_id(0)

    @pl.when((i == 0) | (blk_e_ref[i] != blk_e_ref[jnp.maximum(i - 1, 0)]))
    def _():
        w1b_ref[...] = w1_ref[0].astype(BF16)
        w2b_ref[...] = w2_ref[0].astype(BF16)

    hid = _dot(xs_ref[...], w1b_ref[...]) + b1_ref[0]
    gl = jnp.minimum(hid[:, 0:D_EXPERT], SWIGLU_LIMIT)
    up = jnp.clip(hid[:, D_EXPERT:2 * D_EXPERT], -SWIGLU_LIMIT, SWIGLU_LIMIT)
    act = (up + 1.0) * gl * _sigmoid(SWIGLU_ALPHA * gl)
    o_ref[...] = ((_dot(act.astype(BF16), w2b_ref[...]) + b2_ref[0]) * sw_ref[...]).astype(o_ref.dtype)


def expert_ffn(xs, blk_e, slot_w, w1, b1, w2, b2):
    n_slots, d = xs.shape
    n_blocks = n_slots // MOE_ROWS
    grid_spec = pltpu.PrefetchScalarGridSpec(
        num_scalar_prefetch=1,
        grid=(n_blocks,),
        in_specs=[pl.BlockSpec((MOE_ROWS, d), lambda i, e: (i, 0)),
                  pl.BlockSpec((1, d, 2 * D_EXPERT), lambda i, e: (e[i], 0, 0)),
                  pl.BlockSpec((1, 1, 2 * D_EXPERT), lambda i, e: (e[i], 0, 0)),
                  pl.BlockSpec((1, D_EXPERT, d), lambda i, e: (e[i], 0, 0)),
                  pl.BlockSpec((1, 1, d), lambda i, e: (e[i], 0, 0)),
                  pl.BlockSpec((MOE_ROWS, 1), lambda i, e: (i, 0))],
        out_specs=pl.BlockSpec((MOE_ROWS, d), lambda i, e: (i, 0)),
        scratch_shapes=[pltpu.VMEM((d, 2 * D_EXPERT), BF16), pltpu.VMEM((D_EXPERT, d), BF16)],
    )
    return pl.pallas_call(
        _k_expert,
        grid_spec=grid_spec,
        out_shape=jax.ShapeDtypeStruct((n_slots, d), BF16),
        compiler_params=_params("arbitrary"),
        name="expert_ffn",
    )(blk_e, xs, w1, b1.reshape(b1.shape[0], 1, -1), w2, b2.reshape(b2.shape[0], 1, -1), slot_w.reshape(-1, 1))


def _k_offset_add(tab_ref, e_ref, v_ref, o_ref):
    e = e_ref[...]
    acc = v_ref[...]
    for j in range(N_EXPERTS):
        acc = acc + jnp.where(e == j, tab_ref[j], 0)
    o_ref[...] = acc


def expert_offset_add(flat_e, values, table):
    n = flat_e.shape[0]
    rows = n // LANES
    return pl.pallas_call(
        _k_offset_add,
        in_specs=[pl.BlockSpec(memory_space=pltpu.SMEM), pl.BlockSpec(memory_space=pltpu.VMEM),
                  pl.BlockSpec(memory_space=pltpu.VMEM)],
        out_specs=pl.BlockSpec(memory_space=pltpu.VMEM),
        out_shape=jax.ShapeDtypeStruct((rows, LANES), jnp.int32),
        name="expert_offset_add",
    )(table.astype(jnp.int32), flat_e.reshape(rows, LANES), values.reshape(rows, LANES)).reshape(n)


def moe_ffn(x, shift, scale, g, router_w, router_b, w1, b1, w2, b2, expert_base=0):
    bsz, n, d = x.shape
    h, idx, gate, cnt = moe_route(x, shift, scale, g, router_w, router_b)
    n_tok = bsz * n
    n_asg = n_tok * TOP_K
    flat_e = idx[..., :TOP_K].reshape(-1)
    flat_w = gate[..., :TOP_K].reshape(-1)
    order = jnp.argsort(flat_e).astype(jnp.int32)
    rank_of = jnp.argsort(order).astype(jnp.int32)
    counts = cnt[0, :N_EXPERTS].astype(jnp.int32)
    grp_start = jnp.cumsum(counts) - counts
    padded = (counts + MOE_ROWS - 1) // MOE_ROWS * MOE_ROWS
    pad_end = jnp.cumsum(padded)
    pad_start = pad_end - padded
    n_blocks = -(-n_asg // MOE_ROWS) + N_EXPERTS
    n_slots = n_blocks * MOE_ROWS
    blk_first = jnp.arange(n_blocks, dtype=jnp.int32) * MOE_ROWS
    blk_e = jnp.minimum(jnp.sum(pad_end[None, :] <= blk_first[:, None], axis=1), N_EXPERTS - 1).astype(jnp.int32)
    slot_rank = (blk_first - pad_start[blk_e])[:, None] + jnp.arange(MOE_ROWS, dtype=jnp.int32)[None, :]
    slot_used = (slot_rank < counts[blk_e][:, None]).reshape(-1)
    slot_src = jnp.where(slot_used, (grp_start[blk_e][:, None] + slot_rank).reshape(-1), 0)
    slot_asg = order[slot_src]
    slot_tok = jnp.where(slot_used, slot_asg // TOP_K, 0)
    slot_w = jnp.where(slot_used, flat_w[slot_asg], 0.0)
    slot_of = expert_offset_add(flat_e, rank_of, pad_start - grp_start)
    xs = h.reshape(n_tok, d)[slot_tok]
    ys = expert_ffn(xs, blk_e + expert_base, slot_w, w1, b1, w2, b2)
    slot_of = slot_of.reshape(n_tok, TOP_K)
    out = sum(ys[slot_of[:, j]].astype(F32) for j in range(TOP_K))
    return out.reshape(bsz, n, d)


def _k_rmsnorm(x_ref, g_ref, o_ref):
    x = x_ref[0]
    ms = jnp.mean(x * x, axis=-1, keepdims=True)
    o_ref[0] = x * lax.rsqrt(ms + NORM_EPS) * g_ref[...]


def final_rmsnorm(x, g):
    b, n, d = x.shape
    tm = min(2 * ROW_TILE, n)
    return pl.pallas_call(
        _k_rmsnorm,
        grid=(b, n // tm),
        in_specs=[pl.BlockSpec((1, tm, d), lambda i, j: (i, j, 0)), pl.BlockSpec((1, d), lambda i, j: (0, 0))],
        out_specs=pl.BlockSpec((1, tm, d), lambda i, j: (i, j, 0)),
        out_shape=jax.ShapeDtypeStruct(x.shape, F32),
        compiler_params=_params("arbitrary", "arbitrary"),
        name="final_rmsnorm",
    )(x, g.reshape(1, d))


@functools.lru_cache(maxsize=None)
def _grid_pos_embed(n_tokens):
    rows = n_tokens // GRID_W
    row_id, col_id = np.meshgrid(np.arange(rows), np.arange(GRID_W), indexing="ij")
    quarter = D_MODEL // 4
    omega = (1.0 / (10000.0 ** (np.arange(quarter, dtype=np.float32) / quarter))).astype(np.float32)

    def enc(pos):
        ang = pos.reshape(-1)[:, None].astype(np.float32) * omega
        return np.concatenate([np.sin(ang), np.cos(ang)], axis=-1)

    return np.concatenate([enc(row_id), enc(col_id)], axis=-1).astype(np.float32)


_LAYER_KEYS = ("ada_w", "ada_b", "norm1_g", "norm2_g", "w_in", "hy_conv_w", "hy_conv_b", "hy_w1", "hy_b1",
               "hy_f1", "hy_w2", "hy_b2", "hy_f2", "hy_w3", "hy_b3", "hy_bias", "rw_conv_w", "rw_conv_b",
               "rw_w_up", "rw_w0", "rw_a_up", "rw_a0", "rw_g_up", "rw_k_k", "rw_k_a", "rw_r_k", "rw_ln_g",
               "rw_ln_b", "s5_a_re", "s5_a_im", "s5_log_dt", "s5_b_re", "s5_b_im", "s5_c_re", "s5_c_im",
               "s5_d", "s5_glu_w", "s5_glu_b", "w_branch", "w_out", "router_w", "router_b", "moe_w1",
               "moe_b1", "moe_w2", "moe_b2")


def _token_mixer(x, xc, mod, mod_c, p, need_ctx, ones_bd):
    sh1, sc1, g1 = mod
    csh1, csc1, cg1 = mod_c
    w_in = p["w_in"].astype(BF16)
    c0, c1, c2 = HY_COLS, HY_COLS + RW_COLS, HY_COLS + RW_COLS + MIX_WIDTH
    proj = lambda t, s, c, lo, hi: norm_mod_matmul(t, s, c, p["norm1_g"], w_in[:, lo:hi])
    hy_args = (p["hy_w1"], p["hy_b1"], p["hy_f1"], p["hy_w2"], p["hy_b2"], p["hy_f2"], p["hy_w3"], p["hy_b3"],
               p["hy_bias"])

    z_hy = proj(x, sh1, sc1, 0, c0)
    y_hy = hyena_branch(z_hy, p["hy_conv_w"], p["hy_conv_b"], hyena_spectra(x.shape[1], *hy_args))
    u_rw = short_conv(proj(x, sh1, sc1, c0, c1), p["rw_conv_w"], p["rw_conv_b"])
    uc_rw = short_conv(proj(xc, csh1, csc1, c0, c1), p["rw_conv_w"], p["rw_conv_b"])
    rw = rwkv_scan_branch(uc_rw, u_rw, p, ones_bd)
    n_ctx = xc.shape[1]
    z_s5 = proj(x, sh1, sc1, c1, c2)
    zc_s5 = proj(xc, csh1, csc1, c1, c2)
    s5_y = s5_branch(zc_s5, z_s5, p)
    zg = proj(x, sh1, sc1, c2, c2 + N_BRANCH * D_MODEL)
    x_new = merge_residual(y_hy, rw, s5_y, n_ctx, z_s5, zg, x, g1, p, ones_bd)
    if not need_ctx:
        return x_new, None
    zc_hy = proj(xc, csh1, csc1, 0, c0)
    yc_hy = hyena_branch(zc_hy, p["hy_conv_w"], p["hy_conv_b"], hyena_spectra(xc.shape[1], *hy_args))
    zcg = proj(xc, csh1, csc1, c2, c2 + N_BRANCH * D_MODEL)
    xc_new = merge_residual(yc_hy, rw, s5_y, 0, zc_s5, zcg, xc, cg1, p, ones_bd)
    return x_new, xc_new


def kernel(x, c, ctx, c_ctx, ada_w, ada_b, norm1_g, norm2_g, w_in, hy_conv_w, hy_conv_b, hy_w1, hy_b1, hy_f1, hy_w2, hy_b2, hy_f2, hy_w3, hy_b3, hy_bias, rw_conv_w, rw_conv_b, rw_w_up, rw_w0, rw_a_up, rw_a0, rw_g_up, rw_k_k, rw_k_a, rw_r_k, rw_ln_g, rw_ln_b, s5_a_re, s5_a_im, s5_log_dt, s5_b_re, s5_b_im, s5_c_re, s5_c_im, s5_d, s5_glu_w, s5_glu_b, w_branch, w_out, router_w, router_b, moe_w1, moe_b1, moe_w2, moe_b2, final_g):
    stacked = dict(zip(_LAYER_KEYS, (ada_w, ada_b, norm1_g, norm2_g, w_in, hy_conv_w, hy_conv_b, hy_w1, hy_b1,
                                     hy_f1, hy_w2, hy_b2, hy_f2, hy_w3, hy_b3, hy_bias, rw_conv_w, rw_conv_b,
                                     rw_w_up, rw_w0, rw_a_up, rw_a0, rw_g_up, rw_k_k, rw_k_a, rw_r_k, rw_ln_g,
                                     rw_ln_b, s5_a_re, s5_a_im, s5_log_dt, s5_b_re, s5_b_im, s5_c_re, s5_c_im,
                                     s5_d, s5_glu_w, s5_glu_b, w_branch, w_out, router_w, router_b, moe_w1,
                                     moe_b1, moe_w2, moe_b2)))
    bsz, n, d = x.shape
    depth = ada_w.shape[0]
    x = add_pos(x, jnp.asarray(_grid_pos_embed(n)))
    xc = ctx
    ones_bd = jnp.asarray(np.kron(np.eye(RW_HEADS), np.ones((RW_HEAD, RW_HEAD))), BF16)
    cvec = jnp.zeros((2 * SUBLANES, d), F32).at[:bsz].set(c).at[bsz].set(c_ctx)
    for l in range(depth):
        p = {k: v[l] for k, v in stacked.items()}
        need_ctx = l < depth - 1
        ada = ada_proj(cvec, p["ada_w"], p["ada_b"])
        lat = [t.reshape(bsz, 1, d) for t in jnp.split(ada[:bsz], 6, axis=-1)]
        cx = [jnp.broadcast_to(t.reshape(1, 1, d), (bsz, 1, d)) for t in jnp.split(ada[bsz], 6, axis=-1)]
        x, xc_new = _token_mixer(x, xc, lat[0:3], cx[0:3], p, need_ctx, ones_bd)
        stack = lambda t: t.reshape((depth * N_EXPERTS,) + t.shape[2:])
        moe = functools.partial(moe_ffn, g=p["norm2_g"], router_w=p["router_w"], router_b=p["router_b"],
                                w1=stack(moe_w1), b1=stack(moe_b1), w2=stack(moe_w2), b2=stack(moe_b2),
                                expert_base=l * N_EXPERTS)
        x = x + lat[5] * moe(x, lat[3], lat[4])
        if need_ctx:
            xc = xc_new + cx[5] * moe(xc_new, cx[3], cx[4])
    return final_rmsnorm(x, final_g)
```

```python
import functools
import math

import numpy as np
import jax
import jax.numpy as jnp
from jax import lax
from jax.experimental import pallas as pl
from jax.experimental.pallas import tpu as pltpu

F32 = jnp.float32
BF16 = jnp.bfloat16

D_MODEL = 1024
GRID_W = 64
SHORT_CONV = 3
NORM_EPS = 1e-6
N_BRANCH = 3
MIX_WIDTH = 512

HY_ORDER = 2
HY_BANDS = 16
HY_DECAY_TARGET = 1e-2
HY_FAST_PCT = 0.3
HY_SLOW_PCT = 1.5
HY_COLS = (HY_ORDER + 1) * MIX_WIDTH

RW_HEAD = 64
RW_HEADS = MIX_WIDTH // RW_HEAD
RW_DECAY_RANK = 64
RW_ICLR_RANK = 64
RW_GATE_RANK = 128
RW_LN_EPS = 64e-5
RW_COLS = 3 * MIX_WIDTH + RW_DECAY_RANK + RW_ICLR_RANK + RW_GATE_RANK

S5_GROUP = 16
S5_GROUPS = MIX_WIDTH // S5_GROUP
S5_STATE = 64
S5_HID = S5_GROUPS * S5_STATE

N_EXPERTS = 32
TOP_K = 4
D_EXPERT = 1024
SWIGLU_LIMIT = 7.0
SWIGLU_ALPHA = 1.702

LANES = 128
SUBLANES = 8
VMEM_LIMIT_BYTES = 56 * 1024 * 1024

ROW_TILE = 256
MOE_ROWS = 256
WKV_STEPS = 16
WKV_K_UNROLL = 8
S5_STEPS = 64
S5_LANE_CHUNK = 512
S5_DIAG_PARTS = 2


def _params(*sem):
    return pltpu.CompilerParams(dimension_semantics=sem, vmem_limit_bytes=VMEM_LIMIT_BYTES)


def _dot(a, b):
    return jnp.dot(a, b, preferred_element_type=F32)


def _split3(x):
    hi = x.astype(BF16)
    r1 = x - hi.astype(F32)
    mid = r1.astype(BF16)
    lo = (r1 - mid.astype(F32)).astype(BF16)
    return hi, mid, lo


def _dot_hi(a, b):
    a0, a1, a2 = _split3(a)
    b0, b1, b2 = _split3(b)
    return (_dot(a0, b0) + (_dot(a0, b1) + _dot(a1, b0))
            + (_dot(a1, b1) + _dot(a0, b2) + _dot(a2, b0)))


def _seg_sum(x, ones_bd):
    hi, mid, lo = _split3(x)
    return _dot(hi, ones_bd) + _dot(mid, ones_bd) + _dot(lo, ones_bd)


def _sigmoid(x):
    return 1.0 / (1.0 + jnp.exp(-x))


def _rms_modulate(x, g, shift, scale):
    ms = jnp.mean(x * x, axis=-1, keepdims=True)
    y = x * lax.rsqrt(ms + NORM_EPS) * g
    return y * (1.0 + scale) + shift


def _k_ada(c_ref, w_ref, b_ref, o_ref):
    c = c_ref[...]
    s = c * _sigmoid(c)
    o_ref[...] = _dot_hi(s, w_ref[...]) + b_ref[...]


def ada_proj(cvec, w, b):
    rows, d = cvec.shape
    n = w.shape[1]
    tn = 1536
    return pl.pallas_call(
        _k_ada,
        grid=(n // tn,),
        in_specs=[pl.BlockSpec((rows, d), lambda j: (0, 0)),
                  pl.BlockSpec((d, tn), lambda j: (0, j)),
                  pl.BlockSpec((1, tn), lambda j: (0, j))],
        out_specs=pl.BlockSpec((rows, tn), lambda j: (0, j)),
        out_shape=jax.ShapeDtypeStruct((rows, n), F32),
        compiler_params=_params("arbitrary"),
        name="ada_proj",
    )(cvec, w, b.reshape(1, n))


def _k_add_pos(x_ref, p_ref, o_ref):
    o_ref[0] = x_ref[0] + p_ref[...]


def add_pos(x, pos):
    b, n, d = x.shape
    tm = min(ROW_TILE * 2, n)
    return pl.pallas_call(
        _k_add_pos,
        grid=(b, n // tm),
        in_specs=[pl.BlockSpec((1, tm, d), lambda i, j: (i, j, 0)),
                  pl.BlockSpec((tm, d), lambda i, j: (j, 0))],
        out_specs=pl.BlockSpec((1, tm, d), lambda i, j: (i, j, 0)),
        out_shape=jax.ShapeDtypeStruct(x.shape, F32),
        compiler_params=_params("arbitrary", "arbitrary"),
        name="add_pos",
    )(x, pos)


def _k_norm_mm(x_ref, sh_ref, sc_ref, g_ref, w_ref, o_ref, h_ref):
    @pl.when(pl.program_id(2) == 0)
    def _():
        h = _rms_modulate(x_ref[0], g_ref[...], sh_ref[0], sc_ref[0])
        h_ref[...] = h.astype(BF16)

    o_ref[0] = _dot(h_ref[...], w_ref[...])


def norm_mod_matmul(x, shift, scale, g, w):
    b, n, d = x.shape
    ncol = w.shape[1]
    tm = min(2 * ROW_TILE, n)
    tn = ncol if ncol <= 1792 else 1536
    return pl.pallas_call(
        _k_norm_mm,
        grid=(b, n // tm, ncol // tn),
        in_specs=[pl.BlockSpec((1, tm, d), lambda i, j, k: (i, j, 0)),
                  pl.BlockSpec((1, 1, d), lambda i, j, k: (i, 0, 0)),
                  pl.BlockSpec((1, 1, d), lambda i, j, k: (i, 0, 0)),
                  pl.BlockSpec((1, d), lambda i, j, k: (0, 0)),
                  pl.BlockSpec((d, tn), lambda i, j, k: (0, k))],
        out_specs=pl.BlockSpec((1, tm, tn), lambda i, j, k: (i, j, k)),
        out_shape=jax.ShapeDtypeStruct((b, n, ncol), F32),
        scratch_shapes=[pltpu.VMEM((tm, d), BF16)],
        compiler_params=_params("arbitrary", "arbitrary", "arbitrary"),
        name="norm_mod_matmul",
    )(x, shift, scale, g.reshape(1, d), w)


def _short_conv_val(z, w, b):
    n = z.shape[0]
    row = lax.broadcasted_iota(jnp.int32, z.shape, 0)
    zm = jnp.where(row == 0, 0.0, pltpu.roll(z, 1, 0))
    zp = jnp.where(row == n - 1, 0.0, pltpu.roll(z, n - 1, 0))
    return zm * w[0:1] + z * w[1:2] + zp * w[2:3] + b


def _k_sconv(z_ref, w_ref, b_ref, o_ref):
    o_ref[0] = _short_conv_val(z_ref[0], w_ref[...], b_ref[...])


def short_conv(z, w, b):
    bsz, n, c = z.shape
    tc = 256 if c % 256 == 0 else LANES
    return pl.pallas_call(
        _k_sconv,
        grid=(bsz, c // tc),
        in_specs=[pl.BlockSpec((1, n, tc), lambda i, j: (i, 0, j)),
                  pl.BlockSpec((SHORT_CONV, tc), lambda i, j: (0, j)),
                  pl.BlockSpec((1, tc), lambda i, j: (0, j))],
        out_specs=pl.BlockSpec((1, n, tc), lambda i, j: (i, 0, j)),
        out_shape=jax.ShapeDtypeStruct(z.shape, F32),
        compiler_params=_params("arbitrary", "arbitrary"),
        name="short_conv",
    )(z, w, b.reshape(1, c))


def _fft_split(n):
    total = 2 * n
    bits = total.bit_length() - 1
    assert 1 << bits == total
    n1 = 1 << ((bits + 1) // 2)
    n1 = max(n1, 2 * SUBLANES)
    return n1, total // n1


@functools.lru_cache(maxsize=None)
def _dft_mats(n):
    n1, n2 = _fft_split(n)
    total = n1 * n2
    k1 = np.arange(n1)[:, None]
    m1 = np.arange(n1)[None, :]
    m2 = np.arange(n2)[:, None, None]
    ang = -2.0 * np.pi * ((n2 * k1 * m1)[None] + m2 * k1[None]) / total
    fa = np.concatenate([np.cos(ang), np.sin(ang)], axis=1)
    k2 = np.arange(n2)[:, None]
    mm = np.arange(n2)[None, :]
    gang = -2.0 * np.pi * k2 * mm / n2
    gr, gi = np.cos(gang), np.sin(gang)
    fb = np.block([[gr, -gi], [gi, gr]])
    fc = np.block([[gr, gi], [-gi, gr]])
    m1d = np.arange(n1 // 2)[:, None]
    k1d = np.arange(n1)[None, :]
    dang = 2.0 * np.pi * ((n2 * m1d * k1d)[None] + m2 * k1d[None]) / total
    fd = np.concatenate([np.cos(dang), -np.sin(dang)], axis=2) / total
    return (fa.astype(np.float32), fb.astype(np.float32), fc.astype(np.float32), fd.astype(np.float32))


def _time_pitch(n2):
    return n2 + SUBLANES


def _spec_pitch(n2):
    return 2 * n2 + SUBLANES


def _fft_stage_a(src_ref, y_ref, fa_ref, n1, n2, k_rows):
    tp, sp = _time_pitch(n2), _spec_pitch(n2)

    def body(m2, c):
        xs = src_ref[pl.ds(m2, k_rows, stride=tp), :].astype(BF16)
        res = _dot(fa_ref[m2, :, 0:k_rows], xs)
        y_ref[pl.ds(m2, n1, stride=sp), :] = res[0:n1]
        y_ref[pl.ds(n2 + m2, n1, stride=sp), :] = res[n1:2 * n1]
        return c
    lax.fori_loop(0, n2, body, 0, unroll=8)


def _long_conv(buf_ref, y_ref, h_ref, fa_ref, fb_ref, fc_ref, fd_ref, n1, n2):
    tp, sp = _time_pitch(n2), _spec_pitch(n2)
    _fft_stage_a(buf_ref, y_ref, fa_ref, n1, n2, n1 // 2)

    def body_k1(k1, c):
        rows = pl.ds(pl.multiple_of(k1 * sp, SUBLANES), 2 * n2)
        z = _dot(fb_ref[...], y_ref[rows, :].astype(BF16))
        zr, zi = z[0:n2], z[n2:2 * n2]
        h = h_ref[k1]
        hr, hi = h[0:n2], h[n2:2 * n2]
        p = jnp.concatenate([zr * hr - zi * hi, zr * hi + zi * hr], axis=0)
        y_ref[rows, :] = _dot(fc_ref[...], p.astype(BF16))
        return c
    lax.fori_loop(0, n1, body_k1, 0, unroll=16)

    def body_m2(m2, c):
        qr = y_ref[pl.ds(m2, n1, stride=sp), :]
        qi = y_ref[pl.ds(n2 + m2, n1, stride=sp), :]
        q = jnp.concatenate([qr, qi], axis=0).astype(BF16)
        buf_ref[pl.ds(m2, n1 // 2, stride=tp), :] = _dot(fd_ref[m2], q)
        return c
    lax.fori_loop(0, n2, body_m2, 0, unroll=8)


def _store_time_blocks(buf_ref, val, n2):
    tp = _time_pitch(n2)
    for m1 in range(val.shape[0] // n2):
        buf_ref[m1 * tp:m1 * tp + n2, :] = val[m1 * n2:(m1 + 1) * n2]


def _load_time_blocks(buf_ref, n, n2):
    tp = _time_pitch(n2)
    return jnp.concatenate([buf_ref[m1 * tp:m1 * tp + n2, :] for m1 in range(n // n2)], axis=0)


def _k_hyena(x1_ref, x2_ref, v_ref, w1_ref, w2_ref, wv_ref, b1_ref, b2_ref, bv_ref, h1_ref, h2_ref,
             fa_ref, fb_ref, fc_ref, fd_ref, o_ref, buf_ref, y_ref, *, n1, n2):
    n = x1_ref.shape[1]
    conv = functools.partial(_long_conv, buf_ref, y_ref, fa_ref=fa_ref, fb_ref=fb_ref, fc_ref=fc_ref,
                             fd_ref=fd_ref, n1=n1, n2=n2)
    _store_time_blocks(buf_ref, _short_conv_val(v_ref[0], wv_ref[...], bv_ref[...]), n2)
    conv(h1_ref)
    gated = _short_conv_val(x1_ref[0], w1_ref[...], b1_ref[...]) * _load_time_blocks(buf_ref, n, n2)
    _store_time_blocks(buf_ref, gated, n2)
    conv(h2_ref)
    o_ref[0] = _short_conv_val(x2_ref[0], w2_ref[...], b2_ref[...]) * _load_time_blocks(buf_ref, n, n2)


def _k_fspec(f_ref, fa_ref, fb_ref, o_ref, y_ref, *, n1, n2):
    sp = _spec_pitch(n2)
    _fft_stage_a(f_ref, y_ref, fa_ref, n1, n2, n1)

    def body_k1(k1, c):
        src = pl.ds(pl.multiple_of(k1 * sp, SUBLANES), 2 * n2)
        dst = pl.ds(pl.multiple_of(k1 * 2 * n2, 2 * n2), 2 * n2)
        o_ref[dst, :] = _dot(fb_ref[...], y_ref[src, :].astype(BF16))
        return c
    lax.fori_loop(0, n1, body_k1, 0, unroll=2)


def filter_spectrum(filt, n):
    n1, n2 = _fft_split(n)
    total, c = filt.shape
    tp, sp = _time_pitch(n2), _spec_pitch(n2)
    fa, fb, _, _ = _dft_mats(n)
    blocks = jnp.pad(filt.reshape(n1, n2, c), ((0, 0), (0, tp - n2), (0, 0))).reshape(n1 * tp, c)
    out = pl.pallas_call(
        functools.partial(_k_fspec, n1=n1, n2=n2),
        grid=(c // LANES,),
        in_specs=[pl.BlockSpec((n1 * tp, LANES), lambda j: (0, j)),
                  pl.BlockSpec((n2, 2 * n1, n1), lambda j: (0, 0, 0)),
                  pl.BlockSpec((2 * n2, 2 * n2), lambda j: (0, 0))],
        out_specs=pl.BlockSpec((total * 2, LANES), lambda j: (0, j)),
        out_shape=jax.ShapeDtypeStruct((2 * total, c), F32),
        scratch_shapes=[pltpu.VMEM((n1 * sp, LANES), F32)],
        compiler_params=_params("arbitrary"),
        name="filter_spectrum",
    )(blocks, jnp.asarray(fa, BF16), jnp.asarray(fb, BF16))
    return out.reshape(n1, 2 * n2, c)


def _k_filter_mlp(feat_ref, win_ref, w1_ref, b1_ref, f1_ref, w2_ref, b2_ref, f2_ref, w3_ref, b3_ref, o_ref):
    h = jnp.sin(f1_ref[...] * (_dot_hi(feat_ref[...], w1_ref[...]) + b1_ref[...]))
    h = jnp.sin(f2_ref[...] * (_dot_hi(h, w2_ref[...]) + b2_ref[...]))
    o_ref[...] = (_dot_hi(h, w3_ref[...]) + b3_ref[...]) * win_ref[...]


@functools.lru_cache(maxsize=None)
def _filter_consts(n):
    t = np.linspace(0.0, 1.0, n, dtype=np.float32)[:, None]
    bands = np.linspace(1e-4, HY_BANDS - 1, HY_BANDS, dtype=np.float32)
    ang = (np.float32(2 * math.pi) * np.arange(n, dtype=np.float32) / np.float32(n))[:, None] * bands
    feats = np.concatenate([t, np.cos(ang), -np.sin(ang)], axis=-1).astype(np.float32)
    pad = (-feats.shape[1]) % SUBLANES
    feats = np.pad(feats, ((0, 0), (0, pad)))
    deltas = np.abs(np.linspace(math.log(HY_DECAY_TARGET) / HY_SLOW_PCT,
                                math.log(HY_DECAY_TARGET) / HY_FAST_PCT, MIX_WIDTH, dtype=np.float32))
    window = np.exp(-t * deltas).astype(np.float32)
    return feats, np.tile(window, (1, 2 * HY_ORDER))


def hyena_filter_table(n, w1, b1, f1, w2, b2, f2, w3, b3):
    feats, window = _filter_consts(n)
    kf = feats.shape[1]
    fd = w1.shape[1]
    ncol = w3.shape[1]
    w1p = jnp.pad(w1, ((0, kf - w1.shape[0]), (0, 0)))
    tm = min(ROW_TILE, n)
    full = lambda shape: pl.BlockSpec(shape, lambda i: (0,) * len(shape))
    return pl.pallas_call(
        _k_filter_mlp,
        grid=(n // tm,),
        in_specs=[pl.BlockSpec((tm, kf), lambda i: (i, 0)),
                  pl.BlockSpec((tm, ncol), lambda i: (i, 0)),
                  full((kf, fd)), full((1, fd)), full((1, fd)),
                  full((fd, fd)), full((1, fd)), full((1, fd)),
                  full((fd, ncol)), full((1, ncol))],
        out_specs=pl.BlockSpec((tm, ncol), lambda i: (i, 0)),
        out_shape=jax.ShapeDtypeStruct((n, ncol), F32),
        compiler_params=_params("arbitrary"),
        name="hyena_filter_mlp",
    )(jnp.asarray(feats), jnp.asarray(window), w1p, b1.reshape(1, fd), f1.reshape(1, fd),
      w2, b2.reshape(1, fd), f2.reshape(1, fd), w3, b3.reshape(1, ncol))


def hyena_spectra(n, w1, b1, f1, w2, b2, f2, w3, b3, bias):
    hf = hyena_filter_table(n, w1, b1, f1, w2, b2, f2, w3, b3).reshape(n, HY_ORDER, 2, MIX_WIDTH)
    fwd = hf[:, :, 0]
    bwd = hf[1:, :, 1][::-1]
    fwd = fwd.at[0].add(bias)
    filt = jnp.concatenate([fwd, jnp.zeros((1, HY_ORDER, MIX_WIDTH), F32), bwd], axis=0)
    spec = filter_spectrum(filt.reshape(2 * n, HY_ORDER * MIX_WIDTH), n)
    n1, n2 = _fft_split(n)
    return spec.reshape(n1, 2 * n2, HY_ORDER, MIX_WIDTH).transpose(2, 0, 1, 3)


def hyena_branch(z, conv_w, conv_b, spectra):
    bsz, n, _ = z.shape
    n1, n2 = _fft_split(n)
    fa, fb, fc, fd = (jnp.asarray(m, BF16) for m in _dft_mats(n))
    fa = fa[:, :, : n1 // 2]
    nt = MIX_WIDTH // LANES
    cb = conv_b.reshape(1, HY_COLS)
    zspec = lambda off: pl.BlockSpec((1, n, LANES), lambda j, i: (i, 0, off + j))
    wspec = lambda off: pl.BlockSpec((SHORT_CONV, LANES), lambda j, i: (0, off + j))
    bspec = lambda off: pl.BlockSpec((1, LANES), lambda j, i: (0, off + j))
    once = pl.Buffered(1)
    hspec = lambda o: pl.BlockSpec((None, n1, 2 * n2, LANES), lambda j, i: (o, 0, 0, j), pipeline_mode=once)
    full = lambda shape: pl.BlockSpec(shape, lambda j, i: (0,) * len(shape), pipeline_mode=once)
    return pl.pallas_call(
        functools.partial(_k_hyena, n1=n1, n2=n2),
        grid=(nt, bsz),
        in_specs=[zspec(0), zspec(nt), zspec(2 * nt), wspec(0), wspec(nt), wspec(2 * nt),
                  bspec(0), bspec(nt), bspec(2 * nt), hspec(0), hspec(1),
                  full(fa.shape), full(fb.shape), full(fc.shape), full(fd.shape)],
        out_specs=pl.BlockSpec((1, n, LANES), lambda j, i: (i, 0, j)),
        out_shape=jax.ShapeDtypeStruct((bsz, n, MIX_WIDTH), F32),
        scratch_shapes=[pltpu.VMEM((n1 // 2 * _time_pitch(n2), LANES), F32),
                        pltpu.VMEM((n1 * _spec_pitch(n2), LANES), F32)],
        compiler_params=_params("arbitrary", "arbitrary"),
        name="hyena_long_conv",
    )(z, z, z, conv_w, conv_w, conv_w, cb, cb, cb, spectra, spectra, fa, fb, fc, fd)


def _store_head_pairs(o_ref, left, right):
    rows = left.shape[0]
    lane = lax.broadcasted_iota(jnp.int32, (rows, LANES), 1)
    low = lane < RW_HEAD
    for q in range(RW_HEADS // 2):
        lcol = left[:, q * LANES:(q + 1) * LANES]
        rcol = right[:, q * LANES:(q + 1) * LANES]
        heads = (jnp.where(low, lcol, pltpu.roll(rcol, RW_HEAD, 1)),
                 jnp.where(low, pltpu.roll(lcol, RW_HEAD, 1), rcol))
        for h, piece in zip((2 * q, 2 * q + 1), heads):
            for c in range(rows // WKV_STEPS):
                o_ref[c, h * WKV_STEPS:(h + 1) * WKV_STEPS, :] = piece[c * WKV_STEPS:(c + 1) * WKV_STEPS]


def _load_head_sum(y0_ref, y1_ref, h):
    sl = slice(h * WKV_STEPS, (h + 1) * WKV_STEPS)
    return jnp.concatenate([y0_ref[c, sl, :] + y1_ref[c, sl, :] for c in range(y0_ref.shape[0])], axis=0)


def _k_rwprep(uc_ref, ul_ref, gup_ref, wup_ref, w0_ref, aup_ref, a0_ref, kk_ref, ka_ref, rk_ref, e_ref,
              g_o, bonus_o, rv_o, nb0_o, nb1_o, wk0_o, wk1_o, *, ctx_tiles):
    w = MIX_WIDTH
    u = jnp.where(pl.program_id(1) < ctx_tiles, uc_ref[0], ul_ref[0])
    r, k, v = u[:, 0:w], u[:, w:2 * w], u[:, 2 * w:3 * w]
    xw = u[:, 3 * w:3 * w + RW_DECAY_RANK]
    xa = u[:, 3 * w + RW_DECAY_RANK:3 * w + RW_DECAY_RANK + RW_ICLR_RANK]
    xg = u[:, 3 * w + RW_DECAY_RANK + RW_ICLR_RANK:]
    ones_bd = e_ref[...]
    g_o[0] = _dot(_sigmoid(xg).astype(BF16), gup_ref[...])
    kk = k * kk_ref[...]
    kk = kk * lax.rsqrt(jnp.maximum(_seg_sum(kk * kk, ones_bd), 1e-24))
    txw = jnp.tanh(xw)
    kd_sum = None
    for d, (nb_o, wk_o) in enumerate(((nb0_o, wk0_o), (nb1_o, wk1_o))):
        x = -(w0_ref[d] + _dot_hi(txw, wup_ref[d]))
        softplus = jnp.maximum(x, 0.0) + jnp.log(1.0 + jnp.exp(-jnp.abs(x)))
        decay = jnp.exp(-jnp.exp(-softplus - 0.5))
        a = _sigmoid(a0_ref[d] + _dot_hi(xa, aup_ref[d]))
        kd = k * (1.0 + (a - 1.0) * ka_ref[...])
        _store_head_pairs(nb_o, -kk, kk * a)
        _store_head_pairs(wk_o, decay, kd)
        kd_sum = kd if kd_sum is None else kd_sum + kd
    _store_head_pairs(rv_o, r, v)
    bonus_o[0] = _seg_sum(r * kd_sum * rk_ref[...], ones_bd) * v


def rwkv_prep(u_ctx, u_lat, g_up, w_up, w0, a_up, a0, k_k, k_a, r_k, ones_bd):
    bsz, n_ctx, cols = u_ctx.shape
    n_lat = u_lat.shape[1]
    t_total = n_ctx + n_lat
    w = MIX_WIDTH
    tm = min(ROW_TILE, n_ctx)
    ctx_tiles = n_ctx // tm
    full = lambda shape: pl.BlockSpec(shape, lambda i, j: (0,) * len(shape))
    tile = pl.BlockSpec((1, tm, w), lambda i, j: (i, j, 0))
    pair = pl.BlockSpec((tm // WKV_STEPS, RW_HEADS * WKV_STEPS, LANES), lambda i, j: (j, i, 0))
    pair_shape = jax.ShapeDtypeStruct((t_total // WKV_STEPS, bsz * RW_HEADS * WKV_STEPS, LANES), F32)
    return pl.pallas_call(
        functools.partial(_k_rwprep, ctx_tiles=ctx_tiles),
        grid=(bsz, t_total // tm),
        in_specs=[pl.BlockSpec((1, tm, cols), lambda i, j: (i, jnp.minimum(j, ctx_tiles - 1), 0)),
                  pl.BlockSpec((1, tm, cols), lambda i, j: (i, jnp.maximum(j - ctx_tiles, 0), 0)),
                  full((RW_GATE_RANK, w)), full((2, RW_DECAY_RANK, w)), full((2, 1, w)),
                  full((2, RW_ICLR_RANK, w)), full((2, 1, w)), full((1, w)), full((1, w)), full((1, w)),
                  full((w, w))],
        out_specs=[tile, tile] + [pair] * 5,
        out_shape=[jax.ShapeDtypeStruct((bsz, t_total, w), F32)] * 2 + [pair_shape] * 5,
        compiler_params=_params("arbitrary", "arbitrary"),
        name="rwkv_prep",
    )(u_ctx, u_lat, g_up.astype(BF16), w_up, w0.reshape(2, 1, w), a_up, a0.reshape(2, 1, w),
      k_k.reshape(1, w), k_a.reshape(1, w), r_k.reshape(1, w), ones_bd)


def _k_wkv(rv0_ref, rv1_ref, nb0_ref, nb1_ref, wk0_ref, wk1_ref, y0_ref, y1_ref, s_ref, op_ref, yb_ref, *,
           steps):
    @pl.when(pl.program_id(0) == 0)
    def _():
        s_ref[...] = jnp.zeros_like(s_ref)

    chains = rv0_ref.shape[0] // steps
    r_op, v_op, a_op, b_op, w_op, k_op = range(6)

    def rows(t):
        return pl.ds(t, chains, stride=steps), pl.ds(steps - 1 - t, chains, stride=steps)

    for t in range(steps):
        fwd, bwd = rows(t)
        for j, (ref0, ref1) in enumerate(((rv0_ref, rv1_ref), (nb0_ref, nb1_ref), (wk0_ref, wk1_ref))):
            both = jnp.concatenate([ref0[fwd, :], ref1[bwd, :]], axis=0)
            op_ref[t, 2 * j:2 * j + 2] = both.T.reshape(2, RW_HEAD, 2 * chains)

    def advance(t, sa):
        nxt = jnp.minimum(t + 1, steps - 1)
        row = lambda op, i, slot=t: op_ref[slot, op, pl.ds(i, 1), :]
        vv = op_ref[t, v_op]

        def k_rows(ib, acc):
            y, sa_next = acc
            for u in range(WKV_K_UNROLL):
                i = ib * WKV_K_UNROLL + u
                s_new = s_ref[i] * row(w_op, i) + sa * row(b_op, i) + vv * row(k_op, i)
                s_ref[i] = s_new
                y = y + s_new * row(r_op, i)
                sa_next = sa_next + s_new * row(a_op, i, nxt)
            return y, sa_next
        zero = jnp.zeros(s_ref.shape[1:], F32)
        y, sa_next = lax.fori_loop(0, RW_HEAD // WKV_K_UNROLL, k_rows, (zero, zero))
        yb_ref[t] = y
        return sa_next

    sa = jnp.zeros(s_ref.shape[1:], F32)
    for i in range(RW_HEAD):
        sa = sa + s_ref[i] * op_ref[0, a_op, pl.ds(i, 1), :]
    lax.fori_loop(0, steps, advance, sa)

    for t in range(steps):
        y = yb_ref[t]
        yt = jnp.concatenate([y, y], axis=0).T
        fwd, bwd = rows(t)
        y0_ref[fwd, :] = yt[0:chains]
        y1_ref[bwd, :] = yt[chains:2 * chains]


def wkv_scan(rv, nb0, nb1, wk0, wk1, n_ctx):
    nb, rows, _ = rv.shape
    steps = WKV_STEPS
    chains = rows // steps
    nc = n_ctx // steps
    fwd = pl.BlockSpec((None, rows, LANES), lambda i: (i, 0, 0))
    bwd = pl.BlockSpec((None, rows, LANES), lambda i: (jnp.where(i < nc, nc - 1 - i, nb - 1 + nc - i), 0, 0))
    out = jax.ShapeDtypeStruct(rv.shape, F32)
    return pl.pallas_call(
        functools.partial(_k_wkv, steps=steps),
        grid=(nb,),
        in_specs=[fwd, bwd, fwd, bwd, fwd, bwd],
        out_specs=[fwd, bwd],
        out_shape=[out, out],
        scratch_shapes=[pltpu.VMEM((RW_HEAD, RW_HEAD, 2 * chains), F32),
                        pltpu.VMEM((steps, 6, RW_HEAD, 2 * chains), F32),
                        pltpu.VMEM((steps, RW_HEAD, 2 * chains), F32)],
        compiler_params=_params("arbitrary"),
        name="wkv_scan",
    )(rv, rv, nb0, nb1, wk0, wk1)


def rwkv_scan_branch(u_ctx, u_lat, p, ones_bd):
    g, bonus, rv, nb0, nb1, wk0, wk1 = rwkv_prep(
        u_ctx, u_lat, p["rw_g_up"], p["rw_w_up"], p["rw_w0"], p["rw_a_up"], p["rw_a0"], p["rw_k_k"],
        p["rw_k_a"], p["rw_r_k"].reshape(-1), ones_bd)
    y0, y1 = wkv_scan(rv, nb0, nb1, wk0, wk1, u_ctx.shape[1])
    return y0, y1, bonus, g


def _s5_block(i, n_ctx_blocks, n_blocks, reverse):
    if not reverse:
        return i
    return jnp.where(i < n_ctx_blocks, n_ctx_blocks - 1 - i, n_blocks - 1 + n_ctx_blocks - i)


def _k_s5(uc_ref, ul_ref, bre_ref, bim_ref, lre_ref, lim_ref, cre_ref, cim_ref, y_ref, hre_ref, him_ref,
          st_ref, tb_ref, *, steps, reverse, n_ctx_blocks, n_blocks):
    @pl.when(pl.program_id(0) == 0)
    def _():
        st_ref[...] = jnp.zeros_like(st_ref)

    bsz = uc_ref.shape[0]
    nq = MIX_WIDTH // LANES
    in_ctx = _s5_block(pl.program_id(0), n_ctx_blocks, n_blocks, reverse) < n_ctx_blocks
    for b in range(bsz):
        ub = jnp.where(in_ctx, uc_ref[b], ul_ref[b])
        for q in range(nq):
            tb_ref[q, pl.ds(b, steps, stride=bsz), :] = ub[:, q * LANES:(q + 1) * LANES]
    u = jnp.concatenate([tb_ref[q] for q in range(nq)], axis=1).astype(BF16)
    wc, hc = MIX_WIDTH // S5_DIAG_PARTS, S5_HID // S5_DIAG_PARTS
    for part in range(S5_DIAG_PARTS):
        ws, hs = slice(part * wc, (part + 1) * wc), slice(part * hc, (part + 1) * hc)
        hre_ref[:, hs] = _dot(u[:, ws], bre_ref[ws, hs])
        him_ref[:, hs] = _dot(u[:, ws], bim_ref[ws, hs])
    for ch in range(S5_HID // S5_LANE_CHUNK):
        cols = slice(ch * S5_LANE_CHUNK, (ch + 1) * S5_LANE_CHUNK)
        lre = lre_ref[:, cols]
        lim = lim_ref[:, cols]

        def step(i, carry):
            hr, hi = carry
            t = (steps - 1 - i) if reverse else i
            rows = pl.ds(pl.multiple_of(t * bsz, bsz), bsz)
            nr = lre * hr - lim * hi + hre_ref[rows, cols]
            ni = lre * hi + lim * hr + him_ref[rows, cols]
            hre_ref[rows, cols] = nr
            him_ref[rows, cols] = ni
            return nr, ni
        hr, hi = lax.fori_loop(0, steps, step, (st_ref[0, :, cols], st_ref[1, :, cols]))
        st_ref[0, :, cols] = hr
        st_ref[1, :, cols] = hi
    lanes_per_part = wc // LANES
    for part in range(S5_DIAG_PARTS):
        ws, hs = slice(part * wc, (part + 1) * wc), slice(part * hc, (part + 1) * hc)
        y = (_dot(hre_ref[:, hs].astype(BF16), cre_ref[hs, ws])
             - _dot(him_ref[:, hs].astype(BF16), cim_ref[hs, ws]))
        for q in range(lanes_per_part):
            tb_ref[part * lanes_per_part + q] = y[:, q * LANES:(q + 1) * LANES]
    for b in range(bsz):
        for q in range(nq):
            y_ref[b, :, q * LANES:(q + 1) * LANES] = tb_ref[q, pl.ds(b, steps, stride=bsz), :]


def s5_scan(z_ctx, z_lat, bre, bim, lre, lim, cre, cim, reverse):
    bsz, n_ctx, w = z_ctx.shape
    t_total = n_ctx + z_lat.shape[1]
    steps = S5_STEPS
    nc = n_ctx // steps
    nb = t_total // steps
    blk = functools.partial(_s5_block, n_ctx_blocks=nc, n_blocks=nb, reverse=reverse)
    full = lambda shape: pl.BlockSpec(shape, lambda i: (0,) * len(shape))
    return pl.pallas_call(
        functools.partial(_k_s5, steps=steps, reverse=reverse, n_ctx_blocks=nc, n_blocks=nb),
        grid=(nb,),
        in_specs=[pl.BlockSpec((bsz, steps, w), lambda i: (0, jnp.minimum(blk(i), nc - 1), 0)),
                  pl.BlockSpec((bsz, steps, w), lambda i: (0, jnp.maximum(blk(i) - nc, 0), 0)),
                  full((w, S5_HID)), full((w, S5_HID)), full((bsz, S5_HID)), full((bsz, S5_HID)),
                  full((S5_HID, w)), full((S5_HID, w))],
        out_specs=pl.BlockSpec((bsz, steps, w), lambda i: (0, blk(i), 0)),
        out_shape=jax.ShapeDtypeStruct((bsz, t_total, w), F32),
        scratch_shapes=[pltpu.VMEM((steps * bsz, S5_HID), F32), pltpu.VMEM((steps * bsz, S5_HID), F32),
                        pltpu.VMEM((2, bsz, S5_HID), F32), pltpu.VMEM((w // LANES, steps * bsz, LANES), F32)],
        compiler_params=_params("arbitrary"),
        name="s5_scan",
    )(z_ctx, z_lat, bre, bim, lre, lim, cre, cim)


def _k_s5disc(are_ref, aim_ref, ldt_ref, lre_ref, lim_ref, cr_ref, ci_ref):
    a_re, a_im = are_ref[...], aim_ref[...]
    dt = jnp.exp(ldt_ref[...])
    mag = jnp.exp(a_re * dt)
    lb_re, lb_im = mag * jnp.cos(a_im * dt), mag * jnp.sin(a_im * dt)
    den = a_re * a_re + a_im * a_im
    nr = lb_re - 1.0
    lre_ref[...] = lb_re
    lim_ref[...] = lb_im
    cr_ref[...] = (nr * a_re + lb_im * a_im) / den
    ci_ref[...] = (lb_im * a_re - nr * a_im) / den


def s5_operators(a_re, a_im, log_dt, b_re, b_im, c_re, c_im, bsz):
    g, p, h = S5_GROUPS, S5_STATE, S5_GROUP
    shp = jax.ShapeDtypeStruct((2 * g, p), F32)
    lb_re, lb_im, cr, ci = pl.pallas_call(
        _k_s5disc, out_shape=[shp] * 4, name="s5_discretise",
    )(a_re.reshape(2 * g, p), a_im.reshape(2 * g, p), jnp.broadcast_to(log_dt.reshape(2 * g, 1), (2 * g, p)))
    cr, ci = cr.reshape(2, g, p, 1), ci.reshape(2, g, p, 1)
    bb_re = cr * b_re[None] - ci * b_im[None]
    bb_im = cr * b_im[None] + ci * b_re[None]
    eye = jnp.eye(g, dtype=F32)
    bd_in = lambda m: jnp.einsum("dgph,gk->dghkp", m, eye).reshape(2, g * h, g * p).astype(BF16)
    bd_out = lambda m: jnp.einsum("ghp,gk->gpkh", m, eye).reshape(g * p, g * h).astype(BF16)
    lam = lambda m: jnp.broadcast_to(m.reshape(2, 1, g * p), (2, bsz, g * p))
    return bd_in(bb_re), bd_in(bb_im), lam(lb_re), lam(lb_im), bd_out(c_re), bd_out(c_im)


def s5_branch(z_ctx, z_lat, p):
    bsz = z_ctx.shape[0]
    bre, bim, lre, lim, cre, cim = s5_operators(p["s5_a_re"], p["s5_a_im"], p["s5_log_dt"], p["s5_b_re"],
                                                p["s5_b_im"], p["s5_c_re"], p["s5_c_im"], bsz)
    return (s5_scan(z_ctx, z_lat, bre[0], bim[0], lre[0], lim[0], cre, cim, False),
            s5_scan(z_ctx, z_lat, bre[1], bim[1], lre[1], lim[1], cre, cim, True))


def _k_merge(yhy_ref, rwy0_ref, rwy1_ref, rwb_ref, rwg_ref, s5y0_ref, s5y1_ref, s5u_ref, zg_ref, x_ref, g1_ref,
             e_ref, lng_ref, lnb_ref, s5d_ref, gluw_ref, glub_ref, wb_ref, wo_ref, o_ref):
    w = MIX_WIDTH
    ones_bd = e_ref[...]
    lane = lax.broadcasted_iota(jnp.int32, (x_ref.shape[1], LANES), 1)
    cols = []
    for q in range(RW_HEADS // 2):
        even = _load_head_sum(rwy0_ref, rwy1_ref, 2 * q)
        odd = _load_head_sum(rwy0_ref, rwy1_ref, 2 * q + 1)
        cols.append(jnp.where(lane < RW_HEAD, even, pltpu.roll(odd, RW_HEAD, 1)))
    y = jnp.concatenate(cols, axis=1) + rwb_ref[0]
    mu = _seg_sum(y, ones_bd) * (1.0 / RW_HEAD)
    yc = y - mu
    var = _seg_sum(yc * yc, ones_bd) * (1.0 / RW_HEAD)
    y_rw = (yc * lax.rsqrt(var + RW_LN_EPS) * lng_ref[...] + lnb_ref[...]) * rwg_ref[0]
    s = s5y0_ref[0] + s5y1_ref[0] + s5u_ref[0] * s5d_ref[...]
    s = 0.5 * s * (1.0 + jnp.tanh(math.sqrt(2.0 / math.pi) * (s + 0.044715 * (s * s * s))))
    lg = _dot(s.astype(BF16), gluw_ref[...]) + glub_ref[...]
    y_s5 = lg[:, 0:w] * _sigmoid(lg[:, w:2 * w])
    zg = zg_ref[0]
    d = D_MODEL
    m = (_sigmoid(zg[:, 0:d]) * _dot(yhy_ref[0].astype(BF16), wb_ref[0])
         + _sigmoid(zg[:, d:2 * d]) * _dot(y_rw.astype(BF16), wb_ref[1])
         + _sigmoid(zg[:, 2 * d:3 * d]) * _dot(y_s5.astype(BF16), wb_ref[2]))
    o_ref[0] = x_ref[0] + g1_ref[0] * _dot(m.astype(BF16), wo_ref[...])


def merge_residual(y_hy, rw, s5_y, t_off, s5_u, zg, x, g1, p, ones_bd):
    bsz, n, d = x.shape
    w = MIX_WIDTH
    tm = min(ROW_TILE, n)
    assert t_off % tm == 0
    off = t_off // tm
    tile = lambda c: pl.BlockSpec((1, tm, c), lambda i, j: (i, j, 0))
    seq_tile = pl.BlockSpec((1, tm, w), lambda i, j: (i, off + j, 0))
    chain_tile = pl.BlockSpec((tm // WKV_STEPS, RW_HEADS * WKV_STEPS, LANES), lambda i, j: (off + j, i, 0))
    full = lambda shape: pl.BlockSpec(shape, lambda i, j: (0,) * len(shape))
    rw_y0, rw_y1, rw_bonus, rw_g = rw
    return pl.pallas_call(
        _k_merge,
        grid=(bsz, n // tm),
        in_specs=[tile(w), chain_tile, chain_tile, seq_tile, seq_tile, seq_tile, seq_tile, tile(w),
                  tile(N_BRANCH * d), tile(d), pl.BlockSpec((1, 1, d), lambda i, j: (i, 0, 0)),
                  full((w, w)), full((1, w)), full((1, w)), full((1, w)), full((w, 2 * w)), full((1, 2 * w)),
                  full((N_BRANCH, w, d)), full((d, d))],
        out_specs=tile(d),
        out_shape=jax.ShapeDtypeStruct(x.shape, F32),
        compiler_params=_params("arbitrary", "arbitrary"),
        name="merge_residual",
    )(y_hy, rw_y0, rw_y1, rw_bonus, rw_g, s5_y[0], s5_y[1], s5_u, zg, x, g1, ones_bd, p["rw_ln_g"].reshape(1, w),
      p["rw_ln_b"].reshape(1, w), p["s5_d"].reshape(1, w), p["s5_glu_w"].astype(BF16),
      p["s5_glu_b"].reshape(1, 2 * w), p["w_branch"].astype(BF16), p["w_out"].astype(BF16))


def _k_router(x_ref, sh_ref, sc_ref, g_ref, rw_ref, rb_ref, h_o, idx_o, gate_o, cnt_o):
    @pl.when((pl.program_id(0) == 0) & (pl.program_id(1) == 0))
    def _():
        cnt_o[...] = jnp.zeros_like(cnt_o)

    h = _rms_modulate(x_ref[0], g_ref[...], sh_ref[0], sc_ref[0])
    h_o[0] = h.astype(BF16)
    logits = _dot_hi(h, rw_ref[...]) + rb_ref[...]
    lane = lax.broadcasted_iota(jnp.int32, logits.shape, 1)
    vals, idxs = [], []
    for _ in range(TOP_K):
        m = jnp.max(logits, axis=-1, keepdims=True)
        idx = jnp.min(jnp.where(logits == m, lane, LANES), axis=-1, keepdims=True)
        vals.append(m)
        idxs.append(idx)
        logits = jnp.where(lane == idx, -jnp.inf, logits)
    exps = [jnp.exp(v - vals[0]) for v in vals]
    inv = 1.0 / (exps[0] + exps[1] + exps[2] + exps[3])
    idx_out = jnp.zeros(logits.shape, jnp.int32)
    gate_out = jnp.zeros(logits.shape, F32)
    picked = jnp.zeros(logits.shape, F32)
    for j in range(TOP_K):
        idx_out = jnp.where(lane == j, idxs[j], idx_out)
        gate_out = jnp.where(lane == j, exps[j] * inv, gate_out)
        picked = picked + jnp.where(lane == idxs[j], 1.0, 0.0)
    idx_o[0] = idx_out
    gate_o[0] = gate_out
    cnt_o[...] = cnt_o[...] + jnp.sum(picked, axis=0, keepdims=True)


def moe_route(x, shift, scale, g, router_w, router_b):
    bsz, n, d = x.shape
    tm = min(ROW_TILE, n)
    rw = jnp.pad(router_w, ((0, 0), (0, LANES - N_EXPERTS)))
    rb = jnp.pad(router_b, (0, LANES - N_EXPERTS), constant_values=-jnp.inf).reshape(1, LANES)
    tile = lambda c: pl.BlockSpec((1, tm, c), lambda i, j: (i, j, 0))
    vec = pl.BlockSpec((1, 1, d), lambda i, j: (i, 0, 0))
    full = lambda shape: pl.BlockSpec(shape, lambda i, j: (0,) * len(shape))
    return pl.pallas_call(
        _k_router,
        grid=(bsz, n // tm),
        in_specs=[tile(d), vec, vec, full((1, d)), full((d, LANES)), full((1, LANES))],
        out_specs=[tile(d), tile(LANES), tile(LANES), full((SUBLANES, LANES))],
        out_shape=[jax.ShapeDtypeStruct((bsz, n, d), BF16), jax.ShapeDtypeStruct((bsz, n, LANES), jnp.int32),
                   jax.ShapeDtypeStruct((bsz, n, LANES), F32), jax.ShapeDtypeStruct((SUBLANES, LANES), F32)],
        compiler_params=_params("arbitrary", "arbitrary"),
        name="moe_route",
    )(x, shift, scale, g.reshape(1, d), rw, rb)


def _k_expert(blk_e_ref, xs_ref, w1_ref, b1_ref, w2_ref, b2_ref, sw_ref, o_ref, w1b_ref, w2b_ref):
    i = pl.program_id(0)

    @pl.when((i == 0) | (blk_e_ref[i] != blk_e_ref[jnp.maximum(i - 1, 0)]))
    def _():
        w1b_ref[...] = w1_ref[0].astype(BF16)
        w2b_ref[...] = w2_ref[0].astype(BF16)

    hid = _dot(xs_ref[...], w1b_ref[...]) + b1_ref[0]
    gl = jnp.minimum(hid[:, 0:D_EXPERT], SWIGLU_LIMIT)
    up = jnp.clip(hid[:, D_EXPERT:2 * D_EXPERT], -SWIGLU_LIMIT, SWIGLU_LIMIT)
    act = (up + 1.0) * gl * _sigmoid(SWIGLU_ALPHA * gl)
    o_ref[...] = ((_dot(act.astype(BF16), w2b_ref[...]) + b2_ref[0]) * sw_ref[...]).astype(o_ref.dtype)


def expert_ffn(xs, blk_e, slot_w, w1, b1, w2, b2):
    n_slots, d = xs.shape
    n_blocks = n_slots // MOE_ROWS
    grid_spec = pltpu.PrefetchScalarGridSpec(
        num_scalar_prefetch=1,
        grid=(n_blocks,),
        in_specs=[pl.BlockSpec((MOE_ROWS, d), lambda i, e: (i, 0)),
                  pl.BlockSpec((1, d, 2 * D_EXPERT), lambda i, e: (e[i], 0, 0)),
                  pl.BlockSpec((1, 1, 2 * D_EXPERT), lambda i, e: (e[i], 0, 0)),
                  pl.BlockSpec((1, D_EXPERT, d), lambda i, e: (e[i], 0, 0)),
                  pl.BlockSpec((1, 1, d), lambda i, e: (e[i], 0, 0)),
                  pl.BlockSpec((MOE_ROWS, 1), lambda i, e: (i, 0))],
        out_specs=pl.BlockSpec((MOE_ROWS, d), lambda i, e: (i, 0)),
        scratch_shapes=[pltpu.VMEM((d, 2 * D_EXPERT), BF16), pltpu.VMEM((D_EXPERT, d), BF16)],
    )
    return pl.pallas_call(
        _k_expert,
        grid_spec=grid_spec,
        out_shape=jax.ShapeDtypeStruct((n_slots, d), BF16),
        compiler_params=_params("arbitrary"),
        name="expert_ffn",
    )(blk_e, xs, w1, b1.reshape(b1.shape[0], 1, -1), w2, b2.reshape(b2.shape[0], 1, -1), slot_w.reshape(-1, 1))


def _k_offset_add(tab_ref, e_ref, v_ref, o_ref):
    e = e_ref[...]
    acc = v_ref[...]
    for j in range(N_EXPERTS):
        acc = acc + jnp.where(e == j, tab_ref[j], 0)
    o_ref[...] = acc


def expert_offset_add(flat_e, values, table):
    n = flat_e.shape[0]
    rows = n // LANES
    return pl.pallas_call(
        _k_offset_add,
        in_specs=[pl.BlockSpec(memory_space=pltpu.SMEM), pl.BlockSpec(memory_space=pltpu.VMEM),
                  pl.BlockSpec(memory_space=pltpu.VMEM)],
        out_specs=pl.BlockSpec(memory_space=pltpu.VMEM),
        out_shape=jax.ShapeDtypeStruct((rows, LANES), jnp.int32),
        name="expert_offset_add",
    )(table.astype(jnp.int32), flat_e.reshape(rows, LANES), values.reshape(rows, LANES)).reshape(n)


def moe_ffn(x, shift, scale, g, router_w, router_b, w1, b1, w2, b2, expert_base=0):
    bsz, n, d = x.shape
    h, idx, gate, cnt = moe_route(x, shift, scale, g, router_w, router_b)
    n_tok = bsz * n
    n_asg = n_tok * TOP_K
    flat_e = idx[..., :TOP_K].reshape(-1)
    flat_w = gate[..., :TOP_K].reshape(-1)
    order = jnp.argsort(flat_e).astype(jnp.int32)
    rank_of = jnp.argsort(order).astype(jnp.int32)
    counts = cnt[0, :N_EXPERTS].astype(jnp.int32)
    grp_start = jnp.cumsum(counts) - counts
    padded = (counts + MOE_ROWS - 1) // MOE_ROWS * MOE_ROWS
    pad_end = jnp.cumsum(padded)
    pad_start = pad_end - padded
    n_blocks = -(-n_asg // MOE_ROWS) + N_EXPERTS
    n_slots = n_blocks * MOE_ROWS
    blk_first = jnp.arange(n_blocks, dtype=jnp.int32) * MOE_ROWS
    blk_e = jnp.minimum(jnp.sum(pad_end[None, :] <= blk_first[:, None], axis=1), N_EXPERTS - 1).astype(jnp.int32)
    slot_rank = (blk_first - pad_start[blk_e])[:, None] + jnp.arange(MOE_ROWS, dtype=jnp.int32)[None, :]
    slot_used = (slot_rank < counts[blk_e][:, None]).reshape(-1)
    slot_src = jnp.where(slot_used, (grp_start[blk_e][:, None] + slot_rank).reshape(-1), 0)
    slot_asg = order[slot_src]
    slot_tok = jnp.where(slot_used, slot_asg // TOP_K, 0)
    slot_w = jnp.where(slot_used, flat_w[slot_asg], 0.0)
    slot_of = expert_offset_add(flat_e, rank_of, pad_start - grp_start)
    xs = h.reshape(n_tok, d)[slot_tok]
    ys = expert_ffn(xs, blk_e + expert_base, slot_w, w1, b1, w2, b2)
    slot_of = slot_of.reshape(n_tok, TOP_K)
    out = sum(ys[slot_of[:, j]].astype(F32) for j in range(TOP_K))
    return out.reshape(bsz, n, d)


def _k_rmsnorm(x_ref, g_ref, o_ref):
    x = x_ref[0]
    ms = jnp.mean(x * x, axis=-1, keepdims=True)
    o_ref[0] = x * lax.rsqrt(ms + NORM_EPS) * g_ref[...]


def final_rmsnorm(x, g):
    b, n, d = x.shape
    tm = min(2 * ROW_TILE, n)
    return pl.pallas_call(
        _k_rmsnorm,
        grid=(b, n // tm),
        in_specs=[pl.BlockSpec((1, tm, d), lambda i, j: (i, j, 0)), pl.BlockSpec((1, d), lambda i, j: (0, 0))],
        out_specs=pl.BlockSpec((1, tm, d), lambda i, j: (i, j, 0)),
        out_shape=jax.ShapeDtypeStruct(x.shape, F32),
        compiler_params=_params("arbitrary", "arbitrary"),
        name="final_rmsnorm",
    )(x, g.reshape(1, d))


@functools.lru_cache(maxsize=None)
def _grid_pos_embed(n_tokens):
    rows = n_tokens // GRID_W
    row_id, col_id = np.meshgrid(np.arange(rows), np.arange(GRID_W), indexing="ij")
    quarter = D_MODEL // 4
    omega = (1.0 / (10000.0 ** (np.arange(quarter, dtype=np.float32) / quarter))).astype(np.float32)

    def enc(pos):
        ang = pos.reshape(-1)[:, None].astype(np.float32) * omega
        return np.concatenate([np.sin(ang), np.cos(ang)], axis=-1)

    return np.concatenate([enc(row_id), enc(col_id)], axis=-1).astype(np.float32)


_LAYER_KEYS = ("ada_w", "ada_b", "norm1_g", "norm2_g", "w_in", "hy_conv_w", "hy_conv_b", "hy_w1", "hy_b1",
               "hy_f1", "hy_w2", "hy_b2", "hy_f2", "hy_w3", "hy_b3", "hy_bias", "rw_conv_w", "rw_conv_b",
               "rw_w_up", "rw_w0", "rw_a_up", "rw_a0", "rw_g_up", "rw_k_k", "rw_k_a", "rw_r_k", "rw_ln_g",
               "rw_ln_b", "s5_a_re", "s5_a_im", "s5_log_dt", "s5_b_re", "s5_b_im", "s5_c_re", "s5_c_im",
               "s5_d", "s5_glu_w", "s5_glu_b", "w_branch", "w_out", "router_w", "router_b", "moe_w1",
               "moe_b1", "moe_w2", "moe_b2")


def _token_mixer(x, xc, mod, mod_c, p, need_ctx, ones_bd):
    sh1, sc1, g1 = mod
    csh1, csc1, cg1 = mod_c
    w_in = p["w_in"].astype(BF16)
    c0, c1, c2 = HY_COLS, HY_COLS + RW_COLS, HY_COLS + RW_COLS + MIX_WIDTH
    proj = lambda t, s, c, lo, hi: norm_mod_matmul(t, s, c, p["norm1_g"], w_in[:, lo:hi])
    hy_args = (p["hy_w1"], p["hy_b1"], p["hy_f1"], p["hy_w2"], p["hy_b2"], p["hy_f2"], p["hy_w3"], p["hy_b3"],
               p["hy_bias"])

    z_hy = proj(x, sh1, sc1, 0, c0)
    y_hy = hyena_branch(z_hy, p["hy_conv_w"], p["hy_conv_b"], hyena_spectra(x.shape[1], *hy_args))
    u_rw = short_conv(proj(x, sh1, sc1, c0, c1), p["rw_conv_w"], p["rw_conv_b"])
    uc_rw = short_conv(proj(xc, csh1, csc1, c0, c1), p["rw_conv_w"], p["rw_conv_b"])
    rw = rwkv_scan_branch(uc_rw, u_rw, p, ones_bd)
    n_ctx = xc.shape[1]
    z_s5 = proj(x, sh1, sc1, c1, c2)
    zc_s5 = proj(xc, csh1, csc1, c1, c2)
    s5_y = s5_branch(zc_s5, z_s5, p)
    zg = proj(x, sh1, sc1, c2, c2 + N_BRANCH * D_MODEL)
    x_new = merge_residual(y_hy, rw, s5_y, n_ctx, z_s5, zg, x, g1, p, ones_bd)
    if not need_ctx:
        return x_new, None
    zc_hy = proj(xc, csh1, csc1, 0, c0)
    yc_hy = hyena_branch(zc_hy, p["hy_conv_w"], p["hy_conv_b"], hyena_spectra(xc.shape[1], *hy_args))
    zcg = proj(xc, csh1, csc1, c2, c2 + N_BRANCH * D_MODEL)
    xc_new = merge_residual(yc_hy, rw, s5_y, 0, zc_s5, zcg, xc, cg1, p, ones_bd)
    return x_new, xc_new


def kernel(x, c, ctx, c_ctx, ada_w, ada_b, norm1_g, norm2_g, w_in, hy_conv_w, hy_conv_b, hy_w1, hy_b1, hy_f1, hy_w2, hy_b2, hy_f2, hy_w3, hy_b3, hy_bias, rw_conv_w, rw_conv_b, rw_w_up, rw_w0, rw_a_up, rw_a0, rw_g_up, rw_k_k, rw_k_a, rw_r_k, rw_ln_g, rw_ln_b, s5_a_re, s5_a_im, s5_log_dt, s5_b_re, s5_b_im, s5_c_re, s5_c_im, s5_d, s5_glu_w, s5_glu_b, w_branch, w_out, router_w, router_b, moe_w1, moe_b1, moe_w2, moe_b2, final_g):
    stacked = dict(zip(_LAYER_KEYS, (ada_w, ada_b, norm1_g, norm2_g, w_in, hy_conv_w, hy_conv_b, hy_w1, hy_b1,
                                     hy_f1, hy_w2, hy_b2, hy_f2, hy_w3, hy_b3, hy_bias, rw_conv_w, rw_conv_b,
                                     rw_w_up, rw_w0, rw_a_up, rw_a0, rw_g_up, rw_k_k, rw_k_a, rw_r_k, rw_ln_g,
                                     rw_ln_b, s5_a_re, s5_a_im, s5_log_dt, s5_b_re, s5_b_im, s5_c_re, s5_c_im,
                                     s5_d, s5_glu_w, s5_glu_b, w_branch, w_out, router_w, router_b, moe_w1,
                                     moe_b1, moe_w2, moe_b2)))
    bsz, n, d = x.shape
    depth = ada_w.shape[0]
    x = add_pos(x, jnp.asarray(_grid_pos_embed(n)))
    xc = ctx
    ones_bd = jnp.asarray(np.kron(np.eye(RW_HEADS), np.ones((RW_HEAD, RW_HEAD))), BF16)
    cvec = jnp.zeros((2 * SUBLANES, d), F32).at[:bsz].set(c).at[bsz].set(c_ctx)
    for l in range(depth):
        p = {k: v[l] for k, v in stacked.items()}
        need_ctx = l < depth - 1
        ada = ada_proj(cvec, p["ada_w"], p["ada_b"])
        lat = [t.reshape(bsz, 1, d) for t in jnp.split(ada[:bsz], 6, axis=-1)]
        cx = [jnp.broadcast_to(t.reshape(1, 1, d), (bsz, 1, d)) for t in jnp.split(ada[bsz], 6, axis=-1)]
        x, xc_new = _token_mixer(x, xc, lat[0:3], cx[0:3], p, need_ctx, ones_bd)
        stack = lambda t: t.reshape((depth * N_EXPERTS,) + t.shape[2:])
        moe = functools.partial(moe_ffn, g=p["norm2_g"], router_w=p["router_w"], router_b=p["router_b"],
                                w1=stack(moe_w1), b1=stack(moe_b1), w2=stack(moe_w2), b2=stack(moe_b2),
                                expert_base=l * N_EXPERTS)
        x = x + lat[5] * moe(x, lat[3], lat[4])
        if need_ctx:
            xc = xc_new + cx[5] * moe(xc_new, cx[3], cx[4])
    return final_rmsnorm(x, final_g)
```

```python
import functools
import math

import numpy as np
import jax
import jax.numpy as jnp
from jax import lax
from jax.experimental import pallas as pl
from jax.experimental.pallas import tpu as pltpu

F32 = jnp.float32
BF16 = jnp.bfloat16

D_MODEL = 1024
GRID_W = 64
SHORT_CONV = 3
NORM_EPS = 1e-6
N_BRANCH = 3
MIX_WIDTH = 512

HY_ORDER = 2
HY_BANDS = 16
HY_DECAY_TARGET = 1e-2
HY_FAST_PCT = 0.3
HY_SLOW_PCT = 1.5
HY_COLS = (HY_ORDER + 1) * MIX_WIDTH

RW_HEAD = 64
RW_HEADS = MIX_WIDTH // RW_HEAD
RW_DECAY_RANK = 64
RW_ICLR_RANK = 64
RW_GATE_RANK = 128
RW_LN_EPS = 64e-5
RW_COLS = 3 * MIX_WIDTH + RW_DECAY_RANK + RW_ICLR_RANK + RW_GATE_RANK

S5_GROUP = 16
S5_GROUPS = MIX_WIDTH // S5_GROUP
S5_STATE = 64
S5_HID = S5_GROUPS * S5_STATE

N_EXPERTS = 32
TOP_K = 4
D_EXPERT = 1024
SWIGLU_LIMIT = 7.0
SWIGLU_ALPHA = 1.702

LANES = 128
SUBLANES = 8
VMEM_LIMIT_BYTES = 56 * 1024 * 1024

ROW_TILE = 256
MOE_ROWS = 512
WKV_STEPS = 16
WKV_K_UNROLL = 16
S5_STEPS = 64
S5_LANE_CHUNK = 512
S5_DIAG_PARTS = 2


def _params(*sem):
    return pltpu.CompilerParams(dimension_semantics=sem, vmem_limit_bytes=VMEM_LIMIT_BYTES)


def _dot(a, b):
    return jnp.dot(a, b, preferred_element_type=F32)


def _split3(x):
    hi = x.astype(BF16)
    r1 = x - hi.astype(F32)
    mid = r1.astype(BF16)
    lo = (r1 - mid.astype(F32)).astype(BF16)
    return hi, mid, lo


def _dot_hi(a, b):
    a0, a1, a2 = _split3(a)
    b0, b1, b2 = _split3(b)
    return (_dot(a0, b0) + (_dot(a0, b1) + _dot(a1, b0))
            + (_dot(a1, b1) + _dot(a0, b2) + _dot(a2, b0)))


def _seg_sum(x, ones_bd):
    hi, mid, lo = _split3(x)
    return _dot(hi, ones_bd) + _dot(mid, ones_bd) + _dot(lo, ones_bd)


def _sigmoid(x):
    return 1.0 / (1.0 + jnp.exp(-x))


def _rms_modulate(x, g, shift, scale):
    ms = jnp.mean(x * x, axis=-1, keepdims=True)
    y = x * lax.rsqrt(ms + NORM_EPS) * g
    return y * (1.0 + scale) + shift


def _k_ada(c_ref, w_ref, b_ref, o_ref):
    c = c_ref[...]
    s = c * _sigmoid(c)
    o_ref[...] = _dot_hi(s, w_ref[...]) + b_ref[...]


def ada_proj(cvec, w, b):
    rows, d = cvec.shape
    n = w.shape[1]
    tn = 1536
    return pl.pallas_call(
        _k_ada,
        grid=(n // tn,),
        in_specs=[pl.BlockSpec((rows, d), lambda j: (0, 0)),
                  pl.BlockSpec((d, tn), lambda j: (0, j)),
                  pl.BlockSpec((1, tn), lambda j: (0, j))],
        out_specs=pl.BlockSpec((rows, tn), lambda j: (0, j)),
        out_shape=jax.ShapeDtypeStruct((rows, n), F32),
        compiler_params=_params("arbitrary"),
        name="ada_proj",
    )(cvec, w, b.reshape(1, n))


def _k_add_pos(x_ref, p_ref, o_ref):
    o_ref[0] = x_ref[0] + p_ref[...]


def add_pos(x, pos):
    b, n, d = x.shape
    tm = min(ROW_TILE * 2, n)
    return pl.pallas_call(
        _k_add_pos,
        grid=(b, n // tm),
        in_specs=[pl.BlockSpec((1, tm, d), lambda i, j: (i, j, 0)),
                  pl.BlockSpec((tm, d), lambda i, j: (j, 0))],
        out_specs=pl.BlockSpec((1, tm, d), lambda i, j: (i, j, 0)),
        out_shape=jax.ShapeDtypeStruct(x.shape, F32),
        compiler_params=_params("arbitrary", "arbitrary"),
        name="add_pos",
    )(x, pos)


def _k_norm_mm(x_ref, sh_ref, sc_ref, g_ref, w_ref, o_ref, h_ref):
    @pl.when(pl.program_id(2) == 0)
    def _():
        h = _rms_modulate(x_ref[0], g_ref[...], sh_ref[0], sc_ref[0])
        h_ref[...] = h.astype(BF16)

    o_ref[0] = _dot(h_ref[...], w_ref[...])


def norm_mod_matmul(x, shift, scale, g, w):
    b, n, d = x.shape
    ncol = w.shape[1]
    tm = min(4 * ROW_TILE, n)
    tn = ncol if ncol <= 1792 else 1536
    return pl.pallas_call(
        _k_norm_mm,
        grid=(b, n // tm, ncol // tn),
        in_specs=[pl.BlockSpec((1, tm, d), lambda i, j, k: (i, j, 0)),
                  pl.BlockSpec((1, 1, d), lambda i, j, k: (i, 0, 0)),
                  pl.BlockSpec((1, 1, d), lambda i, j, k: (i, 0, 0)),
                  pl.BlockSpec((1, d), lambda i, j, k: (0, 0)),
                  pl.BlockSpec((d, tn), lambda i, j, k: (0, k))],
        out_specs=pl.BlockSpec((1, tm, tn), lambda i, j, k: (i, j, k)),
        out_shape=jax.ShapeDtypeStruct((b, n, ncol), F32),
        scratch_shapes=[pltpu.VMEM((tm, d), BF16)],
        compiler_params=_params("arbitrary", "arbitrary", "arbitrary"),
        name="norm_mod_matmul",
    )(x, shift, scale, g.reshape(1, d), w)


def _short_conv_val(z, w, b):
    n = z.shape[0]
    row = lax.broadcasted_iota(jnp.int32, z.shape, 0)
    zm = jnp.where(row == 0, 0.0, pltpu.roll(z, 1, 0))
    zp = jnp.where(row == n - 1, 0.0, pltpu.roll(z, n - 1, 0))
    return zm * w[0:1] + z * w[1:2] + zp * w[2:3] + b


def _k_sconv(z_ref, w_ref, b_ref, o_ref):
    o_ref[0] = _short_conv_val(z_ref[0], w_ref[...], b_ref[...])


def short_conv(z, w, b):
    bsz, n, c = z.shape
    tc = 256 if c % 256 == 0 else LANES
    return pl.pallas_call(
        _k_sconv,
        grid=(bsz, c // tc),
        in_specs=[pl.BlockSpec((1, n, tc), lambda i, j: (i, 0, j)),
                  pl.BlockSpec((SHORT_CONV, tc), lambda i, j: (0, j)),
                  pl.BlockSpec((1, tc), lambda i, j: (0, j))],
        out_specs=pl.BlockSpec((1, n, tc), lambda i, j: (i, 0, j)),
        out_shape=jax.ShapeDtypeStruct(z.shape, F32),
        compiler_params=_params("arbitrary", "arbitrary"),
        name="short_conv",
    )(z, w, b.reshape(1, c))


def _fft_split(n):
    total = 2 * n
    bits = total.bit_length() - 1
    assert 1 << bits == total
    n1 = 1 << ((bits + 1) // 2)
    n1 = max(n1, 2 * SUBLANES)
    return n1, total // n1


@functools.lru_cache(maxsize=None)
def _dft_mats(n):
    n1, n2 = _fft_split(n)
    total = n1 * n2
    k1 = np.arange(n1)[:, None]
    m1 = np.arange(n1)[None, :]
    m2 = np.arange(n2)[:, None, None]
    ang = -2.0 * np.pi * ((n2 * k1 * m1)[None] + m2 * k1[None]) / total
    fa = np.concatenate([np.cos(ang), np.sin(ang)], axis=1)
    k2 = np.arange(n2)[:, None]
    mm = np.arange(n2)[None, :]
    gang = -2.0 * np.pi * k2 * mm / n2
    gr, gi = np.cos(gang), np.sin(gang)
    fb = np.block([[gr, -gi], [gi, gr]])
    fc = np.block([[gr, gi], [-gi, gr]])
    m1d = np.arange(n1 // 2)[:, None]
    k1d = np.arange(n1)[None, :]
    dang = 2.0 * np.pi * ((n2 * m1d * k1d)[None] + m2 * k1d[None]) / total
    fd = np.concatenate([np.cos(dang), -np.sin(dang)], axis=2) / total
    return (fa.astype(np.float32), fb.astype(np.float32), fc.astype(np.float32), fd.astype(np.float32))


def _time_pitch(n2):
    return n2 + SUBLANES


def _spec_pitch(n2):
    return 2 * n2 + SUBLANES


def _fft_stage_a(src_ref, y_ref, fa_ref, n1, n2, k_rows):
    tp, sp = _time_pitch(n2), _spec_pitch(n2)

    def body(m2, c):
        xs = src_ref[pl.ds(m2, k_rows, stride=tp), :].astype(BF16)
        res = _dot(fa_ref[m2, :, 0:k_rows], xs)
        y_ref[pl.ds(m2, n1, stride=sp), :] = res[0:n1]
        y_ref[pl.ds(n2 + m2, n1, stride=sp), :] = res[n1:2 * n1]
        return c
    lax.fori_loop(0, n2, body, 0, unroll=8)


def _long_conv(buf_ref, y_ref, h_ref, fa_ref, fb_ref, fc_ref, fd_ref, n1, n2):
    tp, sp = _time_pitch(n2), _spec_pitch(n2)
    _fft_stage_a(buf_ref, y_ref, fa_ref, n1, n2, n1 // 2)

    def body_k1(k1, c):
        rows = pl.ds(pl.multiple_of(k1 * sp, SUBLANES), 2 * n2)
        z = _dot(fb_ref[...], y_ref[rows, :].astype(BF16))
        zr, zi = z[0:n2], z[n2:2 * n2]
        h = h_ref[k1]
        hr, hi = h[0:n2], h[n2:2 * n2]
        p = jnp.concatenate([zr * hr - zi * hi, zr * hi + zi * hr], axis=0)
        y_ref[rows, :] = _dot(fc_ref[...], p.astype(BF16))
        return c
    lax.fori_loop(0, n1, body_k1, 0, unroll=16)

    def body_m2(m2, c):
        qr = y_ref[pl.ds(m2, n1, stride=sp), :]
        qi = y_ref[pl.ds(n2 + m2, n1, stride=sp), :]
        q = jnp.concatenate([qr, qi], axis=0).astype(BF16)
        buf_ref[pl.ds(m2, n1 // 2, stride=tp), :] = _dot(fd_ref[m2], q)
        return c
    lax.fori_loop(0, n2, body_m2, 0, unroll=8)


def _store_time_blocks(buf_ref, val, n2):
    tp = _time_pitch(n2)
    for m1 in range(val.shape[0] // n2):
        buf_ref[m1 * tp:m1 * tp + n2, :] = val[m1 * n2:(m1 + 1) * n2]


def _load_time_blocks(buf_ref, n, n2):
    tp = _time_pitch(n2)
    return jnp.concatenate([buf_ref[m1 * tp:m1 * tp + n2, :] for m1 in range(n // n2)], axis=0)


def _k_hyena(x1_ref, x2_ref, v_ref, w1_ref, w2_ref, wv_ref, b1_ref, b2_ref, bv_ref, h1_ref, h2_ref,
             fa_ref, fb_ref, fc_ref, fd_ref, o_ref, buf_ref, y_ref, *, n1, n2):
    n = x1_ref.shape[1]
    conv = functools.partial(_long_conv, buf_ref, y_ref, fa_ref=fa_ref, fb_ref=fb_ref, fc_ref=fc_ref,
                             fd_ref=fd_ref, n1=n1, n2=n2)
    _store_time_blocks(buf_ref, _short_conv_val(v_ref[0], wv_ref[...], bv_ref[...]), n2)
    conv(h1_ref)
    gated = _short_conv_val(x1_ref[0], w1_ref[...], b1_ref[...]) * _load_time_blocks(buf_ref, n, n2)
    _store_time_blocks(buf_ref, gated, n2)
    conv(h2_ref)
    o_ref[0] = _short_conv_val(x2_ref[0], w2_ref[...], b2_ref[...]) * _load_time_blocks(buf_ref, n, n2)


def _k_fspec(f_ref, fa_ref, fb_ref, o_ref, y_ref, *, n1, n2):
    sp = _spec_pitch(n2)
    _fft_stage_a(f_ref, y_ref, fa_ref, n1, n2, n1)

    def body_k1(k1, c):
        src = pl.ds(pl.multiple_of(k1 * sp, SUBLANES), 2 * n2)
        dst = pl.ds(pl.multiple_of(k1 * 2 * n2, 2 * n2), 2 * n2)
        o_ref[dst, :] = _dot(fb_ref[...], y_ref[src, :].astype(BF16))
        return c
    lax.fori_loop(0, n1, body_k1, 0, unroll=2)


def filter_spectrum(filt, n):
    n1, n2 = _fft_split(n)
    total, c = filt.shape
    tp, sp = _time_pitch(n2), _spec_pitch(n2)
    fa, fb, _, _ = _dft_mats(n)
    blocks = jnp.pad(filt.reshape(n1, n2, c), ((0, 0), (0, tp - n2), (0, 0))).reshape(n1 * tp, c)
    out = pl.pallas_call(
        functools.partial(_k_fspec, n1=n1, n2=n2),
        grid=(c // LANES,),
        in_specs=[pl.BlockSpec((n1 * tp, LANES), lambda j: (0, j)),
                  pl.BlockSpec((n2, 2 * n1, n1), lambda j: (0, 0, 0)),
                  pl.BlockSpec((2 * n2, 2 * n2), lambda j: (0, 0))],
        out_specs=pl.BlockSpec((total * 2, LANES), lambda j: (0, j)),
        out_shape=jax.ShapeDtypeStruct((2 * total, c), F32),
        scratch_shapes=[pltpu.VMEM((n1 * sp, LANES), F32)],
        compiler_params=_params("arbitrary"),
        name="filter_spectrum",
    )(blocks, jnp.asarray(fa, BF16), jnp.asarray(fb, BF16))
    return out.reshape(n1, 2 * n2, c)


def _k_filter_mlp(feat_ref, win_ref, w1_ref, b1_ref, f1_ref, w2_ref, b2_ref, f2_ref, w3_ref, b3_ref, o_ref):
    h = jnp.sin(f1_ref[...] * (_dot_hi(feat_ref[...], w1_ref[...]) + b1_ref[...]))
    h = jnp.sin(f2_ref[...] * (_dot_hi(h, w2_ref[...]) + b2_ref[...]))
    o_ref[...] = (_dot_hi(h, w3_ref[...]) + b3_ref[...]) * win_ref[...]


@functools.lru_cache(maxsize=None)
def _filter_consts(n):
    t = np.linspace(0.0, 1.0, n, dtype=np.float32)[:, None]
    bands = np.linspace(1e-4, HY_BANDS - 1, HY_BANDS, dtype=np.float32)
    ang = (np.float32(2 * math.pi) * np.arange(n, dtype=np.float32) / np.float32(n))[:, None] * bands
    feats = np.concatenate([t, np.cos(ang), -np.sin(ang)], axis=-1).astype(np.float32)
    pad = (-feats.shape[1]) % SUBLANES
    feats = np.pad(feats, ((0, 0), (0, pad)))
    deltas = np.abs(np.linspace(math.log(HY_DECAY_TARGET) / HY_SLOW_PCT,
                                math.log(HY_DECAY_TARGET) / HY_FAST_PCT, MIX_WIDTH, dtype=np.float32))
    window = np.exp(-t * deltas).astype(np.float32)
    return feats, np.tile(window, (1, 2 * HY_ORDER))


def hyena_filter_table(n, w1, b1, f1, w2, b2, f2, w3, b3):
    feats, window = _filter_consts(n)
    kf = feats.shape[1]
    fd = w1.shape[1]
    ncol = w3.shape[1]
    w1p = jnp.pad(w1, ((0, kf - w1.shape[0]), (0, 0)))
    tm = min(ROW_TILE, n)
    full = lambda shape: pl.BlockSpec(shape, lambda i: (0,) * len(shape))
    return pl.pallas_call(
        _k_filter_mlp,
        grid=(n // tm,),
        in_specs=[pl.BlockSpec((tm, kf), lambda i: (i, 0)),
                  pl.BlockSpec((tm, ncol), lambda i: (i, 0)),
                  full((kf, fd)), full((1, fd)), full((1, fd)),
                  full((fd, fd)), full((1, fd)), full((1, fd)),
                  full((fd, ncol)), full((1, ncol))],
        out_specs=pl.BlockSpec((tm, ncol), lambda i: (i, 0)),
        out_shape=jax.ShapeDtypeStruct((n, ncol), F32),
        compiler_params=_params("arbitrary"),
        name="hyena_filter_mlp",
    )(jnp.asarray(feats), jnp.asarray(window), w1p, b1.reshape(1, fd), f1.reshape(1, fd),
      w2, b2.reshape(1, fd), f2.reshape(1, fd), w3, b3.reshape(1, ncol))


def hyena_spectra(n, w1, b1, f1, w2, b2, f2, w3, b3, bias):
    hf = hyena_filter_table(n, w1, b1, f1, w2, b2, f2, w3, b3).reshape(n, HY_ORDER, 2, MIX_WIDTH)
    fwd = hf[:, :, 0]
    bwd = hf[1:, :, 1][::-1]
    fwd = fwd.at[0].add(bias)
    filt = jnp.concatenate([fwd, jnp.zeros((1, HY_ORDER, MIX_WIDTH), F32), bwd], axis=0)
    spec = filter_spectrum(filt.reshape(2 * n, HY_ORDER * MIX_WIDTH), n)
    n1, n2 = _fft_split(n)
    return spec.reshape(n1, 2 * n2, HY_ORDER, MIX_WIDTH).transpose(2, 0, 1, 3)


def hyena_branch(z, conv_w, conv_b, spectra):
    bsz, n, _ = z.shape
    n1, n2 = _fft_split(n)
    fa, fb, fc, fd = (jnp.asarray(m, BF16) for m in _dft_mats(n))
    fa = fa[:, :, : n1 // 2]
    nt = MIX_WIDTH // LANES
    cb = conv_b.reshape(1, HY_COLS)
    zspec = lambda off: pl.BlockSpec((1, n, LANES), lambda j, i: (i, 0, off + j))
    wspec = lambda off: pl.BlockSpec((SHORT_CONV, LANES), lambda j, i: (0, off + j))
    bspec = lambda off: pl.BlockSpec((1, LANES), lambda j, i: (0, off + j))
    once = pl.Buffered(1)
    hspec = lambda o: pl.BlockSpec((None, n1, 2 * n2, LANES), lambda j, i: (o, 0, 0, j), pipeline_mode=once)
    full = lambda shape: pl.BlockSpec(shape, lambda j, i: (0,) * len(shape), pipeline_mode=once)
    return pl.pallas_call(
        functools.partial(_k_hyena, n1=n1, n2=n2),
        grid=(nt, bsz),
        in_specs=[zspec(0), zspec(nt), zspec(2 * nt), wspec(0), wspec(nt), wspec(2 * nt),
                  bspec(0), bspec(nt), bspec(2 * nt), hspec(0), hspec(1),
                  full(fa.shape), full(fb.shape), full(fc.shape), full(fd.shape)],
        out_specs=pl.BlockSpec((1, n, LANES), lambda j, i: (i, 0, j)),
        out_shape=jax.ShapeDtypeStruct((bsz, n, MIX_WIDTH), F32),
        scratch_shapes=[pltpu.VMEM((n1 // 2 * _time_pitch(n2), LANES), F32),
                        pltpu.VMEM((n1 * _spec_pitch(n2), LANES), F32)],
        compiler_params=_params("arbitrary", "arbitrary"),
        name="hyena_long_conv",
    )(z, z, z, conv_w, conv_w, conv_w, cb, cb, cb, spectra, spectra, fa, fb, fc, fd)


def _store_head_pairs(o_ref, left, right):
    rows = left.shape[0]
    lane = lax.broadcasted_iota(jnp.int32, (rows, LANES), 1)
    low = lane < RW_HEAD
    for q in range(RW_HEADS // 2):
        lcol = left[:, q * LANES:(q + 1) * LANES]
        rcol = right[:, q * LANES:(q + 1) * LANES]
        heads = (jnp.where(low, lcol, pltpu.roll(rcol, RW_HEAD, 1)),
                 jnp.where(low, pltpu.roll(lcol, RW_HEAD, 1), rcol))
        for h, piece in zip((2 * q, 2 * q + 1), heads):
            for c in range(rows // WKV_STEPS):
                o_ref[c, h * WKV_STEPS:(h + 1) * WKV_STEPS, :] = piece[c * WKV_STEPS:(c + 1) * WKV_STEPS]


def _load_head_sum(y0_ref, y1_ref, h):
    sl = slice(h * WKV_STEPS, (h + 1) * WKV_STEPS)
    return jnp.concatenate([y0_ref[c, sl, :] + y1_ref[c, sl, :] for c in range(y0_ref.shape[0])], axis=0)


def _k_rwprep(uc_ref, ul_ref, gup_ref, wup_ref, w0_ref, aup_ref, a0_ref, kk_ref, ka_ref, rk_ref, e_ref,
              g_o, bonus_o, rv_o, nb0_o, nb1_o, wk0_o, wk1_o, *, ctx_tiles):
    w = MIX_WIDTH
    u = jnp.where(pl.program_id(1) < ctx_tiles, uc_ref[0], ul_ref[0])
    r, k, v = u[:, 0:w], u[:, w:2 * w], u[:, 2 * w:3 * w]
    xw = u[:, 3 * w:3 * w + RW_DECAY_RANK]
    xa = u[:, 3 * w + RW_DECAY_RANK:3 * w + RW_DECAY_RANK + RW_ICLR_RANK]
    xg = u[:, 3 * w + RW_DECAY_RANK + RW_ICLR_RANK:]
    ones_bd = e_ref[...]
    g_o[0] = _dot(_sigmoid(xg).astype(BF16), gup_ref[...])
    kk = k * kk_ref[...]
    kk = kk * lax.rsqrt(jnp.maximum(_seg_sum(kk * kk, ones_bd), 1e-24))
    txw = jnp.tanh(xw)
    kd_sum = None
    for d, (nb_o, wk_o) in enumerate(((nb0_o, wk0_o), (nb1_o, wk1_o))):
        x = -(w0_ref[d] + _dot_hi(txw, wup_ref[d]))
        softplus = jnp.maximum(x, 0.0) + jnp.log(1.0 + jnp.exp(-jnp.abs(x)))
        decay = jnp.exp(-jnp.exp(-softplus - 0.5))
        a = _sigmoid(a0_ref[d] + _dot_hi(xa, aup_ref[d]))
        kd = k * (1.0 + (a - 1.0) * ka_ref[...])
        _store_head_pairs(nb_o, -kk, kk * a)
        _store_head_pairs(wk_o, decay, kd)
        kd_sum = kd if kd_sum is None else kd_sum + kd
    _store_head_pairs(rv_o, r, v)
    bonus_o[0] = _seg_sum(r * kd_sum * rk_ref[...], ones_bd) * v


def rwkv_prep(u_ctx, u_lat, g_up, w_up, w0, a_up, a0, k_k, k_a, r_k, ones_bd):
    bsz, n_ctx, cols = u_ctx.shape
    n_lat = u_lat.shape[1]
    t_total = n_ctx + n_lat
    w = MIX_WIDTH
    tm = min(ROW_TILE, n_ctx)
    ctx_tiles = n_ctx // tm
    full = lambda shape: pl.BlockSpec(shape, lambda i, j: (0,) * len(shape))
    tile = pl.BlockSpec((1, tm, w), lambda i, j: (i, j, 0))
    pair = pl.BlockSpec((tm // WKV_STEPS, RW_HEADS * WKV_STEPS, LANES), lambda i, j: (j, i, 0))
    pair_shape = jax.ShapeDtypeStruct((t_total // WKV_STEPS, bsz * RW_HEADS * WKV_STEPS, LANES), F32)
    return pl.pallas_call(
        functools.partial(_k_rwprep, ctx_tiles=ctx_tiles),
        grid=(bsz, t_total // tm),
        in_specs=[pl.BlockSpec((1, tm, cols), lambda i, j: (i, jnp.minimum(j, ctx_tiles - 1), 0)),
                  pl.BlockSpec((1, tm, cols), lambda i, j: (i, jnp.maximum(j - ctx_tiles, 0), 0)),
                  full((RW_GATE_RANK, w)), full((2, RW_DECAY_RANK, w)), full((2, 1, w)),
                  full((2, RW_ICLR_RANK, w)), full((2, 1, w)), full((1, w)), full((1, w)), full((1, w)),
                  full((w, w))],
        out_specs=[tile, tile] + [pair] * 5,
        out_shape=[jax.ShapeDtypeStruct((bsz, t_total, w), F32)] * 2 + [pair_shape] * 5,
        compiler_params=_params("arbitrary", "arbitrary"),
        name="rwkv_prep",
    )(u_ctx, u_lat, g_up.astype(BF16), w_up, w0.reshape(2, 1, w), a_up, a0.reshape(2, 1, w),
      k_k.reshape(1, w), k_a.reshape(1, w), r_k.reshape(1, w), ones_bd)


def _k_wkv(rv0_ref, rv1_ref, nb0_ref, nb1_ref, wk0_ref, wk1_ref, y0_ref, y1_ref, s_ref, op_ref, yb_ref, *,
           steps):
    @pl.when(pl.program_id(0) == 0)
    def _():
        s_ref[...] = jnp.zeros_like(s_ref)

    chains = rv0_ref.shape[0] // steps
    r_op, v_op, a_op, b_op, w_op, k_op = range(6)

    def rows(t):
        return pl.ds(t, chains, stride=steps), pl.ds(steps - 1 - t, chains, stride=steps)

    for t in range(steps):
        fwd, bwd = rows(t)
        for j, (ref0, ref1) in enumerate(((rv0_ref, rv1_ref), (nb0_ref, nb1_ref), (wk0_ref, wk1_ref))):
            both = jnp.concatenate([ref0[fwd, :], ref1[bwd, :]], axis=0)
            op_ref[t, 2 * j:2 * j + 2] = both.T.reshape(2, RW_HEAD, 2 * chains)

    def advance(t, sa):
        nxt = jnp.minimum(t + 1, steps - 1)
        row = lambda op, i, slot=t: op_ref[slot, op, pl.ds(i, 1), :]
        vv = op_ref[t, v_op]

        def k_rows(ib, acc):
            y, sa_next = acc
            for u in range(WKV_K_UNROLL):
                i = ib * WKV_K_UNROLL + u
                s_new = s_ref[i] * row(w_op, i) + sa * row(b_op, i) + vv * row(k_op, i)
                s_ref[i] = s_new
                y = y + s_new * row(r_op, i)
                sa_next = sa_next + s_new * row(a_op, i, nxt)
            return y, sa_next
        zero = jnp.zeros(s_ref.shape[1:], F32)
        y, sa_next = lax.fori_loop(0, RW_HEAD // WKV_K_UNROLL, k_rows, (zero, zero))
        yb_ref[t] = y
        return sa_next

    sa = jnp.zeros(s_ref.shape[1:], F32)
    for i in range(RW_HEAD):
        sa = sa + s_ref[i] * op_ref[0, a_op, pl.ds(i, 1), :]
    lax.fori_loop(0, steps, advance, sa)

    for t in range(steps):
        y = yb_ref[t]
        yt = jnp.concatenate([y, y], axis=0).T
        fwd, bwd = rows(t)
        y0_ref[fwd, :] = yt[0:chains]
        y1_ref[bwd, :] = yt[chains:2 * chains]


def wkv_scan(rv, nb0, nb1, wk0, wk1, n_ctx):
    nb, rows, _ = rv.shape
    steps = WKV_STEPS
    chains = rows // steps
    nc = n_ctx // steps
    fwd = pl.BlockSpec((None, rows, LANES), lambda i: (i, 0, 0))
    bwd = pl.BlockSpec((None, rows, LANES), lambda i: (jnp.where(i < nc, nc - 1 - i, nb - 1 + nc - i), 0, 0))
    out = jax.ShapeDtypeStruct(rv.shape, F32)
    return pl.pallas_call(
        functools.partial(_k_wkv, steps=steps),
        grid=(nb,),
        in_specs=[fwd, bwd, fwd, bwd, fwd, bwd],
        out_specs=[fwd, bwd],
        out_shape=[out, out],
        scratch_shapes=[pltpu.VMEM((RW_HEAD, RW_HEAD, 2 * chains), F32),
                        pltpu.VMEM((steps, 6, RW_HEAD, 2 * chains), F32),
                        pltpu.VMEM((steps, RW_HEAD, 2 * chains), F32)],
        compiler_params=_params("arbitrary"),
        name="wkv_scan",
    )(rv, rv, nb0, nb1, wk0, wk1)


def rwkv_scan_branch(u_ctx, u_lat, p, ones_bd):
    g, bonus, rv, nb0, nb1, wk0, wk1 = rwkv_prep(
        u_ctx, u_lat, p["rw_g_up"], p["rw_w_up"], p["rw_w0"], p["rw_a_up"], p["rw_a0"], p["rw_k_k"],
        p["rw_k_a"], p["rw_r_k"].reshape(-1), ones_bd)
    y0, y1 = wkv_scan(rv, nb0, nb1, wk0, wk1, u_ctx.shape[1])
    return y0, y1, bonus, g


def _s5_block(i, n_ctx_blocks, n_blocks, reverse):
    if not reverse:
        return i
    return jnp.where(i < n_ctx_blocks, n_ctx_blocks - 1 - i, n_blocks - 1 + n_ctx_blocks - i)


def _k_s5(uc_ref, ul_ref, bre_ref, bim_ref, lre_ref, lim_ref, cre_ref, cim_ref, y_ref, hre_ref, him_ref,
          st_ref, tb_ref, *, steps, reverse, n_ctx_blocks, n_blocks):
    @pl.when(pl.program_id(0) == 0)
    def _():
        st_ref[...] = jnp.zeros_like(st_ref)

    bsz = uc_ref.shape[0]
    nq = MIX_WIDTH // LANES
    in_ctx = _s5_block(pl.program_id(0), n_ctx_blocks, n_blocks, reverse) < n_ctx_blocks
    for b in range(bsz):
        ub = jnp.where(in_ctx, uc_ref[b], ul_ref[b])
        for q in range(nq):
            tb_ref[q, pl.ds(b, steps, stride=bsz), :] = ub[:, q * LANES:(q + 1) * LANES]
    u = jnp.concatenate([tb_ref[q] for q in range(nq)], axis=1).astype(BF16)
    wc, hc = MIX_WIDTH // S5_DIAG_PARTS, S5_HID // S5_DIAG_PARTS
    for part in range(S5_DIAG_PARTS):
        ws, hs = slice(part * wc, (part + 1) * wc), slice(part * hc, (part + 1) * hc)
        hre_ref[:, hs] = _dot(u[:, ws], bre_ref[ws, hs])
        him_ref[:, hs] = _dot(u[:, ws], bim_ref[ws, hs])
    for ch in range(S5_HID // S5_LANE_CHUNK):
        cols = slice(ch * S5_LANE_CHUNK, (ch + 1) * S5_LANE_CHUNK)
        lre = lre_ref[:, cols]
        lim = lim_ref[:, cols]

        def step(i, carry):
            hr, hi = carry
            t = (steps - 1 - i) if reverse else i
            rows = pl.ds(pl.multiple_of(t * bsz, bsz), bsz)
            nr = lre * hr - lim * hi + hre_ref[rows, cols]
            ni = lre * hi + lim * hr + him_ref[rows, cols]
            hre_ref[rows, cols] = nr
            him_ref[rows, cols] = ni
            return nr, ni
        hr, hi = lax.fori_loop(0, steps, step, (st_ref[0, :, cols], st_ref[1, :, cols]))
        st_ref[0, :, cols] = hr
        st_ref[1, :, cols] = hi
    lanes_per_part = wc // LANES
    for part in range(S5_DIAG_PARTS):
        ws, hs = slice(part * wc, (part + 1) * wc), slice(part * hc, (part + 1) * hc)
        y = (_dot(hre_ref[:, hs].astype(BF16), cre_ref[hs, ws])
             - _dot(him_ref[:, hs].astype(BF16), cim_ref[hs, ws]))
        for q in range(lanes_per_part):
            tb_ref[part * lanes_per_part + q] = y[:, q * LANES:(q + 1) * LANES]
    for b in range(bsz):
        for q in range(nq):
            y_ref[b, :, q * LANES:(q + 1) * LANES] = tb_ref[q, pl.ds(b, steps, stride=bsz), :]


def s5_scan(z_ctx, z_lat, bre, bim, lre, lim, cre, cim, reverse):
    bsz, n_ctx, w = z_ctx.shape
    t_total = n_ctx + z_lat.shape[1]
    steps = S5_STEPS
    nc = n_ctx // steps
    nb = t_total // steps
    blk = functools.partial(_s5_block, n_ctx_blocks=nc, n_blocks=nb, reverse=reverse)
    full = lambda shape: pl.BlockSpec(shape, lambda i: (0,) * len(shape))
    return pl.pallas_call(
        functools.partial(_k_s5, steps=steps, reverse=reverse, n_ctx_blocks=nc, n_blocks=nb),
        grid=(nb,),
        in_specs=[pl.BlockSpec((bsz, steps, w), lambda i: (0, jnp.minimum(blk(i), nc - 1), 0)),
                  pl.BlockSpec((bsz, steps, w), lambda i: (0, jnp.maximum(blk(i) - nc, 0), 0)),
                  full((w, S5_HID)), full((w, S5_HID)), full((bsz, S5_HID)), full((bsz, S5_HID)),
                  full((S5_HID, w)), full((S5_HID, w))],
        out_specs=pl.BlockSpec((bsz, steps, w), lambda i: (0, blk(i), 0)),
        out_shape=jax.ShapeDtypeStruct((bsz, t_total, w), F32),
        scratch_shapes=[pltpu.VMEM((steps * bsz, S5_HID), F32), pltpu.VMEM((steps * bsz, S5_HID), F32),
                        pltpu.VMEM((2, bsz, S5_HID), F32), pltpu.VMEM((w // LANES, steps * bsz, LANES), F32)],
        compiler_params=_params("arbitrary"),
        name="s5_scan",
    )(z_ctx, z_lat, bre, bim, lre, lim, cre, cim)


def _k_s5disc(are_ref, aim_ref, ldt_ref, lre_ref, lim_ref, cr_ref, ci_ref):
    a_re, a_im = are_ref[...], aim_ref[...]
    dt = jnp.exp(ldt_ref[...])
    mag = jnp.exp(a_re * dt)
    lb_re, lb_im = mag * jnp.cos(a_im * dt), mag * jnp.sin(a_im * dt)
    den = a_re * a_re + a_im * a_im
    nr = lb_re - 1.0
    lre_ref[...] = lb_re
    lim_ref[...] = lb_im
    cr_ref[...] = (nr * a_re + lb_im * a_im) / den
    ci_ref[...] = (lb_im * a_re - nr * a_im) / den


def s5_operators(a_re, a_im, log_dt, b_re, b_im, c_re, c_im, bsz):
    g, p, h = S5_GROUPS, S5_STATE, S5_GROUP
    shp = jax.ShapeDtypeStruct((2 * g, p), F32)
    lb_re, lb_im, cr, ci = pl.pallas_call(
        _k_s5disc, out_shape=[shp] * 4, name="s5_discretise",
    )(a_re.reshape(2 * g, p), a_im.reshape(2 * g, p), jnp.broadcast_to(log_dt.reshape(2 * g, 1), (2 * g, p)))
    cr, ci = cr.reshape(2, g, p, 1), ci.reshape(2, g, p, 1)
    bb_re = cr * b_re[None] - ci * b_im[None]
    bb_im = cr * b_im[None] + ci * b_re[None]
    eye = jnp.eye(g, dtype=F32)
    bd_in = lambda m: jnp.einsum("dgph,gk->dghkp", m, eye).reshape(2, g * h, g * p).astype(BF16)
    bd_out = lambda m: jnp.einsum("ghp,gk->gpkh", m, eye).reshape(g * p, g * h).astype(BF16)
    lam = lambda m: jnp.broadcast_to(m.reshape(2, 1, g * p), (2, bsz, g * p))
    return bd_in(bb_re), bd_in(bb_im), lam(lb_re), lam(lb_im), bd_out(c_re), bd_out(c_im)


def s5_branch(z_ctx, z_lat, p):
    bsz = z_ctx.shape[0]
    bre, bim, lre, lim, cre, cim = s5_operators(p["s5_a_re"], p["s5_a_im"], p["s5_log_dt"], p["s5_b_re"],
                                                p["s5_b_im"], p["s5_c_re"], p["s5_c_im"], bsz)
    return (s5_scan(z_ctx, z_lat, bre[0], bim[0], lre[0], lim[0], cre, cim, False),
            s5_scan(z_ctx, z_lat, bre[1], bim[1], lre[1], lim[1], cre, cim, True))


def _k_merge(yhy_ref, rwy0_ref, rwy1_ref, rwb_ref, rwg_ref, s5y0_ref, s5y1_ref, s5u_ref, zg_ref, x_ref, g1_ref,
             e_ref, lng_ref, lnb_ref, s5d_ref, gluw_ref, glub_ref, wb_ref, wo_ref, o_ref):
    w = MIX_WIDTH
    ones_bd = e_ref[...]
    lane = lax.broadcasted_iota(jnp.int32, (x_ref.shape[1], LANES), 1)
    cols = []
    for q in range(RW_HEADS // 2):
        even = _load_head_sum(rwy0_ref, rwy1_ref, 2 * q)
        odd = _load_head_sum(rwy0_ref, rwy1_ref, 2 * q + 1)
        cols.append(jnp.where(lane < RW_HEAD, even, pltpu.roll(odd, RW_HEAD, 1)))
    y = jnp.concatenate(cols, axis=1) + rwb_ref[0]
    mu = _seg_sum(y, ones_bd) * (1.0 / RW_HEAD)
    yc = y - mu
    var = _seg_sum(yc * yc, ones_bd) * (1.0 / RW_HEAD)
    y_rw = (yc * lax.rsqrt(var + RW_LN_EPS) * lng_ref[...] + lnb_ref[...]) * rwg_ref[0]
    s = s5y0_ref[0] + s5y1_ref[0] + s5u_ref[0] * s5d_ref[...]
    s = 0.5 * s * (1.0 + jnp.tanh(math.sqrt(2.0 / math.pi) * (s + 0.044715 * (s * s * s))))
    lg = _dot(s.astype(BF16), gluw_ref[...]) + glub_ref[...]
    y_s5 = lg[:, 0:w] * _sigmoid(lg[:, w:2 * w])
    zg = zg_ref[0]
    d = D_MODEL
    m = (_sigmoid(zg[:, 0:d]) * _dot(yhy_ref[0].astype(BF16), wb_ref[0])
         + _sigmoid(zg[:, d:2 * d]) * _dot(y_rw.astype(BF16), wb_ref[1])
         + _sigmoid(zg[:, 2 * d:3 * d]) * _dot(y_s5.astype(BF16), wb_ref[2]))
    o_ref[0] = x_ref[0] + g1_ref[0] * _dot(m.astype(BF16), wo_ref[...])


def merge_residual(y_hy, rw, s5_y, t_off, s5_u, zg, x, g1, p, ones_bd):
    bsz, n, d = x.shape
    w = MIX_WIDTH
    tm = min(ROW_TILE, n)
    assert t_off % tm == 0
    off = t_off // tm
    tile = lambda c: pl.BlockSpec((1, tm, c), lambda i, j: (i, j, 0))
    seq_tile = pl.BlockSpec((1, tm, w), lambda i, j: (i, off + j, 0))
    chain_tile = pl.BlockSpec((tm // WKV_STEPS, RW_HEADS * WKV_STEPS, LANES), lambda i, j: (off + j, i, 0))
    full = lambda shape: pl.BlockSpec(shape, lambda i, j: (0,) * len(shape))
    rw_y0, rw_y1, rw_bonus, rw_g = rw
    return pl.pallas_call(
        _k_merge,
        grid=(bsz, n // tm),
        in_specs=[tile(w), chain_tile, chain_tile, seq_tile, seq_tile, seq_tile, seq_tile, tile(w),
                  tile(N_BRANCH * d), tile(d), pl.BlockSpec((1, 1, d), lambda i, j: (i, 0, 0)),
                  full((w, w)), full((1, w)), full((1, w)), full((1, w)), full((w, 2 * w)), full((1, 2 * w)),
                  full((N_BRANCH, w, d)), full((d, d))],
        out_specs=tile(d),
        out_shape=jax.ShapeDtypeStruct(x.shape, F32),
        compiler_params=_params("arbitrary", "arbitrary"),
        name="merge_residual",
    )(y_hy, rw_y0, rw_y1, rw_bonus, rw_g, s5_y[0], s5_y[1], s5_u, zg, x, g1, ones_bd, p["rw_ln_g"].reshape(1, w),
      p["rw_ln_b"].reshape(1, w), p["s5_d"].reshape(1, w), p["s5_glu_w"].astype(BF16),
      p["s5_glu_b"].reshape(1, 2 * w), p["w_branch"].astype(BF16), p["w_out"].astype(BF16))


def _k_router(x_ref, sh_ref, sc_ref, g_ref, rw_ref, rb_ref, h_o, idx_o, gate_o, cnt_o):
    @pl.when((pl.program_id(0) == 0) & (pl.program_id(1) == 0))
    def _():
        cnt_o[...] = jnp.zeros_like(cnt_o)

    h = _rms_modulate(x_ref[0], g_ref[...], sh_ref[0], sc_ref[0])
    h_o[0] = h.astype(BF16)
    logits = _dot_hi(h, rw_ref[...]) + rb_ref[...]
    lane = lax.broadcasted_iota(jnp.int32, logits.shape, 1)
    vals, idxs = [], []
    for _ in range(TOP_K):
        m = jnp.max(logits, axis=-1, keepdims=True)
        idx = jnp.min(jnp.where(logits == m, lane, LANES), axis=-1, keepdims=True)
        vals.append(m)
        idxs.append(idx)
        logits = jnp.where(lane == idx, -jnp.inf, logits)
    exps = [jnp.exp(v - vals[0]) for v in vals]
    inv = 1.0 / (exps[0] + exps[1] + exps[2] + exps[3])
    idx_out = jnp.zeros(logits.shape, jnp.int32)
    gate_out = jnp.zeros(logits.shape, F32)
    picked = jnp.zeros(logits.shape, F32)
    for j in range(TOP_K):
        idx_out = jnp.where(lane == j, idxs[j], idx_out)
        gate_out = jnp.where(lane == j, exps[j] * inv, gate_out)
        picked = picked + jnp.where(lane == idxs[j], 1.0, 0.0)
    idx_o[0] = idx_out
    gate_o[0] = gate_out
    cnt_o[...] = cnt_o[...] + jnp.sum(picked, axis=0, keepdims=True)


def moe_route(x, shift, scale, g, router_w, router_b):
    bsz, n, d = x.shape
    tm = min(ROW_TILE, n)
    rw = jnp.pad(router_w, ((0, 0), (0, LANES - N_EXPERTS)))
    rb = jnp.pad(router_b, (0, LANES - N_EXPERTS), constant_values=-jnp.inf).reshape(1, LANES)
    tile = lambda c: pl.BlockSpec((1, tm, c), lambda i, j: (i, j, 0))
    vec = pl.BlockSpec((1, 1, d), lambda i, j: (i, 0, 0))
    full = lambda shape: pl.BlockSpec(shape, lambda i, j: (0,) * len(shape))
    return pl.pallas_call(
        _k_router,
        grid=(bsz, n // tm),
        in_specs=[tile(d), vec, vec, full((1, d)), full((d, LANES)), full((1, LANES))],
        out_specs=[tile(d), tile(LANES), tile(LANES), full((SUBLANES, LANES))],
        out_shape=[jax.ShapeDtypeStruct((bsz, n, d), BF16), jax.ShapeDtypeStruct((bsz, n, LANES), jnp.int32),
                   jax.ShapeDtypeStruct((bsz, n, LANES), F32), jax.ShapeDtypeStruct((SUBLANES, LANES), F32)],
        compiler_params=_params("arbitrary", "arbitrary"),
        name="moe_route",
    )(x, shift, scale, g.reshape(1, d), rw, rb)


def _k_expert(blk_e_ref, used_ref, xs_ref, w1_ref, b1_ref, w2_ref, b2_ref, sw_ref, o_ref, w1b_ref, w2b_ref):
    i = pl.program_id(0)

    @pl.when(i < used_ref[0])
    def _():
        @pl.when((i == 0) | (blk_e_ref[i] != blk_e_ref[jnp.maximum(i - 1, 0)]))
        def _():
            w1b_ref[...] = w1_ref[0].astype(BF16)
            w2b_ref[...] = w2_ref[0].astype(BF16)

        hid = _dot(xs_ref[...], w1b_ref[...]) + b1_ref[0]
        gl = jnp.minimum(hid[:, 0:D_EXPERT], SWIGLU_LIMIT)
        up = jnp.clip(hid[:, D_EXPERT:2 * D_EXPERT], -SWIGLU_LIMIT, SWIGLU_LIMIT)
        act = (up + 1.0) * gl * _sigmoid(SWIGLU_ALPHA * gl)
        o_ref[...] = ((_dot(act.astype(BF16), w2b_ref[...]) + b2_ref[0]) * sw_ref[...]).astype(o_ref.dtype)

    @pl.when(i >= used_ref[0])
    def _():
        o_ref[...] = jnp.zeros_like(o_ref)


def expert_ffn(xs, blk_e, n_used, slot_w, w1, b1, w2, b2):
    n_slots, d = xs.shape
    n_blocks = n_slots // MOE_ROWS
    grid_spec = pltpu.PrefetchScalarGridSpec(
        num_scalar_prefetch=2,
        grid=(n_blocks,),
        in_specs=[pl.BlockSpec((MOE_ROWS, d), lambda i, e, u: (i, 0)),
                  pl.BlockSpec((1, d, 2 * D_EXPERT), lambda i, e, u: (e[i], 0, 0)),
                  pl.BlockSpec((1, 1, 2 * D_EXPERT), lambda i, e, u: (e[i], 0, 0)),
                  pl.BlockSpec((1, D_EXPERT, d), lambda i, e, u: (e[i], 0, 0)),
                  pl.BlockSpec((1, 1, d), lambda i, e, u: (e[i], 0, 0)),
                  pl.BlockSpec((MOE_ROWS, 1), lambda i, e, u: (i, 0))],
        out_specs=pl.BlockSpec((MOE_ROWS, d), lambda i, e, u: (i, 0)),
        scratch_shapes=[pltpu.VMEM((d, 2 * D_EXPERT), BF16), pltpu.VMEM((D_EXPERT, d), BF16)],
    )
    return pl.pallas_call(
        _k_expert,
        grid_spec=grid_spec,
        out_shape=jax.ShapeDtypeStruct((n_slots, d), BF16),
        compiler_params=_params("arbitrary"),
        name="expert_ffn",
    )(blk_e, n_used.reshape(1).astype(jnp.int32), xs, w1, b1.reshape(b1.shape[0], 1, -1), w2,
      b2.reshape(b2.shape[0], 1, -1), slot_w.reshape(-1, 1))


def _k_offset_add(tab_ref, e_ref, v_ref, o_ref):
    e = e_ref[...]
    acc = v_ref[...]
    for j in range(N_EXPERTS):
        acc = acc + jnp.where(e == j, tab_ref[j], 0)
    o_ref[...] = acc


def expert_offset_add(flat_e, values, table):
    n = flat_e.shape[0]
    rows = n // LANES
    return pl.pallas_call(
        _k_offset_add,
        in_specs=[pl.BlockSpec(memory_space=pltpu.SMEM), pl.BlockSpec(memory_space=pltpu.VMEM),
                  pl.BlockSpec(memory_space=pltpu.VMEM)],
        out_specs=pl.BlockSpec(memory_space=pltpu.VMEM),
        out_shape=jax.ShapeDtypeStruct((rows, LANES), jnp.int32),
        name="expert_offset_add",
    )(table.astype(jnp.int32), flat_e.reshape(rows, LANES), values.reshape(rows, LANES)).reshape(n)


def moe_ffn(x, shift, scale, g, router_w, router_b, w1, b1, w2, b2, expert_base=0):
    bsz, n, d = x.shape
    h, idx, gate, cnt = moe_route(x, shift, scale, g, router_w, router_b)
    n_tok = bsz * n
    n_asg = n_tok * TOP_K
    flat_e = idx[..., :TOP_K].reshape(-1)
    flat_w = gate[..., :TOP_K].reshape(-1)
    order = jnp.argsort(flat_e).astype(jnp.int32)
    rank_of = jnp.argsort(order).astype(jnp.int32)
    counts = cnt[0, :N_EXPERTS].astype(jnp.int32)
    grp_start = jnp.cumsum(counts) - counts
    padded = (counts + MOE_ROWS - 1) // MOE_ROWS * MOE_ROWS
    pad_end = jnp.cumsum(padded)
    pad_start = pad_end - padded
    n_blocks = -(-n_asg // MOE_ROWS) + N_EXPERTS
    n_slots = n_blocks * MOE_ROWS
    blk_first = jnp.arange(n_blocks, dtype=jnp.int32) * MOE_ROWS
    blk_e = jnp.minimum(jnp.sum(pad_end[None, :] <= blk_first[:, None], axis=1), N_EXPERTS - 1).astype(jnp.int32)
    slot_rank = (blk_first - pad_start[blk_e])[:, None] + jnp.arange(MOE_ROWS, dtype=jnp.int32)[None, :]
    slot_used = (slot_rank < counts[blk_e][:, None]).reshape(-1)
    slot_src = jnp.where(slot_used, (grp_start[blk_e][:, None] + slot_rank).reshape(-1), 0)
    slot_asg = order[slot_src]
    slot_tok = jnp.where(slot_used, slot_asg // TOP_K, 0)
    slot_w = jnp.where(slot_used, flat_w[slot_asg], 0.0)
    slot_of = expert_offset_add(flat_e, rank_of, pad_start - grp_start)
    xs = h.reshape(n_tok, d)[slot_tok]
    ys = expert_ffn(xs, blk_e + expert_base, pad_end[-1] // MOE_ROWS, slot_w, w1, b1, w2, b2)
    slot_of = slot_of.reshape(n_tok, TOP_K)
    out = sum(ys[slot_of[:, j]].astype(F32) for j in range(TOP_K))
    return out.reshape(bsz, n, d)


def _k_rmsnorm(x_ref, g_ref, o_ref):
    x = x_ref[0]
    ms = jnp.mean(x * x, axis=-1, keepdims=True)
    o_ref[0] = x * lax.rsqrt(ms + NORM_EPS) * g_ref[...]


def final_rmsnorm(x, g):
    b, n, d = x.shape
    tm = min(2 * ROW_TILE, n)
    return pl.pallas_call(
        _k_rmsnorm,
        grid=(b, n // tm),
        in_specs=[pl.BlockSpec((1, tm, d), lambda i, j: (i, j, 0)), pl.BlockSpec((1, d), lambda i, j: (0, 0))],
        out_specs=pl.BlockSpec((1, tm, d), lambda i, j: (i, j, 0)),
        out_shape=jax.ShapeDtypeStruct(x.shape, F32),
        compiler_params=_params("arbitrary", "arbitrary"),
        name="final_rmsnorm",
    )(x, g.reshape(1, d))


@functools.lru_cache(maxsize=None)
def _grid_pos_embed(n_tokens):
    rows = n_tokens // GRID_W
    row_id, col_id = np.meshgrid(np.arange(rows), np.arange(GRID_W), indexing="ij")
    quarter = D_MODEL // 4
    omega = (1.0 / (10000.0 ** (np.arange(quarter, dtype=np.float32) / quarter))).astype(np.float32)

    def enc(pos):
        ang = pos.reshape(-1)[:, None].astype(np.float32) * omega
        return np.concatenate([np.sin(ang), np.cos(ang)], axis=-1)

    return np.concatenate([enc(row_id), enc(col_id)], axis=-1).astype(np.float32)


_LAYER_KEYS = ("ada_w", "ada_b", "norm1_g", "norm2_g", "w_in", "hy_conv_w", "hy_conv_b", "hy_w1", "hy_b1",
               "hy_f1", "hy_w2", "hy_b2", "hy_f2", "hy_w3", "hy_b3", "hy_bias", "rw_conv_w", "rw_conv_b",
               "rw_w_up", "rw_w0", "rw_a_up", "rw_a0", "rw_g_up", "rw_k_k", "rw_k_a", "rw_r_k", "rw_ln_g",
               "rw_ln_b", "s5_a_re", "s5_a_im", "s5_log_dt", "s5_b_re", "s5_b_im", "s5_c_re", "s5_c_im",
               "s5_d", "s5_glu_w", "s5_glu_b", "w_branch", "w_out", "router_w", "router_b", "moe_w1",
               "moe_b1", "moe_w2", "moe_b2")


def _token_mixer(x, xc, mod, mod_c, p, need_ctx, ones_bd):
    sh1, sc1, g1 = mod
    csh1, csc1, cg1 = mod_c
    w_in = p["w_in"].astype(BF16)
    c0, c1, c2 = HY_COLS, HY_COLS + RW_COLS, HY_COLS + RW_COLS + MIX_WIDTH
    proj = lambda t, s, c, lo, hi: norm_mod_matmul(t, s, c, p["norm1_g"], w_in[:, lo:hi])
    hy_args = (p["hy_w1"], p["hy_b1"], p["hy_f1"], p["hy_w2"], p["hy_b2"], p["hy_f2"], p["hy_w3"], p["hy_b3"],
               p["hy_bias"])

    z_hy = proj(x, sh1, sc1, 0, c0)
    y_hy = hyena_branch(z_hy, p["hy_conv_w"], p["hy_conv_b"], hyena_spectra(x.shape[1], *hy_args))
    u_rw = short_conv(proj(x, sh1, sc1, c0, c1), p["rw_conv_w"], p["rw_conv_b"])
    uc_rw = short_conv(proj(xc, csh1, csc1, c0, c1), p["rw_conv_w"], p["rw_conv_b"])
    rw = rwkv_scan_branch(uc_rw, u_rw, p, ones_bd)
    n_ctx = xc.shape[1]
    z_s5 = proj(x, sh1, sc1, c1, c2)
    zc_s5 = proj(xc, csh1, csc1, c1, c2)
    s5_y = s5_branch(zc_s5, z_s5, p)
    zg = proj(x, sh1, sc1, c2, c2 + N_BRANCH * D_MODEL)
    x_new = merge_residual(y_hy, rw, s5_y, n_ctx, z_s5, zg, x, g1, p, ones_bd)
    if not need_ctx:
        return x_new, None
    zc_hy = proj(xc, csh1, csc1, 0, c0)
    yc_hy = hyena_branch(zc_hy, p["hy_conv_w"], p["hy_conv_b"], hyena_spectra(xc.shape[1], *hy_args))
    zcg = proj(xc, csh1, csc1, c2, c2 + N_BRANCH * D_MODEL)
    xc_new = merge_residual(yc_hy, rw, s5_y, 0, zc_s5, zcg, xc, cg1, p, ones_bd)
    return x_new, xc_new


def kernel(x, c, ctx, c_ctx, ada_w, ada_b, norm1_g, norm2_g, w_in, hy_conv_w, hy_conv_b, hy_w1, hy_b1, hy_f1, hy_w2, hy_b2, hy_f2, hy_w3, hy_b3, hy_bias, rw_conv_w, rw_conv_b, rw_w_up, rw_w0, rw_a_up, rw_a0, rw_g_up, rw_k_k, rw_k_a, rw_r_k, rw_ln_g, rw_ln_b, s5_a_re, s5_a_im, s5_log_dt, s5_b_re, s5_b_im, s5_c_re, s5_c_im, s5_d, s5_glu_w, s5_glu_b, w_branch, w_out, router_w, router_b, moe_w1, moe_b1, moe_w2, moe_b2, final_g):
    stacked = dict(zip(_LAYER_KEYS, (ada_w, ada_b, norm1_g, norm2_g, w_in, hy_conv_w, hy_conv_b, hy_w1, hy_b1,
                                     hy_f1, hy_w2, hy_b2, hy_f2, hy_w3, hy_b3, hy_bias, rw_conv_w, rw_conv_b,
                                     rw_w_up, rw_w0, rw_a_up, rw_a0, rw_g_up, rw_k_k, rw_k_a, rw_r_k, rw_ln_g,
                                     rw_ln_b, s5_a_re, s5_a_im, s5_log_dt, s5_b_re, s5_b_im, s5_c_re, s5_c_im,
                                     s5_d, s5_glu_w, s5_glu_b, w_branch, w_out, router_w, router_b, moe_w1,
                                     moe_b1, moe_w2, moe_b2)))
    bsz, n, d = x.shape
    depth = ada_w.shape[0]
    x = add_pos(x, jnp.asarray(_grid_pos_embed(n)))
    xc = ctx
    ones_bd = jnp.asarray(np.kron(np.eye(RW_HEADS), np.ones((RW_HEAD, RW_HEAD))), BF16)
    cvec = jnp.zeros((2 * SUBLANES, d), F32).at[:bsz].set(c).at[bsz].set(c_ctx)
    for l in range(depth):
        p = {k: v[l] for k, v in stacked.items()}
        need_ctx = l < depth - 1
        ada = ada_proj(cvec, p["ada_w"], p["ada_b"])
        lat = [t.reshape(bsz, 1, d) for t in jnp.split(ada[:bsz], 6, axis=-1)]
        cx = [jnp.broadcast_to(t.reshape(1, 1, d), (bsz, 1, d)) for t in jnp.split(ada[bsz], 6, axis=-1)]
        x, xc_new = _token_mixer(x, xc, lat[0:3], cx[0:3], p, need_ctx, ones_bd)
        stack = lambda t: t.reshape((depth * N_EXPERTS,) + t.shape[2:])
        moe = functools.partial(moe_ffn, g=p["norm2_g"], router_w=p["router_w"], router_b=p["router_b"],
                                w1=stack(moe_w1), b1=stack(moe_b1), w2=stack(moe_w2), b2=stack(moe_b2),
                                expert_base=l * N_EXPERTS)
        x = x + lat[5] * moe(x, lat[3], lat[4])
        if need_ctx:
            xc = xc_new + cx[5] * moe(xc_new, cx[3], cx[4])
    return final_rmsnorm(x, final_g)
```

```python
import functools
import math

import numpy as np
import jax
import jax.numpy as jnp
from jax import lax
from jax.experimental import pallas as pl
from jax.experimental.pallas import tpu as pltpu

F32 = jnp.float32
BF16 = jnp.bfloat16

D_MODEL = 1024
GRID_W = 64
SHORT_CONV = 3
NORM_EPS = 1e-6
N_BRANCH = 3
MIX_WIDTH = 512

HY_ORDER = 2
HY_BANDS = 16
HY_DECAY_TARGET = 1e-2
HY_FAST_PCT = 0.3
HY_SLOW_PCT = 1.5
HY_COLS = (HY_ORDER + 1) * MIX_WIDTH

RW_HEAD = 64
RW_HEADS = MIX_WIDTH // RW_HEAD
RW_DECAY_RANK = 64
RW_ICLR_RANK = 64
RW_GATE_RANK = 128
RW_LN_EPS = 64e-5
RW_COLS = 3 * MIX_WIDTH + RW_DECAY_RANK + RW_ICLR_RANK + RW_GATE_RANK

S5_GROUP = 16
S5_GROUPS = MIX_WIDTH // S5_GROUP
S5_STATE = 64
S5_HID = S5_GROUPS * S5_STATE

N_EXPERTS = 32
TOP_K = 4
D_EXPERT = 1024
SWIGLU_LIMIT = 7.0
SWIGLU_ALPHA = 1.702

LANES = 128
SUBLANES = 8
VMEM_LIMIT_BYTES = 56 * 1024 * 1024

ROW_TILE = 256
MOE_ROWS = 512
WKV_STEPS = 16
WKV_K_UNROLL = 16
S5_STEPS = 64
S5_LANE_CHUNK = 512
S5_DIAG_PARTS = 2


def _params(*sem):
    return pltpu.CompilerParams(dimension_semantics=sem, vmem_limit_bytes=VMEM_LIMIT_BYTES)


def _dot(a, b):
    return jnp.dot(a, b, preferred_element_type=F32)


def _split3(x):
    hi = x.astype(BF16)
    r1 = x - hi.astype(F32)
    mid = r1.astype(BF16)
    lo = (r1 - mid.astype(F32)).astype(BF16)
    return hi, mid, lo


def _dot_hi(a, b):
    a0, a1, a2 = _split3(a)
    b0, b1, b2 = _split3(b)
    return (_dot(a0, b0) + (_dot(a0, b1) + _dot(a1, b0))
            + (_dot(a1, b1) + _dot(a0, b2) + _dot(a2, b0)))


def _seg_sum(x, ones_bd):
    hi, mid, lo = _split3(x)
    return _dot(hi, ones_bd) + _dot(mid, ones_bd) + _dot(lo, ones_bd)


def _sigmoid(x):
    return 1.0 / (1.0 + jnp.exp(-x))


def _rms_modulate(x, g, shift, scale):
    ms = jnp.mean(x * x, axis=-1, keepdims=True)
    y = x * lax.rsqrt(ms + NORM_EPS) * g
    return y * (1.0 + scale) + shift


def _k_ada(c_ref, w_ref, b_ref, o_ref):
    c = c_ref[...]
    s = c * _sigmoid(c)
    o_ref[...] = _dot_hi(s, w_ref[...]) + b_ref[...]


def ada_proj(cvec, w, b):
    rows, d = cvec.shape
    n = w.shape[1]
    tn = 1536
    return pl.pallas_call(
        _k_ada,
        grid=(n // tn,),
        in_specs=[pl.BlockSpec((rows, d), lambda j: (0, 0)),
                  pl.BlockSpec((d, tn), lambda j: (0, j)),
                  pl.BlockSpec((1, tn), lambda j: (0, j))],
        out_specs=pl.BlockSpec((rows, tn), lambda j: (0, j)),
        out_shape=jax.ShapeDtypeStruct((rows, n), F32),
        compiler_params=_params("arbitrary"),
        name="ada_proj",
    )(cvec, w, b.reshape(1, n))


def _k_add_pos(x_ref, p_ref, o_ref):
    o_ref[0] = x_ref[0] + p_ref[...]


def add_pos(x, pos):
    b, n, d = x.shape
    tm = min(ROW_TILE * 2, n)
    return pl.pallas_call(
        _k_add_pos,
        grid=(b, n // tm),
        in_specs=[pl.BlockSpec((1, tm, d), lambda i, j: (i, j, 0)),
                  pl.BlockSpec((tm, d), lambda i, j: (j, 0))],
        out_specs=pl.BlockSpec((1, tm, d), lambda i, j: (i, j, 0)),
        out_shape=jax.ShapeDtypeStruct(x.shape, F32),
        compiler_params=_params("arbitrary", "arbitrary"),
        name="add_pos",
    )(x, pos)


def _k_norm_mm(x_ref, sh_ref, sc_ref, g_ref, w_ref, o_ref, h_ref):
    @pl.when(pl.program_id(2) == 0)
    def _():
        h = _rms_modulate(x_ref[0], g_ref[...], sh_ref[0], sc_ref[0])
        h_ref[...] = h.astype(BF16)

    o_ref[0] = _dot(h_ref[...], w_ref[...])


def norm_mod_matmul(x, shift, scale, g, w):
    b, n, d = x.shape
    ncol = w.shape[1]
    tm = min(4 * ROW_TILE, n)
    tn = ncol if ncol <= 1792 else 1536
    return pl.pallas_call(
        _k_norm_mm,
        grid=(b, n // tm, ncol // tn),
        in_specs=[pl.BlockSpec((1, tm, d), lambda i, j, k: (i, j, 0)),
                  pl.BlockSpec((1, 1, d), lambda i, j, k: (i, 0, 0)),
                  pl.BlockSpec((1, 1, d), lambda i, j, k: (i, 0, 0)),
                  pl.BlockSpec((1, d), lambda i, j, k: (0, 0)),
                  pl.BlockSpec((d, tn), lambda i, j, k: (0, k))],
        out_specs=pl.BlockSpec((1, tm, tn), lambda i, j, k: (i, j, k)),
        out_shape=jax.ShapeDtypeStruct((b, n, ncol), F32),
        scratch_shapes=[pltpu.VMEM((tm, d), BF16)],
        compiler_params=_params("arbitrary", "arbitrary", "arbitrary"),
        name="norm_mod_matmul",
    )(x, shift, scale, g.reshape(1, d), w)


def _short_conv_val(z, w, b):
    n = z.shape[0]
    row = lax.broadcasted_iota(jnp.int32, z.shape, 0)
    zm = jnp.where(row == 0, 0.0, pltpu.roll(z, 1, 0))
    zp = jnp.where(row == n - 1, 0.0, pltpu.roll(z, n - 1, 0))
    return zm * w[0:1] + z * w[1:2] + zp * w[2:3] + b


def _k_sconv(z_ref, w_ref, b_ref, o_ref):
    o_ref[0] = _short_conv_val(z_ref[0], w_ref[...], b_ref[...])


def short_conv(z, w, b):
    bsz, n, c = z.shape
    tc = 256 if c % 256 == 0 else LANES
    return pl.pallas_call(
        _k_sconv,
        grid=(bsz, c // tc),
        in_specs=[pl.BlockSpec((1, n, tc), lambda i, j: (i, 0, j)),
                  pl.BlockSpec((SHORT_CONV, tc), lambda i, j: (0, j)),
                  pl.BlockSpec((1, tc), lambda i, j: (0, j))],
        out_specs=pl.BlockSpec((1, n, tc), lambda i, j: (i, 0, j)),
        out_shape=jax.ShapeDtypeStruct(z.shape, F32),
        compiler_params=_params("arbitrary", "arbitrary"),
        name="short_conv",
    )(z, w, b.reshape(1, c))


def _fft_split(n):
    total = 2 * n
    bits = total.bit_length() - 1
    assert 1 << bits == total
    n1 = 1 << ((bits + 1) // 2)
    n1 = max(n1, 2 * SUBLANES)
    return n1, total // n1


@functools.lru_cache(maxsize=None)
def _dft_mats(n):
    n1, n2 = _fft_split(n)
    total = n1 * n2
    k1 = np.arange(n1)[:, None]
    m1 = np.arange(n1)[None, :]
    m2 = np.arange(n2)[:, None, None]
    ang = -2.0 * np.pi * ((n2 * k1 * m1)[None] + m2 * k1[None]) / total
    fa = np.concatenate([np.cos(ang), np.sin(ang)], axis=1)
    k2 = np.arange(n2)[:, None]
    mm = np.arange(n2)[None, :]
    gang = -2.0 * np.pi * k2 * mm / n2
    gr, gi = np.cos(gang), np.sin(gang)
    fb = np.block([[gr, -gi], [gi, gr]])
    fc = np.block([[gr, gi], [-gi, gr]])
    m1d = np.arange(n1 // 2)[:, None]
    k1d = np.arange(n1)[None, :]
    dang = 2.0 * np.pi * ((n2 * m1d * k1d)[None] + m2 * k1d[None]) / total
    fd = np.concatenate([np.cos(dang), -np.sin(dang)], axis=2) / total
    return (fa.astype(np.float32), fb.astype(np.float32), fc.astype(np.float32), fd.astype(np.float32))


def _time_pitch(n2):
    return n2 + SUBLANES


def _spec_pitch(n2):
    return 2 * n2 + SUBLANES


def _fft_stage_a(src_ref, y_ref, fa_ref, n1, n2, k_rows):
    tp, sp = _time_pitch(n2), _spec_pitch(n2)

    def body(m2, c):
        xs = src_ref[pl.ds(m2, k_rows, stride=tp), :].astype(BF16)
        res = _dot(fa_ref[m2, :, 0:k_rows], xs)
        y_ref[pl.ds(m2, n1, stride=sp), :] = res[0:n1]
        y_ref[pl.ds(n2 + m2, n1, stride=sp), :] = res[n1:2 * n1]
        return c
    lax.fori_loop(0, n2, body, 0, unroll=8)


def _long_conv(buf_ref, y_ref, h_ref, fa_ref, fb_ref, fc_ref, fd_ref, n1, n2):
    tp, sp = _time_pitch(n2), _spec_pitch(n2)
    _fft_stage_a(buf_ref, y_ref, fa_ref, n1, n2, n1 // 2)

    def body_k1(k1, c):
        rows = pl.ds(pl.multiple_of(k1 * sp, SUBLANES), 2 * n2)
        z = _dot(fb_ref[...], y_ref[rows, :].astype(BF16))
        zr, zi = z[0:n2], z[n2:2 * n2]
        h = h_ref[k1]
        hr, hi = h[0:n2], h[n2:2 * n2]
        p = jnp.concatenate([zr * hr - zi * hi, zr * hi + zi * hr], axis=0)
        y_ref[rows, :] = _dot(fc_ref[...], p.astype(BF16))
        return c
    lax.fori_loop(0, n1, body_k1, 0, unroll=16)

    def body_m2(m2, c):
        qr = y_ref[pl.ds(m2, n1, stride=sp), :]
        qi = y_ref[pl.ds(n2 + m2, n1, stride=sp), :]
        q = jnp.concatenate([qr, qi], axis=0).astype(BF16)
        buf_ref[pl.ds(m2, n1 // 2, stride=tp), :] = _dot(fd_ref[m2], q)
        return c
    lax.fori_loop(0, n2, body_m2, 0, unroll=8)


def _store_time_blocks(buf_ref, val, n2):
    tp = _time_pitch(n2)
    for m1 in range(val.shape[0] // n2):
        buf_ref[m1 * tp:m1 * tp + n2, :] = val[m1 * n2:(m1 + 1) * n2]


def _load_time_blocks(buf_ref, n, n2):
    tp = _time_pitch(n2)
    return jnp.concatenate([buf_ref[m1 * tp:m1 * tp + n2, :] for m1 in range(n // n2)], axis=0)


def _k_hyena(x1_ref, x2_ref, v_ref, w1_ref, w2_ref, wv_ref, b1_ref, b2_ref, bv_ref, h1_ref, h2_ref,
             fa_ref, fb_ref, fc_ref, fd_ref, o_ref, buf_ref, y_ref, *, n1, n2):
    n = x1_ref.shape[1]
    conv = functools.partial(_long_conv, buf_ref, y_ref, fa_ref=fa_ref, fb_ref=fb_ref, fc_ref=fc_ref,
                             fd_ref=fd_ref, n1=n1, n2=n2)
    _store_time_blocks(buf_ref, _short_conv_val(v_ref[0], wv_ref[...], bv_ref[...]), n2)
    conv(h1_ref)
    gated = _short_conv_val(x1_ref[0], w1_ref[...], b1_ref[...]) * _load_time_blocks(buf_ref, n, n2)
    _store_time_blocks(buf_ref, gated, n2)
    conv(h2_ref)
    o_ref[0] = _short_conv_val(x2_ref[0], w2_ref[...], b2_ref[...]) * _load_time_blocks(buf_ref, n, n2)


def _k_fspec(f_ref, fa_ref, fb_ref, o_ref, y_ref, *, n1, n2):
    sp = _spec_pitch(n2)
    _fft_stage_a(f_ref, y_ref, fa_ref, n1, n2, n1)

    def body_k1(k1, c):
        src = pl.ds(pl.multiple_of(k1 * sp, SUBLANES), 2 * n2)
        dst = pl.ds(pl.multiple_of(k1 * 2 * n2, 2 * n2), 2 * n2)
        o_ref[dst, :] = _dot(fb_ref[...], y_ref[src, :].astype(BF16))
        return c
    lax.fori_loop(0, n1, body_k1, 0, unroll=2)


def filter_spectrum(filt, n):
    n1, n2 = _fft_split(n)
    total, c = filt.shape
    tp, sp = _time_pitch(n2), _spec_pitch(n2)
    fa, fb, _, _ = _dft_mats(n)
    blocks = jnp.pad(filt.reshape(n1, n2, c), ((0, 0), (0, tp - n2), (0, 0))).reshape(n1 * tp, c)
    out = pl.pallas_call(
        functools.partial(_k_fspec, n1=n1, n2=n2),
        grid=(c // LANES,),
        in_specs=[pl.BlockSpec((n1 * tp, LANES), lambda j: (0, j)),
                  pl.BlockSpec((n2, 2 * n1, n1), lambda j: (0, 0, 0)),
                  pl.BlockSpec((2 * n2, 2 * n2), lambda j: (0, 0))],
        out_specs=pl.BlockSpec((total * 2, LANES), lambda j: (0, j)),
        out_shape=jax.ShapeDtypeStruct((2 * total, c), F32),
        scratch_shapes=[pltpu.VMEM((n1 * sp, LANES), F32)],
        compiler_params=_params("arbitrary"),
        name="filter_spectrum",
    )(blocks, jnp.asarray(fa, BF16), jnp.asarray(fb, BF16))
    return out.reshape(n1, 2 * n2, c)


def _k_filter_mlp(feat_ref, win_ref, w1_ref, b1_ref, f1_ref, w2_ref, b2_ref, f2_ref, w3_ref, b3_ref, o_ref):
    h = jnp.sin(f1_ref[...] * (_dot_hi(feat_ref[...], w1_ref[...]) + b1_ref[...]))
    h = jnp.sin(f2_ref[...] * (_dot_hi(h, w2_ref[...]) + b2_ref[...]))
    o_ref[...] = (_dot_hi(h, w3_ref[...]) + b3_ref[...]) * win_ref[...]


@functools.lru_cache(maxsize=None)
def _filter_consts(n):
    t = np.linspace(0.0, 1.0, n, dtype=np.float32)[:, None]
    bands = np.linspace(1e-4, HY_BANDS - 1, HY_BANDS, dtype=np.float32)
    ang = (np.float32(2 * math.pi) * np.arange(n, dtype=np.float32) / np.float32(n))[:, None] * bands
    feats = np.concatenate([t, np.cos(ang), -np.sin(ang)], axis=-1).astype(np.float32)
    pad = (-feats.shape[1]) % SUBLANES
    feats = np.pad(feats, ((0, 0), (0, pad)))
    deltas = np.abs(np.linspace(math.log(HY_DECAY_TARGET) / HY_SLOW_PCT,
                                math.log(HY_DECAY_TARGET) / HY_FAST_PCT, MIX_WIDTH, dtype=np.float32))
    window = np.exp(-t * deltas).astype(np.float32)
    return feats, np.tile(window, (1, 2 * HY_ORDER))


def hyena_filter_table(n, w1, b1, f1, w2, b2, f2, w3, b3):
    feats, window = _filter_consts(n)
    kf = feats.shape[1]
    fd = w1.shape[1]
    ncol = w3.shape[1]
    w1p = jnp.pad(w1, ((0, kf - w1.shape[0]), (0, 0)))
    tm = min(ROW_TILE, n)
    full = lambda shape: pl.BlockSpec(shape, lambda i: (0,) * len(shape))
    return pl.pallas_call(
        _k_filter_mlp,
        grid=(n // tm,),
        in_specs=[pl.BlockSpec((tm, kf), lambda i: (i, 0)),
                  pl.BlockSpec((tm, ncol), lambda i: (i, 0)),
                  full((kf, fd)), full((1, fd)), full((1, fd)),
                  full((fd, fd)), full((1, fd)), full((1, fd)),
                  full((fd, ncol)), full((1, ncol))],
        out_specs=pl.BlockSpec((tm, ncol), lambda i: (i, 0)),
        out_shape=jax.ShapeDtypeStruct((n, ncol), F32),
        compiler_params=_params("arbitrary"),
        name="hyena_filter_mlp",
    )(jnp.asarray(feats), jnp.asarray(window), w1p, b1.reshape(1, fd), f1.reshape(1, fd),
      w2, b2.reshape(1, fd), f2.reshape(1, fd), w3, b3.reshape(1, ncol))


def hyena_spectra(n, w1, b1, f1, w2, b2, f2, w3, b3, bias):
    hf = hyena_filter_table(n, w1, b1, f1, w2, b2, f2, w3, b3).reshape(n, HY_ORDER, 2, MIX_WIDTH)
    fwd = hf[:, :, 0]
    bwd = hf[1:, :, 1][::-1]
    fwd = fwd.at[0].add(bias)
    filt = jnp.concatenate([fwd, jnp.zeros((1, HY_ORDER, MIX_WIDTH), F32), bwd], axis=0)
    spec = filter_spectrum(filt.reshape(2 * n, HY_ORDER * MIX_WIDTH), n)
    n1, n2 = _fft_split(n)
    return spec.reshape(n1, 2 * n2, HY_ORDER, MIX_WIDTH).transpose(2, 0, 1, 3)


def hyena_branch(z, conv_w, conv_b, spectra):
    bsz, n, _ = z.shape
    n1, n2 = _fft_split(n)
    fa, fb, fc, fd = (jnp.asarray(m, BF16) for m in _dft_mats(n))
    fa = fa[:, :, : n1 // 2]
    nt = MIX_WIDTH // LANES
    cb = conv_b.reshape(1, HY_COLS)
    zspec = lambda off: pl.BlockSpec((1, n, LANES), lambda j, i: (i, 0, off + j))
    wspec = lambda off: pl.BlockSpec((SHORT_CONV, LANES), lambda j, i: (0, off + j))
    bspec = lambda off: pl.BlockSpec((1, LANES), lambda j, i: (0, off + j))
    once = pl.Buffered(1)
    hspec = lambda o: pl.BlockSpec((None, n1, 2 * n2, LANES), lambda j, i: (o, 0, 0, j), pipeline_mode=once)
    full = lambda shape: pl.BlockSpec(shape, lambda j, i: (0,) * len(shape), pipeline_mode=once)
    return pl.pallas_call(
        functools.partial(_k_hyena, n1=n1, n2=n2),
        grid=(nt, bsz),
        in_specs=[zspec(0), zspec(nt), zspec(2 * nt), wspec(0), wspec(nt), wspec(2 * nt),
                  bspec(0), bspec(nt), bspec(2 * nt), hspec(0), hspec(1),
                  full(fa.shape), full(fb.shape), full(fc.shape), full(fd.shape)],
        out_specs=pl.BlockSpec((1, n, LANES), lambda j, i: (i, 0, j)),
        out_shape=jax.ShapeDtypeStruct((bsz, n, MIX_WIDTH), F32),
        scratch_shapes=[pltpu.VMEM((n1 // 2 * _time_pitch(n2), LANES), F32),
                        pltpu.VMEM((n1 * _spec_pitch(n2), LANES), F32)],
        compiler_params=_params("arbitrary", "arbitrary"),
        name="hyena_long_conv",
    )(z, z, z, conv_w, conv_w, conv_w, cb, cb, cb, spectra, spectra, fa, fb, fc, fd)


def _store_head_pairs(o_ref, left, right):
    rows = left.shape[0]
    lane = lax.broadcasted_iota(jnp.int32, (rows, LANES), 1)
    low = lane < RW_HEAD
    for q in range(RW_HEADS // 2):
        lcol = left[:, q * LANES:(q + 1) * LANES]
        rcol = right[:, q * LANES:(q + 1) * LANES]
        heads = (jnp.where(low, lcol, pltpu.roll(rcol, RW_HEAD, 1)),
                 jnp.where(low, pltpu.roll(lcol, RW_HEAD, 1), rcol))
        for h, piece in zip((2 * q, 2 * q + 1), heads):
            for c in range(rows // WKV_STEPS):
                o_ref[c, h * WKV_STEPS:(h + 1) * WKV_STEPS, :] = piece[c * WKV_STEPS:(c + 1) * WKV_STEPS]


def _load_head_sum(y0_ref, y1_ref, h):
    sl = slice(h * WKV_STEPS, (h + 1) * WKV_STEPS)
    return jnp.concatenate([y0_ref[c, sl, :] + y1_ref[c, sl, :] for c in range(y0_ref.shape[0])], axis=0)


def _k_rwprep(uc_ref, ul_ref, gup_ref, wup_ref, w0_ref, aup_ref, a0_ref, kk_ref, ka_ref, rk_ref, e_ref,
              g_o, bonus_o, rv_o, nb0_o, nb1_o, wk0_o, wk1_o, *, ctx_tiles):
    w = MIX_WIDTH
    u = jnp.where(pl.program_id(1) < ctx_tiles, uc_ref[0], ul_ref[0])
    r, k, v = u[:, 0:w], u[:, w:2 * w], u[:, 2 * w:3 * w]
    xw = u[:, 3 * w:3 * w + RW_DECAY_RANK]
    xa = u[:, 3 * w + RW_DECAY_RANK:3 * w + RW_DECAY_RANK + RW_ICLR_RANK]
    xg = u[:, 3 * w + RW_DECAY_RANK + RW_ICLR_RANK:]
    ones_bd = e_ref[...]
    g_o[0] = _dot(_sigmoid(xg).astype(BF16), gup_ref[...])
    kk = k * kk_ref[...]
    kk = kk * lax.rsqrt(jnp.maximum(_seg_sum(kk * kk, ones_bd), 1e-24))
    txw = jnp.tanh(xw)
    kd_sum = None
    for d, (nb_o, wk_o) in enumerate(((nb0_o, wk0_o), (nb1_o, wk1_o))):
        x = -(w0_ref[d] + _dot_hi(txw, wup_ref[d]))
        softplus = jnp.maximum(x, 0.0) + jnp.log(1.0 + jnp.exp(-jnp.abs(x)))
        decay = jnp.exp(-jnp.exp(-softplus - 0.5))
        a = _sigmoid(a0_ref[d] + _dot_hi(xa, aup_ref[d]))
        kd = k * (1.0 + (a - 1.0) * ka_ref[...])
        _store_head_pairs(nb_o, -kk, kk * a)
        _store_head_pairs(wk_o, decay, kd)
        kd_sum = kd if kd_sum is None else kd_sum + kd
    _store_head_pairs(rv_o, r, v)
    bonus_o[0] = _seg_sum(r * kd_sum * rk_ref[...], ones_bd) * v


def rwkv_prep(u_ctx, u_lat, g_up, w_up, w0, a_up, a0, k_k, k_a, r_k, ones_bd):
    bsz, n_ctx, cols = u_ctx.shape
    n_lat = u_lat.shape[1]
    t_total = n_ctx + n_lat
    w = MIX_WIDTH
    tm = min(ROW_TILE, n_ctx)
    ctx_tiles = n_ctx // tm
    full = lambda shape: pl.BlockSpec(shape, lambda i, j: (0,) * len(shape))
    tile = pl.BlockSpec((1, tm, w), lambda i, j: (i, j, 0))
    pair = pl.BlockSpec((tm // WKV_STEPS, RW_HEADS * WKV_STEPS, LANES), lambda i, j: (j, i, 0))
    pair_shape = jax.ShapeDtypeStruct((t_total // WKV_STEPS, bsz * RW_HEADS * WKV_STEPS, LANES), F32)
    return pl.pallas_call(
        functools.partial(_k_rwprep, ctx_tiles=ctx_tiles),
        grid=(bsz, t_total // tm),
        in_specs=[pl.BlockSpec((1, tm, cols), lambda i, j: (i, jnp.minimum(j, ctx_tiles - 1), 0)),
                  pl.BlockSpec((1, tm, cols), lambda i, j: (i, jnp.maximum(j - ctx_tiles, 0), 0)),
                  full((RW_GATE_RANK, w)), full((2, RW_DECAY_RANK, w)), full((2, 1, w)),
                  full((2, RW_ICLR_RANK, w)), full((2, 1, w)), full((1, w)), full((1, w)), full((1, w)),
                  full((w, w))],
        out_specs=[tile, tile] + [pair] * 5,
        out_shape=[jax.ShapeDtypeStruct((bsz, t_total, w), F32)] * 2 + [pair_shape] * 5,
        compiler_params=_params("arbitrary", "arbitrary"),
        name="rwkv_prep",
    )(u_ctx, u_lat, g_up.astype(BF16), w_up, w0.reshape(2, 1, w), a_up, a0.reshape(2, 1, w),
      k_k.reshape(1, w), k_a.reshape(1, w), r_k.reshape(1, w), ones_bd)


def _k_wkv(rv0_ref, rv1_ref, nb0_ref, nb1_ref, wk0_ref, wk1_ref, y0_ref, y1_ref, s_ref, op_ref, yb_ref, *,
           steps):
    @pl.when(pl.program_id(0) == 0)
    def _():
        s_ref[...] = jnp.zeros_like(s_ref)

    chains = rv0_ref.shape[0] // steps
    r_op, v_op, a_op, b_op, w_op, k_op = range(6)

    def rows(t):
        return pl.ds(t, chains, stride=steps), pl.ds(steps - 1 - t, chains, stride=steps)

    for t in range(steps):
        fwd, bwd = rows(t)
        for j, (ref0, ref1) in enumerate(((rv0_ref, rv1_ref), (nb0_ref, nb1_ref), (wk0_ref, wk1_ref))):
            both = jnp.concatenate([ref0[fwd, :], ref1[bwd, :]], axis=0)
            op_ref[t, 2 * j:2 * j + 2] = both.T.reshape(2, RW_HEAD, 2 * chains)

    def advance(t, sa):
        nxt = jnp.minimum(t + 1, steps - 1)
        row = lambda op, i, slot=t: op_ref[slot, op, pl.ds(i, 1), :]
        vv = op_ref[t, v_op]

        def k_rows(ib, acc):
            y, sa_next = acc
            for u in range(WKV_K_UNROLL):
                i = ib * WKV_K_UNROLL + u
                s_new = s_ref[i] * row(w_op, i) + sa * row(b_op, i) + vv * row(k_op, i)
                s_ref[i] = s_new
                y = y + s_new * row(r_op, i)
                sa_next = sa_next + s_new * row(a_op, i, nxt)
            return y, sa_next
        zero = jnp.zeros(s_ref.shape[1:], F32)
        y, sa_next = lax.fori_loop(0, RW_HEAD // WKV_K_UNROLL, k_rows, (zero, zero))
        yb_ref[t] = y
        return sa_next

    sa = jnp.zeros(s_ref.shape[1:], F32)
    for i in range(RW_HEAD):
        sa = sa + s_ref[i] * op_ref[0, a_op, pl.ds(i, 1), :]
    lax.fori_loop(0, steps, advance, sa)

    for t in range(steps):
        y = yb_ref[t]
        yt = jnp.concatenate([y, y], axis=0).T
        fwd, bwd = rows(t)
        y0_ref[fwd, :] = yt[0:chains]
        y1_ref[bwd, :] = yt[chains:2 * chains]


def wkv_scan(rv, nb0, nb1, wk0, wk1, n_ctx):
    nb, rows, _ = rv.shape
    steps = WKV_STEPS
    chains = rows // steps
    nc = n_ctx // steps
    fwd = pl.BlockSpec((None, rows, LANES), lambda i: (i, 0, 0))
    bwd = pl.BlockSpec((None, rows, LANES), lambda i: (jnp.where(i < nc, nc - 1 - i, nb - 1 + nc - i), 0, 0))
    out = jax.ShapeDtypeStruct(rv.shape, F32)
    return pl.pallas_call(
        functools.partial(_k_wkv, steps=steps),
        grid=(nb,),
        in_specs=[fwd, bwd, fwd, bwd, fwd, bwd],
        out_specs=[fwd, bwd],
        out_shape=[out, out],
        scratch_shapes=[pltpu.VMEM((RW_HEAD, RW_HEAD, 2 * chains), F32),
                        pltpu.VMEM((steps, 6, RW_HEAD, 2 * chains), F32),
                        pltpu.VMEM((steps, RW_HEAD, 2 * chains), F32)],
        compiler_params=_params("arbitrary"),
        name="wkv_scan",
    )(rv, rv, nb0, nb1, wk0, wk1)


def rwkv_scan_branch(u_ctx, u_lat, p, ones_bd):
    g, bonus, rv, nb0, nb1, wk0, wk1 = rwkv_prep(
        u_ctx, u_lat, p["rw_g_up"], p["rw_w_up"], p["rw_w0"], p["rw_a_up"], p["rw_a0"], p["rw_k_k"],
        p["rw_k_a"], p["rw_r_k"].reshape(-1), ones_bd)
    y0, y1 = wkv_scan(rv, nb0, nb1, wk0, wk1, u_ctx.shape[1])
    return y0, y1, bonus, g


def _s5_block(i, n_ctx_blocks, n_blocks, reverse):
    if not reverse:
        return i
    return jnp.where(i < n_ctx_blocks, n_ctx_blocks - 1 - i, n_blocks - 1 + n_ctx_blocks - i)


def _k_s5(uc_ref, ul_ref, bre_ref, bim_ref, lre_ref, lim_ref, cre_ref, cim_ref, y_ref, hre_ref, him_ref,
          st_ref, tb_ref, *, steps, reverse, n_ctx_blocks, n_blocks):
    @pl.when(pl.program_id(0) == 0)
    def _():
        st_ref[...] = jnp.zeros_like(st_ref)

    bsz = uc_ref.shape[0]
    nq = MIX_WIDTH // LANES
    in_ctx = _s5_block(pl.program_id(0), n_ctx_blocks, n_blocks, reverse) < n_ctx_blocks
    for b in range(bsz):
        ub = jnp.where(in_ctx, uc_ref[b], ul_ref[b])
        for q in range(nq):
            tb_ref[q, pl.ds(b, steps, stride=bsz), :] = ub[:, q * LANES:(q + 1) * LANES]
    u = jnp.concatenate([tb_ref[q] for q in range(nq)], axis=1).astype(BF16)
    wc, hc = MIX_WIDTH // S5_DIAG_PARTS, S5_HID // S5_DIAG_PARTS
    for part in range(S5_DIAG_PARTS):
        ws, hs = slice(part * wc, (part + 1) * wc), slice(part * hc, (part + 1) * hc)
        hre_ref[:, hs] = _dot(u[:, ws], bre_ref[ws, hs])
        him_ref[:, hs] = _dot(u[:, ws], bim_ref[ws, hs])
    for ch in range(S5_HID // S5_LANE_CHUNK):
        cols = slice(ch * S5_LANE_CHUNK, (ch + 1) * S5_LANE_CHUNK)
        lre = lre_ref[:, cols]
        lim = lim_ref[:, cols]

        def step(i, carry):
            hr, hi = carry
            t = (steps - 1 - i) if reverse else i
            rows = pl.ds(pl.multiple_of(t * bsz, bsz), bsz)
            nr = lre * hr - lim * hi + hre_ref[rows, cols]
            ni = lre * hi + lim * hr + him_ref[rows, cols]
            hre_ref[rows, cols] = nr
            him_ref[rows, cols] = ni
            return nr, ni
        hr, hi = lax.fori_loop(0, steps, step, (st_ref[0, :, cols], st_ref[1, :, cols]))
        st_ref[0, :, cols] = hr
        st_ref[1, :, cols] = hi
    lanes_per_part = wc // LANES
    for part in range(S5_DIAG_PARTS):
        ws, hs = slice(part * wc, (part + 1) * wc), slice(part * hc, (part + 1) * hc)
        y = (_dot(hre_ref[:, hs].astype(BF16), cre_ref[hs, ws])
             - _dot(him_ref[:, hs].astype(BF16), cim_ref[hs, ws]))
        for q in range(lanes_per_part):
            tb_ref[part * lanes_per_part + q] = y[:, q * LANES:(q + 1) * LANES]
    for b in range(bsz):
        for q in range(nq):
            y_ref[b, :, q * LANES:(q + 1) * LANES] = tb_ref[q, pl.ds(b, steps, stride=bsz), :]


def s5_scan(z_ctx, z_lat, bre, bim, lre, lim, cre, cim, reverse):
    bsz, n_ctx, w = z_ctx.shape
    t_total = n_ctx + z_lat.shape[1]
    steps = S5_STEPS
    nc = n_ctx // steps
    nb = t_total // steps
    blk = functools.partial(_s5_block, n_ctx_blocks=nc, n_blocks=nb, reverse=reverse)
    full = lambda shape: pl.BlockSpec(shape, lambda i: (0,) * len(shape))
    return pl.pallas_call(
        functools.partial(_k_s5, steps=steps, reverse=reverse, n_ctx_blocks=nc, n_blocks=nb),
        grid=(nb,),
        in_specs=[pl.BlockSpec((bsz, steps, w), lambda i: (0, jnp.minimum(blk(i), nc - 1), 0)),
                  pl.BlockSpec((bsz, steps, w), lambda i: (0, jnp.maximum(blk(i) - nc, 0), 0)),
                  full((w, S5_HID)), full((w, S5_HID)), full((bsz, S5_HID)), full((bsz, S5_HID)),
                  full((S5_HID, w)), full((S5_HID, w))],
        out_specs=pl.BlockSpec((bsz, steps, w), lambda i: (0, blk(i), 0)),
        out_shape=jax.ShapeDtypeStruct((bsz, t_total, w), F32),
        scratch_shapes=[pltpu.VMEM((steps * bsz, S5_HID), F32), pltpu.VMEM((steps * bsz, S5_HID), F32),
                        pltpu.VMEM((2, bsz, S5_HID), F32), pltpu.VMEM((w // LANES, steps * bsz, LANES), F32)],
        compiler_params=_params("arbitrary"),
        name="s5_scan",
    )(z_ctx, z_lat, bre, bim, lre, lim, cre, cim)


def _k_s5disc(are_ref, aim_ref, ldt_ref, lre_ref, lim_ref, cr_ref, ci_ref):
    a_re, a_im = are_ref[...], aim_ref[...]
    dt = jnp.exp(ldt_ref[...])
    mag = jnp.exp(a_re * dt)
    lb_re, lb_im = mag * jnp.cos(a_im * dt), mag * jnp.sin(a_im * dt)
    den = a_re * a_re + a_im * a_im
    nr = lb_re - 1.0
    lre_ref[...] = lb_re
    lim_ref[...] = lb_im
    cr_ref[...] = (nr * a_re + lb_im * a_im) / den
    ci_ref[...] = (lb_im * a_re - nr * a_im) / den


def s5_operators(a_re, a_im, log_dt, b_re, b_im, c_re, c_im, bsz):
    g, p, h = S5_GROUPS, S5_STATE, S5_GROUP
    shp = jax.ShapeDtypeStruct((2 * g, p), F32)
    lb_re, lb_im, cr, ci = pl.pallas_call(
        _k_s5disc, out_shape=[shp] * 4, name="s5_discretise",
    )(a_re.reshape(2 * g, p), a_im.reshape(2 * g, p), jnp.broadcast_to(log_dt.reshape(2 * g, 1), (2 * g, p)))
    cr, ci = cr.reshape(2, g, p, 1), ci.reshape(2, g, p, 1)
    bb_re = cr * b_re[None] - ci * b_im[None]
    bb_im = cr * b_im[None] + ci * b_re[None]
    eye = jnp.eye(g, dtype=F32)
    bd_in = lambda m: jnp.einsum("dgph,gk->dghkp", m, eye).reshape(2, g * h, g * p).astype(BF16)
    bd_out = lambda m: jnp.einsum("ghp,gk->gpkh", m, eye).reshape(g * p, g * h).astype(BF16)
    lam = lambda m: jnp.broadcast_to(m.reshape(2, 1, g * p), (2, bsz, g * p))
    return bd_in(bb_re), bd_in(bb_im), lam(lb_re), lam(lb_im), bd_out(c_re), bd_out(c_im)


def s5_branch(z_ctx, z_lat, p):
    bsz = z_ctx.shape[0]
    bre, bim, lre, lim, cre, cim = s5_operators(p["s5_a_re"], p["s5_a_im"], p["s5_log_dt"], p["s5_b_re"],
                                                p["s5_b_im"], p["s5_c_re"], p["s5_c_im"], bsz)
    return (s5_scan(z_ctx, z_lat, bre[0], bim[0], lre[0], lim[0], cre, cim, False),
            s5_scan(z_ctx, z_lat, bre[1], bim[1], lre[1], lim[1], cre, cim, True))


def _k_merge(yhy_ref, rwy0_ref, rwy1_ref, rwb_ref, rwg_ref, s5y0_ref, s5y1_ref, s5u_ref, zg_ref, x_ref, g1_ref,
             e_ref, lng_ref, lnb_ref, s5d_ref, gluw_ref, glub_ref, wb_ref, wo_ref, o_ref):
    w = MIX_WIDTH
    ones_bd = e_ref[...]
    lane = lax.broadcasted_iota(jnp.int32, (x_ref.shape[1], LANES), 1)
    cols = []
    for q in range(RW_HEADS // 2):
        even = _load_head_sum(rwy0_ref, rwy1_ref, 2 * q)
        odd = _load_head_sum(rwy0_ref, rwy1_ref, 2 * q + 1)
        cols.append(jnp.where(lane < RW_HEAD, even, pltpu.roll(odd, RW_HEAD, 1)))
    y = jnp.concatenate(cols, axis=1) + rwb_ref[0]
    mu = _seg_sum(y, ones_bd) * (1.0 / RW_HEAD)
    yc = y - mu
    var = _seg_sum(yc * yc, ones_bd) * (1.0 / RW_HEAD)
    y_rw = (yc * lax.rsqrt(var + RW_LN_EPS) * lng_ref[...] + lnb_ref[...]) * rwg_ref[0]
    s = s5y0_ref[0] + s5y1_ref[0] + s5u_ref[0] * s5d_ref[...]
    s = 0.5 * s * (1.0 + jnp.tanh(math.sqrt(2.0 / math.pi) * (s + 0.044715 * (s * s * s))))
    lg = _dot(s.astype(BF16), gluw_ref[...]) + glub_ref[...]
    y_s5 = lg[:, 0:w] * _sigmoid(lg[:, w:2 * w])
    zg = zg_ref[0]
    d = D_MODEL
    m = (_sigmoid(zg[:, 0:d]) * _dot(yhy_ref[0].astype(BF16), wb_ref[0])
         + _sigmoid(zg[:, d:2 * d]) * _dot(y_rw.astype(BF16), wb_ref[1])
         + _sigmoid(zg[:, 2 * d:3 * d]) * _dot(y_s5.astype(BF16), wb_ref[2]))
    o_ref[0] = x_ref[0] + g1_ref[0] * _dot(m.astype(BF16), wo_ref[...])


def merge_residual(y_hy, rw, s5_y, t_off, s5_u, zg, x, g1, p, ones_bd):
    bsz, n, d = x.shape
    w = MIX_WIDTH
    tm = min(ROW_TILE, n)
    assert t_off % tm == 0
    off = t_off // tm
    tile = lambda c: pl.BlockSpec((1, tm, c), lambda i, j: (i, j, 0))
    seq_tile = pl.BlockSpec((1, tm, w), lambda i, j: (i, off + j, 0))
    chain_tile = pl.BlockSpec((tm // WKV_STEPS, RW_HEADS * WKV_STEPS, LANES), lambda i, j: (off + j, i, 0))
    full = lambda shape: pl.BlockSpec(shape, lambda i, j: (0,) * len(shape))
    rw_y0, rw_y1, rw_bonus, rw_g = rw
    return pl.pallas_call(
        _k_merge,
        grid=(bsz, n // tm),
        in_specs=[tile(w), chain_tile, chain_tile, seq_tile, seq_tile, seq_tile, seq_tile, tile(w),
                  tile(N_BRANCH * d), tile(d), pl.BlockSpec((1, 1, d), lambda i, j: (i, 0, 0)),
                  full((w, w)), full((1, w)), full((1, w)), full((1, w)), full((w, 2 * w)), full((1, 2 * w)),
                  full((N_BRANCH, w, d)), full((d, d))],
        out_specs=tile(d),
        out_shape=jax.ShapeDtypeStruct(x.shape, F32),
        compiler_params=_params("arbitrary", "arbitrary"),
        name="merge_residual",
    )(y_hy, rw_y0, rw_y1, rw_bonus, rw_g, s5_y[0], s5_y[1], s5_u, zg, x, g1, ones_bd, p["rw_ln_g"].reshape(1, w),
      p["rw_ln_b"].reshape(1, w), p["s5_d"].reshape(1, w), p["s5_glu_w"].astype(BF16),
      p["s5_glu_b"].reshape(1, 2 * w), p["w_branch"].astype(BF16), p["w_out"].astype(BF16))


def _k_router(x_ref, sh_ref, sc_ref, g_ref, rw_ref, rb_ref, h_o, idx_o, gate_o, cnt_o):
    @pl.when((pl.program_id(0) == 0) & (pl.program_id(1) == 0))
    def _():
        cnt_o[...] = jnp.zeros_like(cnt_o)

    h = _rms_modulate(x_ref[0], g_ref[...], sh_ref[0], sc_ref[0])
    h_o[0] = h.astype(BF16)
    logits = _dot_hi(h, rw_ref[...]) + rb_ref[...]
    lane = lax.broadcasted_iota(jnp.int32, logits.shape, 1)
    vals, idxs = [], []
    for _ in range(TOP_K):
        m = jnp.max(logits, axis=-1, keepdims=True)
        idx = jnp.min(jnp.where(logits == m, lane, LANES), axis=-1, keepdims=True)
        vals.append(m)
        idxs.append(idx)
        logits = jnp.where(lane == idx, -jnp.inf, logits)
    exps = [jnp.exp(v - vals[0]) for v in vals]
    inv = 1.0 / (exps[0] + exps[1] + exps[2] + exps[3])
    idx_out = jnp.zeros(logits.shape, jnp.int32)
    gate_out = jnp.zeros(logits.shape, F32)
    picked = jnp.zeros(logits.shape, F32)
    for j in range(TOP_K):
        idx_out = jnp.where(lane == j, idxs[j], idx_out)
        gate_out = jnp.where(lane == j, exps[j] * inv, gate_out)
        picked = picked + jnp.where(lane == idxs[j], 1.0, 0.0)
    idx_o[0] = idx_out
    gate_o[0] = gate_out
    cnt_o[...] = cnt_o[...] + jnp.sum(picked, axis=0, keepdims=True)


def moe_route(x, shift, scale, g, router_w, router_b):
    bsz, n, d = x.shape
    tm = min(ROW_TILE, n)
    rw = jnp.pad(router_w, ((0, 0), (0, LANES - N_EXPERTS)))
    rb = jnp.pad(router_b, (0, LANES - N_EXPERTS), constant_values=-jnp.inf).reshape(1, LANES)
    tile = lambda c: pl.BlockSpec((1, tm, c), lambda i, j: (i, j, 0))
    vec = pl.BlockSpec((1, 1, d), lambda i, j: (i, 0, 0))
    full = lambda shape: pl.BlockSpec(shape, lambda i, j: (0,) * len(shape))
    return pl.pallas_call(
        _k_router,
        grid=(bsz, n // tm),
        in_specs=[tile(d), vec, vec, full((1, d)), full((d, LANES)), full((1, LANES))],
        out_specs=[tile(d), tile(LANES), tile(LANES), full((SUBLANES, LANES))],
        out_shape=[jax.ShapeDtypeStruct((bsz, n, d), BF16), jax.ShapeDtypeStruct((bsz, n, LANES), jnp.int32),
                   jax.ShapeDtypeStruct((bsz, n, LANES), F32), jax.ShapeDtypeStruct((SUBLANES, LANES), F32)],
        compiler_params=_params("arbitrary", "arbitrary"),
        name="moe_route",
    )(x, shift, scale, g.reshape(1, d), rw, rb)


def _k_expert(blk_e_ref, used_ref, xs_ref, w1_ref, b1_ref, w2_ref, b2_ref, o_ref, w1b_ref, w2b_ref):
    i = pl.program_id(0)

    @pl.when(i < used_ref[0])
    def _():
        @pl.when((i == 0) | (blk_e_ref[i] != blk_e_ref[jnp.maximum(i - 1, 0)]))
        def _():
            w1b_ref[...] = w1_ref[0].astype(BF16)
            w2b_ref[...] = w2_ref[0].astype(BF16)

        hid = _dot(xs_ref[...], w1b_ref[...]) + b1_ref[0]
        gl = jnp.minimum(hid[:, 0:D_EXPERT], SWIGLU_LIMIT)
        up = jnp.clip(hid[:, D_EXPERT:2 * D_EXPERT], -SWIGLU_LIMIT, SWIGLU_LIMIT)
        act = (up + 1.0) * gl * _sigmoid(SWIGLU_ALPHA * gl)
        o_ref[...] = (_dot(act.astype(BF16), w2b_ref[...]) + b2_ref[0]).astype(o_ref.dtype)

    @pl.when(i >= used_ref[0])
    def _():
        o_ref[...] = jnp.zeros_like(o_ref)


def expert_ffn(xs, blk_e, n_used, w1, b1, w2, b2):
    n_slots, d = xs.shape
    n_blocks = n_slots // MOE_ROWS
    grid_spec = pltpu.PrefetchScalarGridSpec(
        num_scalar_prefetch=2,
        grid=(n_blocks,),
        in_specs=[pl.BlockSpec((MOE_ROWS, d), lambda i, e, u: (i, 0)),
                  pl.BlockSpec((1, d, 2 * D_EXPERT), lambda i, e, u: (e[i], 0, 0)),
                  pl.BlockSpec((1, 1, 2 * D_EXPERT), lambda i, e, u: (e[i], 0, 0)),
                  pl.BlockSpec((1, D_EXPERT, d), lambda i, e, u: (e[i], 0, 0)),
                  pl.BlockSpec((1, 1, d), lambda i, e, u: (e[i], 0, 0))],
        out_specs=pl.BlockSpec((MOE_ROWS, d), lambda i, e, u: (i, 0)),
        scratch_shapes=[pltpu.VMEM((d, 2 * D_EXPERT), BF16), pltpu.VMEM((D_EXPERT, d), BF16)],
    )
    return pl.pallas_call(
        _k_expert,
        grid_spec=grid_spec,
        out_shape=jax.ShapeDtypeStruct((n_slots, d), BF16),
        compiler_params=_params("arbitrary"),
        name="expert_ffn",
    )(blk_e, n_used.reshape(1).astype(jnp.int32), xs, w1, b1.reshape(b1.shape[0], 1, -1), w2,
      b2.reshape(b2.shape[0], 1, -1))


def _k_offset_add(tab_ref, e_ref, v_ref, o_ref):
    e = e_ref[...]
    acc = v_ref[...]
    for j in range(N_EXPERTS):
        acc = acc + jnp.where(e == j, tab_ref[j], 0)
    o_ref[...] = acc


def expert_offset_add(flat_e, values, table):
    n = flat_e.shape[0]
    rows = n // LANES
    return pl.pallas_call(
        _k_offset_add,
        in_specs=[pl.BlockSpec(memory_space=pltpu.SMEM), pl.BlockSpec(memory_space=pltpu.VMEM),
                  pl.BlockSpec(memory_space=pltpu.VMEM)],
        out_specs=pl.BlockSpec(memory_space=pltpu.VMEM),
        out_shape=jax.ShapeDtypeStruct((rows, LANES), jnp.int32),
        name="expert_offset_add",
    )(table.astype(jnp.int32), flat_e.reshape(rows, LANES), values.reshape(rows, LANES)).reshape(n)


def moe_ffn(x, shift, scale, g, router_w, router_b, w1, b1, w2, b2, expert_base=0):
    bsz, n, d = x.shape
    h, idx, gate, cnt = moe_route(x, shift, scale, g, router_w, router_b)
    n_tok = bsz * n
    n_asg = n_tok * TOP_K
    flat_e = idx[..., :TOP_K].reshape(-1)
    gates = gate[..., :TOP_K].reshape(n_tok, TOP_K)
    order = jnp.argsort(flat_e).astype(jnp.int32)
    rank_of = jnp.argsort(order).astype(jnp.int32)
    counts = cnt[0, :N_EXPERTS].astype(jnp.int32)
    grp_start = jnp.cumsum(counts) - counts
    padded = (counts + MOE_ROWS - 1) // MOE_ROWS * MOE_ROWS
    pad_end = jnp.cumsum(padded)
    pad_start = pad_end - padded
    n_blocks = -(-n_asg // MOE_ROWS) + N_EXPERTS
    n_slots = n_blocks * MOE_ROWS
    blk_first = jnp.arange(n_blocks, dtype=jnp.int32) * MOE_ROWS
    blk_e = jnp.minimum(jnp.sum(pad_end[None, :] <= blk_first[:, None], axis=1), N_EXPERTS - 1).astype(jnp.int32)
    slot_rank = (blk_first - pad_start[blk_e])[:, None] + jnp.arange(MOE_ROWS, dtype=jnp.int32)[None, :]
    slot_used = (slot_rank < counts[blk_e][:, None]).reshape(-1)
    slot_src = jnp.where(slot_used, (grp_start[blk_e][:, None] + slot_rank).reshape(-1), 0)
    slot_asg = order[slot_src]
    slot_tok = jnp.where(slot_used, slot_asg // TOP_K, jnp.arange(n_slots, dtype=jnp.int32) % n_tok)
    slot_of = expert_offset_add(flat_e, rank_of, pad_start - grp_start)
    xs = h.reshape(n_tok, d)[slot_tok]
    ys = expert_ffn(xs, blk_e + expert_base, pad_end[-1] // MOE_ROWS, w1, b1, w2, b2)
    slot_of = slot_of.reshape(n_tok, TOP_K)
    out = sum(ys[slot_of[:, j]].astype(F32) * gates[:, j:j + 1] for j in range(TOP_K))
    return out.reshape(bsz, n, d)


def _k_rmsnorm(x_ref, g_ref, o_ref):
    x = x_ref[0]
    ms = jnp.mean(x * x, axis=-1, keepdims=True)
    o_ref[0] = x * lax.rsqrt(ms + NORM_EPS) * g_ref[...]


def final_rmsnorm(x, g):
    b, n, d = x.shape
    tm = min(2 * ROW_TILE, n)
    return pl.pallas_call(
        _k_rmsnorm,
        grid=(b, n // tm),
        in_specs=[pl.BlockSpec((1, tm, d), lambda i, j: (i, j, 0)), pl.BlockSpec((1, d), lambda i, j: (0, 0))],
        out_specs=pl.BlockSpec((1, tm, d), lambda i, j: (i, j, 0)),
        out_shape=jax.ShapeDtypeStruct(x.shape, F32),
        compiler_params=_params("arbitrary", "arbitrary"),
        name="final_rmsnorm",
    )(x, g.reshape(1, d))


@functools.lru_cache(maxsize=None)
def _grid_pos_embed(n_tokens):
    rows = n_tokens // GRID_W
    row_id, col_id = np.meshgrid(np.arange(rows), np.arange(GRID_W), indexing="ij")
    quarter = D_MODEL // 4
    omega = (1.0 / (10000.0 ** (np.arange(quarter, dtype=np.float32) / quarter))).astype(np.float32)

    def enc(pos):
        ang = pos.reshape(-1)[:, None].astype(np.float32) * omega
        return np.concatenate([np.sin(ang), np.cos(ang)], axis=-1)

    return np.concatenate([enc(row_id), enc(col_id)], axis=-1).astype(np.float32)


_LAYER_KEYS = ("ada_w", "ada_b", "norm1_g", "norm2_g", "w_in", "hy_conv_w", "hy_conv_b", "hy_w1", "hy_b1",
               "hy_f1", "hy_w2", "hy_b2", "hy_f2", "hy_w3", "hy_b3", "hy_bias", "rw_conv_w", "rw_conv_b",
               "rw_w_up", "rw_w0", "rw_a_up", "rw_a0", "rw_g_up", "rw_k_k", "rw_k_a", "rw_r_k", "rw_ln_g",
               "rw_ln_b", "s5_a_re", "s5_a_im", "s5_log_dt", "s5_b_re", "s5_b_im", "s5_c_re", "s5_c_im",
               "s5_d", "s5_glu_w", "s5_glu_b", "w_branch", "w_out", "router_w", "router_b", "moe_w1",
               "moe_b1", "moe_w2", "moe_b2")


def _token_mixer(x, xc, mod, mod_c, p, need_ctx, ones_bd):
    sh1, sc1, g1 = mod
    csh1, csc1, cg1 = mod_c
    w_in = p["w_in"].astype(BF16)
    c0, c1, c2 = HY_COLS, HY_COLS + RW_COLS, HY_COLS + RW_COLS + MIX_WIDTH
    proj = lambda t, s, c, lo, hi: norm_mod_matmul(t, s, c, p["norm1_g"], w_in[:, lo:hi])
    hy_args = (p["hy_w1"], p["hy_b1"], p["hy_f1"], p["hy_w2"], p["hy_b2"], p["hy_f2"], p["hy_w3"], p["hy_b3"],
               p["hy_bias"])

    z_hy = proj(x, sh1, sc1, 0, c0)
    y_hy = hyena_branch(z_hy, p["hy_conv_w"], p["hy_conv_b"], hyena_spectra(x.shape[1], *hy_args))
    u_rw = short_conv(proj(x, sh1, sc1, c0, c1), p["rw_conv_w"], p["rw_conv_b"])
    uc_rw = short_conv(proj(xc, csh1, csc1, c0, c1), p["rw_conv_w"], p["rw_conv_b"])
    rw = rwkv_scan_branch(uc_rw, u_rw, p, ones_bd)
    n_ctx = xc.shape[1]
    z_s5 = proj(x, sh1, sc1, c1, c2)
    zc_s5 = proj(xc, csh1, csc1, c1, c2)
    s5_y = s5_branch(zc_s5, z_s5, p)
    zg = proj(x, sh1, sc1, c2, c2 + N_BRANCH * D_MODEL)
    x_new = merge_residual(y_hy, rw, s5_y, n_ctx, z_s5, zg, x, g1, p, ones_bd)
    if not need_ctx:
        return x_new, None
    zc_hy = proj(xc, csh1, csc1, 0, c0)
    yc_hy = hyena_branch(zc_hy, p["hy_conv_w"], p["hy_conv_b"], hyena_spectra(xc.shape[1], *hy_args))
    zcg = proj(xc, csh1, csc1, c2, c2 + N_BRANCH * D_MODEL)
    xc_new = merge_residual(yc_hy, rw, s5_y, 0, zc_s5, zcg, xc, cg1, p, ones_bd)
    return x_new, xc_new


def kernel(x, c, ctx, c_ctx, ada_w, ada_b, norm1_g, norm2_g, w_in, hy_conv_w, hy_conv_b, hy_w1, hy_b1, hy_f1, hy_w2, hy_b2, hy_f2, hy_w3, hy_b3, hy_bias, rw_conv_w, rw_conv_b, rw_w_up, rw_w0, rw_a_up, rw_a0, rw_g_up, rw_k_k, rw_k_a, rw_r_k, rw_ln_g, rw_ln_b, s5_a_re, s5_a_im, s5_log_dt, s5_b_re, s5_b_im, s5_c_re, s5_c_im, s5_d, s5_glu_w, s5_glu_b, w_branch, w_out, router_w, router_b, moe_w1, moe_b1, moe_w2, moe_b2, final_g):
    stacked = dict(zip(_LAYER_KEYS, (ada_w, ada_b, norm1_g, norm2_g, w_in, hy_conv_w, hy_conv_b, hy_w1, hy_b1,
                                     hy_f1, hy_w2, hy_b2, hy_f2, hy_w3, hy_b3, hy_bias, rw_conv_w, rw_conv_b,
                                     rw_w_up, rw_w0, rw_a_up, rw_a0, rw_g_up, rw_k_k, rw_k_a, rw_r_k, rw_ln_g,
                                     rw_ln_b, s5_a_re, s5_a_im, s5_log_dt, s5_b_re, s5_b_im, s5_c_re, s5_c_im,
                                     s5_d, s5_glu_w, s5_glu_b, w_branch, w_out, router_w, router_b, moe_w1,
                                     moe_b1, moe_w2, moe_b2)))
    bsz, n, d = x.shape
    depth = ada_w.shape[0]
    x = add_pos(x, jnp.asarray(_grid_pos_embed(n)))
    xc = ctx
    ones_bd = jnp.asarray(np.kron(np.eye(RW_HEADS), np.ones((RW_HEAD, RW_HEAD))), BF16)
    cvec = jnp.zeros((2 * SUBLANES, d), F32).at[:bsz].set(c).at[bsz].set(c_ctx)
    for l in range(depth):
        p = {k: v[l] for k, v in stacked.items()}
        need_ctx = l < depth - 1
        ada = ada_proj(cvec, p["ada_w"], p["ada_b"])
        lat = [t.reshape(bsz, 1, d) for t in jnp.split(ada[:bsz], 6, axis=-1)]
        cx = [jnp.broadcast_to(t.reshape(1, 1, d), (bsz, 1, d)) for t in jnp.split(ada[bsz], 6, axis=-1)]
        x, xc_new = _token_mixer(x, xc, lat[0:3], cx[0:3], p, need_ctx, ones_bd)
        stack = lambda t: t.reshape((depth * N_EXPERTS,) + t.shape[2:])
        moe = functools.partial(moe_ffn, g=p["norm2_g"], router_w=p["router_w"], router_b=p["router_b"],
                                w1=stack(moe_w1), b1=stack(moe_b1), w2=stack(moe_w2), b2=stack(moe_b2),
                                expert_base=l * N_EXPERTS)
        x = x + lat[5] * moe(x, lat[3], lat[4])
        if need_ctx:
            xc = xc_new + cx[5] * moe(xc_new, cx[3], cx[4])
    return final_rmsnorm(x, final_g)
```

```python
import functools
import math

import numpy as np
import jax
import jax.numpy as jnp
from jax import lax
from jax.experimental import pallas as pl
from jax.experimental.pallas import tpu as pltpu

F32 = jnp.float32
BF16 = jnp.bfloat16

D_MODEL = 1024
GRID_W = 64
SHORT_CONV = 3
NORM_EPS = 1e-6
N_BRANCH = 3
MIX_WIDTH = 512

HY_ORDER = 2
HY_BANDS = 16
HY_DECAY_TARGET = 1e-2
HY_FAST_PCT = 0.3
HY_SLOW_PCT = 1.5
HY_COLS = (HY_ORDER + 1) * MIX_WIDTH

RW_HEAD = 64
RW_HEADS = MIX_WIDTH // RW_HEAD
RW_DECAY_RANK = 64
RW_ICLR_RANK = 64
RW_GATE_RANK = 128
RW_LN_EPS = 64e-5
RW_COLS = 3 * MIX_WIDTH + RW_DECAY_RANK + RW_ICLR_RANK + RW_GATE_RANK

S5_GROUP = 16
S5_GROUPS = MIX_WIDTH // S5_GROUP
S5_STATE = 64
S5_HID = S5_GROUPS * S5_STATE

N_EXPERTS = 32
TOP_K = 4
D_EXPERT = 1024
SWIGLU_LIMIT = 7.0
SWIGLU_ALPHA = 1.702

LANES = 128
SUBLANES = 8
VMEM_LIMIT_BYTES = 56 * 1024 * 1024

ROW_TILE = 256
MOE_ROWS = 512
WKV_STEPS = 16
WKV_K_UNROLL = 16
S5_STEPS = 64
S5_LANE_CHUNK = 512
S5_DIAG_PARTS = 2


def _params(*sem):
    return pltpu.CompilerParams(dimension_semantics=sem, vmem_limit_bytes=VMEM_LIMIT_BYTES)


def _dot(a, b):
    return jnp.dot(a, b, preferred_element_type=F32)


def _split3(x):
    hi = x.astype(BF16)
    r1 = x - hi.astype(F32)
    mid = r1.astype(BF16)
    lo = (r1 - mid.astype(F32)).astype(BF16)
    return hi, mid, lo


def _dot_hi(a, b):
    a0, a1, a2 = _split3(a)
    b0, b1, b2 = _split3(b)
    return (_dot(a0, b0) + (_dot(a0, b1) + _dot(a1, b0))
            + (_dot(a1, b1) + _dot(a0, b2) + _dot(a2, b0)))


def _seg_sum(x, ones_bd):
    hi, mid, lo = _split3(x)
    return _dot(hi, ones_bd) + _dot(mid, ones_bd) + _dot(lo, ones_bd)


def _sigmoid(x):
    return 1.0 / (1.0 + jnp.exp(-x))


def _rms_modulate(x, g, shift, scale):
    ms = jnp.mean(x * x, axis=-1, keepdims=True)
    y = x * lax.rsqrt(ms + NORM_EPS) * g
    return y * (1.0 + scale) + shift


def _k_ada(c_ref, w_ref, b_ref, o_ref):
    c = c_ref[...]
    s = c * _sigmoid(c)
    o_ref[...] = _dot_hi(s, w_ref[...]) + b_ref[...]


def ada_proj(cvec, w, b):
    rows, d = cvec.shape
    n = w.shape[1]
    tn = 1536
    return pl.pallas_call(
        _k_ada,
        grid=(n // tn,),
        in_specs=[pl.BlockSpec((rows, d), lambda j: (0, 0)),
                  pl.BlockSpec((d, tn), lambda j: (0, j)),
                  pl.BlockSpec((1, tn), lambda j: (0, j))],
        out_specs=pl.BlockSpec((rows, tn), lambda j: (0, j)),
        out_shape=jax.ShapeDtypeStruct((rows, n), F32),
        compiler_params=_params("arbitrary"),
        name="ada_proj",
    )(cvec, w, b.reshape(1, n))


def _k_add_pos(x_ref, p_ref, o_ref):
    o_ref[0] = x_ref[0] + p_ref[...]


def add_pos(x, pos):
    b, n, d = x.shape
    tm = min(ROW_TILE * 2, n)
    return pl.pallas_call(
        _k_add_pos,
        grid=(b, n // tm),
        in_specs=[pl.BlockSpec((1, tm, d), lambda i, j: (i, j, 0)),
                  pl.BlockSpec((tm, d), lambda i, j: (j, 0))],
        out_specs=pl.BlockSpec((1, tm, d), lambda i, j: (i, j, 0)),
        out_shape=jax.ShapeDtypeStruct(x.shape, F32),
        compiler_params=_params("arbitrary", "arbitrary"),
        name="add_pos",
    )(x, pos)


def _k_norm_mm(x_ref, sh_ref, sc_ref, g_ref, w_ref, o_ref, h_ref):
    @pl.when(pl.program_id(2) == 0)
    def _():
        h = _rms_modulate(x_ref[0], g_ref[...], sh_ref[0], sc_ref[0])
        h_ref[...] = h.astype(BF16)

    o_ref[0] = _dot(h_ref[...], w_ref[...])


def norm_mod_matmul(x, shift, scale, g, w):
    b, n, d = x.shape
    ncol = w.shape[1]
    tm = min(4 * ROW_TILE, n)
    tn = ncol if ncol <= 1792 else 1536
    return pl.pallas_call(
        _k_norm_mm,
        grid=(b, n // tm, ncol // tn),
        in_specs=[pl.BlockSpec((1, tm, d), lambda i, j, k: (i, j, 0)),
                  pl.BlockSpec((1, 1, d), lambda i, j, k: (i, 0, 0)),
                  pl.BlockSpec((1, 1, d), lambda i, j, k: (i, 0, 0)),
                  pl.BlockSpec((1, d), lambda i, j, k: (0, 0)),
                  pl.BlockSpec((d, tn), lambda i, j, k: (0, k))],
        out_specs=pl.BlockSpec((1, tm, tn), lambda i, j, k: (i, j, k)),
        out_shape=jax.ShapeDtypeStruct((b, n, ncol), F32),
        scratch_shapes=[pltpu.VMEM((tm, d), BF16)],
        compiler_params=_params("arbitrary", "arbitrary", "arbitrary"),
        name="norm_mod_matmul",
    )(x, shift, scale, g.reshape(1, d), w)


def _short_conv_val(z, w, b):
    n = z.shape[0]
    row = lax.broadcasted_iota(jnp.int32, z.shape, 0)
    zm = jnp.where(row == 0, 0.0, pltpu.roll(z, 1, 0))
    zp = jnp.where(row == n - 1, 0.0, pltpu.roll(z, n - 1, 0))
    return zm * w[0:1] + z * w[1:2] + zp * w[2:3] + b


def _k_sconv(z_ref, w_ref, b_ref, o_ref):
    o_ref[0] = _short_conv_val(z_ref[0], w_ref[...], b_ref[...])


def short_conv(z, w, b):
    bsz, n, c = z.shape
    tc = 256 if c % 256 == 0 else LANES
    return pl.pallas_call(
        _k_sconv,
        grid=(bsz, c // tc),
        in_specs=[pl.BlockSpec((1, n, tc), lambda i, j: (i, 0, j)),
                  pl.BlockSpec((SHORT_CONV, tc), lambda i, j: (0, j)),
                  pl.BlockSpec((1, tc), lambda i, j: (0, j))],
        out_specs=pl.BlockSpec((1, n, tc), lambda i, j: (i, 0, j)),
        out_shape=jax.ShapeDtypeStruct(z.shape, F32),
        compiler_params=_params("arbitrary", "arbitrary"),
        name="short_conv",
    )(z, w, b.reshape(1, c))


def _fft_split(n):
    total = 2 * n
    bits = total.bit_length() - 1
    assert 1 << bits == total
    n1 = 1 << ((bits + 1) // 2)
    n1 = max(n1, 2 * SUBLANES)
    return n1, total // n1


@functools.lru_cache(maxsize=None)
def _dft_mats(n):
    n1, n2 = _fft_split(n)
    total = n1 * n2
    k1 = np.arange(n1)[:, None]
    m1 = np.arange(n1)[None, :]
    m2 = np.arange(n2)[:, None, None]
    ang = -2.0 * np.pi * ((n2 * k1 * m1)[None] + m2 * k1[None]) / total
    fa = np.concatenate([np.cos(ang), np.sin(ang)], axis=1)
    k2 = np.arange(n2)[:, None]
    mm = np.arange(n2)[None, :]
    gang = -2.0 * np.pi * k2 * mm / n2
    gr, gi = np.cos(gang), np.sin(gang)
    fb = np.block([[gr, -gi], [gi, gr]])
    fc = np.block([[gr, gi], [-gi, gr]])
    m1d = np.arange(n1 // 2)[:, None]
    k1d = np.arange(n1)[None, :]
    dang = 2.0 * np.pi * ((n2 * m1d * k1d)[None] + m2 * k1d[None]) / total
    fd = np.concatenate([np.cos(dang), -np.sin(dang)], axis=2) / total
    return (fa.astype(np.float32), fb.astype(np.float32), fc.astype(np.float32), fd.astype(np.float32))


def _time_pitch(n2):
    return n2 + SUBLANES


def _spec_pitch(n2):
    return 2 * n2 + SUBLANES


def _fft_stage_a(src_ref, y_ref, fa_ref, n1, n2, k_rows):
    tp, sp = _time_pitch(n2), _spec_pitch(n2)

    def body(m2, c):
        xs = src_ref[pl.ds(m2, k_rows, stride=tp), :].astype(BF16)
        res = _dot(fa_ref[m2, :, 0:k_rows], xs)
        y_ref[pl.ds(m2, n1, stride=sp), :] = res[0:n1]
        y_ref[pl.ds(n2 + m2, n1, stride=sp), :] = res[n1:2 * n1]
        return c
    lax.fori_loop(0, n2, body, 0, unroll=8)


def _long_conv(buf_ref, y_ref, h_ref, fa_ref, fb_ref, fc_ref, fd_ref, n1, n2):
    tp, sp = _time_pitch(n2), _spec_pitch(n2)
    _fft_stage_a(buf_ref, y_ref, fa_ref, n1, n2, n1 // 2)

    def body_k1(k1, c):
        rows = pl.ds(pl.multiple_of(k1 * sp, SUBLANES), 2 * n2)
        z = _dot(fb_ref[...], y_ref[rows, :].astype(BF16))
        zr, zi = z[0:n2], z[n2:2 * n2]
        h = h_ref[k1]
        hr, hi = h[0:n2], h[n2:2 * n2]
        p = jnp.concatenate([zr * hr - zi * hi, zr * hi + zi * hr], axis=0)
        y_ref[rows, :] = _dot(fc_ref[...], p.astype(BF16))
        return c
    lax.fori_loop(0, n1, body_k1, 0, unroll=16)

    def body_m2(m2, c):
        qr = y_ref[pl.ds(m2, n1, stride=sp), :]
        qi = y_ref[pl.ds(n2 + m2, n1, stride=sp), :]
        q = jnp.concatenate([qr, qi], axis=0).astype(BF16)
        buf_ref[pl.ds(m2, n1 // 2, stride=tp), :] = _dot(fd_ref[m2], q)
        return c
    lax.fori_loop(0, n2, body_m2, 0, unroll=8)


def _store_time_blocks(buf_ref, val, n2):
    tp = _time_pitch(n2)
    for m1 in range(val.shape[0] // n2):
        buf_ref[m1 * tp:m1 * tp + n2, :] = val[m1 * n2:(m1 + 1) * n2]


def _load_time_blocks(buf_ref, n, n2):
    tp = _time_pitch(n2)
    return jnp.concatenate([buf_ref[m1 * tp:m1 * tp + n2, :] for m1 in range(n // n2)], axis=0)


def _k_hyena(x1_ref, x2_ref, v_ref, w1_ref, w2_ref, wv_ref, b1_ref, b2_ref, bv_ref, h1_ref, h2_ref,
             fa_ref, fb_ref, fc_ref, fd_ref, o_ref, buf_ref, y_ref, *, n1, n2):
    n = x1_ref.shape[1]
    conv = functools.partial(_long_conv, buf_ref, y_ref, fa_ref=fa_ref, fb_ref=fb_ref, fc_ref=fc_ref,
                             fd_ref=fd_ref, n1=n1, n2=n2)
    _store_time_blocks(buf_ref, _short_conv_val(v_ref[0], wv_ref[...], bv_ref[...]), n2)
    conv(h1_ref)
    gated = _short_conv_val(x1_ref[0], w1_ref[...], b1_ref[...]) * _load_time_blocks(buf_ref, n, n2)
    _store_time_blocks(buf_ref, gated, n2)
    conv(h2_ref)
    o_ref[0] = _short_conv_val(x2_ref[0], w2_ref[...], b2_ref[...]) * _load_time_blocks(buf_ref, n, n2)


def _k_fspec(f_ref, fa_ref, fb_ref, o_ref, y_ref, *, n1, n2):
    sp = _spec_pitch(n2)
    _fft_stage_a(f_ref, y_ref, fa_ref, n1, n2, n1)

    def body_k1(k1, c):
        src = pl.ds(pl.multiple_of(k1 * sp, SUBLANES), 2 * n2)
        dst = pl.ds(pl.multiple_of(k1 * 2 * n2, 2 * n2), 2 * n2)
        o_ref[dst, :] = _dot(fb_ref[...], y_ref[src, :].astype(BF16))
        return c
    lax.fori_loop(0, n1, body_k1, 0, unroll=2)


def filter_spectrum(filt, n):
    n1, n2 = _fft_split(n)
    total, c = filt.shape
    tp, sp = _time_pitch(n2), _spec_pitch(n2)
    fa, fb, _, _ = _dft_mats(n)
    blocks = jnp.pad(filt.reshape(n1, n2, c), ((0, 0), (0, tp - n2), (0, 0))).reshape(n1 * tp, c)
    out = pl.pallas_call(
        functools.partial(_k_fspec, n1=n1, n2=n2),
        grid=(c // LANES,),
        in_specs=[pl.BlockSpec((n1 * tp, LANES), lambda j: (0, j)),
                  pl.BlockSpec((n2, 2 * n1, n1), lambda j: (0, 0, 0)),
                  pl.BlockSpec((2 * n2, 2 * n2), lambda j: (0, 0))],
        out_specs=pl.BlockSpec((total * 2, LANES), lambda j: (0, j)),
        out_shape=jax.ShapeDtypeStruct((2 * total, c), F32),
        scratch_shapes=[pltpu.VMEM((n1 * sp, LANES), F32)],
        compiler_params=_params("arbitrary"),
        name="filter_spectrum",
    )(blocks, jnp.asarray(fa, BF16), jnp.asarray(fb, BF16))
    return out.reshape(n1, 2 * n2, c)


def _k_filter_mlp(feat_ref, win_ref, w1_ref, b1_ref, f1_ref, w2_ref, b2_ref, f2_ref, w3_ref, b3_ref, o_ref):
    h = jnp.sin(f1_ref[...] * (_dot_hi(feat_ref[...], w1_ref[...]) + b1_ref[...]))
    h = jnp.sin(f2_ref[...] * (_dot_hi(h, w2_ref[...]) + b2_ref[...]))
    o_ref[...] = (_dot_hi(h, w3_ref[...]) + b3_ref[...]) * win_ref[...]


@functools.lru_cache(maxsize=None)
def _filter_consts(n):
    t = np.linspace(0.0, 1.0, n, dtype=np.float32)[:, None]
    bands = np.linspace(1e-4, HY_BANDS - 1, HY_BANDS, dtype=np.float32)
    ang = (np.float32(2 * math.pi) * np.arange(n, dtype=np.float32) / np.float32(n))[:, None] * bands
    feats = np.concatenate([t, np.cos(ang), -np.sin(ang)], axis=-1).astype(np.float32)
    pad = (-feats.shape[1]) % SUBLANES
    feats = np.pad(feats, ((0, 0), (0, pad)))
    deltas = np.abs(np.linspace(math.log(HY_DECAY_TARGET) / HY_SLOW_PCT,
                                math.log(HY_DECAY_TARGET) / HY_FAST_PCT, MIX_WIDTH, dtype=np.float32))
    window = np.exp(-t * deltas).astype(np.float32)
    return feats, np.tile(window, (1, 2 * HY_ORDER))


def hyena_filter_table(n, w1, b1, f1, w2, b2, f2, w3, b3):
    feats, window = _filter_consts(n)
    kf = feats.shape[1]
    fd = w1.shape[1]
    ncol = w3.shape[1]
    w1p = jnp.pad(w1, ((0, kf - w1.shape[0]), (0, 0)))
    tm = min(ROW_TILE, n)
    full = lambda shape: pl.BlockSpec(shape, lambda i: (0,) * len(shape))
    return pl.pallas_call(
        _k_filter_mlp,
        grid=(n // tm,),
        in_specs=[pl.BlockSpec((tm, kf), lambda i: (i, 0)),
                  pl.BlockSpec((tm, ncol), lambda i: (i, 0)),
                  full((kf, fd)), full((1, fd)), full((1, fd)),
                  full((fd, fd)), full((1, fd)), full((1, fd)),
                  full((fd, ncol)), full((1, ncol))],
        out_specs=pl.BlockSpec((tm, ncol), lambda i: (i, 0)),
        out_shape=jax.ShapeDtypeStruct((n, ncol), F32),
        compiler_params=_params("arbitrary"),
        name="hyena_filter_mlp",
    )(jnp.asarray(feats), jnp.asarray(window), w1p, b1.reshape(1, fd), f1.reshape(1, fd),
      w2, b2.reshape(1, fd), f2.reshape(1, fd), w3, b3.reshape(1, ncol))


def hyena_spectra(n, w1, b1, f1, w2, b2, f2, w3, b3, bias):
    hf = hyena_filter_table(n, w1, b1, f1, w2, b2, f2, w3, b3).reshape(n, HY_ORDER, 2, MIX_WIDTH)
    fwd = hf[:, :, 0]
    bwd = hf[1:, :, 1][::-1]
    fwd = fwd.at[0].add(bias)
    filt = jnp.concatenate([fwd, jnp.zeros((1, HY_ORDER, MIX_WIDTH), F32), bwd], axis=0)
    spec = filter_spectrum(filt.reshape(2 * n, HY_ORDER * MIX_WIDTH), n)
    n1, n2 = _fft_split(n)
    return spec.reshape(n1, 2 * n2, HY_ORDER, MIX_WIDTH).transpose(2, 0, 1, 3)


def hyena_branch(z, conv_w, conv_b, spectra):
    bsz, n, _ = z.shape
    n1, n2 = _fft_split(n)
    fa, fb, fc, fd = (jnp.asarray(m, BF16) for m in _dft_mats(n))
    fa = fa[:, :, : n1 // 2]
    nt = MIX_WIDTH // LANES
    cb = conv_b.reshape(1, HY_COLS)
    zspec = lambda off: pl.BlockSpec((1, n, LANES), lambda j, i: (i, 0, off + j))
    wspec = lambda off: pl.BlockSpec((SHORT_CONV, LANES), lambda j, i: (0, off + j))
    bspec = lambda off: pl.BlockSpec((1, LANES), lambda j, i: (0, off + j))
    once = pl.Buffered(1)
    hspec = lambda o: pl.BlockSpec((None, n1, 2 * n2, LANES), lambda j, i: (o, 0, 0, j), pipeline_mode=once)
    full = lambda shape: pl.BlockSpec(shape, lambda j, i: (0,) * len(shape), pipeline_mode=once)
    return pl.pallas_call(
        functools.partial(_k_hyena, n1=n1, n2=n2),
        grid=(nt, bsz),
        in_specs=[zspec(0), zspec(nt), zspec(2 * nt), wspec(0), wspec(nt), wspec(2 * nt),
                  bspec(0), bspec(nt), bspec(2 * nt), hspec(0), hspec(1),
                  full(fa.shape), full(fb.shape), full(fc.shape), full(fd.shape)],
        out_specs=pl.BlockSpec((1, n, LANES), lambda j, i: (i, 0, j)),
        out_shape=jax.ShapeDtypeStruct((bsz, n, MIX_WIDTH), F32),
        scratch_shapes=[pltpu.VMEM((n1 // 2 * _time_pitch(n2), LANES), F32),
                        pltpu.VMEM((n1 * _spec_pitch(n2), LANES), F32)],
        compiler_params=_params("arbitrary", "arbitrary"),
        name="hyena_long_conv",
    )(z, z, z, conv_w, conv_w, conv_w, cb, cb, cb, spectra, spectra, fa, fb, fc, fd)


def _store_head_pairs(o_ref, left, right):
    rows = left.shape[0]
    lane = lax.broadcasted_iota(jnp.int32, (rows, LANES), 1)
    low = lane < RW_HEAD
    for q in range(RW_HEADS // 2):
        lcol = left[:, q * LANES:(q + 1) * LANES]
        rcol = right[:, q * LANES:(q + 1) * LANES]
        heads = (jnp.where(low, lcol, pltpu.roll(rcol, RW_HEAD, 1)),
                 jnp.where(low, pltpu.roll(lcol, RW_HEAD, 1), rcol))
        for h, piece in zip((2 * q, 2 * q + 1), heads):
            for c in range(rows // WKV_STEPS):
                o_ref[c, h * WKV_STEPS:(h + 1) * WKV_STEPS, :] = piece[c * WKV_STEPS:(c + 1) * WKV_STEPS]


def _load_head_sum(y0_ref, y1_ref, h):
    sl = slice(h * WKV_STEPS, (h + 1) * WKV_STEPS)
    return jnp.concatenate([y0_ref[c, sl, :] + y1_ref[c, sl, :] for c in range(y0_ref.shape[0])], axis=0)


def _k_rwprep(uc_ref, ul_ref, gup_ref, wup_ref, w0_ref, aup_ref, a0_ref, kk_ref, ka_ref, rk_ref, e_ref,
              g_o, bonus_o, rv_o, nb0_o, nb1_o, wk0_o, wk1_o, *, ctx_tiles):
    w = MIX_WIDTH
    u = jnp.where(pl.program_id(1) < ctx_tiles, uc_ref[0], ul_ref[0])
    r, k, v = u[:, 0:w], u[:, w:2 * w], u[:, 2 * w:3 * w]
    xw = u[:, 3 * w:3 * w + RW_DECAY_RANK]
    xa = u[:, 3 * w + RW_DECAY_RANK:3 * w + RW_DECAY_RANK + RW_ICLR_RANK]
    xg = u[:, 3 * w + RW_DECAY_RANK + RW_ICLR_RANK:]
    ones_bd = e_ref[...]
    g_o[0] = _dot(_sigmoid(xg).astype(BF16), gup_ref[...])
    kk = k * kk_ref[...]
    kk = kk * lax.rsqrt(jnp.maximum(_seg_sum(kk * kk, ones_bd), 1e-24))
    txw = jnp.tanh(xw)
    kd_sum = None
    for d, (nb_o, wk_o) in enumerate(((nb0_o, wk0_o), (nb1_o, wk1_o))):
        x = -(w0_ref[d] + _dot_hi(txw, wup_ref[d]))
        softplus = jnp.maximum(x, 0.0) + jnp.log(1.0 + jnp.exp(-jnp.abs(x)))
        decay = jnp.exp(-jnp.exp(-softplus - 0.5))
        a = _sigmoid(a0_ref[d] + _dot_hi(xa, aup_ref[d]))
        kd = k * (1.0 + (a - 1.0) * ka_ref[...])
        _store_head_pairs(nb_o, -kk, kk * a)
        _store_head_pairs(wk_o, decay, kd)
        kd_sum = kd if kd_sum is None else kd_sum + kd
    _store_head_pairs(rv_o, r, v)
    bonus_o[0] = _seg_sum(r * kd_sum * rk_ref[...], ones_bd) * v


def rwkv_prep(u_ctx, u_lat, g_up, w_up, w0, a_up, a0, k_k, k_a, r_k, ones_bd):
    bsz, n_ctx, cols = u_ctx.shape
    n_lat = u_lat.shape[1]
    t_total = n_ctx + n_lat
    w = MIX_WIDTH
    tm = min(ROW_TILE, n_ctx)
    ctx_tiles = n_ctx // tm
    full = lambda shape: pl.BlockSpec(shape, lambda i, j: (0,) * len(shape))
    tile = pl.BlockSpec((1, tm, w), lambda i, j: (i, j, 0))
    pair = pl.BlockSpec((tm // WKV_STEPS, RW_HEADS * WKV_STEPS, LANES), lambda i, j: (j, i, 0))
    pair_shape = jax.ShapeDtypeStruct((t_total // WKV_STEPS, bsz * RW_HEADS * WKV_STEPS, LANES), F32)
    return pl.pallas_call(
        functools.partial(_k_rwprep, ctx_tiles=ctx_tiles),
        grid=(bsz, t_total // tm),
        in_specs=[pl.BlockSpec((1, tm, cols), lambda i, j: (i, jnp.minimum(j, ctx_tiles - 1), 0)),
                  pl.BlockSpec((1, tm, cols), lambda i, j: (i, jnp.maximum(j - ctx_tiles, 0), 0)),
                  full((RW_GATE_RANK, w)), full((2, RW_DECAY_RANK, w)), full((2, 1, w)),
                  full((2, RW_ICLR_RANK, w)), full((2, 1, w)), full((1, w)), full((1, w)), full((1, w)),
                  full((w, w))],
        out_specs=[tile, tile] + [pair] * 5,
        out_shape=[jax.ShapeDtypeStruct((bsz, t_total, w), F32)] * 2 + [pair_shape] * 5,
        compiler_params=_params("arbitrary", "arbitrary"),
        name="rwkv_prep",
    )(u_ctx, u_lat, g_up.astype(BF16), w_up, w0.reshape(2, 1, w), a_up, a0.reshape(2, 1, w),
      k_k.reshape(1, w), k_a.reshape(1, w), r_k.reshape(1, w), ones_bd)


def _k_wkv(rv0_ref, rv1_ref, nb0_ref, nb1_ref, wk0_ref, wk1_ref, y0_ref, y1_ref, s_ref, op_ref, yb_ref, *,
           steps):
    @pl.when(pl.program_id(0) == 0)
    def _():
        s_ref[...] = jnp.zeros_like(s_ref)

    chains = rv0_ref.shape[0] // steps
    r_op, v_op, a_op, b_op, w_op, k_op = range(6)

    def rows(t):
        return pl.ds(t, chains, stride=steps), pl.ds(steps - 1 - t, chains, stride=steps)

    for t in range(steps):
        fwd, bwd = rows(t)
        for j, (ref0, ref1) in enumerate(((rv0_ref, rv1_ref), (nb0_ref, nb1_ref), (wk0_ref, wk1_ref))):
            both = jnp.concatenate([ref0[fwd, :], ref1[bwd, :]], axis=0)
            op_ref[t, 2 * j:2 * j + 2] = both.T.reshape(2, RW_HEAD, 2 * chains)

    def advance(t, sa):
        nxt = jnp.minimum(t + 1, steps - 1)
        row = lambda op, i, slot=t: op_ref[slot, op, pl.ds(i, 1), :]
        vv = op_ref[t, v_op]

        def k_rows(ib, acc):
            y, sa_next = acc
            for u in range(WKV_K_UNROLL):
                i = ib * WKV_K_UNROLL + u
                s_new = s_ref[i] * row(w_op, i) + sa * row(b_op, i) + vv * row(k_op, i)
                s_ref[i] = s_new
                y = y + s_new * row(r_op, i)
                sa_next = sa_next + s_new * row(a_op, i, nxt)
            return y, sa_next
        zero = jnp.zeros(s_ref.shape[1:], F32)
        y, sa_next = lax.fori_loop(0, RW_HEAD // WKV_K_UNROLL, k_rows, (zero, zero))
        yb_ref[t] = y
        return sa_next

    sa = jnp.zeros(s_ref.shape[1:], F32)
    for i in range(RW_HEAD):
        sa = sa + s_ref[i] * op_ref[0, a_op, pl.ds(i, 1), :]
    lax.fori_loop(0, steps, advance, sa)

    for t in range(steps):
        y = yb_ref[t]
        yt = jnp.concatenate([y, y], axis=0).T
        fwd, bwd = rows(t)
        y0_ref[fwd, :] = yt[0:chains]
        y1_ref[bwd, :] = yt[chains:2 * chains]


def wkv_scan(rv, nb0, nb1, wk0, wk1, n_ctx):
    nb, rows, _ = rv.shape
    steps = WKV_STEPS
    chains = rows // steps
    nc = n_ctx // steps
    fwd = pl.BlockSpec((None, rows, LANES), lambda i: (i, 0, 0))
    bwd = pl.BlockSpec((None, rows, LANES), lambda i: (jnp.where(i < nc, nc - 1 - i, nb - 1 + nc - i), 0, 0))
    out = jax.ShapeDtypeStruct(rv.shape, F32)
    return pl.pallas_call(
        functools.partial(_k_wkv, steps=steps),
        grid=(nb,),
        in_specs=[fwd, bwd, fwd, bwd, fwd, bwd],
        out_specs=[fwd, bwd],
        out_shape=[out, out],
        scratch_shapes=[pltpu.VMEM((RW_HEAD, RW_HEAD, 2 * chains), F32),
                        pltpu.VMEM((steps, 6, RW_HEAD, 2 * chains), F32),
                        pltpu.VMEM((steps, RW_HEAD, 2 * chains), F32)],
        compiler_params=_params("arbitrary"),
        name="wkv_scan",
    )(rv, rv, nb0, nb1, wk0, wk1)


def rwkv_scan_branch(u_ctx, u_lat, p, ones_bd):
    g, bonus, rv, nb0, nb1, wk0, wk1 = rwkv_prep(
        u_ctx, u_lat, p["rw_g_up"], p["rw_w_up"], p["rw_w0"], p["rw_a_up"], p["rw_a0"], p["rw_k_k"],
        p["rw_k_a"], p["rw_r_k"].reshape(-1), ones_bd)
    y0, y1 = wkv_scan(rv, nb0, nb1, wk0, wk1, u_ctx.shape[1])
    return y0, y1, bonus, g


def _s5_block(i, n_ctx_blocks, n_blocks, reverse):
    if not reverse:
        return i
    return jnp.where(i < n_ctx_blocks, n_ctx_blocks - 1 - i, n_blocks - 1 + n_ctx_blocks - i)


def _k_s5(uc_ref, ul_ref, bre_ref, bim_ref, lre_ref, lim_ref, cre_ref, cim_ref, y_ref, hre_ref, him_ref,
          st_ref, tb_ref, *, steps, reverse, n_ctx_blocks, n_blocks):
    @pl.when(pl.program_id(0) == 0)
    def _():
        st_ref[...] = jnp.zeros_like(st_ref)

    bsz = uc_ref.shape[0]
    nq = MIX_WIDTH // LANES
    in_ctx = _s5_block(pl.program_id(0), n_ctx_blocks, n_blocks, reverse) < n_ctx_blocks
    for b in range(bsz):
        ub = jnp.where(in_ctx, uc_ref[b], ul_ref[b])
        for q in range(nq):
            tb_ref[q, pl.ds(b, steps, stride=bsz), :] = ub[:, q * LANES:(q + 1) * LANES]
    u = jnp.concatenate([tb_ref[q] for q in range(nq)], axis=1).astype(BF16)
    wc, hc = MIX_WIDTH // S5_DIAG_PARTS, S5_HID // S5_DIAG_PARTS
    for part in range(S5_DIAG_PARTS):
        ws, hs = slice(part * wc, (part + 1) * wc), slice(part * hc, (part + 1) * hc)
        hre_ref[:, hs] = _dot(u[:, ws], bre_ref[ws, hs])
        him_ref[:, hs] = _dot(u[:, ws], bim_ref[ws, hs])
    for ch in range(S5_HID // S5_LANE_CHUNK):
        cols = slice(ch * S5_LANE_CHUNK, (ch + 1) * S5_LANE_CHUNK)
        lre = lre_ref[:, cols]
        lim = lim_ref[:, cols]

        def step(i, carry):
            hr, hi = carry
            t = (steps - 1 - i) if reverse else i
            rows = pl.ds(pl.multiple_of(t * bsz, bsz), bsz)
            nr = lre * hr - lim * hi + hre_ref[rows, cols]
            ni = lre * hi + lim * hr + him_ref[rows, cols]
            hre_ref[rows, cols] = nr
            him_ref[rows, cols] = ni
            return nr, ni
        hr, hi = lax.fori_loop(0, steps, step, (st_ref[0, :, cols], st_ref[1, :, cols]))
        st_ref[0, :, cols] = hr
        st_ref[1, :, cols] = hi
    lanes_per_part = wc // LANES
    for part in range(S5_DIAG_PARTS):
        ws, hs = slice(part * wc, (part + 1) * wc), slice(part * hc, (part + 1) * hc)
        y = (_dot(hre_ref[:, hs].astype(BF16), cre_ref[hs, ws])
             - _dot(him_ref[:, hs].astype(BF16), cim_ref[hs, ws]))
        for q in range(lanes_per_part):
            tb_ref[part * lanes_per_part + q] = y[:, q * LANES:(q + 1) * LANES]
    for b in range(bsz):
        for q in range(nq):
            y_ref[b, :, q * LANES:(q + 1) * LANES] = tb_ref[q, pl.ds(b, steps, stride=bsz), :]


def s5_scan(z_ctx, z_lat, bre, bim, lre, lim, cre, cim, reverse):
    bsz, n_ctx, w = z_ctx.shape
    t_total = n_ctx + z_lat.shape[1]
    steps = S5_STEPS
    nc = n_ctx // steps
    nb = t_total // steps
    blk = functools.partial(_s5_block, n_ctx_blocks=nc, n_blocks=nb, reverse=reverse)
    full = lambda shape: pl.BlockSpec(shape, lambda i: (0,) * len(shape))
    return pl.pallas_call(
        functools.partial(_k_s5, steps=steps, reverse=reverse, n_ctx_blocks=nc, n_blocks=nb),
        grid=(nb,),
        in_specs=[pl.BlockSpec((bsz, steps, w), lambda i: (0, jnp.minimum(blk(i), nc - 1), 0)),
                  pl.BlockSpec((bsz, steps, w), lambda i: (0, jnp.maximum(blk(i) - nc, 0), 0)),
                  full((w, S5_HID)), full((w, S5_HID)), full((bsz, S5_HID)), full((bsz, S5_HID)),
                  full((S5_HID, w)), full((S5_HID, w))],
        out_specs=pl.BlockSpec((bsz, steps, w), lambda i: (0, blk(i), 0)),
        out_shape=jax.ShapeDtypeStruct((bsz, t_total, w), F32),
        scratch_shapes=[pltpu.VMEM((steps * bsz, S5_HID), F32), pltpu.VMEM((steps * bsz, S5_HID), F32),
                        pltpu.VMEM((2, bsz, S5_HID), F32), pltpu.VMEM((w // LANES, steps * bsz, LANES), F32)],
        compiler_params=_params("arbitrary"),
        name="s5_scan",
    )(z_ctx, z_lat, bre, bim, lre, lim, cre, cim)


def _k_s5disc(are_ref, aim_ref, ldt_ref, lre_ref, lim_ref, cr_ref, ci_ref):
    a_re, a_im = are_ref[...], aim_ref[...]
    dt = jnp.exp(ldt_ref[...])
    mag = jnp.exp(a_re * dt)
    lb_re, lb_im = mag * jnp.cos(a_im * dt), mag * jnp.sin(a_im * dt)
    den = a_re * a_re + a_im * a_im
    nr = lb_re - 1.0
    lre_ref[...] = lb_re
    lim_ref[...] = lb_im
    cr_ref[...] = (nr * a_re + lb_im * a_im) / den
    ci_ref[...] = (lb_im * a_re - nr * a_im) / den


def s5_operators(a_re, a_im, log_dt, b_re, b_im, c_re, c_im, bsz):
    g, p, h = S5_GROUPS, S5_STATE, S5_GROUP
    shp = jax.ShapeDtypeStruct((2 * g, p), F32)
    lb_re, lb_im, cr, ci = pl.pallas_call(
        _k_s5disc, out_shape=[shp] * 4, name="s5_discretise",
    )(a_re.reshape(2 * g, p), a_im.reshape(2 * g, p), jnp.broadcast_to(log_dt.reshape(2 * g, 1), (2 * g, p)))
    cr, ci = cr.reshape(2, g, p, 1), ci.reshape(2, g, p, 1)
    bb_re = cr * b_re[None] - ci * b_im[None]
    bb_im = cr * b_im[None] + ci * b_re[None]
    eye = jnp.eye(g, dtype=F32)
    bd_in = lambda m: jnp.einsum("dgph,gk->dghkp", m, eye).reshape(2, g * h, g * p).astype(BF16)
    bd_out = lambda m: jnp.einsum("ghp,gk->gpkh", m, eye).reshape(g * p, g * h).astype(BF16)
    lam = lambda m: jnp.broadcast_to(m.reshape(2, 1, g * p), (2, bsz, g * p))
    return bd_in(bb_re), bd_in(bb_im), lam(lb_re), lam(lb_im), bd_out(c_re), bd_out(c_im)


def s5_branch(z_ctx, z_lat, p):
    bsz = z_ctx.shape[0]
    bre, bim, lre, lim, cre, cim = s5_operators(p["s5_a_re"], p["s5_a_im"], p["s5_log_dt"], p["s5_b_re"],
                                                p["s5_b_im"], p["s5_c_re"], p["s5_c_im"], bsz)
    return (s5_scan(z_ctx, z_lat, bre[0], bim[0], lre[0], lim[0], cre, cim, False),
            s5_scan(z_ctx, z_lat, bre[1], bim[1], lre[1], lim[1], cre, cim, True))


def _k_merge(yhy_ref, rwy0_ref, rwy1_ref, rwb_ref, rwg_ref, s5y0_ref, s5y1_ref, s5u_ref, zg_ref, x_ref, g1_ref,
             e_ref, lng_ref, lnb_ref, s5d_ref, gluw_ref, glub_ref, wb_ref, wo_ref, o_ref):
    w = MIX_WIDTH
    ones_bd = e_ref[...]
    lane = lax.broadcasted_iota(jnp.int32, (x_ref.shape[1], LANES), 1)
    cols = []
    for q in range(RW_HEADS // 2):
        even = _load_head_sum(rwy0_ref, rwy1_ref, 2 * q)
        odd = _load_head_sum(rwy0_ref, rwy1_ref, 2 * q + 1)
        cols.append(jnp.where(lane < RW_HEAD, even, pltpu.roll(odd, RW_HEAD, 1)))
    y = jnp.concatenate(cols, axis=1) + rwb_ref[0]
    mu = _seg_sum(y, ones_bd) * (1.0 / RW_HEAD)
    yc = y - mu
    var = _seg_sum(yc * yc, ones_bd) * (1.0 / RW_HEAD)
    y_rw = (yc * lax.rsqrt(var + RW_LN_EPS) * lng_ref[...] + lnb_ref[...]) * rwg_ref[0]
    s = s5y0_ref[0] + s5y1_ref[0] + s5u_ref[0] * s5d_ref[...]
    s = 0.5 * s * (1.0 + jnp.tanh(math.sqrt(2.0 / math.pi) * (s + 0.044715 * (s * s * s))))
    lg = _dot(s.astype(BF16), gluw_ref[...]) + glub_ref[...]
    y_s5 = lg[:, 0:w] * _sigmoid(lg[:, w:2 * w])
    zg = zg_ref[0]
    d = D_MODEL
    m = (_sigmoid(zg[:, 0:d]) * _dot(yhy_ref[0].astype(BF16), wb_ref[0])
         + _sigmoid(zg[:, d:2 * d]) * _dot(y_rw.astype(BF16), wb_ref[1])
         + _sigmoid(zg[:, 2 * d:3 * d]) * _dot(y_s5.astype(BF16), wb_ref[2]))
    o_ref[0] = x_ref[0] + g1_ref[0] * _dot(m.astype(BF16), wo_ref[...])


def merge_residual(y_hy, rw, s5_y, t_off, s5_u, zg, x, g1, p, ones_bd):
    bsz, n, d = x.shape
    w = MIX_WIDTH
    tm = min(ROW_TILE, n)
    assert t_off % tm == 0
    off = t_off // tm
    tile = lambda c: pl.BlockSpec((1, tm, c), lambda i, j: (i, j, 0))
    seq_tile = pl.BlockSpec((1, tm, w), lambda i, j: (i, off + j, 0))
    chain_tile = pl.BlockSpec((tm // WKV_STEPS, RW_HEADS * WKV_STEPS, LANES), lambda i, j: (off + j, i, 0))
    full = lambda shape: pl.BlockSpec(shape, lambda i, j: (0,) * len(shape))
    rw_y0, rw_y1, rw_bonus, rw_g = rw
    return pl.pallas_call(
        _k_merge,
        grid=(bsz, n // tm),
        in_specs=[tile(w), chain_tile, chain_tile, seq_tile, seq_tile, seq_tile, seq_tile, tile(w),
                  tile(N_BRANCH * d), tile(d), pl.BlockSpec((1, 1, d), lambda i, j: (i, 0, 0)),
                  full((w, w)), full((1, w)), full((1, w)), full((1, w)), full((w, 2 * w)), full((1, 2 * w)),
                  full((N_BRANCH, w, d)), full((d, d))],
        out_specs=tile(d),
        out_shape=jax.ShapeDtypeStruct(x.shape, F32),
        compiler_params=_params("arbitrary", "arbitrary"),
        name="merge_residual",
    )(y_hy, rw_y0, rw_y1, rw_bonus, rw_g, s5_y[0], s5_y[1], s5_u, zg, x, g1, ones_bd, p["rw_ln_g"].reshape(1, w),
      p["rw_ln_b"].reshape(1, w), p["s5_d"].reshape(1, w), p["s5_glu_w"].astype(BF16),
      p["s5_glu_b"].reshape(1, 2 * w), p["w_branch"].astype(BF16), p["w_out"].astype(BF16))


def _k_router(x_ref, sh_ref, sc_ref, g_ref, rw_ref, rb_ref, h_o, idx_o, gate_o, cnt_o):
    @pl.when((pl.program_id(0) == 0) & (pl.program_id(1) == 0))
    def _():
        cnt_o[...] = jnp.zeros_like(cnt_o)

    h = _rms_modulate(x_ref[0], g_ref[...], sh_ref[0], sc_ref[0])
    h_o[0] = h.astype(BF16)
    logits = _dot_hi(h, rw_ref[...]) + rb_ref[...]
    lane = lax.broadcasted_iota(jnp.int32, logits.shape, 1)
    vals, idxs = [], []
    for _ in range(TOP_K):
        m = jnp.max(logits, axis=-1, keepdims=True)
        idx = jnp.min(jnp.where(logits == m, lane, LANES), axis=-1, keepdims=True)
        vals.append(m)
        idxs.append(idx)
        logits = jnp.where(lane == idx, -jnp.inf, logits)
    exps = [jnp.exp(v - vals[0]) for v in vals]
    inv = 1.0 / (exps[0] + exps[1] + exps[2] + exps[3])
    idx_out = jnp.zeros(logits.shape, jnp.int32)
    gate_out = jnp.zeros(logits.shape, F32)
    picked = jnp.zeros(logits.shape, F32)
    for j in range(TOP_K):
        idx_out = jnp.where(lane == j, idxs[j], idx_out)
        gate_out = jnp.where(lane == j, exps[j] * inv, gate_out)
        picked = picked + jnp.where(lane == idxs[j], 1.0, 0.0)
    idx_o[0] = idx_out
    gate_o[0] = gate_out
    cnt_o[...] = cnt_o[...] + jnp.sum(picked, axis=0, keepdims=True)


def moe_route(x, shift, scale, g, router_w, router_b):
    bsz, n, d = x.shape
    tm = min(ROW_TILE, n)
    rw = jnp.pad(router_w, ((0, 0), (0, LANES - N_EXPERTS)))
    rb = jnp.pad(router_b, (0, LANES - N_EXPERTS), constant_values=-jnp.inf).reshape(1, LANES)
    tile = lambda c: pl.BlockSpec((1, tm, c), lambda i, j: (i, j, 0))
    vec = pl.BlockSpec((1, 1, d), lambda i, j: (i, 0, 0))
    full = lambda shape: pl.BlockSpec(shape, lambda i, j: (0,) * len(shape))
    return pl.pallas_call(
        _k_router,
        grid=(bsz, n // tm),
        in_specs=[tile(d), vec, vec, full((1, d)), full((d, LANES)), full((1, LANES))],
        out_specs=[tile(d), tile(LANES), tile(LANES), full((SUBLANES, LANES))],
        out_shape=[jax.ShapeDtypeStruct((bsz, n, d), BF16), jax.ShapeDtypeStruct((bsz, n, LANES), jnp.int32),
                   jax.ShapeDtypeStruct((bsz, n, LANES), F32), jax.ShapeDtypeStruct((SUBLANES, LANES), F32)],
        compiler_params=_params("arbitrary", "arbitrary"),
        name="moe_route",
    )(x, shift, scale, g.reshape(1, d), rw, rb)


def _k_expert(blk_e_ref, used_ref, xs_ref, w1_ref, b1_ref, w2_ref, b2_ref, o_ref, w1b_ref, w2b_ref):
    i = pl.program_id(0)

    @pl.when(i < used_ref[0])
    def _():
        @pl.when((i == 0) | (blk_e_ref[i] != blk_e_ref[jnp.maximum(i - 1, 0)]))
        def _():
            w1b_ref[...] = w1_ref[0].astype(BF16)
            w2b_ref[...] = w2_ref[0].astype(BF16)

        hid = _dot(xs_ref[...], w1b_ref[...]) + b1_ref[0]
        gl = jnp.minimum(hid[:, 0:D_EXPERT], SWIGLU_LIMIT)
        up = jnp.clip(hid[:, D_EXPERT:2 * D_EXPERT], -SWIGLU_LIMIT, SWIGLU_LIMIT)
        act = (up + 1.0) * gl * _sigmoid(SWIGLU_ALPHA * gl)
        o_ref[...] = (_dot(act.astype(BF16), w2b_ref[...]) + b2_ref[0]).astype(o_ref.dtype)

    @pl.when(i >= used_ref[0])
    def _():
        o_ref[...] = jnp.zeros_like(o_ref)


def expert_ffn(xs, blk_e, n_used, w1, b1, w2, b2):
    n_slots, d = xs.shape
    n_blocks = n_slots // MOE_ROWS
    grid_spec = pltpu.PrefetchScalarGridSpec(
        num_scalar_prefetch=2,
        grid=(n_blocks,),
        in_specs=[pl.BlockSpec((MOE_ROWS, d), lambda i, e, u: (i, 0)),
                  pl.BlockSpec((1, d, 2 * D_EXPERT), lambda i, e, u: (e[i], 0, 0)),
                  pl.BlockSpec((1, 1, 2 * D_EXPERT), lambda i, e, u: (e[i], 0, 0)),
                  pl.BlockSpec((1, D_EXPERT, d), lambda i, e, u: (e[i], 0, 0)),
                  pl.BlockSpec((1, 1, d), lambda i, e, u: (e[i], 0, 0))],
        out_specs=pl.BlockSpec((MOE_ROWS, d), lambda i, e, u: (i, 0)),
        scratch_shapes=[pltpu.VMEM((d, 2 * D_EXPERT), BF16), pltpu.VMEM((D_EXPERT, d), BF16)],
    )
    return pl.pallas_call(
        _k_expert,
        grid_spec=grid_spec,
        out_shape=jax.ShapeDtypeStruct((n_slots, d), BF16),
        compiler_params=_params("arbitrary"),
        name="expert_ffn",
    )(blk_e, n_used.reshape(1).astype(jnp.int32), xs, w1, b1.reshape(b1.shape[0], 1, -1), w2,
      b2.reshape(b2.shape[0], 1, -1))


def _k_offset_add(tab_ref, e_ref, v_ref, o_ref):
    e = e_ref[...]
    acc = v_ref[...]
    for j in range(N_EXPERTS):
        acc = acc + jnp.where(e == j, tab_ref[j], 0)
    o_ref[...] = acc


def expert_offset_add(flat_e, values, table):
    n = flat_e.shape[0]
    rows = n // LANES
    return pl.pallas_call(
        _k_offset_add,
        in_specs=[pl.BlockSpec(memory_space=pltpu.SMEM), pl.BlockSpec(memory_space=pltpu.VMEM),
                  pl.BlockSpec(memory_space=pltpu.VMEM)],
        out_specs=pl.BlockSpec(memory_space=pltpu.VMEM),
        out_shape=jax.ShapeDtypeStruct((rows, LANES), jnp.int32),
        name="expert_offset_add",
    )(table.astype(jnp.int32), flat_e.reshape(rows, LANES), values.reshape(rows, LANES)).reshape(n)


def _k_combine(y0_ref, y1_ref, y2_ref, y3_ref, gate_ref, x_ref, g2_ref, o_ref):
    gate = gate_ref[0]
    acc = None
    for j, y_ref in enumerate((y0_ref, y1_ref, y2_ref, y3_ref)):
        term = y_ref[...].astype(F32) * gate[:, j:j + 1]
        acc = term if acc is None else acc + term
    o_ref[0] = x_ref[0] + g2_ref[0] * acc


def moe_combine(ys, gate, x, g2):
    bsz, n, d = x.shape
    tm = min(ROW_TILE, n)
    tiles = n // tm
    row_tile = pl.BlockSpec((tm, d), lambda i, j: (i * tiles + j, 0))
    return pl.pallas_call(
        _k_combine,
        grid=(bsz, tiles),
        in_specs=[row_tile] * TOP_K + [pl.BlockSpec((1, tm, LANES), lambda i, j: (i, j, 0)),
                                       pl.BlockSpec((1, tm, d), lambda i, j: (i, j, 0)),
                                       pl.BlockSpec((1, 1, d), lambda i, j: (i, 0, 0))],
        out_specs=pl.BlockSpec((1, tm, d), lambda i, j: (i, j, 0)),
        out_shape=jax.ShapeDtypeStruct(x.shape, F32),
        compiler_params=_params("arbitrary", "arbitrary"),
        name="moe_combine",
    )(*ys, gate, x, g2)


def moe_ffn(x, shift, scale, g2, g, router_w, router_b, w1, b1, w2, b2, expert_base=0):
    bsz, n, d = x.shape
    h, idx, gate, cnt = moe_route(x, shift, scale, g, router_w, router_b)
    n_tok = bsz * n
    n_asg = n_tok * TOP_K
    flat_e = idx[..., :TOP_K].reshape(-1)
    order = jnp.argsort(flat_e).astype(jnp.int32)
    rank_of = jnp.argsort(order).astype(jnp.int32)
    counts = cnt[0, :N_EXPERTS].astype(jnp.int32)
    grp_start = jnp.cumsum(counts) - counts
    padded = (counts + MOE_ROWS - 1) // MOE_ROWS * MOE_ROWS
    pad_end = jnp.cumsum(padded)
    pad_start = pad_end - padded
    n_blocks = -(-n_asg // MOE_ROWS) + N_EXPERTS
    n_slots = n_blocks * MOE_ROWS
    blk_first = jnp.arange(n_blocks, dtype=jnp.int32) * MOE_ROWS
    blk_e = jnp.minimum(jnp.sum(pad_end[None, :] <= blk_first[:, None], axis=1), N_EXPERTS - 1).astype(jnp.int32)
    slot_rank = (blk_first - pad_start[blk_e])[:, None] + jnp.arange(MOE_ROWS, dtype=jnp.int32)[None, :]
    slot_used = (slot_rank < counts[blk_e][:, None]).reshape(-1)
    slot_src = jnp.where(slot_used, (grp_start[blk_e][:, None] + slot_rank).reshape(-1), 0)
    slot_asg = order[slot_src]
    slot_tok = jnp.where(slot_used, slot_asg // TOP_K, jnp.arange(n_slots, dtype=jnp.int32) % n_tok)
    slot_of = expert_offset_add(flat_e, rank_of, pad_start - grp_start)
    xs = h.reshape(n_tok, d)[slot_tok]
    ys = expert_ffn(xs, blk_e + expert_base, pad_end[-1] // MOE_ROWS, w1, b1, w2, b2)
    slot_of = slot_of.reshape(n_tok, TOP_K)
    return moe_combine([ys[slot_of[:, j]] for j in range(TOP_K)], gate, x, g2)


def _k_rmsnorm(x_ref, g_ref, o_ref):
    x = x_ref[0]
    ms = jnp.mean(x * x, axis=-1, keepdims=True)
    o_ref[0] = x * lax.rsqrt(ms + NORM_EPS) * g_ref[...]


def final_rmsnorm(x, g):
    b, n, d = x.shape
    tm = min(2 * ROW_TILE, n)
    return pl.pallas_call(
        _k_rmsnorm,
        grid=(b, n // tm),
        in_specs=[pl.BlockSpec((1, tm, d), lambda i, j: (i, j, 0)), pl.BlockSpec((1, d), lambda i, j: (0, 0))],
        out_specs=pl.BlockSpec((1, tm, d), lambda i, j: (i, j, 0)),
        out_shape=jax.ShapeDtypeStruct(x.shape, F32),
        compiler_params=_params("arbitrary", "arbitrary"),
        name="final_rmsnorm",
    )(x, g.reshape(1, d))


@functools.lru_cache(maxsize=None)
def _grid_pos_embed(n_tokens):
    rows = n_tokens // GRID_W
    row_id, col_id = np.meshgrid(np.arange(rows), np.arange(GRID_W), indexing="ij")
    quarter = D_MODEL // 4
    omega = (1.0 / (10000.0 ** (np.arange(quarter, dtype=np.float32) / quarter))).astype(np.float32)

    def enc(pos):
        ang = pos.reshape(-1)[:, None].astype(np.float32) * omega
        return np.concatenate([np.sin(ang), np.cos(ang)], axis=-1)

    return np.concatenate([enc(row_id), enc(col_id)], axis=-1).astype(np.float32)


_LAYER_KEYS = ("ada_w", "ada_b", "norm1_g", "norm2_g", "w_in", "hy_conv_w", "hy_conv_b", "hy_w1", "hy_b1",
               "hy_f1", "hy_w2", "hy_b2", "hy_f2", "hy_w3", "hy_b3", "hy_bias", "rw_conv_w", "rw_conv_b",
               "rw_w_up", "rw_w0", "rw_a_up", "rw_a0", "rw_g_up", "rw_k_k", "rw_k_a", "rw_r_k", "rw_ln_g",
               "rw_ln_b", "s5_a_re", "s5_a_im", "s5_log_dt", "s5_b_re", "s5_b_im", "s5_c_re", "s5_c_im",
               "s5_d", "s5_glu_w", "s5_glu_b", "w_branch", "w_out", "router_w", "router_b", "moe_w1",
               "moe_b1", "moe_w2", "moe_b2")


def _token_mixer(x, xc, mod, mod_c, p, need_ctx, ones_bd):
    sh1, sc1, g1 = mod
    csh1, csc1, cg1 = mod_c
    w_in = p["w_in"].astype(BF16)
    c0, c1, c2 = HY_COLS, HY_COLS + RW_COLS, HY_COLS + RW_COLS + MIX_WIDTH
    proj = lambda t, s, c, lo, hi: norm_mod_matmul(t, s, c, p["norm1_g"], w_in[:, lo:hi])
    hy_args = (p["hy_w1"], p["hy_b1"], p["hy_f1"], p["hy_w2"], p["hy_b2"], p["hy_f2"], p["hy_w3"], p["hy_b3"],
               p["hy_bias"])

    z_hy = proj(x, sh1, sc1, 0, c0)
    y_hy = hyena_branch(z_hy, p["hy_conv_w"], p["hy_conv_b"], hyena_spectra(x.shape[1], *hy_args))
    u_rw = short_conv(proj(x, sh1, sc1, c0, c1), p["rw_conv_w"], p["rw_conv_b"])
    uc_rw = short_conv(proj(xc, csh1, csc1, c0, c1), p["rw_conv_w"], p["rw_conv_b"])
    rw = rwkv_scan_branch(uc_rw, u_rw, p, ones_bd)
    n_ctx = xc.shape[1]
    z_s5 = proj(x, sh1, sc1, c1, c2)
    zc_s5 = proj(xc, csh1, csc1, c1, c2)
    s5_y = s5_branch(zc_s5, z_s5, p)
    zg = proj(x, sh1, sc1, c2, c2 + N_BRANCH * D_MODEL)
    x_new = merge_residual(y_hy, rw, s5_y, n_ctx, z_s5, zg, x, g1, p, ones_bd)
    if not need_ctx:
        return x_new, None
    zc_hy = proj(xc, csh1, csc1, 0, c0)
    yc_hy = hyena_branch(zc_hy, p["hy_conv_w"], p["hy_conv_b"], hyena_spectra(xc.shape[1], *hy_args))
    zcg = proj(xc, csh1, csc1, c2, c2 + N_BRANCH * D_MODEL)
    xc_new = merge_residual(yc_hy, rw, s5_y, 0, zc_s5, zcg, xc, cg1, p, ones_bd)
    return x_new, xc_new


def kernel(x, c, ctx, c_ctx, ada_w, ada_b, norm1_g, norm2_g, w_in, hy_conv_w, hy_conv_b, hy_w1, hy_b1, hy_f1, hy_w2, hy_b2, hy_f2, hy_w3, hy_b3, hy_bias, rw_conv_w, rw_conv_b, rw_w_up, rw_w0, rw_a_up, rw_a0, rw_g_up, rw_k_k, rw_k_a, rw_r_k, rw_ln_g, rw_ln_b, s5_a_re, s5_a_im, s5_log_dt, s5_b_re, s5_b_im, s5_c_re, s5_c_im, s5_d, s5_glu_w, s5_glu_b, w_branch, w_out, router_w, router_b, moe_w1, moe_b1, moe_w2, moe_b2, final_g):
    stacked = dict(zip(_LAYER_KEYS, (ada_w, ada_b, norm1_g, norm2_g, w_in, hy_conv_w, hy_conv_b, hy_w1, hy_b1,
                                     hy_f1, hy_w2, hy_b2, hy_f2, hy_w3, hy_b3, hy_bias, rw_conv_w, rw_conv_b,
                                     rw_w_up, rw_w0, rw_a_up, rw_a0, rw_g_up, rw_k_k, rw_k_a, rw_r_k, rw_ln_g,
                                     rw_ln_b, s5_a_re, s5_a_im, s5_log_dt, s5_b_re, s5_b_im, s5_c_re, s5_c_im,
                                     s5_d, s5_glu_w, s5_glu_b, w_branch, w_out, router_w, router_b, moe_w1,
                                     moe_b1, moe_w2, moe_b2)))
    bsz, n, d = x.shape
    depth = ada_w.shape[0]
    x = add_pos(x, jnp.asarray(_grid_pos_embed(n)))
    xc = ctx
    ones_bd = jnp.asarray(np.kron(np.eye(RW_HEADS), np.ones((RW_HEAD, RW_HEAD))), BF16)
    cvec = jnp.zeros((2 * SUBLANES, d), F32).at[:bsz].set(c).at[bsz].set(c_ctx)
    for l in range(depth):
        p = {k: v[l] for k, v in stacked.items()}
        need_ctx = l < depth - 1
        ada = ada_proj(cvec, p["ada_w"], p["ada_b"])
        lat = [t.reshape(bsz, 1, d) for t in jnp.split(ada[:bsz], 6, axis=-1)]
        cx = [jnp.broadcast_to(t.reshape(1, 1, d), (bsz, 1, d)) for t in jnp.split(ada[bsz], 6, axis=-1)]
        x, xc_new = _token_mixer(x, xc, lat[0:3], cx[0:3], p, need_ctx, ones_bd)
        stack = lambda t: t.reshape((depth * N_EXPERTS,) + t.shape[2:])
        moe = functools.partial(moe_ffn, g=p["norm2_g"], router_w=p["router_w"], router_b=p["router_b"],
                                w1=stack(moe_w1), b1=stack(moe_b1), w2=stack(moe_w2), b2=stack(moe_b2),
                                expert_base=l * N_EXPERTS)
        x = moe(x, lat[3], lat[4], lat[5])
        if need_ctx:
            xc = moe(xc_new, cx[3], cx[4], cx[5])
    return final_rmsnorm(x, final_g)
```

```python
import functools
import math

import numpy as np
import jax
import jax.numpy as jnp
from jax import lax
from jax.experimental import pallas as pl
from jax.experimental.pallas import tpu as pltpu

F32 = jnp.float32
BF16 = jnp.bfloat16

D_MODEL = 1024
GRID_W = 64
SHORT_CONV = 3
NORM_EPS = 1e-6
N_BRANCH = 3
MIX_WIDTH = 512

HY_ORDER = 2
HY_BANDS = 16
HY_DECAY_TARGET = 1e-2
HY_FAST_PCT = 0.3
HY_SLOW_PCT = 1.5
HY_COLS = (HY_ORDER + 1) * MIX_WIDTH

RW_HEAD = 64
RW_HEADS = MIX_WIDTH // RW_HEAD
RW_DECAY_RANK = 64
RW_ICLR_RANK = 64
RW_GATE_RANK = 128
RW_LN_EPS = 64e-5
RW_COLS = 3 * MIX_WIDTH + RW_DECAY_RANK + RW_ICLR_RANK + RW_GATE_RANK

S5_GROUP = 16
S5_GROUPS = MIX_WIDTH // S5_GROUP
S5_STATE = 64
S5_HID = S5_GROUPS * S5_STATE

N_EXPERTS = 32
TOP_K = 4
D_EXPERT = 1024
SWIGLU_LIMIT = 7.0
SWIGLU_ALPHA = 1.702

LANES = 128
SUBLANES = 8
VMEM_LIMIT_BYTES = 56 * 1024 * 1024

ROW_TILE = 256
MOE_ROWS = 512
WKV_STEPS = 16
WKV_K_UNROLL = 16
S5_STEPS = 64
S5_LANE_CHUNK = 512
S5_DIAG_PARTS = 2


def _params(*sem):
    return pltpu.CompilerParams(dimension_semantics=sem, vmem_limit_bytes=VMEM_LIMIT_BYTES)


def _dot(a, b):
    return jnp.dot(a, b, preferred_element_type=F32)


def _split3(x):
    hi = x.astype(BF16)
    r1 = x - hi.astype(F32)
    mid = r1.astype(BF16)
    lo = (r1 - mid.astype(F32)).astype(BF16)
    return hi, mid, lo


def _dot_hi(a, b):
    a0, a1, a2 = _split3(a)
    b0, b1, b2 = _split3(b)
    return (_dot(a0, b0) + (_dot(a0, b1) + _dot(a1, b0))
            + (_dot(a1, b1) + _dot(a0, b2) + _dot(a2, b0)))


def _seg_sum(x, ones_bd):
    hi, mid, lo = _split3(x)
    return _dot(hi, ones_bd) + _dot(mid, ones_bd) + _dot(lo, ones_bd)


def _sigmoid(x):
    return 1.0 / (1.0 + jnp.exp(-x))


def _rms_modulate(x, g, shift, scale):
    ms = jnp.mean(x * x, axis=-1, keepdims=True)
    y = x * lax.rsqrt(ms + NORM_EPS) * g
    return y * (1.0 + scale) + shift


def _k_ada(c_ref, w_ref, b_ref, o_ref):
    c = c_ref[...]
    s = c * _sigmoid(c)
    o_ref[...] = _dot_hi(s, w_ref[...]) + b_ref[...]


def ada_proj(cvec, w, b):
    rows, d = cvec.shape
    n = w.shape[1]
    tn = 1536
    return pl.pallas_call(
        _k_ada,
        grid=(n // tn,),
        in_specs=[pl.BlockSpec((rows, d), lambda j: (0, 0)),
                  pl.BlockSpec((d, tn), lambda j: (0, j)),
                  pl.BlockSpec((1, tn), lambda j: (0, j))],
        out_specs=pl.BlockSpec((rows, tn), lambda j: (0, j)),
        out_shape=jax.ShapeDtypeStruct((rows, n), F32),
        compiler_params=_params("arbitrary"),
        name="ada_proj",
    )(cvec, w, b.reshape(1, n))


def _k_add_pos(x_ref, p_ref, o_ref):
    o_ref[0] = x_ref[0] + p_ref[...]


def add_pos(x, pos):
    b, n, d = x.shape
    tm = min(ROW_TILE * 2, n)
    return pl.pallas_call(
        _k_add_pos,
        grid=(b, n // tm),
        in_specs=[pl.BlockSpec((1, tm, d), lambda i, j: (i, j, 0)),
                  pl.BlockSpec((tm, d), lambda i, j: (j, 0))],
        out_specs=pl.BlockSpec((1, tm, d), lambda i, j: (i, j, 0)),
        out_shape=jax.ShapeDtypeStruct(x.shape, F32),
        compiler_params=_params("arbitrary", "arbitrary"),
        name="add_pos",
    )(x, pos)


def _k_norm_mm(x_ref, sh_ref, sc_ref, g_ref, w_ref, o_ref, h_ref):
    @pl.when(pl.program_id(2) == 0)
    def _():
        h = _rms_modulate(x_ref[0], g_ref[...], sh_ref[0], sc_ref[0])
        h_ref[...] = h.astype(BF16)

    o_ref[0] = _dot(h_ref[...], w_ref[...])


def norm_mod_matmul(x, shift, scale, g, w):
    b, n, d = x.shape
    ncol = w.shape[1]
    tm = min(4 * ROW_TILE, n)
    tn = ncol if ncol <= 1792 else 1536
    return pl.pallas_call(
        _k_norm_mm,
        grid=(b, n // tm, ncol // tn),
        in_specs=[pl.BlockSpec((1, tm, d), lambda i, j, k: (i, j, 0)),
                  pl.BlockSpec((1, 1, d), lambda i, j, k: (i, 0, 0)),
                  pl.BlockSpec((1, 1, d), lambda i, j, k: (i, 0, 0)),
                  pl.BlockSpec((1, d), lambda i, j, k: (0, 0)),
                  pl.BlockSpec((d, tn), lambda i, j, k: (0, k))],
        out_specs=pl.BlockSpec((1, tm, tn), lambda i, j, k: (i, j, k)),
        out_shape=jax.ShapeDtypeStruct((b, n, ncol), F32),
        scratch_shapes=[pltpu.VMEM((tm, d), BF16)],
        compiler_params=_params("arbitrary", "arbitrary", "arbitrary"),
        name="norm_mod_matmul",
    )(x, shift, scale, g.reshape(1, d), w)


def _short_conv_val(z, w, b):
    n = z.shape[0]
    row = lax.broadcasted_iota(jnp.int32, z.shape, 0)
    zm = jnp.where(row == 0, 0.0, pltpu.roll(z, 1, 0))
    zp = jnp.where(row == n - 1, 0.0, pltpu.roll(z, n - 1, 0))
    return zm * w[0:1] + z * w[1:2] + zp * w[2:3] + b


def _k_sconv(z_ref, w_ref, b_ref, o_ref):
    o_ref[0] = _short_conv_val(z_ref[0], w_ref[...], b_ref[...])


def short_conv(z, w, b):
    bsz, n, c = z.shape
    tc = 256 if c % 256 == 0 else LANES
    return pl.pallas_call(
        _k_sconv,
        grid=(bsz, c // tc),
        in_specs=[pl.BlockSpec((1, n, tc), lambda i, j: (i, 0, j)),
                  pl.BlockSpec((SHORT_CONV, tc), lambda i, j: (0, j)),
                  pl.BlockSpec((1, tc), lambda i, j: (0, j))],
        out_specs=pl.BlockSpec((1, n, tc), lambda i, j: (i, 0, j)),
        out_shape=jax.ShapeDtypeStruct(z.shape, F32),
        compiler_params=_params("arbitrary", "arbitrary"),
        name="short_conv",
    )(z, w, b.reshape(1, c))


def _fft_split(n):
    total = 2 * n
    bits = total.bit_length() - 1
    assert 1 << bits == total
    n1 = 1 << ((bits + 1) // 2)
    n1 = max(n1, 2 * SUBLANES)
    return n1, total // n1


@functools.lru_cache(maxsize=None)
def _dft_mats(n):
    n1, n2 = _fft_split(n)
    total = n1 * n2
    k1 = np.arange(n1)[:, None]
    m1 = np.arange(n1)[None, :]
    m2 = np.arange(n2)[:, None, None]
    ang = -2.0 * np.pi * ((n2 * k1 * m1)[None] + m2 * k1[None]) / total
    fa = np.concatenate([np.cos(ang), np.sin(ang)], axis=1)
    k2 = np.arange(n2)[:, None]
    mm = np.arange(n2)[None, :]
    gang = -2.0 * np.pi * k2 * mm / n2
    gr, gi = np.cos(gang), np.sin(gang)
    fb = np.block([[gr, -gi], [gi, gr]])
    fc = np.block([[gr, gi], [-gi, gr]])
    m1d = np.arange(n1 // 2)[:, None]
    k1d = np.arange(n1)[None, :]
    dang = 2.0 * np.pi * ((n2 * m1d * k1d)[None] + m2 * k1d[None]) / total
    fd = np.concatenate([np.cos(dang), -np.sin(dang)], axis=2) / total
    return (fa.astype(np.float32), fb.astype(np.float32), fc.astype(np.float32), fd.astype(np.float32))


def _time_pitch(n2):
    return n2 + SUBLANES


def _spec_pitch(n2):
    return 2 * n2 + SUBLANES


def _fft_stage_a(src_ref, y_ref, fa_ref, n1, n2, k_rows):
    tp, sp = _time_pitch(n2), _spec_pitch(n2)

    def body(m2, c):
        xs = src_ref[pl.ds(m2, k_rows, stride=tp), :].astype(BF16)
        res = _dot(fa_ref[m2, :, 0:k_rows], xs)
        y_ref[pl.ds(m2, n1, stride=sp), :] = res[0:n1]
        y_ref[pl.ds(n2 + m2, n1, stride=sp), :] = res[n1:2 * n1]
        return c
    lax.fori_loop(0, n2, body, 0, unroll=16)


def _long_conv(buf_ref, y_ref, h_ref, fa_ref, fb_ref, fc_ref, fd_ref, n1, n2):
    tp, sp = _time_pitch(n2), _spec_pitch(n2)
    _fft_stage_a(buf_ref, y_ref, fa_ref, n1, n2, n1 // 2)

    def body_k1(k1, c):
        rows = pl.ds(pl.multiple_of(k1 * sp, SUBLANES), 2 * n2)
        z = _dot(fb_ref[...], y_ref[rows, :].astype(BF16))
        zr, zi = z[0:n2], z[n2:2 * n2]
        h = h_ref[k1]
        hr, hi = h[0:n2], h[n2:2 * n2]
        p = jnp.concatenate([zr * hr - zi * hi, zr * hi + zi * hr], axis=0)
        y_ref[rows, :] = _dot(fc_ref[...], p.astype(BF16))
        return c
    lax.fori_loop(0, n1, body_k1, 0, unroll=32)

    def body_m2(m2, c):
        qr = y_ref[pl.ds(m2, n1, stride=sp), :]
        qi = y_ref[pl.ds(n2 + m2, n1, stride=sp), :]
        q = jnp.concatenate([qr, qi], axis=0).astype(BF16)
        buf_ref[pl.ds(m2, n1 // 2, stride=tp), :] = _dot(fd_ref[m2], q)
        return c
    lax.fori_loop(0, n2, body_m2, 0, unroll=16)


def _store_time_blocks(buf_ref, val, n2):
    tp = _time_pitch(n2)
    for m1 in range(val.shape[0] // n2):
        buf_ref[m1 * tp:m1 * tp + n2, :] = val[m1 * n2:(m1 + 1) * n2]


def _load_time_blocks(buf_ref, n, n2):
    tp = _time_pitch(n2)
    return jnp.concatenate([buf_ref[m1 * tp:m1 * tp + n2, :] for m1 in range(n // n2)], axis=0)


def _k_hyena(x1_ref, x2_ref, v_ref, w1_ref, w2_ref, wv_ref, b1_ref, b2_ref, bv_ref, h1_ref, h2_ref,
             fa_ref, fb_ref, fc_ref, fd_ref, o_ref, buf_ref, y_ref, *, n1, n2):
    n = x1_ref.shape[1]
    conv = functools.partial(_long_conv, buf_ref, y_ref, fa_ref=fa_ref, fb_ref=fb_ref, fc_ref=fc_ref,
                             fd_ref=fd_ref, n1=n1, n2=n2)
    _store_time_blocks(buf_ref, _short_conv_val(v_ref[0], wv_ref[...], bv_ref[...]), n2)
    conv(h1_ref)
    gated = _short_conv_val(x1_ref[0], w1_ref[...], b1_ref[...]) * _load_time_blocks(buf_ref, n, n2)
    _store_time_blocks(buf_ref, gated, n2)
    conv(h2_ref)
    o_ref[0] = _short_conv_val(x2_ref[0], w2_ref[...], b2_ref[...]) * _load_time_blocks(buf_ref, n, n2)


def _k_fspec(f_ref, fa_ref, fb_ref, o_ref, y_ref, *, n1, n2):
    sp = _spec_pitch(n2)
    _fft_stage_a(f_ref, y_ref, fa_ref, n1, n2, n1)

    def body_k1(k1, c):
        src = pl.ds(pl.multiple_of(k1 * sp, SUBLANES), 2 * n2)
        dst = pl.ds(pl.multiple_of(k1 * 2 * n2, 2 * n2), 2 * n2)
        o_ref[dst, :] = _dot(fb_ref[...], y_ref[src, :].astype(BF16))
        return c
    lax.fori_loop(0, n1, body_k1, 0, unroll=2)


def filter_spectrum(filt, n):
    n1, n2 = _fft_split(n)
    total, c = filt.shape
    tp, sp = _time_pitch(n2), _spec_pitch(n2)
    fa, fb, _, _ = _dft_mats(n)
    blocks = jnp.pad(filt.reshape(n1, n2, c), ((0, 0), (0, tp - n2), (0, 0))).reshape(n1 * tp, c)
    out = pl.pallas_call(
        functools.partial(_k_fspec, n1=n1, n2=n2),
        grid=(c // LANES,),
        in_specs=[pl.BlockSpec((n1 * tp, LANES), lambda j: (0, j)),
                  pl.BlockSpec((n2, 2 * n1, n1), lambda j: (0, 0, 0)),
                  pl.BlockSpec((2 * n2, 2 * n2), lambda j: (0, 0))],
        out_specs=pl.BlockSpec((total * 2, LANES), lambda j: (0, j)),
        out_shape=jax.ShapeDtypeStruct((2 * total, c), F32),
        scratch_shapes=[pltpu.VMEM((n1 * sp, LANES), F32)],
        compiler_params=_params("arbitrary"),
        name="filter_spectrum",
    )(blocks, jnp.asarray(fa, BF16), jnp.asarray(fb, BF16))
    return out.reshape(n1, 2 * n2, c)


def _k_filter_mlp(feat_ref, win_ref, w1_ref, b1_ref, f1_ref, w2_ref, b2_ref, f2_ref, w3_ref, b3_ref, o_ref):
    h = jnp.sin(f1_ref[...] * (_dot_hi(feat_ref[...], w1_ref[...]) + b1_ref[...]))
    h = jnp.sin(f2_ref[...] * (_dot_hi(h, w2_ref[...]) + b2_ref[...]))
    o_ref[...] = (_dot_hi(h, w3_ref[...]) + b3_ref[...]) * win_ref[...]


@functools.lru_cache(maxsize=None)
def _filter_consts(n):
    t = np.linspace(0.0, 1.0, n, dtype=np.float32)[:, None]
    bands = np.linspace(1e-4, HY_BANDS - 1, HY_BANDS, dtype=np.float32)
    ang = (np.float32(2 * math.pi) * np.arange(n, dtype=np.float32) / np.float32(n))[:, None] * bands
    feats = np.concatenate([t, np.cos(ang), -np.sin(ang)], axis=-1).astype(np.float32)
    pad = (-feats.shape[1]) % SUBLANES
    feats = np.pad(feats, ((0, 0), (0, pad)))
    deltas = np.abs(np.linspace(math.log(HY_DECAY_TARGET) / HY_SLOW_PCT,
                                math.log(HY_DECAY_TARGET) / HY_FAST_PCT, MIX_WIDTH, dtype=np.float32))
    window = np.exp(-t * deltas).astype(np.float32)
    return feats, np.tile(window, (1, 2 * HY_ORDER))


def hyena_filter_table(n, w1, b1, f1, w2, b2, f2, w3, b3):
    feats, window = _filter_consts(n)
    kf = feats.shape[1]
    fd = w1.shape[1]
    ncol = w3.shape[1]
    w1p = jnp.pad(w1, ((0, kf - w1.shape[0]), (0, 0)))
    tm = min(ROW_TILE, n)
    full = lambda shape: pl.BlockSpec(shape, lambda i: (0,) * len(shape))
    return pl.pallas_call(
        _k_filter_mlp,
        grid=(n // tm,),
        in_specs=[pl.BlockSpec((tm, kf), lambda i: (i, 0)),
                  pl.BlockSpec((tm, ncol), lambda i: (i, 0)),
                  full((kf, fd)), full((1, fd)), full((1, fd)),
                  full((fd, fd)), full((1, fd)), full((1, fd)),
                  full((fd, ncol)), full((1, ncol))],
        out_specs=pl.BlockSpec((tm, ncol), lambda i: (i, 0)),
        out_shape=jax.ShapeDtypeStruct((n, ncol), F32),
        compiler_params=_params("arbitrary"),
        name="hyena_filter_mlp",
    )(jnp.asarray(feats), jnp.asarray(window), w1p, b1.reshape(1, fd), f1.reshape(1, fd),
      w2, b2.reshape(1, fd), f2.reshape(1, fd), w3, b3.reshape(1, ncol))


def hyena_spectra(n, w1, b1, f1, w2, b2, f2, w3, b3, bias):
    hf = hyena_filter_table(n, w1, b1, f1, w2, b2, f2, w3, b3).reshape(n, HY_ORDER, 2, MIX_WIDTH)
    fwd = hf[:, :, 0]
    bwd = hf[1:, :, 1][::-1]
    fwd = fwd.at[0].add(bias)
    filt = jnp.concatenate([fwd, jnp.zeros((1, HY_ORDER, MIX_WIDTH), F32), bwd], axis=0)
    spec = filter_spectrum(filt.reshape(2 * n, HY_ORDER * MIX_WIDTH), n)
    n1, n2 = _fft_split(n)
    return spec.reshape(n1, 2 * n2, HY_ORDER, MIX_WIDTH).transpose(2, 0, 1, 3)


def hyena_branch(z, conv_w, conv_b, spectra):
    bsz, n, _ = z.shape
    n1, n2 = _fft_split(n)
    fa, fb, fc, fd = (jnp.asarray(m, BF16) for m in _dft_mats(n))
    fa = fa[:, :, : n1 // 2]
    nt = MIX_WIDTH // LANES
    cb = conv_b.reshape(1, HY_COLS)
    zspec = lambda off: pl.BlockSpec((1, n, LANES), lambda j, i: (i, 0, off + j))
    wspec = lambda off: pl.BlockSpec((SHORT_CONV, LANES), lambda j, i: (0, off + j))
    bspec = lambda off: pl.BlockSpec((1, LANES), lambda j, i: (0, off + j))
    once = pl.Buffered(1)
    hspec = lambda o: pl.BlockSpec((None, n1, 2 * n2, LANES), lambda j, i: (o, 0, 0, j), pipeline_mode=once)
    full = lambda shape: pl.BlockSpec(shape, lambda j, i: (0,) * len(shape), pipeline_mode=once)
    return pl.pallas_call(
        functools.partial(_k_hyena, n1=n1, n2=n2),
        grid=(nt, bsz),
        in_specs=[zspec(0), zspec(nt), zspec(2 * nt), wspec(0), wspec(nt), wspec(2 * nt),
                  bspec(0), bspec(nt), bspec(2 * nt), hspec(0), hspec(1),
                  full(fa.shape), full(fb.shape), full(fc.shape), full(fd.shape)],
        out_specs=pl.BlockSpec((1, n, LANES), lambda j, i: (i, 0, j)),
        out_shape=jax.ShapeDtypeStruct((bsz, n, MIX_WIDTH), F32),
        scratch_shapes=[pltpu.VMEM((n1 // 2 * _time_pitch(n2), LANES), F32),
                        pltpu.VMEM((n1 * _spec_pitch(n2), LANES), F32)],
        compiler_params=_params("arbitrary", "arbitrary"),
        name="hyena_long_conv",
    )(z, z, z, conv_w, conv_w, conv_w, cb, cb, cb, spectra, spectra, fa, fb, fc, fd)


def _store_head_pairs(o_ref, left, right):
    rows = left.shape[0]
    lane = lax.broadcasted_iota(jnp.int32, (rows, LANES), 1)
    low = lane < RW_HEAD
    for q in range(RW_HEADS // 2):
        lcol = left[:, q * LANES:(q + 1) * LANES]
        rcol = right[:, q * LANES:(q + 1) * LANES]
        heads = (jnp.where(low, lcol, pltpu.roll(rcol, RW_HEAD, 1)),
                 jnp.where(low, pltpu.roll(lcol, RW_HEAD, 1), rcol))
        for h, piece in zip((2 * q, 2 * q + 1), heads):
            for c in range(rows // WKV_STEPS):
                o_ref[c, h * WKV_STEPS:(h + 1) * WKV_STEPS, :] = piece[c * WKV_STEPS:(c + 1) * WKV_STEPS]


def _load_head_sum(y0_ref, y1_ref, h):
    sl = slice(h * WKV_STEPS, (h + 1) * WKV_STEPS)
    return jnp.concatenate([y0_ref[c, sl, :] + y1_ref[c, sl, :] for c in range(y0_ref.shape[0])], axis=0)


def _k_rwprep(uc_ref, ul_ref, gup_ref, wup_ref, w0_ref, aup_ref, a0_ref, kk_ref, ka_ref, rk_ref, e_ref,
              g_o, bonus_o, rv_o, nb0_o, nb1_o, wk0_o, wk1_o, *, ctx_tiles):
    w = MIX_WIDTH
    u = jnp.where(pl.program_id(1) < ctx_tiles, uc_ref[0], ul_ref[0])
    r, k, v = u[:, 0:w], u[:, w:2 * w], u[:, 2 * w:3 * w]
    xw = u[:, 3 * w:3 * w + RW_DECAY_RANK]
    xa = u[:, 3 * w + RW_DECAY_RANK:3 * w + RW_DECAY_RANK + RW_ICLR_RANK]
    xg = u[:, 3 * w + RW_DECAY_RANK + RW_ICLR_RANK:]
    ones_bd = e_ref[...]
    g_o[0] = _dot(_sigmoid(xg).astype(BF16), gup_ref[...])
    kk = k * kk_ref[...]
    kk = kk * lax.rsqrt(jnp.maximum(_seg_sum(kk * kk, ones_bd), 1e-24))
    txw = jnp.tanh(xw)
    kd_sum = None
    for d, (nb_o, wk_o) in enumerate(((nb0_o, wk0_o), (nb1_o, wk1_o))):
        x = -(w0_ref[d] + _dot_hi(txw, wup_ref[d]))
        softplus = jnp.maximum(x, 0.0) + jnp.log(1.0 + jnp.exp(-jnp.abs(x)))
        decay = jnp.exp(-jnp.exp(-softplus - 0.5))
        a = _sigmoid(a0_ref[d] + _dot_hi(xa, aup_ref[d]))
        kd = k * (1.0 + (a - 1.0) * ka_ref[...])
        _store_head_pairs(nb_o, -kk, kk * a)
        _store_head_pairs(wk_o, decay, kd)
        kd_sum = kd if kd_sum is None else kd_sum + kd
    _store_head_pairs(rv_o, r, v)
    bonus_o[0] = _seg_sum(r * kd_sum * rk_ref[...], ones_bd) * v


def rwkv_prep(u_ctx, u_lat, g_up, w_up, w0, a_up, a0, k_k, k_a, r_k, ones_bd):
    bsz, n_ctx, cols = u_ctx.shape
    n_lat = u_lat.shape[1]
    t_total = n_ctx + n_lat
    w = MIX_WIDTH
    tm = min(ROW_TILE, n_ctx)
    ctx_tiles = n_ctx // tm
    full = lambda shape: pl.BlockSpec(shape, lambda i, j: (0,) * len(shape))
    tile = pl.BlockSpec((1, tm, w), lambda i, j: (i, j, 0))
    pair = pl.BlockSpec((tm // WKV_STEPS, RW_HEADS * WKV_STEPS, LANES), lambda i, j: (j, i, 0))
    pair_shape = jax.ShapeDtypeStruct((t_total // WKV_STEPS, bsz * RW_HEADS * WKV_STEPS, LANES), F32)
    return pl.pallas_call(
        functools.partial(_k_rwprep, ctx_tiles=ctx_tiles),
        grid=(bsz, t_total // tm),
        in_specs=[pl.BlockSpec((1, tm, cols), lambda i, j: (i, jnp.minimum(j, ctx_tiles - 1), 0)),
                  pl.BlockSpec((1, tm, cols), lambda i, j: (i, jnp.maximum(j - ctx_tiles, 0), 0)),
                  full((RW_GATE_RANK, w)), full((2, RW_DECAY_RANK, w)), full((2, 1, w)),
                  full((2, RW_ICLR_RANK, w)), full((2, 1, w)), full((1, w)), full((1, w)), full((1, w)),
                  full((w, w))],
        out_specs=[tile, tile] + [pair] * 5,
        out_shape=[jax.ShapeDtypeStruct((bsz, t_total, w), F32)] * 2 + [pair_shape] * 5,
        compiler_params=_params("arbitrary", "arbitrary"),
        name="rwkv_prep",
    )(u_ctx, u_lat, g_up.astype(BF16), w_up, w0.reshape(2, 1, w), a_up, a0.reshape(2, 1, w),
      k_k.reshape(1, w), k_a.reshape(1, w), r_k.reshape(1, w), ones_bd)


def _k_wkv(rv0_ref, rv1_ref, nb0_ref, nb1_ref, wk0_ref, wk1_ref, y0_ref, y1_ref, s_ref, op_ref, yb_ref, *,
           steps):
    @pl.when(pl.program_id(0) == 0)
    def _():
        s_ref[...] = jnp.zeros_like(s_ref)

    chains = rv0_ref.shape[0] // steps
    r_op, v_op, a_op, b_op, w_op, k_op = range(6)

    def rows(t):
        return pl.ds(t, chains, stride=steps), pl.ds(steps - 1 - t, chains, stride=steps)

    for t in range(steps):
        fwd, bwd = rows(t)
        for j, (ref0, ref1) in enumerate(((rv0_ref, rv1_ref), (nb0_ref, nb1_ref), (wk0_ref, wk1_ref))):
            both = jnp.concatenate([ref0[fwd, :], ref1[bwd, :]], axis=0)
            op_ref[t, 2 * j:2 * j + 2] = both.T.reshape(2, RW_HEAD, 2 * chains)

    def advance(t, sa):
        nxt = jnp.minimum(t + 1, steps - 1)
        row = lambda op, i, slot=t: op_ref[slot, op, pl.ds(i, 1), :]
        vv = op_ref[t, v_op]

        def k_rows(ib, acc):
            y, sa_next = acc
            for u in range(WKV_K_UNROLL):
                i = ib * WKV_K_UNROLL + u
                s_new = s_ref[i] * row(w_op, i) + sa * row(b_op, i) + vv * row(k_op, i)
                s_ref[i] = s_new
                y = y + s_new * row(r_op, i)
                sa_next = sa_next + s_new * row(a_op, i, nxt)
            return y, sa_next
        zero = jnp.zeros(s_ref.shape[1:], F32)
        y, sa_next = lax.fori_loop(0, RW_HEAD // WKV_K_UNROLL, k_rows, (zero, zero))
        yb_ref[t] = y
        return sa_next

    sa = jnp.zeros(s_ref.shape[1:], F32)
    for i in range(RW_HEAD):
        sa = sa + s_ref[i] * op_ref[0, a_op, pl.ds(i, 1), :]
    lax.fori_loop(0, steps, advance, sa)

    for t in range(steps):
        y = yb_ref[t]
        yt = jnp.concatenate([y, y], axis=0).T
        fwd, bwd = rows(t)
        y0_ref[fwd, :] = yt[0:chains]
        y1_ref[bwd, :] = yt[chains:2 * chains]


def wkv_scan(rv, nb0, nb1, wk0, wk1, n_ctx):
    nb, rows, _ = rv.shape
    steps = WKV_STEPS
    chains = rows // steps
    nc = n_ctx // steps
    fwd = pl.BlockSpec((None, rows, LANES), lambda i: (i, 0, 0))
    bwd = pl.BlockSpec((None, rows, LANES), lambda i: (jnp.where(i < nc, nc - 1 - i, nb - 1 + nc - i), 0, 0))
    out = jax.ShapeDtypeStruct(rv.shape, F32)
    return pl.pallas_call(
        functools.partial(_k_wkv, steps=steps),
        grid=(nb,),
        in_specs=[fwd, bwd, fwd, bwd, fwd, bwd],
        out_specs=[fwd, bwd],
        out_shape=[out, out],
        scratch_shapes=[pltpu.VMEM((RW_HEAD, RW_HEAD, 2 * chains), F32),
                        pltpu.VMEM((steps, 6, RW_HEAD, 2 * chains), F32),
                        pltpu.VMEM((steps, RW_HEAD, 2 * chains), F32)],
        compiler_params=_params("arbitrary"),
        name="wkv_scan",
    )(rv, rv, nb0, nb1, wk0, wk1)


def rwkv_scan_branch(u_ctx, u_lat, p, ones_bd):
    g, bonus, rv, nb0, nb1, wk0, wk1 = rwkv_prep(
        u_ctx, u_lat, p["rw_g_up"], p["rw_w_up"], p["rw_w0"], p["rw_a_up"], p["rw_a0"], p["rw_k_k"],
        p["rw_k_a"], p["rw_r_k"].reshape(-1), ones_bd)
    y0, y1 = wkv_scan(rv, nb0, nb1, wk0, wk1, u_ctx.shape[1])
    return y0, y1, bonus, g


def _s5_block(i, n_ctx_blocks, n_blocks, reverse):
    if not reverse:
        return i
    return jnp.where(i < n_ctx_blocks, n_ctx_blocks - 1 - i, n_blocks - 1 + n_ctx_blocks - i)


def _k_s5(uc_ref, ul_ref, bre_ref, bim_ref, lre_ref, lim_ref, cre_ref, cim_ref, y_ref, hre_ref, him_ref,
          st_ref, tb_ref, *, steps, reverse, n_ctx_blocks, n_blocks):
    @pl.when(pl.program_id(0) == 0)
    def _():
        st_ref[...] = jnp.zeros_like(st_ref)

    bsz = uc_ref.shape[0]
    nq = MIX_WIDTH // LANES
    in_ctx = _s5_block(pl.program_id(0), n_ctx_blocks, n_blocks, reverse) < n_ctx_blocks
    for b in range(bsz):
        ub = jnp.where(in_ctx, uc_ref[b], ul_ref[b])
        for q in range(nq):
            tb_ref[q, pl.ds(b, steps, stride=bsz), :] = ub[:, q * LANES:(q + 1) * LANES]
    u = jnp.concatenate([tb_ref[q] for q in range(nq)], axis=1).astype(BF16)
    wc, hc = MIX_WIDTH // S5_DIAG_PARTS, S5_HID // S5_DIAG_PARTS
    for part in range(S5_DIAG_PARTS):
        ws, hs = slice(part * wc, (part + 1) * wc), slice(part * hc, (part + 1) * hc)
        hre_ref[:, hs] = _dot(u[:, ws], bre_ref[ws, hs])
        him_ref[:, hs] = _dot(u[:, ws], bim_ref[ws, hs])
    for ch in range(S5_HID // S5_LANE_CHUNK):
        cols = slice(ch * S5_LANE_CHUNK, (ch + 1) * S5_LANE_CHUNK)
        lre = lre_ref[:, cols]
        lim = lim_ref[:, cols]

        def step(i, carry):
            hr, hi = carry
            t = (steps - 1 - i) if reverse else i
            rows = pl.ds(pl.multiple_of(t * bsz, bsz), bsz)
            nr = lre * hr - lim * hi + hre_ref[rows, cols]
            ni = lre * hi + lim * hr + him_ref[rows, cols]
            hre_ref[rows, cols] = nr
            him_ref[rows, cols] = ni
            return nr, ni
        hr, hi = lax.fori_loop(0, steps, step, (st_ref[0, :, cols], st_ref[1, :, cols]))
        st_ref[0, :, cols] = hr
        st_ref[1, :, cols] = hi
    lanes_per_part = wc // LANES
    for part in range(S5_DIAG_PARTS):
        ws, hs = slice(part * wc, (part + 1) * wc), slice(part * hc, (part + 1) * hc)
        y = (_dot(hre_ref[:, hs].astype(BF16), cre_ref[hs, ws])
             - _dot(him_ref[:, hs].astype(BF16), cim_ref[hs, ws]))
        for q in range(lanes_per_part):
            tb_ref[part * lanes_per_part + q] = y[:, q * LANES:(q + 1) * LANES]
    for b in range(bsz):
        for q in range(nq):
            y_ref[b, :, q * LANES:(q + 1) * LANES] = tb_ref[q, pl.ds(b, steps, stride=bsz), :]


def s5_scan(z_ctx, z_lat, bre, bim, lre, lim, cre, cim, reverse):
    bsz, n_ctx, w = z_ctx.shape
    t_total = n_ctx + z_lat.shape[1]
    steps = S5_STEPS
    nc = n_ctx // steps
    nb = t_total // steps
    blk = functools.partial(_s5_block, n_ctx_blocks=nc, n_blocks=nb, reverse=reverse)
    full = lambda shape: pl.BlockSpec(shape, lambda i: (0,) * len(shape))
    return pl.pallas_call(
        functools.partial(_k_s5, steps=steps, reverse=reverse, n_ctx_blocks=nc, n_blocks=nb),
        grid=(nb,),
        in_specs=[pl.BlockSpec((bsz, steps, w), lambda i: (0, jnp.minimum(blk(i), nc - 1), 0)),
                  pl.BlockSpec((bsz, steps, w), lambda i: (0, jnp.maximum(blk(i) - nc, 0), 0)),
                  full((w, S5_HID)), full((w, S5_HID)), full((bsz, S5_HID)), full((bsz, S5_HID)),
                  full((S5_HID, w)), full((S5_HID, w))],
        out_specs=pl.BlockSpec((bsz, steps, w), lambda i: (0, blk(i), 0)),
        out_shape=jax.ShapeDtypeStruct((bsz, t_total, w), F32),
        scratch_shapes=[pltpu.VMEM((steps * bsz, S5_HID), F32), pltpu.VMEM((steps * bsz, S5_HID), F32),
                        pltpu.VMEM((2, bsz, S5_HID), F32), pltpu.VMEM((w // LANES, steps * bsz, LANES), F32)],
        compiler_params=_params("arbitrary"),
        name="s5_scan",
    )(z_ctx, z_lat, bre, bim, lre, lim, cre, cim)


def _k_s5disc(are_ref, aim_ref, ldt_ref, lre_ref, lim_ref, cr_ref, ci_ref):
    a_re, a_im = are_ref[...], aim_ref[...]
    dt = jnp.exp(ldt_ref[...])
    mag = jnp.exp(a_re * dt)
    lb_re, lb_im = mag * jnp.cos(a_im * dt), mag * jnp.sin(a_im * dt)
    den = a_re * a_re + a_im * a_im
    nr = lb_re - 1.0
    lre_ref[...] = lb_re
    lim_ref[...] = lb_im
    cr_ref[...] = (nr * a_re + lb_im * a_im) / den
    ci_ref[...] = (lb_im * a_re - nr * a_im) / den


def s5_operators(a_re, a_im, log_dt, b_re, b_im, c_re, c_im, bsz):
    g, p, h = S5_GROUPS, S5_STATE, S5_GROUP
    shp = jax.ShapeDtypeStruct((2 * g, p), F32)
    lb_re, lb_im, cr, ci = pl.pallas_call(
        _k_s5disc, out_shape=[shp] * 4, name="s5_discretise",
    )(a_re.reshape(2 * g, p), a_im.reshape(2 * g, p), jnp.broadcast_to(log_dt.reshape(2 * g, 1), (2 * g, p)))
    cr, ci = cr.reshape(2, g, p, 1), ci.reshape(2, g, p, 1)
    bb_re = cr * b_re[None] - ci * b_im[None]
    bb_im = cr * b_im[None] + ci * b_re[None]
    eye = jnp.eye(g, dtype=F32)
    bd_in = lambda m: jnp.einsum("dgph,gk->dghkp", m, eye).reshape(2, g * h, g * p).astype(BF16)
    bd_out = lambda m: jnp.einsum("ghp,gk->gpkh", m, eye).reshape(g * p, g * h).astype(BF16)
    lam = lambda m: jnp.broadcast_to(m.reshape(2, 1, g * p), (2, bsz, g * p))
    return bd_in(bb_re), bd_in(bb_im), lam(lb_re), lam(lb_im), bd_out(c_re), bd_out(c_im)


def s5_branch(z_ctx, z_lat, p):
    bsz = z_ctx.shape[0]
    bre, bim, lre, lim, cre, cim = s5_operators(p["s5_a_re"], p["s5_a_im"], p["s5_log_dt"], p["s5_b_re"],
                                                p["s5_b_im"], p["s5_c_re"], p["s5_c_im"], bsz)
    return (s5_scan(z_ctx, z_lat, bre[0], bim[0], lre[0], lim[0], cre, cim, False),
            s5_scan(z_ctx, z_lat, bre[1], bim[1], lre[1], lim[1], cre, cim, True))


def _k_merge(yhy_ref, rwy0_ref, rwy1_ref, rwb_ref, rwg_ref, s5y0_ref, s5y1_ref, s5u_ref, zg_ref, x_ref, g1_ref,
             e_ref, lng_ref, lnb_ref, s5d_ref, gluw_ref, glub_ref, wb_ref, wo_ref, o_ref):
    w = MIX_WIDTH
    ones_bd = e_ref[...]
    lane = lax.broadcasted_iota(jnp.int32, (x_ref.shape[1], LANES), 1)
    cols = []
    for q in range(RW_HEADS // 2):
        even = _load_head_sum(rwy0_ref, rwy1_ref, 2 * q)
        odd = _load_head_sum(rwy0_ref, rwy1_ref, 2 * q + 1)
        cols.append(jnp.where(lane < RW_HEAD, even, pltpu.roll(odd, RW_HEAD, 1)))
    y = jnp.concatenate(cols, axis=1) + rwb_ref[0]
    mu = _seg_sum(y, ones_bd) * (1.0 / RW_HEAD)
    yc = y - mu
    var = _seg_sum(yc * yc, ones_bd) * (1.0 / RW_HEAD)
    y_rw = (yc * lax.rsqrt(var + RW_LN_EPS) * lng_ref[...] + lnb_ref[...]) * rwg_ref[0]
    s = s5y0_ref[0] + s5y1_ref[0] + s5u_ref[0] * s5d_ref[...]
    s = 0.5 * s * (1.0 + jnp.tanh(math.sqrt(2.0 / math.pi) * (s + 0.044715 * (s * s * s))))
    lg = _dot(s.astype(BF16), gluw_ref[...]) + glub_ref[...]
    y_s5 = lg[:, 0:w] * _sigmoid(lg[:, w:2 * w])
    zg = zg_ref[0]
    d = D_MODEL
    m = (_sigmoid(zg[:, 0:d]) * _dot(yhy_ref[0].astype(BF16), wb_ref[0])
         + _sigmoid(zg[:, d:2 * d]) * _dot(y_rw.astype(BF16), wb_ref[1])
         + _sigmoid(zg[:, 2 * d:3 * d]) * _dot(y_s5.astype(BF16), wb_ref[2]))
    o_ref[0] = x_ref[0] + g1_ref[0] * _dot(m.astype(BF16), wo_ref[...])


def merge_residual(y_hy, rw, s5_y, t_off, s5_u, zg, x, g1, p, ones_bd):
    bsz, n, d = x.shape
    w = MIX_WIDTH
    tm = min(ROW_TILE, n)
    assert t_off % tm == 0
    off = t_off // tm
    tile = lambda c: pl.BlockSpec((1, tm, c), lambda i, j: (i, j, 0))
    seq_tile = pl.BlockSpec((1, tm, w), lambda i, j: (i, off + j, 0))
    chain_tile = pl.BlockSpec((tm // WKV_STEPS, RW_HEADS * WKV_STEPS, LANES), lambda i, j: (off + j, i, 0))
    full = lambda shape: pl.BlockSpec(shape, lambda i, j: (0,) * len(shape))
    rw_y0, rw_y1, rw_bonus, rw_g = rw
    return pl.pallas_call(
        _k_merge,
        grid=(bsz, n // tm),
        in_specs=[tile(w), chain_tile, chain_tile, seq_tile, seq_tile, seq_tile, seq_tile, tile(w),
                  tile(N_BRANCH * d), tile(d), pl.BlockSpec((1, 1, d), lambda i, j: (i, 0, 0)),
                  full((w, w)), full((1, w)), full((1, w)), full((1, w)), full((w, 2 * w)), full((1, 2 * w)),
                  full((N_BRANCH, w, d)), full((d, d))],
        out_specs=tile(d),
        out_shape=jax.ShapeDtypeStruct(x.shape, F32),
        compiler_params=_params("arbitrary", "arbitrary"),
        name="merge_residual",
    )(y_hy, rw_y0, rw_y1, rw_bonus, rw_g, s5_y[0], s5_y[1], s5_u, zg, x, g1, ones_bd, p["rw_ln_g"].reshape(1, w),
      p["rw_ln_b"].reshape(1, w), p["s5_d"].reshape(1, w), p["s5_glu_w"].astype(BF16),
      p["s5_glu_b"].reshape(1, 2 * w), p["w_branch"].astype(BF16), p["w_out"].astype(BF16))


def _k_router(x_ref, sh_ref, sc_ref, g_ref, rw_ref, rb_ref, h_o, idx_o, gate_o, cnt_o):
    @pl.when((pl.program_id(0) == 0) & (pl.program_id(1) == 0))
    def _():
        cnt_o[...] = jnp.zeros_like(cnt_o)

    h = _rms_modulate(x_ref[0], g_ref[...], sh_ref[0], sc_ref[0])
    h_o[0] = h.astype(BF16)
    logits = _dot_hi(h, rw_ref[...]) + rb_ref[...]
    lane = lax.broadcasted_iota(jnp.int32, logits.shape, 1)
    vals, idxs = [], []
    for _ in range(TOP_K):
        m = jnp.max(logits, axis=-1, keepdims=True)
        idx = jnp.min(jnp.where(logits == m, lane, LANES), axis=-1, keepdims=True)
        vals.append(m)
        idxs.append(idx)
        logits = jnp.where(lane == idx, -jnp.inf, logits)
    exps = [jnp.exp(v - vals[0]) for v in vals]
    inv = 1.0 / (exps[0] + exps[1] + exps[2] + exps[3])
    idx_out = jnp.zeros(logits.shape, jnp.int32)
    gate_out = jnp.zeros(logits.shape, F32)
    picked = jnp.zeros(logits.shape, F32)
    for j in range(TOP_K):
        idx_out = jnp.where(lane == j, idxs[j], idx_out)
        gate_out = jnp.where(lane == j, exps[j] * inv, gate_out)
        picked = picked + jnp.where(lane == idxs[j], 1.0, 0.0)
    idx_o[0] = idx_out
    gate_o[0] = gate_out
    cnt_o[...] = cnt_o[...] + jnp.sum(picked, axis=0, keepdims=True)


def moe_route(x, shift, scale, g, router_w, router_b):
    bsz, n, d = x.shape
    tm = min(ROW_TILE, n)
    rw = jnp.pad(router_w, ((0, 0), (0, LANES - N_EXPERTS)))
    rb = jnp.pad(router_b, (0, LANES - N_EXPERTS), constant_values=-jnp.inf).reshape(1, LANES)
    tile = lambda c: pl.BlockSpec((1, tm, c), lambda i, j: (i, j, 0))
    vec = pl.BlockSpec((1, 1, d), lambda i, j: (i, 0, 0))
    full = lambda shape: pl.BlockSpec(shape, lambda i, j: (0,) * len(shape))
    return pl.pallas_call(
        _k_router,
        grid=(bsz, n // tm),
        in_specs=[tile(d), vec, vec, full((1, d)), full((d, LANES)), full((1, LANES))],
        out_specs=[tile(d), tile(LANES), tile(LANES), full((SUBLANES, LANES))],
        out_shape=[jax.ShapeDtypeStruct((bsz, n, d), BF16), jax.ShapeDtypeStruct((bsz, n, LANES), jnp.int32),
                   jax.ShapeDtypeStruct((bsz, n, LANES), F32), jax.ShapeDtypeStruct((SUBLANES, LANES), F32)],
        compiler_params=_params("arbitrary", "arbitrary"),
        name="moe_route",
    )(x, shift, scale, g.reshape(1, d), rw, rb)


def _k_expert(blk_e_ref, used_ref, xs_ref, w1_ref, b1_ref, w2_ref, b2_ref, o_ref, w1b_ref, w2b_ref):
    i = pl.program_id(0)

    @pl.when(i < used_ref[0])
    def _():
        @pl.when((i == 0) | (blk_e_ref[i] != blk_e_ref[jnp.maximum(i - 1, 0)]))
        def _():
            w1b_ref[...] = w1_ref[0].astype(BF16)
            w2b_ref[...] = w2_ref[0].astype(BF16)

        hid = _dot(xs_ref[...], w1b_ref[...]) + b1_ref[0]
        gl = jnp.minimum(hid[:, 0:D_EXPERT], SWIGLU_LIMIT)
        up = jnp.clip(hid[:, D_EXPERT:2 * D_EXPERT], -SWIGLU_LIMIT, SWIGLU_LIMIT)
        act = (up + 1.0) * gl * _sigmoid(SWIGLU_ALPHA * gl)
        o_ref[...] = (_dot(act.astype(BF16), w2b_ref[...]) + b2_ref[0]).astype(o_ref.dtype)

    @pl.when(i >= used_ref[0])
    def _():
        o_ref[...] = jnp.zeros_like(o_ref)


def expert_ffn(xs, blk_e, n_used, w1, b1, w2, b2):
    n_slots, d = xs.shape
    n_blocks = n_slots // MOE_ROWS
    grid_spec = pltpu.PrefetchScalarGridSpec(
        num_scalar_prefetch=2,
        grid=(n_blocks,),
        in_specs=[pl.BlockSpec((MOE_ROWS, d), lambda i, e, u: (i, 0)),
                  pl.BlockSpec((1, d, 2 * D_EXPERT), lambda i, e, u: (e[i], 0, 0)),
                  pl.BlockSpec((1, 1, 2 * D_EXPERT), lambda i, e, u: (e[i], 0, 0)),
                  pl.BlockSpec((1, D_EXPERT, d), lambda i, e, u: (e[i], 0, 0)),
                  pl.BlockSpec((1, 1, d), lambda i, e, u: (e[i], 0, 0))],
        out_specs=pl.BlockSpec((MOE_ROWS, d), lambda i, e, u: (i, 0)),
        scratch_shapes=[pltpu.VMEM((d, 2 * D_EXPERT), BF16), pltpu.VMEM((D_EXPERT, d), BF16)],
    )
    return pl.pallas_call(
        _k_expert,
        grid_spec=grid_spec,
        out_shape=jax.ShapeDtypeStruct((n_slots, d), BF16),
        compiler_params=_params("arbitrary"),
        name="expert_ffn",
    )(blk_e, n_used.reshape(1).astype(jnp.int32), xs, w1, b1.reshape(b1.shape[0], 1, -1), w2,
      b2.reshape(b2.shape[0], 1, -1))


def _k_offset_add(tab_ref, e_ref, v_ref, o_ref):
    e = e_ref[...]
    acc = v_ref[...]
    for j in range(N_EXPERTS):
        acc = acc + jnp.where(e == j, tab_ref[j], 0)
    o_ref[...] = acc


def expert_offset_add(flat_e, values, table):
    n = flat_e.shape[0]
    rows = n // LANES
    return pl.pallas_call(
        _k_offset_add,
        in_specs=[pl.BlockSpec(memory_space=pltpu.SMEM), pl.BlockSpec(memory_space=pltpu.VMEM),
                  pl.BlockSpec(memory_space=pltpu.VMEM)],
        out_specs=pl.BlockSpec(memory_space=pltpu.VMEM),
        out_shape=jax.ShapeDtypeStruct((rows, LANES), jnp.int32),
        name="expert_offset_add",
    )(table.astype(jnp.int32), flat_e.reshape(rows, LANES), values.reshape(rows, LANES)).reshape(n)


def _k_combine(y0_ref, y1_ref, y2_ref, y3_ref, gate_ref, x_ref, g2_ref, o_ref):
    gate = gate_ref[0]
    acc = None
    for j, y_ref in enumerate((y0_ref, y1_ref, y2_ref, y3_ref)):
        term = y_ref[...].astype(F32) * gate[:, j:j + 1]
        acc = term if acc is None else acc + term
    o_ref[0] = x_ref[0] + g2_ref[0] * acc


def moe_combine(ys, gate, x, g2):
    bsz, n, d = x.shape
    tm = min(ROW_TILE, n)
    tiles = n // tm
    row_tile = pl.BlockSpec((tm, d), lambda i, j: (i * tiles + j, 0))
    return pl.pallas_call(
        _k_combine,
        grid=(bsz, tiles),
        in_specs=[row_tile] * TOP_K + [pl.BlockSpec((1, tm, LANES), lambda i, j: (i, j, 0)),
                                       pl.BlockSpec((1, tm, d), lambda i, j: (i, j, 0)),
                                       pl.BlockSpec((1, 1, d), lambda i, j: (i, 0, 0))],
        out_specs=pl.BlockSpec((1, tm, d), lambda i, j: (i, j, 0)),
        out_shape=jax.ShapeDtypeStruct(x.shape, F32),
        compiler_params=_params("arbitrary", "arbitrary"),
        name="moe_combine",
    )(*ys, gate, x, g2)


def moe_ffn(x, shift, scale, g2, g, router_w, router_b, w1, b1, w2, b2, expert_base=0):
    bsz, n, d = x.shape
    h, idx, gate, cnt = moe_route(x, shift, scale, g, router_w, router_b)
    n_tok = bsz * n
    n_asg = n_tok * TOP_K
    flat_e = idx[..., :TOP_K].reshape(-1)
    order = jnp.argsort(flat_e).astype(jnp.int32)
    rank_of = jnp.argsort(order).astype(jnp.int32)
    counts = cnt[0, :N_EXPERTS].astype(jnp.int32)
    grp_start = jnp.cumsum(counts) - counts
    padded = (counts + MOE_ROWS - 1) // MOE_ROWS * MOE_ROWS
    pad_end = jnp.cumsum(padded)
    pad_start = pad_end - padded
    n_blocks = -(-n_asg // MOE_ROWS) + N_EXPERTS
    n_slots = n_blocks * MOE_ROWS
    blk_first = jnp.arange(n_blocks, dtype=jnp.int32) * MOE_ROWS
    blk_e = jnp.minimum(jnp.sum(pad_end[None, :] <= blk_first[:, None], axis=1), N_EXPERTS - 1).astype(jnp.int32)
    slot_rank = (blk_first - pad_start[blk_e])[:, None] + jnp.arange(MOE_ROWS, dtype=jnp.int32)[None, :]
    slot_used = (slot_rank < counts[blk_e][:, None]).reshape(-1)
    slot_src = jnp.where(slot_used, (grp_start[blk_e][:, None] + slot_rank).reshape(-1), 0)
    slot_asg = order[slot_src]
    slot_tok = jnp.where(slot_used, slot_asg // TOP_K, jnp.arange(n_slots, dtype=jnp.int32) % n_tok)
    slot_of = expert_offset_add(flat_e, rank_of, pad_start - grp_start)
    xs = h.reshape(n_tok, d)[slot_tok]
    ys = expert_ffn(xs, blk_e + expert_base, pad_end[-1] // MOE_ROWS, w1, b1, w2, b2)
    slot_of = slot_of.reshape(n_tok, TOP_K)
    return moe_combine([ys[slot_of[:, j]] for j in range(TOP_K)], gate, x, g2)


def _k_rmsnorm(x_ref, g_ref, o_ref):
    x = x_ref[0]
    ms = jnp.mean(x * x, axis=-1, keepdims=True)
    o_ref[0] = x * lax.rsqrt(ms + NORM_EPS) * g_ref[...]


def final_rmsnorm(x, g):
    b, n, d = x.shape
    tm = min(2 * ROW_TILE, n)
    return pl.pallas_call(
        _k_rmsnorm,
        grid=(b, n // tm),
        in_specs=[pl.BlockSpec((1, tm, d), lambda i, j: (i, j, 0)), pl.BlockSpec((1, d), lambda i, j: (0, 0))],
        out_specs=pl.BlockSpec((1, tm, d), lambda i, j: (i, j, 0)),
        out_shape=jax.ShapeDtypeStruct(x.shape, F32),
        compiler_params=_params("arbitrary", "arbitrary"),
        name="final_rmsnorm",
    )(x, g.reshape(1, d))


@functools.lru_cache(maxsize=None)
def _grid_pos_embed(n_tokens):
    rows = n_tokens // GRID_W
    row_id, col_id = np.meshgrid(np.arange(rows), np.arange(GRID_W), indexing="ij")
    quarter = D_MODEL // 4
    omega = (1.0 / (10000.0 ** (np.arange(quarter, dtype=np.float32) / quarter))).astype(np.float32)

    def enc(pos):
        ang = pos.reshape(-1)[:, None].astype(np.float32) * omega
        return np.concatenate([np.sin(ang), np.cos(ang)], axis=-1)

    return np.concatenate([enc(row_id), enc(col_id)], axis=-1).astype(np.float32)


_LAYER_KEYS = ("ada_w", "ada_b", "norm1_g", "norm2_g", "w_in", "hy_conv_w", "hy_conv_b", "hy_w1", "hy_b1",
               "hy_f1", "hy_w2", "hy_b2", "hy_f2", "hy_w3", "hy_b3", "hy_bias", "rw_conv_w", "rw_conv_b",
               "rw_w_up", "rw_w0", "rw_a_up", "rw_a0", "rw_g_up", "rw_k_k", "rw_k_a", "rw_r_k", "rw_ln_g",
               "rw_ln_b", "s5_a_re", "s5_a_im", "s5_log_dt", "s5_b_re", "s5_b_im", "s5_c_re", "s5_c_im",
               "s5_d", "s5_glu_w", "s5_glu_b", "w_branch", "w_out", "router_w", "router_b", "moe_w1",
               "moe_b1", "moe_w2", "moe_b2")


def _token_mixer(x, xc, mod, mod_c, p, need_ctx, ones_bd):
    sh1, sc1, g1 = mod
    csh1, csc1, cg1 = mod_c
    w_in = p["w_in"].astype(BF16)
    c0, c1, c2 = HY_COLS, HY_COLS + RW_COLS, HY_COLS + RW_COLS + MIX_WIDTH
    proj = lambda t, s, c, lo, hi: norm_mod_matmul(t, s, c, p["norm1_g"], w_in[:, lo:hi])
    hy_args = (p["hy_w1"], p["hy_b1"], p["hy_f1"], p["hy_w2"], p["hy_b2"], p["hy_f2"], p["hy_w3"], p["hy_b3"],
               p["hy_bias"])

    z_hy = proj(x, sh1, sc1, 0, c0)
    y_hy = hyena_branch(z_hy, p["hy_conv_w"], p["hy_conv_b"], hyena_spectra(x.shape[1], *hy_args))
    u_rw = short_conv(proj(x, sh1, sc1, c0, c1), p["rw_conv_w"], p["rw_conv_b"])
    uc_rw = short_conv(proj(xc, csh1, csc1, c0, c1), p["rw_conv_w"], p["rw_conv_b"])
    rw = rwkv_scan_branch(uc_rw, u_rw, p, ones_bd)
    n_ctx = xc.shape[1]
    z_s5 = proj(x, sh1, sc1, c1, c2)
    zc_s5 = proj(xc, csh1, csc1, c1, c2)
    s5_y = s5_branch(zc_s5, z_s5, p)
    zg = proj(x, sh1, sc1, c2, c2 + N_BRANCH * D_MODEL)
    x_new = merge_residual(y_hy, rw, s5_y, n_ctx, z_s5, zg, x, g1, p, ones_bd)
    if not need_ctx:
        return x_new, None
    zc_hy = proj(xc, csh1, csc1, 0, c0)
    yc_hy = hyena_branch(zc_hy, p["hy_conv_w"], p["hy_conv_b"], hyena_spectra(xc.shape[1], *hy_args))
    zcg = proj(xc, csh1, csc1, c2, c2 + N_BRANCH * D_MODEL)
    xc_new = merge_residual(yc_hy, rw, s5_y, 0, zc_s5, zcg, xc, cg1, p, ones_bd)
    return x_new, xc_new


def kernel(x, c, ctx, c_ctx, ada_w, ada_b, norm1_g, norm2_g, w_in, hy_conv_w, hy_conv_b, hy_w1, hy_b1, hy_f1, hy_w2, hy_b2, hy_f2, hy_w3, hy_b3, hy_bias, rw_conv_w, rw_conv_b, rw_w_up, rw_w0, rw_a_up, rw_a0, rw_g_up, rw_k_k, rw_k_a, rw_r_k, rw_ln_g, rw_ln_b, s5_a_re, s5_a_im, s5_log_dt, s5_b_re, s5_b_im, s5_c_re, s5_c_im, s5_d, s5_glu_w, s5_glu_b, w_branch, w_out, router_w, router_b, moe_w1, moe_b1, moe_w2, moe_b2, final_g):
    stacked = dict(zip(_LAYER_KEYS, (ada_w, ada_b, norm1_g, norm2_g, w_in, hy_conv_w, hy_conv_b, hy_w1, hy_b1,
                                     hy_f1, hy_w2, hy_b2, hy_f2, hy_w3, hy_b3, hy_bias, rw_conv_w, rw_conv_b,
                                     rw_w_up, rw_w0, rw_a_up, rw_a0, rw_g_up, rw_k_k, rw_k_a, rw_r_k, rw_ln_g,
                                     rw_ln_b, s5_a_re, s5_a_im, s5_log_dt, s5_b_re, s5_b_im, s5_c_re, s5_c_im,
                                     s5_d, s5_glu_w, s5_glu_b, w_branch, w_out, router_w, router_b, moe_w1,
                                     moe_b1, moe_w2, moe_b2)))
    bsz, n, d = x.shape
    depth = ada_w.shape[0]
    x = add_pos(x, jnp.asarray(_grid_pos_embed(n)))
    xc = ctx
    ones_bd = jnp.asarray(np.kron(np.eye(RW_HEADS), np.ones((RW_HEAD, RW_HEAD))), BF16)
    cvec = jnp.zeros((2 * SUBLANES, d), F32).at[:bsz].set(c).at[bsz].set(c_ctx)
    for l in range(depth):
        p = {k: v[l] for k, v in stacked.items()}
        need_ctx = l < depth - 1
        ada = ada_proj(cvec, p["ada_w"], p["ada_b"])
        lat = [t.reshape(bsz, 1, d) for t in jnp.split(ada[:bsz], 6, axis=-1)]
        cx = [jnp.broadcast_to(t.reshape(1, 1, d), (bsz, 1, d)) for t in jnp.split(ada[bsz], 6, axis=-1)]
        x, xc_new = _token_mixer(x, xc, lat[0:3], cx[0:3], p, need_ctx, ones_bd)
        stack = lambda t: t.reshape((depth * N_EXPERTS,) + t.shape[2:])
        moe = functools.partial(moe_ffn, g=p["norm2_g"], router_w=p["router_w"], router_b=p["router_b"],
                                w1=stack(moe_w1), b1=stack(moe_b1), w2=stack(moe_w2), b2=stack(moe_b2),
                                expert_base=l * N_EXPERTS)
        x = moe(x, lat[3], lat[4], lat[5])
        if need_ctx:
            xc = moe(xc_new, cx[3], cx[4], cx[5])
    return final_rmsnorm(x, final_g)
```
